```python
import math
import jax, jax.numpy as jnp
from jax import lax
import numpy as np

D_MODEL = 1024
BATCH = 16
SEQ = 256
DEPTH = 2
DEC_BATCH = 4
DEC_SEQ = 1024
PAST_LEN = 256

GRID_W = 64
HEAD_DIM = 64
DA_HEADS = 4
DA_QK = 2 * HEAD_DIM
DA_V = 2 * HEAD_DIM
WG_Q_HEADS = 8
WG_KV_HEADS = 2
WINDOW = 128
RT_HEADS = 4
RT_DK = HEAD_DIM
RT_DV = 2 * HEAD_DIM
RT_CHUNK = 128
BRANCH_W = 512
SPLITS = (DA_HEADS * DA_QK, DA_HEADS * DA_QK, DA_HEADS * DA_V,
          WG_Q_HEADS * HEAD_DIM, WG_KV_HEADS * HEAD_DIM, WG_KV_HEADS * HEAD_DIM,
          RT_HEADS * RT_DK, RT_HEADS * RT_DK, RT_HEADS * RT_DV, RT_HEADS * RT_DV)
D_IN = sum(SPLITS)
D_FF = 4 * D_MODEL
Q_BLOCK = 128
ROPE_BASE = 10000.0
LN_EPS = 1e-5
NEG_INF = -1e30
ALPHA = (2 * DEPTH) ** 0.25
BETA = (8 * DEPTH) ** -0.25

kernel_name = "hybrid_diffattn_swa_retention_dit_step"

F32 = jnp.float32


def layer_norm(x, g, b):
    xf = x.astype(F32)
    mu = jnp.mean(xf, axis=-1, keepdims=True)
    var = jnp.mean(jnp.square(xf - mu), axis=-1, keepdims=True)
    y = (xf - mu) * lax.rsqrt(var + LN_EPS)
    return (y * g.astype(F32) + b.astype(F32)).astype(x.dtype)


def head_rms_norm(x, g):
    xf = x.astype(F32)
    y = xf * lax.rsqrt(jnp.mean(jnp.square(xf), axis=-1, keepdims=True) + LN_EPS)
    return (y * g.astype(F32)).astype(x.dtype)


def head_layer_norm(x, g):
    xf = x.astype(F32)
    mu = jnp.mean(xf, axis=-1, keepdims=True)
    var = jnp.mean(jnp.square(xf - mu), axis=-1, keepdims=True)
    return ((xf - mu) * lax.rsqrt(var + LN_EPS) * g.astype(F32)).astype(x.dtype)


def axial_rope_angles(rows, dim):
    r, col = jnp.meshgrid(jnp.arange(rows), jnp.arange(GRID_W), indexing="ij")
    r = r.reshape(-1).astype(F32)
    col = col.reshape(-1).astype(F32)
    nf = dim // 4
    inv = ROPE_BASE ** (-jnp.arange(nf, dtype=F32) / nf)
    return r[:, None] * inv[None, :], col[:, None] * inv[None, :]


def apply_axial_rope(x, ang):
    ang_r, ang_c = ang

    def rot(u, a):
        cos = jnp.cos(a)[None, :, None, :]
        sin = jnp.sin(a)[None, :, None, :]
        u1, u2 = jnp.split(u.astype(F32), 2, axis=-1)
        return jnp.concatenate([u1 * cos - u2 * sin, u1 * sin + u2 * cos], axis=-1)

    xr, xc = jnp.split(x, 2, axis=-1)
    return jnp.concatenate([rot(xr, ang_r), rot(xc, ang_c)], axis=-1).astype(x.dtype)


def rope_subheads(x, ang):
    b, t, h, _ = x.shape
    return apply_axial_rope(x.reshape(b, t, h * 2, HEAD_DIM), ang).reshape(b, t, h, 2 * HEAD_DIM)


def to_blocks(t, size):
    b, n = t.shape[:2]
    return jnp.moveaxis(t.reshape(b, n // size, size, *t.shape[2:]), 1, 0)


def from_blocks(o):
    o = jnp.moveaxis(o, 0, 1)
    return o.reshape(o.shape[0], o.shape[1] * o.shape[2], *o.shape[3:])


def diff_attention(q, k, v, lam):
    scale = HEAD_DIM ** -0.5
    k1, k2 = k[..., :HEAD_DIM], k[..., HEAD_DIM:]

    def block(qb):
        q1, q2 = qb[..., :HEAD_DIM], qb[..., HEAD_DIM:]
        s1 = jnp.einsum("bqhd,bkhd->bhqk", q1, k1).astype(F32) * scale
        s2 = jnp.einsum("bqhd,bkhd->bhqk", q2, k2).astype(F32) * scale
        a = jax.nn.softmax(s1, axis=-1) - lam * jax.nn.softmax(s2, axis=-1)
        return jnp.einsum("bhqk,bkhe->bqhe", a.astype(v.dtype), v)

    return from_blocks(lax.map(block, to_blocks(q, Q_BLOCK)))


def sink_logits(sink, shape):
    hkv, g = shape[1], shape[2]
    return jnp.broadcast_to(sink.astype(F32).reshape(hkv, g)[None, :, :, None, None], shape[:-1] + (1,))


def dense_sink_attention(q, k, v, sink):
    b, t, hq, d = q.shape
    hkv = k.shape[2]
    q = q.reshape(b, t, hkv, hq // hkv, d)
    scale = d ** -0.5

    def block(qb):
        s = jnp.einsum("bqhgd,bkhd->bhgqk", qb, k).astype(F32) * scale
        p = jax.nn.softmax(jnp.concatenate([s, sink_logits(sink, s.shape)], axis=-1), axis=-1)[..., :-1]
        return jnp.einsum("bhgqk,bkhd->bqhgd", p.astype(v.dtype), v)

    return from_blocks(lax.map(block, to_blocks(q, Q_BLOCK))).reshape(b, t, hq * d)


def window_sink_attention(q, k, v, k_ctx, v_ctx, sink):
    b, t, hq, d = q.shape
    hkv = k.shape[2]
    nb = t // WINDOW
    q = q.reshape(b, t, hkv, hq // hkv, d)
    scale = d ** -0.5
    pad = ((0, 0), (WINDOW, WINDOW), (0, 0), (0, 0))
    kp = jnp.pad(k, pad)
    vp = jnp.pad(v, pad)
    qi = jnp.arange(WINDOW)[:, None]
    kj = jnp.arange(3 * WINDOW)[None, :]
    rel = kj - WINDOW - qi

    def block(args):
        n, qb = args
        kb = lax.dynamic_slice_in_dim(kp, n * WINDOW, 3 * WINDOW, axis=1)
        vb = lax.dynamic_slice_in_dim(vp, n * WINDOW, 3 * WINDOW, axis=1)
        kpos = n * WINDOW - WINDOW + kj
        valid = (jnp.abs(rel) <= WINDOW) & (kpos >= 0) & (kpos < t)
        s_loc = jnp.einsum("bqhgd,bkhd->bhgqk", qb, kb).astype(F32) * scale
        s_loc = jnp.where(valid, s_loc, NEG_INF)
        s_ctx = jnp.einsum("bqhgd,bchd->bhgqc", qb, k_ctx).astype(F32) * scale
        p = jax.nn.softmax(jnp.concatenate([s_loc, s_ctx, sink_logits(sink, s_loc.shape)], axis=-1), axis=-1)
        p_loc = p[..., :3 * WINDOW].astype(v.dtype)
        p_ctx = p[..., 3 * WINDOW:-1].astype(v.dtype)
        return (jnp.einsum("bhgqk,bkhd->bqhgd", p_loc, vb)
                + jnp.einsum("bhgqc,bchd->bqhgd", p_ctx, v_ctx))

    o = lax.map(block, (jnp.arange(nb), to_blocks(q, WINDOW)))
    return from_blocks(o).reshape(b, t, hq * d)


def retention_direction(q, k, v, log_g, s0):
    c = RT_CHUNK
    pos = jnp.arange(c, dtype=F32)
    diff = pos[:, None] - pos[None, :]
    decay_in = jnp.where(diff[None] >= 0,
                         jnp.exp(jnp.maximum(diff, 0.0)[None] * log_g[:, None, None]), 0.0)
    xi = jnp.exp((pos + 1.0)[None, :] * log_g[:, None]).T[None, :, :, None]
    zeta = jnp.exp((c - 1.0 - pos)[None, :] * log_g[:, None])
    g_chunk = jnp.exp(c * log_g)[None, :, None, None]

    def chunk(s, inp):
        qc, kc, vc = inp
        a = jnp.einsum("bqhd,bkhd->bhqk", qc, kc) * decay_in
        inner = jnp.einsum("bhqk,bkhe->bqhe", a, vc)
        cross = jnp.einsum("bqhd,bhde->bqhe", qc, s) * xi
        s_new = g_chunk * s + jnp.einsum("bkhd,bkhe,hk->bhde", kc, vc, zeta)
        return s_new, inner + cross

    s_fin, out = lax.scan(chunk, s0, (to_blocks(q, c), to_blocks(k, c), to_blocks(v, c)))
    return from_blocks(out), s_fin


def bidirectional_retention(q, k, v, log_g2, s0):
    q, k, v = q.astype(F32), k.astype(F32), v.astype(F32)
    s0 = s0.astype(F32)
    y_f, s_f = retention_direction(q, k, v, log_g2[0], s0[:, 0])
    flip = lambda a: jnp.flip(a, axis=1)
    y_b, s_b = retention_direction(flip(q), flip(k), flip(v), log_g2[1], s0[:, 1])
    return y_f + flip(y_b), jnp.stack([s_f, s_b], axis=1)


def token_mixer(h, P, lam_init, ctx, rope_ang):
    b, t, _ = h.shape
    z = h @ P["w_in"]
    parts = []
    off = 0
    for s in SPLITS:
        parts.append(z[..., off:off + s])
        off += s
    aq, ak, av, bq, bk, bv, cq, ck, cv, cg = parts
    aq = aq.reshape(b, t, DA_HEADS, DA_QK)
    ak = ak.reshape(b, t, DA_HEADS, DA_QK)
    av = av.reshape(b, t, DA_HEADS, DA_V)
    bq = bq.reshape(b, t, WG_Q_HEADS, HEAD_DIM)
    bk = bk.reshape(b, t, WG_KV_HEADS, HEAD_DIM)
    bv = bv.reshape(b, t, WG_KV_HEADS, HEAD_DIM)
    cq = cq.reshape(b, t, RT_HEADS, RT_DK)
    ck = ck.reshape(b, t, RT_HEADS, RT_DK) * (RT_DK ** -0.5)
    cv = cv.reshape(b, t, RT_HEADS, RT_DV)
    if rope_ang is not None:
        aq = rope_subheads(aq, rope_ang)
        ak = rope_subheads(ak, rope_ang)
        bq = apply_axial_rope(bq, rope_ang)
        bk = apply_axial_rope(bk, rope_ang)
    lv = P["diff_lam"].astype(F32)
    lam = jnp.exp(jnp.sum(lv[0] * lv[1])) - jnp.exp(jnp.sum(lv[2] * lv[3])) + lam_init
    if ctx is None:
        oa = diff_attention(aq, ak, av, lam)
        ob = dense_sink_attention(bq, bk, bv, P["win_sink"])
        s0 = jnp.zeros((b, 2, RT_HEADS, RT_DK, RT_DV), F32)
    else:
        ctx_ak, ctx_av, ctx_bk, ctx_bv, s0 = ctx
        oa = diff_attention(aq, jnp.concatenate([ak, ctx_ak], axis=1),
                            jnp.concatenate([av, ctx_av], axis=1), lam)
        ob = window_sink_attention(bq, bk, bv, ctx_bk, ctx_bv, P["win_sink"])
    oa = (head_rms_norm(oa, P["diff_norm_g"]) * (1.0 - lam_init)).reshape(b, t, BRANCH_W)
    log_g2 = jax.nn.log_sigmoid(P["ret_decay"].astype(F32))
    yc, s_fin = bidirectional_retention(cq, ck, cv, log_g2, s0)
    oc = head_layer_norm(yc, P["ret_norm_g"]).astype(h.dtype).reshape(b, t, BRANCH_W) * jax.nn.silu(cg)
    gates = jax.nn.sigmoid(h @ P["w_gate"] + P["b_gate"])
    ga, gb, gc = jnp.split(gates, 3, axis=-1)
    merged = ga * (oa @ P["w_pa"]) + gb * (ob @ P["w_pb"]) + gc * (oc @ P["w_pc"])
    out = merged @ P["w_o"]
    new_ctx = (ak, av, bk, bv, s_fin.astype(h.dtype)) if ctx is None else None
    return out, new_ctx


def trunk_layer(x, mod, P, lam_init, ctx, rope_ang):
    sh1, sc1, g1, sh2, sc2, g2 = jnp.split(mod, 6, axis=-1)
    h = x * (1.0 + sc1) + sh1
    y, new_ctx = token_mixer(h, P, lam_init, ctx, rope_ang)
    x = layer_norm(ALPHA * x + g1 * y, P["ln1_g"], P["ln1_b"])
    h = x * (1.0 + sc2) + sh2
    f = jnp.square(jax.nn.relu(h @ P["w_ff1"])) @ P["w_ff2"]
    x = layer_norm(ALPHA * x + g2 * f, P["ln2_g"], P["ln2_b"])
    return x, new_ctx


def setup_inputs(seed: int = 0) -> dict:
    key = jax.random.key(seed)
    ks = jax.random.split(key, 40)
    nrm = lambda k, shape, s: jax.random.normal(k, shape, F32) * s
    D = D_MODEL
    p = 2.0 ** (-(5.0 + jnp.arange(RT_HEADS, dtype=F32)))
    base_logit = jnp.log1p(-p) - jnp.log(p)
    return {
        "x_prompt": nrm(ks[0], (BATCH, SEQ, D), 1.0),
        "x_sample": nrm(ks[1], (DEC_BATCH, DEC_SEQ, D), 1.0),
        "c": nrm(ks[2], (DEC_BATCH, D), 1.0),
        "cache_diff_k": nrm(ks[3], (DEC_BATCH, DEPTH, PAST_LEN, DA_HEADS, DA_QK), 1.0),
        "cache_diff_v": nrm(ks[4], (DEC_BATCH, DEPTH, PAST_LEN, DA_HEADS, DA_V), 1.0),
        "cache_win_k": nrm(ks[5], (DEC_BATCH, DEPTH, PAST_LEN, WG_KV_HEADS, HEAD_DIM), 1.0),
        "cache_win_v": nrm(ks[6], (DEC_BATCH, DEPTH, PAST_LEN, WG_KV_HEADS, HEAD_DIM), 1.0),
        "state_ret": nrm(ks[7], (DEC_BATCH, DEPTH, 2, RT_HEADS, RT_DK, RT_DV), 1.0),
        "c_ctx": nrm(ks[8], (D,), 1.0),
        "w_mod": nrm(ks[9], (DEPTH, D, 6 * D), D ** -0.5),
        "b_mod": nrm(ks[10], (DEPTH, 6 * D), 0.02),
        "w_in": nrm(ks[11], (DEPTH, D, D_IN), D ** -0.5),
        "diff_lam": nrm(ks[12], (DEPTH, 4, HEAD_DIM), 0.1),
        "diff_norm_g": 1.0 + nrm(ks[13], (DEPTH, DA_V), 0.02),
        "win_sink": nrm(ks[14], (DEPTH, WG_Q_HEADS), 0.5),
        "ret_decay": base_logit[None, None, :] + nrm(ks[15], (DEPTH, 2, RT_HEADS), 0.05),
        "ret_norm_g": 1.0 + nrm(ks[16], (DEPTH, RT_DV), 0.02),
        "w_pa": nrm(ks[17], (DEPTH, BRANCH_W, D), BRANCH_W ** -0.5),
        "w_pb": nrm(ks[18], (DEPTH, BRANCH_W, D), BRANCH_W ** -0.5),
        "w_pc": nrm(ks[19], (DEPTH, BRANCH_W, D), BRANCH_W ** -0.5),
        "w_gate": nrm(ks[20], (DEPTH, D, 3 * D), D ** -0.5),
        "b_gate": nrm(ks[21], (DEPTH, 3 * D), 0.02),
        "w_o": nrm(ks[22], (DEPTH, D, D), BETA * D ** -0.5),
        "ln1_g": 1.0 + nrm(ks[23], (DEPTH, D), 0.02),
        "ln1_b": nrm(ks[24], (DEPTH, D), 0.02),
        "w_ff1": nrm(ks[25], (DEPTH, D, D_FF), D ** -0.5),
        "w_ff2": nrm(ks[26], (DEPTH, D_FF, D), BETA * D_FF ** -0.5),
        "ln2_g": 1.0 + nrm(ks[27], (DEPTH, D), 0.02),
        "ln2_b": nrm(ks[28], (DEPTH, D), 0.02),
    }


def reference(x_prompt, x_sample, c, cache_diff_k, cache_diff_v, cache_win_k, cache_win_v, state_ret,
              c_ctx, w_mod, b_mod, w_in, diff_lam, diff_norm_g, win_sink, ret_decay, ret_norm_g,
              w_pa, w_pb, w_pc, w_gate, b_gate, w_o, ln1_g, ln1_b, w_ff1, w_ff2, ln2_g, ln2_b):
    rows = x_sample.shape[1] // GRID_W
    rope_ang = axial_rope_angles(rows, HEAD_DIM)
    xp = x_prompt
    xs = x_sample
    diff_k, diff_v, win_k, win_v, ret_s = [], [], [], [], []
    for l in range(DEPTH):
        P = {
            "w_in": w_in[l], "diff_lam": diff_lam[l], "diff_norm_g": diff_norm_g[l],
            "win_sink": win_sink[l], "ret_decay": ret_decay[l], "ret_norm_g": ret_norm_g[l],
            "w_pa": w_pa[l], "w_pb": w_pb[l], "w_pc": w_pc[l], "w_gate": w_gate[l],
            "b_gate": b_gate[l], "w_o": w_o[l], "ln1_g": ln1_g[l], "ln1_b": ln1_b[l],
            "w_ff1": w_ff1[l], "w_ff2": w_ff2[l], "ln2_g": ln2_g[l], "ln2_b": ln2_b[l],
        }
        lam_init = 0.8 - 0.6 * math.exp(-0.3 * l)
        mod_ctx = (jax.nn.silu(c_ctx) @ w_mod[l] + b_mod[l])[None, None, :]
        xp, ctx_l = trunk_layer(xp, mod_ctx, P, lam_init, None, None)
        diff_k.append(ctx_l[0])
        diff_v.append(ctx_l[1])
        win_k.append(ctx_l[2])
        win_v.append(ctx_l[3])
        ret_s.append(ctx_l[4])
        mod_lat = (jax.nn.silu(c) @ w_mod[l] + b_mod[l])[:, None, :]
        cache_l = (cache_diff_k[:, l], cache_diff_v[:, l], cache_win_k[:, l], cache_win_v[:, l], state_ret[:, l])
        xs, _ = trunk_layer(xs, mod_lat, P, lam_init, cache_l, rope_ang)
    new_diff_k = jnp.stack(diff_k, axis=1)
    new_diff_v = jnp.stack(diff_v, axis=1)
    new_win_k = jnp.stack(win_k, axis=1)
    new_win_v = jnp.stack(win_v, axis=1)
    new_state_ret = jnp.stack(ret_s, axis=1)
    return (xp, xs, new_diff_k, new_diff_v, new_win_k, new_win_v, new_state_ret)
```

```python
import functools
import math

import jax
import jax.numpy as jnp
from jax import lax
from jax.experimental import pallas as pl
from jax.experimental.pallas import tpu as pltpu

F32 = jnp.float32
BF16 = jnp.bfloat16

D_MODEL = 1024
DEPTH = 2
GRID_W = 64
HEAD_DIM = 64
DA_HEADS = 4
WG_Q_HEADS = 8
WG_KV_HEADS = 2
WG_GROUP = WG_Q_HEADS // WG_KV_HEADS
WINDOW = 128
RT_HEADS = 4
RT_DK = 64
RT_DV = 128
BRANCH_W = 512
D_IN = 3840
D_GATE = 3 * D_MODEL
D_FF = 4 * D_MODEL
D_MOD = 6 * D_MODEL
ROPE_BASE = 10000.0
LN_EPS = 1e-5
ALPHA = (2 * DEPTH) ** 0.25
QK_SCALE = HEAD_DIM ** -0.5

LANE = 128
MOD_ROWS = 8

COL_AQ, COL_AK, COL_AV = 0, 4, 8
COL_BQ, COL_BK, COL_BV = 12, 16, 17
COL_CQ, COL_CK, COL_CV, COL_CG = 18, 20, 22, 26
ROPE_BLOCKS = tuple(range(0, 8)) + tuple(range(12, 17))

VMEM_LIMIT = 56 * 1024 * 1024
NT_DIMS = (((1,), (1,)), ((), ()))
TN_DIMS = (((0,), (0,)), ((), ()))


def _params():
    return pltpu.CompilerParams(vmem_limit_bytes=VMEM_LIMIT)


def _resident(shape, index_map):
    return pl.BlockSpec(shape, index_map, pipeline_mode=pl.Buffered(1))


def _dot(a, b):
    return jnp.dot(a, b, preferred_element_type=F32)


def _mod_kernel(c_ref, w_ref, b_ref, o_ref):
    c = c_ref[...]
    a = (c * jax.nn.sigmoid(c)).astype(BF16)
    o_ref[...] = _dot(a, w_ref[...].astype(BF16)) + b_ref[...]


def _mod_call(c_rows, w_mod, b_mod):
    tn = 1536
    return pl.pallas_call(
        _mod_kernel,
        grid=(DEPTH, D_MOD // tn),
        in_specs=[
            pl.BlockSpec((MOD_ROWS, D_MODEL), lambda l, n: (0, 0)),
            pl.BlockSpec((None, D_MODEL, tn), lambda l, n: (l, 0, n)),
            pl.BlockSpec((None, 1, tn), lambda l, n: (l, 0, n)),
        ],
        out_specs=pl.BlockSpec((None, MOD_ROWS, tn), lambda l, n: (l, 0, n)),
        out_shape=jax.ShapeDtypeStruct((DEPTH, MOD_ROWS, D_MOD), F32),
        compiler_params=_params(),
        name="mod_vectors",
    )(c_rows, w_mod, b_mod.reshape(DEPTH, 1, D_MOD))


def _proj_kernel(*refs, rope):
    if rope:
        x_ref, mod_ref, win_ref, wg_ref, bg_ref, cos_ref, sa_ref, sb_ref, z_ref, g_ref = refs
    else:
        x_ref, mod_ref, win_ref, wg_ref, bg_ref, z_ref, g_ref = refs
    sh1 = mod_ref[:, 0:D_MODEL]
    sc1 = mod_ref[:, D_MODEL:2 * D_MODEL]
    h = (x_ref[...] * (1.0 + sc1) + sh1).astype(BF16)
    nc = 768
    for c0 in range(0, D_IN, nc):
        z = _dot(h, win_ref[:, c0:c0 + nc])
        for j in range(nc // LANE):
            blk = c0 // LANE + j
            u = z[:, j * LANE:(j + 1) * LANE]
            if rope and blk in ROPE_BLOCKS:
                u = (u * cos_ref[...] + pltpu.roll(u, LANE - 16, 1) * sa_ref[...]
                     + pltpu.roll(u, 16, 1) * sb_ref[...])
            z_ref[:, blk * LANE:(blk + 1) * LANE] = u
    for c0 in range(0, D_GATE, nc):
        g = _dot(h, wg_ref[:, c0:c0 + nc]) + bg_ref[:, c0:c0 + nc]
        g_ref[:, c0:c0 + nc] = jax.nn.sigmoid(g)


def _proj_call(x2d, mod4, layer, w_in_b, w_gate_b, b_gate_l, rope_tabs, tokens_per_seq, latent, tm=256):
    m = x2d.shape[0]
    tiles_per_seq = tokens_per_seq // tm

    def mod_idx(i):
        row = (1 + i // tiles_per_seq) if latent else 0
        return (layer, row, 0, 0)

    in_specs = [
        pl.BlockSpec((tm, D_MODEL), lambda i: (i, 0)),
        pl.BlockSpec((None, None, 1, D_MOD), mod_idx),
        _resident((D_MODEL, D_IN), lambda i: (0, 0)),
        _resident((D_MODEL, D_GATE), lambda i: (0, 0)),
        _resident((1, D_GATE), lambda i: (0, 0)),
    ]
    args = [x2d, mod4, w_in_b, w_gate_b, b_gate_l]
    if latent:
        for tab in rope_tabs:
            in_specs.append(pl.BlockSpec((tm, LANE), lambda i: (i % tiles_per_seq, 0)))
            args.append(tab)
    return pl.pallas_call(
        functools.partial(_proj_kernel, rope=latent),
        grid=(m // tm,),
        in_specs=in_specs,
        out_specs=[pl.BlockSpec((tm, D_IN), lambda i: (i, 0)),
                   pl.BlockSpec((tm, D_GATE), lambda i: (i, 0))],
        out_shape=[jax.ShapeDtypeStruct((m, D_IN), F32),
                   jax.ShapeDtypeStruct((m, D_GATE), F32)],
        compiler_params=_params(),
        name="proj_lat" if latent else "proj_ctx",
    )(*args)


def _joint_softmax(blocks):
    m = blocks[0].max(axis=-1, keepdims=True)
    for s in blocks[1:]:
        m = jnp.maximum(m, s.max(axis=-1, keepdims=True))
    es = [jnp.exp(s - m) for s in blocks]
    d = es[0].sum(axis=-1, keepdims=True)
    for e in es[1:]:
        d = d + e.sum(axis=-1, keepdims=True)
    return es, d


def _diff_kernel(*refs, has_ctx, lam_init):
    if has_ctx:
        q_ref, k_ref, v_ref, kc_ref, vc_ref, lam_ref, g_ref, o_ref = refs
    else:
        q_ref, k_ref, v_ref, lam_ref, g_ref, o_ref = refs
    q = q_ref[...] * QK_SCALE
    lane = lax.broadcasted_iota(jnp.int32, q.shape, 1)
    q1 = jnp.where(lane < HEAD_DIM, q, 0.0).astype(BF16)
    q2 = jnp.where(lane >= HEAD_DIM, q, 0.0).astype(BF16)
    ks = [k_ref[...].astype(BF16)]
    vs = [v_ref[...].astype(BF16)]
    if has_ctx:
        ks.append(kc_ref[...].astype(BF16))
        vs.append(vc_ref[...].astype(BF16))
    s1 = [lax.dot_general(q1, k, NT_DIMS, preferred_element_type=F32) for k in ks]
    s2 = [lax.dot_general(q2, k, NT_DIMS, preferred_element_type=F32) for k in ks]
    lv = lam_ref[...]
    lam = (jnp.exp(jnp.sum(lv[0:1] * lv[1:2], axis=-1, keepdims=True))
           - jnp.exp(jnp.sum(lv[2:3] * lv[3:4], axis=-1, keepdims=True)) + lam_init)
    e1, d1 = _joint_softmax(s1)
    e2, d2 = _joint_softmax(s2)
    r1 = 1.0 / d1
    r2 = lam / d2
    o = None
    for a1, a2, v in zip(e1, e2, vs):
        part = _dot((a1 * r1 - a2 * r2).astype(BF16), v)
        o = part if o is None else o + part
    y = o * lax.rsqrt(jnp.mean(o * o, axis=-1, keepdims=True) + LN_EPS)
    o_ref[...] = y * g_ref[...] * (1.0 - lam_init)


def _diff_call(z3, cache_k4, cache_v4, layer, diff_lam_l, diff_norm_g_l, lam_init, latent):
    b, t, _ = z3.shape
    tq = 256
    has_ctx = latent
    in_specs = [
        pl.BlockSpec((None, tq, LANE), lambda bi, h, qi: (bi, qi, COL_AQ + h)),
        pl.BlockSpec((None, t, LANE), lambda bi, h, qi: (bi, 0, COL_AK + h)),
        pl.BlockSpec((None, t, LANE), lambda bi, h, qi: (bi, 0, COL_AV + h)),
    ]
    args = [z3, z3, z3]
    if has_ctx:
        past = cache_k4.shape[2]
        in_specs += [
            pl.BlockSpec((None, None, past, LANE), lambda bi, h, qi: (bi, layer, 0, h)),
            pl.BlockSpec((None, None, past, LANE), lambda bi, h, qi: (bi, layer, 0, h)),
        ]
        args += [cache_k4, cache_v4]
    in_specs += [
        pl.BlockSpec((4, HEAD_DIM), lambda bi, h, qi: (0, 0)),
        pl.BlockSpec((1, LANE), lambda bi, h, qi: (0, 0)),
    ]
    args += [diff_lam_l, diff_norm_g_l]
    return pl.pallas_call(
        functools.partial(_diff_kernel, has_ctx=has_ctx, lam_init=lam_init),
        grid=(b, DA_HEADS, t // tq),
        in_specs=in_specs,
        out_specs=pl.BlockSpec((None, tq, LANE), lambda bi, h, qi: (bi, qi, h)),
        out_shape=jax.ShapeDtypeStruct((b, t, BRANCH_W), F32),
        compiler_params=_params(),
        name="diff_lat" if latent else "diff_ctx",
    )(*args)


def _win_kernel(*refs, latent, t):
    if latent:
        q_ref, k_ref, v_ref, kc_ref, vc_ref, sink_ref, o_ref = refs
    else:
        q_ref, k_ref, v_ref, sink_ref, o_ref = refs
    tq = q_ref.shape[0]
    if latent:
        n = pl.program_id(1)
        span = 3 * WINDOW
        start = pl.multiple_of(jnp.clip((n - 1) * WINDOW, 0, t - span), WINDOW)
        k_loc = k_ref[pl.ds(start, span), :].astype(BF16)
        v_loc = v_ref[pl.ds(start, span), :].astype(BF16)
        qpos = n * tq + lax.broadcasted_iota(jnp.int32, (tq, span), 0)
        kpos = start + lax.broadcasted_iota(jnp.int32, (tq, span), 1)
        valid = jnp.abs(kpos - qpos) <= WINDOW
        kcs = kc_ref[...].astype(BF16)
        vcs = vc_ref[...].astype(BF16)
    else:
        k_loc = k_ref[...].astype(BF16)
        v_loc = v_ref[...].astype(BF16)
    outs = []
    for kv in range(WG_KV_HEADS):
        lo = kv * HEAD_DIM
        k_h = k_loc[:, lo:lo + HEAD_DIM]
        v_h = v_loc[:, lo:lo + HEAD_DIM]
        for g in range(WG_GROUP):
            h = kv * WG_GROUP + g
            q_h = (q_ref[:, h * HEAD_DIM:(h + 1) * HEAD_DIM] * QK_SCALE).astype(BF16)
            sk = sink_ref[:, h:h + 1]
            s = lax.dot_general(q_h, k_h, NT_DIMS, preferred_element_type=F32)
            if latent:
                sc = lax.dot_general(q_h, kcs[:, lo:lo + HEAD_DIM], NT_DIMS, preferred_element_type=F32)
                s = jnp.where(valid, s, -1e30)
                m = jnp.maximum(jnp.maximum(s.max(axis=-1, keepdims=True),
                                            sc.max(axis=-1, keepdims=True)), sk)
                e = jnp.exp(s - m)
                ec = jnp.exp(sc - m)
                d = e.sum(axis=-1, keepdims=True) + ec.sum(axis=-1, keepdims=True) + jnp.exp(sk - m)
                o = _dot(e.astype(BF16), v_h) + _dot(ec.astype(BF16), vcs[:, lo:lo + HEAD_DIM])
            else:
                m = jnp.maximum(s.max(axis=-1, keepdims=True), sk)
                e = jnp.exp(s - m)
                d = e.sum(axis=-1, keepdims=True) + jnp.exp(sk - m)
                o = _dot(e.astype(BF16), v_h)
            outs.append(o * (1.0 / d))
    o_ref[...] = jnp.concatenate(outs, axis=-1)


def _win_call(z3, cache_k4, cache_v4, layer, sink_l, latent):
    b, t, _ = z3.shape
    tq = WINDOW if latent else t
    in_specs = [
        pl.BlockSpec((None, tq, BRANCH_W), lambda bi, n: (bi, n, COL_BQ // 4)),
        pl.BlockSpec((None, t, LANE), lambda bi, n: (bi, 0, COL_BK)),
        pl.BlockSpec((None, t, LANE), lambda bi, n: (bi, 0, COL_BV)),
    ]
    args = [z3, z3, z3]
    if latent:
        past = cache_k4.shape[2]
        in_specs += [
            pl.BlockSpec((None, None, past, LANE), lambda bi, n: (bi, layer, 0, 0)),
            pl.BlockSpec((None, None, past, LANE), lambda bi, n: (bi, layer, 0, 0)),
        ]
        args += [cache_k4, cache_v4]
    in_specs.append(pl.BlockSpec((1, WG_Q_HEADS), lambda bi, n: (0, 0)))
    args.append(sink_l)
    return pl.pallas_call(
        functools.partial(_win_kernel, latent=latent, t=t),
        grid=(b, t // tq),
        in_specs=in_specs,
        out_specs=pl.BlockSpec((None, tq, BRANCH_W), lambda bi, n: (bi, n, 0)),
        out_shape=jax.ShapeDtypeStruct((b, t, BRANCH_W), F32),
        compiler_params=_params(),
        name="win_lat" if latent else "win_ctx",
    )(*args)


def _log_sigmoid(x):
    return jnp.minimum(x, 0.0) - jnp.log1p(jnp.exp(-jnp.abs(x)))


def _ret_kernel(*refs, latent, t):
    if latent:
        q_ref, k_ref, v_ref, cg_ref, dec_ref, g_ref, s0_ref, o_ref = refs
    else:
        q_ref, k_ref, v_ref, cg_ref, dec_ref, g_ref, o_ref, sfin_ref = refs
    tq = q_ref.shape[0]
    q0 = pl.program_id(2) * tq
    lg = _log_sigmoid(dec_ref[...])
    tpos = q0 + lax.broadcasted_iota(jnp.int32, (tq, t), 0)
    spos = lax.broadcasted_iota(jnp.int32, (tq, t), 1)
    diff = (tpos - spos).astype(F32)
    outs = []
    for j in range(2):
        lgf = lg[0:1, j:j + 1]
        lgb = lg[1:2, j:j + 1]
        dmat = (jnp.where(diff >= 0, jnp.exp(jnp.maximum(diff, 0.0) * lgf), 0.0)
                + jnp.where(diff <= 0, jnp.exp(jnp.maximum(-diff, 0.0) * lgb), 0.0))
        q = q_ref[:, j * RT_DK:(j + 1) * RT_DK].astype(BF16)
        kf = k_ref[:, j * RT_DK:(j + 1) * RT_DK] * (RT_DK ** -0.5)
        k = kf.astype(BF16)
        v = v_ref[:, j * RT_DV:(j + 1) * RT_DV].astype(BF16)
        s = lax.dot_general(q, k, NT_DIMS, preferred_element_type=F32)
        y = _dot((s * dmat).astype(BF16), v)
        if latent:
            tq_pos = (q0 + lax.broadcasted_iota(jnp.int32, (tq, 1), 0)).astype(F32)
            cf = jnp.exp((tq_pos + 1.0) * lgf)
            cb = jnp.exp((float(t) - tq_pos) * lgb)
            y = (y + _dot(q, s0_ref[0, j].astype(BF16)) * cf
                 + _dot(q, s0_ref[1, j].astype(BF16)) * cb)
        else:
            sp = lax.broadcasted_iota(jnp.int32, (t, 1), 0).astype(F32)
            zf = jnp.exp((float(t) - 1.0 - sp) * lgf)
            zb = jnp.exp(sp * lgb)
            sfin_ref[0, j] = lax.dot_general((kf * zf).astype(BF16), v, TN_DIMS,
                                             preferred_element_type=F32)
            sfin_ref[1, j] = lax.dot_general((kf * zb).astype(BF16), v, TN_DIMS,
                                             preferred_element_type=F32)
        mu = jnp.mean(y, axis=-1, keepdims=True)
        yc = y - mu
        var = jnp.mean(yc * yc, axis=-1, keepdims=True)
        yn = yc * lax.rsqrt(var + LN_EPS) * g_ref[...]
        cg = cg_ref[:, j * RT_DV:(j + 1) * RT_DV]
        outs.append(yn * (cg * jax.nn.sigmoid(cg)))
    o_ref[...] = jnp.concatenate(outs, axis=-1)


def _ret_call(z3, state6, layer, decay_l, ret_norm_g_l, latent):
    b, t, _ = z3.shape
    tq = 256
    hp_n = RT_HEADS // 2
    dec3 = decay_l.reshape(2, hp_n, 2).transpose(1, 0, 2)
    in_specs = [
        pl.BlockSpec((None, tq, LANE), lambda bi, hp, qi: (bi, qi, COL_CQ + hp)),
        pl.BlockSpec((None, t, LANE), lambda bi, hp, qi: (bi, 0, COL_CK + hp)),
        pl.BlockSpec((None, t, 2 * LANE), lambda bi, hp, qi: (bi, 0, COL_CV // 2 + hp)),
        pl.BlockSpec((None, tq, 2 * LANE), lambda bi, hp, qi: (bi, qi, COL_CG // 2 + hp)),
        pl.BlockSpec((None, 2, 2), lambda bi, hp, qi: (hp, 0, 0)),
        pl.BlockSpec((1, RT_DV), lambda bi, hp, qi: (0, 0)),
    ]
    args = [z3, z3, z3, z3, dec3, ret_norm_g_l]
    o_spec = pl.BlockSpec((None, tq, 2 * LANE), lambda bi, hp, qi: (bi, qi, hp))
    o_shape = jax.ShapeDtypeStruct((b, t, BRANCH_W), F32)
    if latent:
        in_specs.append(pl.BlockSpec((None, None, 2, 2, RT_DK, RT_DV),
                                     lambda bi, hp, qi: (bi, layer, 0, hp, 0, 0)))
        args.append(state6)
        out_specs, out_shape = o_spec, o_shape
    else:
        out_specs = [o_spec, pl.BlockSpec((None, 2, 2, RT_DK, RT_DV),
                                          lambda bi, hp, qi: (bi, 0, hp, 0, 0))]
        out_shape = [o_shape, jax.ShapeDtypeStruct((b, 2, RT_HEADS, RT_DK, RT_DV), F32)]
    return pl.pallas_call(
        functools.partial(_ret_kernel, latent=latent, t=t),
        grid=(b, hp_n, t // tq),
        in_specs=in_specs,
        out_specs=out_specs,
        out_shape=out_shape,
        compiler_params=_params(),
        name="ret_lat" if latent else "ret_ctx",
    )(*args)


def _layer_norm(x, g, b):
    mu = jnp.mean(x, axis=-1, keepdims=True)
    xc = x - mu
    var = jnp.mean(xc * xc, axis=-1, keepdims=True)
    return xc * lax.rsqrt(var + LN_EPS) * g + b


def _post_kernel(x_ref, mod_ref, oa_ref, ob_ref, oc_ref, gate_ref, wpa_ref, wpb_ref, wpc_ref,
                 wo_ref, lng_ref, lnb_ref, o_ref):
    d = D_MODEL
    merged = (gate_ref[:, 0:d] * _dot(oa_ref[...].astype(BF16), wpa_ref[...])
              + gate_ref[:, d:2 * d] * _dot(ob_ref[...].astype(BF16), wpb_ref[...])
              + gate_ref[:, 2 * d:3 * d] * _dot(oc_ref[...].astype(BF16), wpc_ref[...]))
    y = _dot(merged.astype(BF16), wo_ref[...])
    g1 = mod_ref[:, 2 * d:3 * d]
    o_ref[...] = _layer_norm(ALPHA * x_ref[...] + g1 * y, lng_ref[...], lnb_ref[...])


def _post_call(x2d, mod4, layer, oa, ob, oc, gates, wpa, wpb, wpc, wo, ln_g, ln_b,
               tokens_per_seq, latent, tm=256):
    m = x2d.shape[0]
    tiles_per_seq = tokens_per_seq // tm

    def mod_idx(i):
        row = (1 + i // tiles_per_seq) if latent else 0
        return (layer, row, 0, 0)

    row_spec = lambda w: pl.BlockSpec((tm, w), lambda i: (i, 0))
    return pl.pallas_call(
        _post_kernel,
        grid=(m // tm,),
        in_specs=[
            row_spec(D_MODEL),
            pl.BlockSpec((None, None, 1, D_MOD), mod_idx),
            row_spec(BRANCH_W), row_spec(BRANCH_W), row_spec(BRANCH_W),
            row_spec(D_GATE),
            _resident((BRANCH_W, D_MODEL), lambda i: (0, 0)),
            _resident((BRANCH_W, D_MODEL), lambda i: (0, 0)),
            _resident((BRANCH_W, D_MODEL), lambda i: (0, 0)),
            _resident((D_MODEL, D_MODEL), lambda i: (0, 0)),
            _resident((1, D_MODEL), lambda i: (0, 0)),
            _resident((1, D_MODEL), lambda i: (0, 0)),
        ],
        out_specs=row_spec(D_MODEL),
        out_shape=jax.ShapeDtypeStruct((m, D_MODEL), F32),
        compiler_params=_params(),
        name="post_lat" if latent else "post_ctx",
    )(x2d, mod4, oa, ob, oc, gates, wpa, wpb, wpc, wo, ln_g, ln_b)


def _ffn_kernel(x_ref, mod_ref, w1_ref, w2_ref, lng_ref, lnb_ref, o_ref):
    d = D_MODEL
    x = x_ref[...]
    sh2 = mod_ref[:, 3 * d:4 * d]
    sc2 = mod_ref[:, 4 * d:5 * d]
    g2 = mod_ref[:, 5 * d:6 * d]
    h = (x * (1.0 + sc2) + sh2).astype(BF16)
    u = jnp.maximum(_dot(h, w1_ref[...]), 0.0)
    f = _dot((u * u).astype(BF16), w2_ref[...])
    o_ref[...] = _layer_norm(ALPHA * x + g2 * f, lng_ref[...], lnb_ref[...])


def _ffn_call(x2d, mod4, layer, w1, w2, ln_g, ln_b, tokens_per_seq, latent, tm=256):
    m = x2d.shape[0]
    tiles_per_seq = tokens_per_seq // tm

    def mod_idx(i):
        row = (1 + i // tiles_per_seq) if latent else 0
        return (layer, row, 0, 0)

    return pl.pallas_call(
        _ffn_kernel,
        grid=(m // tm,),
        in_specs=[
            pl.BlockSpec((tm, D_MODEL), lambda i: (i, 0)),
            pl.BlockSpec((None, None, 1, D_MOD), mod_idx),
            _resident((D_MODEL, D_FF), lambda i: (0, 0)),
            _resident((D_FF, D_MODEL), lambda i: (0, 0)),
            _resident((1, D_MODEL), lambda i: (0, 0)),
            _resident((1, D_MODEL), lambda i: (0, 0)),
        ],
        out_specs=pl.BlockSpec((tm, D_MODEL), lambda i: (i, 0)),
        out_shape=jax.ShapeDtypeStruct((m, D_MODEL), F32),
        compiler_params=_params(),
        name="ffn_lat" if latent else "ffn_ctx",
    )(x2d, mod4, w1, w2, ln_g, ln_b)


def _rope_tables(n_tokens):
    rows = n_tokens // GRID_W
    r, col = jnp.meshgrid(jnp.arange(rows), jnp.arange(GRID_W), indexing="ij")
    r = r.reshape(-1).astype(F32)
    col = col.reshape(-1).astype(F32)
    nf = HEAD_DIM // 4
    inv = ROPE_BASE ** (-jnp.arange(nf, dtype=F32) / nf)
    ang_r = r[:, None] * inv[None, :]
    ang_c = col[:, None] * inv[None, :]
    zero = jnp.zeros_like(ang_r)
    cos = jnp.concatenate([jnp.cos(ang_r)] * 2 + [jnp.cos(ang_c)] * 2, axis=-1)
    s_next = jnp.concatenate([-jnp.sin(ang_r), zero, -jnp.sin(ang_c), zero], axis=-1)
    s_prev = jnp.concatenate([zero, jnp.sin(ang_r), zero, jnp.sin(ang_c)], axis=-1)
    return tuple(jnp.tile(t, (1, LANE // HEAD_DIM)) for t in (cos, s_next, s_prev))


def kernel(x_prompt, x_sample, c, cache_diff_k, cache_diff_v, cache_win_k, cache_win_v, state_ret,
           c_ctx, w_mod, b_mod, w_in, diff_lam, diff_norm_g, win_sink, ret_decay, ret_norm_g,
           w_pa, w_pb, w_pc, w_gate, b_gate, w_o, ln1_g, ln1_b, w_ff1, w_ff2, ln2_g, ln2_b):
    bp, tp, d = x_prompt.shape
    bs, ts, _ = x_sample.shape
    assert d == D_MODEL and w_in.shape == (DEPTH, D_MODEL, D_IN) and c.shape[0] + 1 <= MOD_ROWS
    past = cache_diff_k.shape[2]

    c_rows = jnp.concatenate(
        [c_ctx[None, :], c, jnp.zeros((MOD_ROWS - 1 - bs, d), F32)], axis=0)
    mod4 = _mod_call(c_rows, w_mod, b_mod).reshape(DEPTH, MOD_ROWS, 1, D_MOD)
    rope_tabs = _rope_tables(ts)

    ck_a = cache_diff_k.reshape(bs, DEPTH, past, DA_HEADS * 2 * HEAD_DIM)
    cv_a = cache_diff_v.reshape(bs, DEPTH, past, DA_HEADS * 2 * HEAD_DIM)
    ck_b = cache_win_k.reshape(bs, DEPTH, past, WG_KV_HEADS * HEAD_DIM)
    cv_b = cache_win_v.reshape(bs, DEPTH, past, WG_KV_HEADS * HEAD_DIM)

    xp = x_prompt.reshape(bp * tp, d)
    xs = x_sample.reshape(bs * ts, d)
    diff_k, diff_v, win_k, win_v, ret_s = [], [], [], [], []
    for l in range(DEPTH):
        lam_init = 0.8 - 0.6 * math.exp(-0.3 * l)
        w_in_b = w_in[l].astype(BF16)
        w_gate_b = w_gate[l].astype(BF16)
        wpa, wpb, wpc = w_pa[l].astype(BF16), w_pb[l].astype(BF16), w_pc[l].astype(BF16)
        wo = w_o[l].astype(BF16)
        w1, w2 = w_ff1[l].astype(BF16), w_ff2[l].astype(BF16)
        b_gate_l = b_gate[l].reshape(1, D_GATE)
        lam_l = diff_lam[l]
        dng_l = diff_norm_g[l].reshape(1, LANE)
        sink_l = win_sink[l].reshape(1, WG_Q_HEADS)
        rng_l = ret_norm_g[l].reshape(1, RT_DV)
        ln1 = (ln1_g[l].reshape(1, d), ln1_b[l].reshape(1, d))
        ln2 = (ln2_g[l].reshape(1, d), ln2_b[l].reshape(1, d))

        for latent in (False, True):
            x2d = xs if latent else xp
            b, t = (bs, ts) if latent else (bp, tp)
            z, gates = _proj_call(x2d, mod4, l, w_in_b, w_gate_b, b_gate_l, rope_tabs, t, latent)
            z3 = z.reshape(b, t, D_IN)
            oa = _diff_call(z3, ck_a, cv_a, l, lam_l, dng_l, lam_init, latent)
            ob = _win_call(z3, ck_b, cv_b, l, sink_l, latent)
            if latent:
                oc = _ret_call(z3, state_ret, l, ret_decay[l], rng_l, latent)
            else:
                oc, s_fin = _ret_call(z3, state_ret, l, ret_decay[l], rng_l, latent)
                diff_k.append(z3[:, :, COL_AK * LANE:COL_AV * LANE].reshape(b, t, DA_HEADS, 2 * HEAD_DIM))
                diff_v.append(z3[:, :, COL_AV * LANE:COL_BQ * LANE].reshape(b, t, DA_HEADS, 2 * HEAD_DIM))
                win_k.append(z3[:, :, COL_BK * LANE:COL_BV * LANE].reshape(b, t, WG_KV_HEADS, HEAD_DIM))
                win_v.append(z3[:, :, COL_BV * LANE:COL_CQ * LANE].reshape(b, t, WG_KV_HEADS, HEAD_DIM))
                ret_s.append(s_fin)
            x1 = _post_call(x2d, mod4, l, oa.reshape(b * t, BRANCH_W), ob.reshape(b * t, BRANCH_W),
                            oc.reshape(b * t, BRANCH_W), gates, wpa, wpb, wpc, wo, *ln1, t, latent)
            x2 = _ffn_call(x1, mod4, l, w1, w2, *ln2, t, latent)
            if latent:
                xs = x2
            else:
                xp = x2
    return (xp.reshape(bp, tp, d), xs.reshape(bs, ts, d),
            jnp.stack(diff_k, axis=1), jnp.stack(diff_v, axis=1),
            jnp.stack(win_k, axis=1), jnp.stack(win_v, axis=1), jnp.stack(ret_s, axis=1))
```

```python
import functools
import math

import jax
import jax.numpy as jnp
from jax import lax
from jax.experimental import pallas as pl
from jax.experimental.pallas import tpu as pltpu

F32 = jnp.float32
BF16 = jnp.bfloat16

D_MODEL = 1024
DEPTH = 2
GRID_W = 64
HEAD_DIM = 64
DA_HEADS = 4
WG_Q_HEADS = 8
WG_KV_HEADS = 2
WG_GROUP = WG_Q_HEADS // WG_KV_HEADS
WINDOW = 128
RT_HEADS = 4
RT_DK = 64
RT_DV = 128
BRANCH_W = 512
D_IN = 3840
D_GATE = 3 * D_MODEL
D_FF = 4 * D_MODEL
D_MOD = 6 * D_MODEL
ROPE_BASE = 10000.0
LN_EPS = 1e-5
ALPHA = (2 * DEPTH) ** 0.25
QK_SCALE = HEAD_DIM ** -0.5
LOG2E = math.log2(math.e)

LANE = 128
MOD_ROWS = 8

COL_AQ, COL_AK, COL_AV = 0, 4, 8
COL_BQ, COL_BK, COL_BV = 12, 16, 17
COL_CQ, COL_CK, COL_CV, COL_CG = 18, 20, 22, 26
ROPE_BLOCKS = tuple(range(0, 8)) + tuple(range(12, 17))

VMEM_LIMIT = 56 * 1024 * 1024
NT_DIMS = (((1,), (1,)), ((), ()))
TN_DIMS = (((0,), (0,)), ((), ()))


def _params():
    return pltpu.CompilerParams(vmem_limit_bytes=VMEM_LIMIT)


def _resident(shape, index_map):
    return pl.BlockSpec(shape, index_map, pipeline_mode=pl.Buffered(1))


def _dot(a, b):
    return jnp.dot(a, b, preferred_element_type=F32)


def _mod_kernel(c_ref, w_ref, b_ref, o_ref):
    c = c_ref[...]
    a = (c * jax.nn.sigmoid(c)).astype(BF16)
    o_ref[...] = _dot(a, w_ref[...].astype(BF16)) + b_ref[...]


def _mod_call(c_rows, w_mod, b_mod):
    tn = 1536
    return pl.pallas_call(
        _mod_kernel,
        grid=(DEPTH, D_MOD // tn),
        in_specs=[
            pl.BlockSpec((MOD_ROWS, D_MODEL), lambda l, n: (0, 0)),
            pl.BlockSpec((None, D_MODEL, tn), lambda l, n: (l, 0, n)),
            pl.BlockSpec((None, 1, tn), lambda l, n: (l, 0, n)),
        ],
        out_specs=pl.BlockSpec((None, MOD_ROWS, tn), lambda l, n: (l, 0, n)),
        out_shape=jax.ShapeDtypeStruct((DEPTH, MOD_ROWS, D_MOD), F32),
        compiler_params=_params(),
        name="mod_vectors",
    )(c_rows, w_mod, b_mod.reshape(DEPTH, 1, D_MOD))


def _proj_kernel(*refs, rope):
    if rope:
        x_ref, mod_ref, win_ref, wg_ref, bg_ref, cos_ref, sa_ref, sb_ref, z_ref, g_ref = refs
    else:
        x_ref, mod_ref, win_ref, wg_ref, bg_ref, z_ref, g_ref = refs
    sh1 = mod_ref[:, 0:D_MODEL]
    sc1 = mod_ref[:, D_MODEL:2 * D_MODEL]
    h = (x_ref[...] * (1.0 + sc1) + sh1).astype(BF16)
    nc = 768
    for c0 in range(0, D_IN, nc):
        z = _dot(h, win_ref[:, c0:c0 + nc])
        for j in range(nc // LANE):
            blk = c0 // LANE + j
            u = z[:, j * LANE:(j + 1) * LANE]
            if rope and blk in ROPE_BLOCKS:
                u = (u * cos_ref[...] + pltpu.roll(u, LANE - 16, 1) * sa_ref[...]
                     + pltpu.roll(u, 16, 1) * sb_ref[...])
            z_ref[:, blk * LANE:(blk + 1) * LANE] = u
    for c0 in range(0, D_GATE, nc):
        g = _dot(h, wg_ref[:, c0:c0 + nc]) + bg_ref[:, c0:c0 + nc]
        g_ref[:, c0:c0 + nc] = jax.nn.sigmoid(g)


def _proj_call(x2d, mod4, layer, w_in_b, w_gate_b, b_gate_l, rope_tabs, tokens_per_seq, latent, tm=256):
    m = x2d.shape[0]
    tiles_per_seq = tokens_per_seq // tm

    def mod_idx(i):
        row = (1 + i // tiles_per_seq) if latent else 0
        return (layer, row, 0, 0)

    in_specs = [
        pl.BlockSpec((tm, D_MODEL), lambda i: (i, 0)),
        pl.BlockSpec((None, None, 1, D_MOD), mod_idx),
        _resident((D_MODEL, D_IN), lambda i: (0, 0)),
        _resident((D_MODEL, D_GATE), lambda i: (0, 0)),
        _resident((1, D_GATE), lambda i: (0, 0)),
    ]
    args = [x2d, mod4, w_in_b, w_gate_b, b_gate_l]
    if latent:
        for tab in rope_tabs:
            in_specs.append(pl.BlockSpec((tm, LANE), lambda i: (i % tiles_per_seq, 0)))
            args.append(tab)
    return pl.pallas_call(
        functools.partial(_proj_kernel, rope=latent),
        grid=(m // tm,),
        in_specs=in_specs,
        out_specs=[pl.BlockSpec((tm, D_IN), lambda i: (i, 0)),
                   pl.BlockSpec((tm, D_GATE), lambda i: (i, 0))],
        out_shape=[jax.ShapeDtypeStruct((m, D_IN), F32),
                   jax.ShapeDtypeStruct((m, D_GATE), F32)],
        compiler_params=_params(),
        name="proj_lat" if latent else "proj_ctx",
    )(*args)


def _diff_kernel(*refs, has_ctx, lam_init):
    if has_ctx:
        q_ref, k_ref, v_ref, kc_ref, vc_ref, lam_ref, g_ref, o_ref, k_scr, v_scr = refs
    else:
        q_ref, k_ref, v_ref, lam_ref, g_ref, o_ref, k_scr, v_scr = refs
    tq = q_ref.shape[0]
    t = k_ref.shape[0]
    nk = k_scr.shape[0]

    @pl.when(pl.program_id(2) == 0)
    def _():
        k_scr[0:t, :] = k_ref[...].astype(BF16)
        v_scr[0:t, 0:LANE] = v_ref[...].astype(BF16)
        if has_ctx:
            k_scr[t:nk, :] = kc_ref[...].astype(BF16)
            v_scr[t:nk, 0:LANE] = vc_ref[...].astype(BF16)
        v_scr[:, LANE:2 * LANE] = jnp.ones((nk, LANE), BF16)

    q = q_ref[...] * (QK_SCALE * LOG2E)
    lane = lax.broadcasted_iota(jnp.int32, q.shape, 1)
    qs = jnp.concatenate([jnp.where(lane < HEAD_DIM, q, 0.0),
                          jnp.where(lane >= HEAD_DIM, q, 0.0)], axis=0).astype(BF16)
    s = lax.dot_general(qs, k_scr[...], NT_DIMS, preferred_element_type=F32)
    e = jnp.exp2(s - s.max(axis=-1, keepdims=True)).astype(BF16)
    ov = _dot(e, v_scr[...])
    o = ov[:, 0:LANE] * (1.0 / ov[:, LANE:LANE + 1])
    lv = lam_ref[...]
    lam = (jnp.exp(jnp.sum(lv[0:1] * lv[1:2], axis=-1, keepdims=True))
           - jnp.exp(jnp.sum(lv[2:3] * lv[3:4], axis=-1, keepdims=True)) + lam_init)
    od = o[0:tq] - lam * o[tq:2 * tq]
    y = od * lax.rsqrt(jnp.mean(od * od, axis=-1, keepdims=True) + LN_EPS)
    o_ref[...] = y * g_ref[...] * (1.0 - lam_init)


def _diff_call(z3, cache_k4, cache_v4, layer, diff_lam_l, diff_norm_g_l, lam_init, latent):
    b, t, _ = z3.shape
    tq = 256
    has_ctx = latent
    in_specs = [
        pl.BlockSpec((None, tq, LANE), lambda bi, h, qi: (bi, qi, COL_AQ + h)),
        pl.BlockSpec((None, t, LANE), lambda bi, h, qi: (bi, 0, COL_AK + h)),
        pl.BlockSpec((None, t, LANE), lambda bi, h, qi: (bi, 0, COL_AV + h)),
    ]
    args = [z3, z3, z3]
    if has_ctx:
        past = cache_k4.shape[2]
        in_specs += [
            pl.BlockSpec((None, None, past, LANE), lambda bi, h, qi: (bi, layer, 0, h)),
            pl.BlockSpec((None, None, past, LANE), lambda bi, h, qi: (bi, layer, 0, h)),
        ]
        args += [cache_k4, cache_v4]
    in_specs += [
        pl.BlockSpec((4, HEAD_DIM), lambda bi, h, qi: (0, 0)),
        pl.BlockSpec((1, LANE), lambda bi, h, qi: (0, 0)),
    ]
    args += [diff_lam_l, diff_norm_g_l]
    nk = t + (cache_k4.shape[2] if has_ctx else 0)
    return pl.pallas_call(
        functools.partial(_diff_kernel, has_ctx=has_ctx, lam_init=lam_init),
        grid=(b, DA_HEADS, t // tq),
        in_specs=in_specs,
        out_specs=pl.BlockSpec((None, tq, LANE), lambda bi, h, qi: (bi, qi, h)),
        out_shape=jax.ShapeDtypeStruct((b, t, BRANCH_W), F32),
        scratch_shapes=[pltpu.VMEM((nk, LANE), BF16), pltpu.VMEM((nk, 2 * LANE), BF16)],
        compiler_params=_params(),
        name="diff_lat" if latent else "diff_ctx",
    )(*args)


def _win_kernel(*refs, latent, t):
    if latent:
        q_ref, k_ref, v_ref, kc_ref, vc_ref, sink_ref, o_ref = refs
    else:
        q_ref, k_ref, v_ref, sink_ref, o_ref = refs
    tq = q_ref.shape[0]
    if latent:
        n = pl.program_id(1)
        span = 3 * WINDOW
        start = pl.multiple_of(jnp.clip((n - 1) * WINDOW, 0, t - span), WINDOW)
        k_all = jnp.concatenate([k_ref[pl.ds(start, span), :], kc_ref[...]], axis=0).astype(BF16)
        v_all = jnp.concatenate([v_ref[pl.ds(start, span), :], vc_ref[...]], axis=0)
        qpos = n * tq + lax.broadcasted_iota(jnp.int32, (tq, span), 0)
        kpos = start + lax.broadcasted_iota(jnp.int32, (tq, span), 1)
        bias = jnp.where(jnp.abs(kpos - qpos) <= WINDOW, 0.0, -1e30)
        bias = jnp.concatenate([bias] * WG_GROUP, axis=0)
    else:
        span = 0
        k_all = k_ref[...].astype(BF16)
        v_all = v_ref[...]
    vlane = lax.broadcasted_iota(jnp.int32, v_all.shape, 1)
    pieces = []
    for kv in range(WG_KV_HEADS):
        lo = kv * HEAD_DIM
        heads = range(kv * WG_GROUP, (kv + 1) * WG_GROUP)
        q_g = (jnp.concatenate([q_ref[:, h * HEAD_DIM:(h + 1) * HEAD_DIM] for h in heads], axis=0)
               * (QK_SCALE * LOG2E)).astype(BF16)
        sk = jnp.concatenate([jnp.broadcast_to(sink_ref[:, h:h + 1] * LOG2E, (tq, 1)) for h in heads],
                             axis=0)
        s = lax.dot_general(q_g, k_all[:, lo:lo + HEAD_DIM], NT_DIMS, preferred_element_type=F32)
        if latent:
            s_loc = s[:, 0:span] + bias
            s_ctx = s[:, span:]
            m = jnp.maximum(jnp.maximum(s_loc.max(axis=-1, keepdims=True),
                                        s_ctx.max(axis=-1, keepdims=True)), sk)
            e = jnp.concatenate([jnp.exp2(s_loc - m), jnp.exp2(s_ctx - m)], axis=-1).astype(BF16)
        else:
            m = jnp.maximum(s.max(axis=-1, keepdims=True), sk)
            e = jnp.exp2(s - m).astype(BF16)
        mine = (vlane >= lo) & (vlane < lo + HEAD_DIM)
        ov = _dot(e, jnp.where(mine, v_all, 1.0).astype(BF16))
        ones_lane = HEAD_DIM - lo
        d = ov[:, ones_lane:ones_lane + 1] + jnp.exp2(sk - m)
        on = ov * (1.0 / d)
        for g in range(WG_GROUP):
            pieces.append(on[g * tq:(g + 1) * tq, lo:lo + HEAD_DIM])
    o_ref[...] = jnp.concatenate(pieces, axis=-1)


def _win_call(z3, cache_k4, cache_v4, layer, sink_l, latent):
    b, t, _ = z3.shape
    tq = WINDOW if latent else t
    in_specs = [
        pl.BlockSpec((None, tq, BRANCH_W), lambda bi, n: (bi, n, COL_BQ // 4)),
        pl.BlockSpec((None, t, LANE), lambda bi, n: (bi, 0, COL_BK)),
        pl.BlockSpec((None, t, LANE), lambda bi, n: (bi, 0, COL_BV)),
    ]
    args = [z3, z3, z3]
    if latent:
        past = cache_k4.shape[2]
        in_specs += [
            pl.BlockSpec((None, None, past, LANE), lambda bi, n: (bi, layer, 0, 0)),
            pl.BlockSpec((None, None, past, LANE), lambda bi, n: (bi, layer, 0, 0)),
        ]
        args += [cache_k4, cache_v4]
    in_specs.append(pl.BlockSpec((1, WG_Q_HEADS), lambda bi, n: (0, 0)))
    args.append(sink_l)
    return pl.pallas_call(
        functools.partial(_win_kernel, latent=latent, t=t),
        grid=(b, t // tq),
        in_specs=in_specs,
        out_specs=pl.BlockSpec((None, tq, BRANCH_W), lambda bi, n: (bi, n, 0)),
        out_shape=jax.ShapeDtypeStruct((b, t, BRANCH_W), F32),
        compiler_params=_params(),
        name="win_lat" if latent else "win_ctx",
    )(*args)


def _log_sigmoid(x):
    return jnp.minimum(x, 0.0) - jnp.log1p(jnp.exp(-jnp.abs(x)))


def _ret_kernel(*refs, latent, t):
    if latent:
        q_ref, k_ref, v_ref, cg_ref, dec_ref, g_ref, s0_ref, o_ref, d_scr = refs
    else:
        q_ref, k_ref, v_ref, cg_ref, dec_ref, g_ref, o_ref, sfin_ref, d_scr = refs
    tq = q_ref.shape[0]
    q0 = pl.program_id(1) * tq
    lg = _log_sigmoid(dec_ref[...])

    @pl.when(pl.program_id(2) == 0)
    def _():
        tpos = q0 + lax.broadcasted_iota(jnp.int32, (tq, t), 0)
        spos = lax.broadcasted_iota(jnp.int32, (tq, t), 1)
        diff = (tpos - spos).astype(F32)
        for j in range(2):
            d_scr[j] = (jnp.where(diff >= 0, jnp.exp(jnp.maximum(diff, 0.0) * lg[0:1, j:j + 1]), 0.0)
                        + jnp.where(diff <= 0, jnp.exp(jnp.maximum(-diff, 0.0) * lg[1:2, j:j + 1]), 0.0))

    outs = []
    for j in range(2):
        lgf = lg[0:1, j:j + 1]
        lgb = lg[1:2, j:j + 1]
        q = q_ref[:, j * RT_DK:(j + 1) * RT_DK].astype(BF16)
        kf = k_ref[:, j * RT_DK:(j + 1) * RT_DK] * (RT_DK ** -0.5)
        k = kf.astype(BF16)
        v = v_ref[:, j * RT_DV:(j + 1) * RT_DV].astype(BF16)
        s = lax.dot_general(q, k, NT_DIMS, preferred_element_type=F32)
        y = _dot((s * d_scr[j]).astype(BF16), v)
        if latent:
            tq_pos = (q0 + lax.broadcasted_iota(jnp.int32, (tq, 1), 0)).astype(F32)
            cf = jnp.exp((tq_pos + 1.0) * lgf)
            cb = jnp.exp((float(t) - tq_pos) * lgb)
            y = (y + _dot(q, s0_ref[0, j].astype(BF16)) * cf
                 + _dot(q, s0_ref[1, j].astype(BF16)) * cb)
        else:
            sp = lax.broadcasted_iota(jnp.int32, (t, 1), 0).astype(F32)
            zf = jnp.exp((float(t) - 1.0 - sp) * lgf)
            zb = jnp.exp(sp * lgb)
            sfin_ref[0, j] = lax.dot_general((kf * zf).astype(BF16), v, TN_DIMS,
                                             preferred_element_type=F32)
            sfin_ref[1, j] = lax.dot_general((kf * zb).astype(BF16), v, TN_DIMS,
                                             preferred_element_type=F32)
        mu = jnp.mean(y, axis=-1, keepdims=True)
        yc = y - mu
        var = jnp.mean(yc * yc, axis=-1, keepdims=True)
        yn = yc * lax.rsqrt(var + LN_EPS) * g_ref[...]
        cg = cg_ref[:, j * RT_DV:(j + 1) * RT_DV]
        outs.append(yn * (cg * jax.nn.sigmoid(cg)))
    o_ref[...] = jnp.concatenate(outs, axis=-1)


def _ret_call(z3, state6, layer, decay_l, ret_norm_g_l, latent):
    b, t, _ = z3.shape
    tq = 256
    hp_n = RT_HEADS // 2
    dec3 = decay_l.reshape(2, hp_n, 2).transpose(1, 0, 2)
    in_specs = [
        pl.BlockSpec((None, tq, LANE), lambda hp, qi, bi: (bi, qi, COL_CQ + hp)),
        pl.BlockSpec((None, t, LANE), lambda hp, qi, bi: (bi, 0, COL_CK + hp)),
        pl.BlockSpec((None, t, 2 * LANE), lambda hp, qi, bi: (bi, 0, COL_CV // 2 + hp)),
        pl.BlockSpec((None, tq, 2 * LANE), lambda hp, qi, bi: (bi, qi, COL_CG // 2 + hp)),
        pl.BlockSpec((None, 2, 2), lambda hp, qi, bi: (hp, 0, 0)),
        pl.BlockSpec((1, RT_DV), lambda hp, qi, bi: (0, 0)),
    ]
    args = [z3, z3, z3, z3, dec3, ret_norm_g_l]
    o_spec = pl.BlockSpec((None, tq, 2 * LANE), lambda hp, qi, bi: (bi, qi, hp))
    o_shape = jax.ShapeDtypeStruct((b, t, BRANCH_W), F32)
    if latent:
        in_specs.append(pl.BlockSpec((None, None, 2, 2, RT_DK, RT_DV),
                                     lambda hp, qi, bi: (bi, layer, 0, hp, 0, 0)))
        args.append(state6)
        out_specs, out_shape = o_spec, o_shape
    else:
        out_specs = [o_spec, pl.BlockSpec((None, 2, 2, RT_DK, RT_DV),
                                          lambda hp, qi, bi: (bi, 0, hp, 0, 0))]
        out_shape = [o_shape, jax.ShapeDtypeStruct((b, 2, RT_HEADS, RT_DK, RT_DV), F32)]
    return pl.pallas_call(
        functools.partial(_ret_kernel, latent=latent, t=t),
        grid=(hp_n, t // tq, b),
        in_specs=in_specs,
        out_specs=out_specs,
        out_shape=out_shape,
        scratch_shapes=[pltpu.VMEM((2, tq, t), F32)],
        compiler_params=_params(),
        name="ret_lat" if latent else "ret_ctx",
    )(*args)


def _layer_norm(x, g, b):
    mu = jnp.mean(x, axis=-1, keepdims=True)
    xc = x - mu
    var = jnp.mean(xc * xc, axis=-1, keepdims=True)
    return xc * lax.rsqrt(var + LN_EPS) * g + b


def _post_kernel(x_ref, mod_ref, oa_ref, ob_ref, oc_ref, gate_ref, wpa_ref, wpb_ref, wpc_ref,
                 wo_ref, lng_ref, lnb_ref, o_ref):
    d = D_MODEL
    merged = (gate_ref[:, 0:d] * _dot(oa_ref[...].astype(BF16), wpa_ref[...])
              + gate_ref[:, d:2 * d] * _dot(ob_ref[...].astype(BF16), wpb_ref[...])
              + gate_ref[:, 2 * d:3 * d] * _dot(oc_ref[...].astype(BF16), wpc_ref[...]))
    y = _dot(merged.astype(BF16), wo_ref[...])
    g1 = mod_ref[:, 2 * d:3 * d]
    o_ref[...] = _layer_norm(ALPHA * x_ref[...] + g1 * y, lng_ref[...], lnb_ref[...])


def _post_call(x2d, mod4, layer, oa, ob, oc, gates, wpa, wpb, wpc, wo, ln_g, ln_b,
               tokens_per_seq, latent, tm=256):
    m = x2d.shape[0]
    tiles_per_seq = tokens_per_seq // tm

    def mod_idx(i):
        row = (1 + i // tiles_per_seq) if latent else 0
        return (layer, row, 0, 0)

    row_spec = lambda w: pl.BlockSpec((tm, w), lambda i: (i, 0))
    return pl.pallas_call(
        _post_kernel,
        grid=(m // tm,),
        in_specs=[
            row_spec(D_MODEL),
            pl.BlockSpec((None, None, 1, D_MOD), mod_idx),
            row_spec(BRANCH_W), row_spec(BRANCH_W), row_spec(BRANCH_W),
            row_spec(D_GATE),
            _resident((BRANCH_W, D_MODEL), lambda i: (0, 0)),
            _resident((BRANCH_W, D_MODEL), lambda i: (0, 0)),
            _resident((BRANCH_W, D_MODEL), lambda i: (0, 0)),
            _resident((D_MODEL, D_MODEL), lambda i: (0, 0)),
            _resident((1, D_MODEL), lambda i: (0, 0)),
            _resident((1, D_MODEL), lambda i: (0, 0)),
        ],
        out_specs=row_spec(D_MODEL),
        out_shape=jax.ShapeDtypeStruct((m, D_MODEL), F32),
        compiler_params=_params(),
        name="post_lat" if latent else "post_ctx",
    )(x2d, mod4, oa, ob, oc, gates, wpa, wpb, wpc, wo, ln_g, ln_b)


def _ffn_kernel(x_ref, mod_ref, w1_ref, w2_ref, lng_ref, lnb_ref, o_ref):
    d = D_MODEL
    x = x_ref[...]
    sh2 = mod_ref[:, 3 * d:4 * d]
    sc2 = mod_ref[:, 4 * d:5 * d]
    g2 = mod_ref[:, 5 * d:6 * d]
    h = (x * (1.0 + sc2) + sh2).astype(BF16)
    u = jnp.maximum(_dot(h, w1_ref[...]), 0.0)
    f = _dot((u * u).astype(BF16), w2_ref[...])
    o_ref[...] = _layer_norm(ALPHA * x + g2 * f, lng_ref[...], lnb_ref[...])


def _ffn_call(x2d, mod4, layer, w1, w2, ln_g, ln_b, tokens_per_seq, latent, tm=256):
    m = x2d.shape[0]
    tiles_per_seq = tokens_per_seq // tm

    def mod_idx(i):
        row = (1 + i // tiles_per_seq) if latent else 0
        return (layer, row, 0, 0)

    return pl.pallas_call(
        _ffn_kernel,
        grid=(m // tm,),
        in_specs=[
            pl.BlockSpec((tm, D_MODEL), lambda i: (i, 0)),
            pl.BlockSpec((None, None, 1, D_MOD), mod_idx),
            _resident((D_MODEL, D_FF), lambda i: (0, 0)),
            _resident((D_FF, D_MODEL), lambda i: (0, 0)),
            _resident((1, D_MODEL), lambda i: (0, 0)),
            _resident((1, D_MODEL), lambda i: (0, 0)),
        ],
        out_specs=pl.BlockSpec((tm, D_MODEL), lambda i: (i, 0)),
        out_shape=jax.ShapeDtypeStruct((m, D_MODEL), F32),
        compiler_params=_params(),
        name="ffn_lat" if latent else "ffn_ctx",
    )(x2d, mod4, w1, w2, ln_g, ln_b)


def _rope_tables(n_tokens):
    rows = n_tokens // GRID_W
    r, col = jnp.meshgrid(jnp.arange(rows), jnp.arange(GRID_W), indexing="ij")
    r = r.reshape(-1).astype(F32)
    col = col.reshape(-1).astype(F32)
    nf = HEAD_DIM // 4
    inv = ROPE_BASE ** (-jnp.arange(nf, dtype=F32) / nf)
    ang_r = r[:, None] * inv[None, :]
    ang_c = col[:, None] * inv[None, :]
    zero = jnp.zeros_like(ang_r)
    cos = jnp.concatenate([jnp.cos(ang_r)] * 2 + [jnp.cos(ang_c)] * 2, axis=-1)
    s_next = jnp.concatenate([-jnp.sin(ang_r), zero, -jnp.sin(ang_c), zero], axis=-1)
    s_prev = jnp.concatenate([zero, jnp.sin(ang_r), zero, jnp.sin(ang_c)], axis=-1)
    return tuple(jnp.tile(t, (1, LANE // HEAD_DIM)) for t in (cos, s_next, s_prev))


def kernel(x_prompt, x_sample, c, cache_diff_k, cache_diff_v, cache_win_k, cache_win_v, state_ret,
           c_ctx, w_mod, b_mod, w_in, diff_lam, diff_norm_g, win_sink, ret_decay, ret_norm_g,
           w_pa, w_pb, w_pc, w_gate, b_gate, w_o, ln1_g, ln1_b, w_ff1, w_ff2, ln2_g, ln2_b):
    bp, tp, d = x_prompt.shape
    bs, ts, _ = x_sample.shape
    assert d == D_MODEL and w_in.shape == (DEPTH, D_MODEL, D_IN) and c.shape[0] + 1 <= MOD_ROWS
    past = cache_diff_k.shape[2]

    c_rows = jnp.concatenate(
        [c_ctx[None, :], c, jnp.zeros((MOD_ROWS - 1 - bs, d), F32)], axis=0)
    mod4 = _mod_call(c_rows, w_mod, b_mod).reshape(DEPTH, MOD_ROWS, 1, D_MOD)
    rope_tabs = _rope_tables(ts)

    ck_a = cache_diff_k.reshape(bs, DEPTH, past, DA_HEADS * 2 * HEAD_DIM)
    cv_a = cache_diff_v.reshape(bs, DEPTH, past, DA_HEADS * 2 * HEAD_DIM)
    ck_b = cache_win_k.reshape(bs, DEPTH, past, WG_KV_HEADS * HEAD_DIM)
    cv_b = cache_win_v.reshape(bs, DEPTH, past, WG_KV_HEADS * HEAD_DIM)

    xp = x_prompt.reshape(bp * tp, d)
    xs = x_sample.reshape(bs * ts, d)
    diff_k, diff_v, win_k, win_v, ret_s = [], [], [], [], []
    for l in range(DEPTH):
        lam_init = 0.8 - 0.6 * math.exp(-0.3 * l)
        w_in_b = w_in[l].astype(BF16)
        w_gate_b = w_gate[l].astype(BF16)
        wpa, wpb, wpc = w_pa[l].astype(BF16), w_pb[l].astype(BF16), w_pc[l].astype(BF16)
        wo = w_o[l].astype(BF16)
        w1, w2 = w_ff1[l].astype(BF16), w_ff2[l].astype(BF16)
        b_gate_l = b_gate[l].reshape(1, D_GATE)
        lam_l = diff_lam[l]
        dng_l = diff_norm_g[l].reshape(1, LANE)
        sink_l = win_sink[l].reshape(1, WG_Q_HEADS)
        rng_l = ret_norm_g[l].reshape(1, RT_DV)
        ln1 = (ln1_g[l].reshape(1, d), ln1_b[l].reshape(1, d))
        ln2 = (ln2_g[l].reshape(1, d), ln2_b[l].reshape(1, d))

        for latent in (False, True):
            x2d = xs if latent else xp
            b, t = (bs, ts) if latent else (bp, tp)
            z, gates = _proj_call(x2d, mod4, l, w_in_b, w_gate_b, b_gate_l, rope_tabs, t, latent)
            z3 = z.reshape(b, t, D_IN)
            oa = _diff_call(z3, ck_a, cv_a, l, lam_l, dng_l, lam_init, latent)
            ob = _win_call(z3, ck_b, cv_b, l, sink_l, latent)
            if latent:
                oc = _ret_call(z3, state_ret, l, ret_decay[l], rng_l, latent)
            else:
                oc, s_fin = _ret_call(z3, state_ret, l, ret_decay[l], rng_l, latent)
                diff_k.append(z3[:, :, COL_AK * LANE:COL_AV * LANE].reshape(b, t, DA_HEADS, 2 * HEAD_DIM))
                diff_v.append(z3[:, :, COL_AV * LANE:COL_BQ * LANE].reshape(b, t, DA_HEADS, 2 * HEAD_DIM))
                win_k.append(z3[:, :, COL_BK * LANE:COL_BV * LANE].reshape(b, t, WG_KV_HEADS, HEAD_DIM))
                win_v.append(z3[:, :, COL_BV * LANE:COL_CQ * LANE].reshape(b, t, WG_KV_HEADS, HEAD_DIM))
                ret_s.append(s_fin)
            x1 = _post_call(x2d, mod4, l, oa.reshape(b * t, BRANCH_W), ob.reshape(b * t, BRANCH_W),
                            oc.reshape(b * t, BRANCH_W), gates, wpa, wpb, wpc, wo, *ln1, t, latent)
            x2 = _ffn_call(x1, mod4, l, w1, w2, *ln2, t, latent)
            if latent:
                xs = x2
            else:
                xp = x2
    return (xp.reshape(bp, tp, d), xs.reshape(bs, ts, d),
            jnp.stack(diff_k, axis=1), jnp.stack(diff_v, axis=1),
            jnp.stack(win_k, axis=1), jnp.stack(win_v, axis=1), jnp.stack(ret_s, axis=1))
```

```python
import functools
import math

import jax
import jax.numpy as jnp
from jax import lax
from jax.experimental import pallas as pl
from jax.experimental.pallas import tpu as pltpu

F32 = jnp.float32
BF16 = jnp.bfloat16

D_MODEL = 1024
DEPTH = 2
GRID_W = 64
HEAD_DIM = 64
DA_HEADS = 4
WG_Q_HEADS = 8
WG_KV_HEADS = 2
WG_GROUP = WG_Q_HEADS // WG_KV_HEADS
WINDOW = 128
RT_HEADS = 4
RT_DK = 64
RT_DV = 128
BRANCH_W = 512
D_IN = 3840
D_GATE = 3 * D_MODEL
D_FF = 4 * D_MODEL
D_MOD = 6 * D_MODEL
ROPE_BASE = 10000.0
LN_EPS = 1e-5
ALPHA = (2 * DEPTH) ** 0.25
QK_SCALE = HEAD_DIM ** -0.5
LOG2E = math.log2(math.e)

LANE = 128
MOD_ROWS = 8

COL_AQ, COL_AK, COL_AV = 0, 4, 8
COL_BQ, COL_BK, COL_BV = 12, 16, 17
COL_CQ, COL_CK, COL_CV, COL_CG = 18, 20, 22, 26
ROPE_BLOCKS = tuple(range(0, 8)) + tuple(range(12, 17))

VMEM_LIMIT = 56 * 1024 * 1024
NT_DIMS = (((1,), (1,)), ((), ()))
TN_DIMS = (((0,), (0,)), ((), ()))


def _params():
    return pltpu.CompilerParams(vmem_limit_bytes=VMEM_LIMIT)


def _resident(shape, index_map):
    return pl.BlockSpec(shape, index_map, pipeline_mode=pl.Buffered(1))


def _dot(a, b):
    return jnp.dot(a, b, preferred_element_type=F32)


def _mod_kernel(c_ref, w_ref, b_ref, o_ref):
    c = c_ref[...]
    a = (c * jax.nn.sigmoid(c)).astype(BF16)
    o_ref[...] = _dot(a, w_ref[...].astype(BF16)) + b_ref[...]


def _mod_call(c_rows, w_mod, b_mod):
    tn = 1536
    return pl.pallas_call(
        _mod_kernel,
        grid=(DEPTH, D_MOD // tn),
        in_specs=[
            pl.BlockSpec((MOD_ROWS, D_MODEL), lambda l, n: (0, 0)),
            pl.BlockSpec((None, D_MODEL, tn), lambda l, n: (l, 0, n)),
            pl.BlockSpec((None, 1, tn), lambda l, n: (l, 0, n)),
        ],
        out_specs=pl.BlockSpec((None, MOD_ROWS, tn), lambda l, n: (l, 0, n)),
        out_shape=jax.ShapeDtypeStruct((DEPTH, MOD_ROWS, D_MOD), F32),
        compiler_params=_params(),
        name="mod_vectors",
    )(c_rows, w_mod, b_mod.reshape(DEPTH, 1, D_MOD))


def _proj_kernel(*refs, rope):
    if rope:
        x_ref, mod_ref, win_ref, cos_ref, sa_ref, sb_ref, z_ref = refs
    else:
        x_ref, mod_ref, win_ref, z_ref = refs
    sh1 = mod_ref[:, 0:D_MODEL]
    sc1 = mod_ref[:, D_MODEL:2 * D_MODEL]
    h = (x_ref[...] * (1.0 + sc1) + sh1).astype(BF16)
    nc = 768
    for c0 in range(0, D_IN, nc):
        z = _dot(h, win_ref[:, c0:c0 + nc])
        for j in range(nc // LANE):
            blk = c0 // LANE + j
            u = z[:, j * LANE:(j + 1) * LANE]
            if rope and blk in ROPE_BLOCKS:
                u = (u * cos_ref[...] + pltpu.roll(u, LANE - 16, 1) * sa_ref[...]
                     + pltpu.roll(u, 16, 1) * sb_ref[...])
            z_ref[:, blk * LANE:(blk + 1) * LANE] = u


def _proj_call(x2d, mod4, layer, w_in_b, rope_tabs, tokens_per_seq, latent, tm=256):
    m = x2d.shape[0]
    tiles_per_seq = tokens_per_seq // tm

    def mod_idx(i):
        row = (1 + i // tiles_per_seq) if latent else 0
        return (layer, row, 0, 0)

    in_specs = [
        pl.BlockSpec((tm, D_MODEL), lambda i: (i, 0)),
        pl.BlockSpec((None, None, 1, D_MOD), mod_idx),
        _resident((D_MODEL, D_IN), lambda i: (0, 0)),
    ]
    args = [x2d, mod4, w_in_b]
    if latent:
        for tab in rope_tabs:
            in_specs.append(pl.BlockSpec((tm, LANE), lambda i: (i % tiles_per_seq, 0)))
            args.append(tab)
    return pl.pallas_call(
        functools.partial(_proj_kernel, rope=latent),
        grid=(m // tm,),
        in_specs=in_specs,
        out_specs=pl.BlockSpec((tm, D_IN), lambda i: (i, 0)),
        out_shape=jax.ShapeDtypeStruct((m, D_IN), F32),
        compiler_params=_params(),
        name="proj_lat" if latent else "proj_ctx",
    )(*args)


def _diff_kernel(*refs, has_ctx, lam_init):
    if has_ctx:
        q_ref, k_ref, v_ref, kc_ref, vc_ref, lam_ref, g_ref, o_ref, k_scr, v_scr = refs
    else:
        q_ref, k_ref, v_ref, lam_ref, g_ref, o_ref, k_scr, v_scr = refs
    tq = q_ref.shape[0]
    t = k_ref.shape[0]
    nk = k_scr.shape[0]

    @pl.when(pl.program_id(2) == 0)
    def _():
        k_scr[0:t, :] = k_ref[...].astype(BF16)
        v_scr[0:t, 0:LANE] = v_ref[...].astype(BF16)
        if has_ctx:
            k_scr[t:nk, :] = kc_ref[...].astype(BF16)
            v_scr[t:nk, 0:LANE] = vc_ref[...].astype(BF16)
        v_scr[:, LANE:2 * LANE] = jnp.ones((nk, LANE), BF16)

    q = q_ref[...] * (QK_SCALE * LOG2E)
    lane = lax.broadcasted_iota(jnp.int32, q.shape, 1)
    qs = jnp.concatenate([jnp.where(lane < HEAD_DIM, q, 0.0),
                          jnp.where(lane >= HEAD_DIM, q, 0.0)], axis=0).astype(BF16)
    s = lax.dot_general(qs, k_scr[...], NT_DIMS, preferred_element_type=F32)
    e = jnp.exp2(s - s.max(axis=-1, keepdims=True)).astype(BF16)
    ov = _dot(e, v_scr[...])
    o = ov[:, 0:LANE] * (1.0 / ov[:, LANE:LANE + 1])
    lv = lam_ref[...]
    lam = (jnp.exp(jnp.sum(lv[0:1] * lv[1:2], axis=-1, keepdims=True))
           - jnp.exp(jnp.sum(lv[2:3] * lv[3:4], axis=-1, keepdims=True)) + lam_init)
    od = o[0:tq] - lam * o[tq:2 * tq]
    y = od * lax.rsqrt(jnp.mean(od * od, axis=-1, keepdims=True) + LN_EPS)
    o_ref[...] = y * g_ref[...] * (1.0 - lam_init)


def _diff_call(z3, cache_k4, cache_v4, layer, diff_lam_l, diff_norm_g_l, lam_init, latent):
    b, t, _ = z3.shape
    tq = 256
    has_ctx = latent
    in_specs = [
        pl.BlockSpec((None, tq, LANE), lambda bi, h, qi: (bi, qi, COL_AQ + h)),
        pl.BlockSpec((None, t, LANE), lambda bi, h, qi: (bi, 0, COL_AK + h)),
        pl.BlockSpec((None, t, LANE), lambda bi, h, qi: (bi, 0, COL_AV + h)),
    ]
    args = [z3, z3, z3]
    if has_ctx:
        past = cache_k4.shape[2]
        in_specs += [
            pl.BlockSpec((None, None, past, LANE), lambda bi, h, qi: (bi, layer, 0, h)),
            pl.BlockSpec((None, None, past, LANE), lambda bi, h, qi: (bi, layer, 0, h)),
        ]
        args += [cache_k4, cache_v4]
    in_specs += [
        pl.BlockSpec((4, HEAD_DIM), lambda bi, h, qi: (0, 0)),
        pl.BlockSpec((1, LANE), lambda bi, h, qi: (0, 0)),
    ]
    args += [diff_lam_l, diff_norm_g_l]
    nk = t + (cache_k4.shape[2] if has_ctx else 0)
    return pl.pallas_call(
        functools.partial(_diff_kernel, has_ctx=has_ctx, lam_init=lam_init),
        grid=(b, DA_HEADS, t // tq),
        in_specs=in_specs,
        out_specs=pl.BlockSpec((None, tq, LANE), lambda bi, h, qi: (bi, qi, h)),
        out_shape=jax.ShapeDtypeStruct((b, t, BRANCH_W), F32),
        scratch_shapes=[pltpu.VMEM((nk, LANE), BF16), pltpu.VMEM((nk, 2 * LANE), BF16)],
        compiler_params=_params(),
        name="diff_lat" if latent else "diff_ctx",
    )(*args)


def _win_kernel(*refs, latent, t):
    if latent:
        q_ref, k_ref, v_ref, kc_ref, vc_ref, sink_ref, o_ref = refs
    else:
        q_ref, k_ref, v_ref, sink_ref, o_ref = refs
    tq = q_ref.shape[0]
    if latent:
        n = pl.program_id(1)
        span = 3 * WINDOW
        start = pl.multiple_of(jnp.clip((n - 1) * WINDOW, 0, t - span), WINDOW)
        k_all = jnp.concatenate([k_ref[pl.ds(start, span), :], kc_ref[...]], axis=0).astype(BF16)
        v_all = jnp.concatenate([v_ref[pl.ds(start, span), :], vc_ref[...]], axis=0)
        qpos = n * tq + lax.broadcasted_iota(jnp.int32, (tq, span), 0)
        kpos = start + lax.broadcasted_iota(jnp.int32, (tq, span), 1)
        bias = jnp.where(jnp.abs(kpos - qpos) <= WINDOW, 0.0, -1e30)
        bias = jnp.concatenate([bias] * WG_GROUP, axis=0)
    else:
        span = 0
        k_all = k_ref[...].astype(BF16)
        v_all = v_ref[...]
    vlane = lax.broadcasted_iota(jnp.int32, v_all.shape, 1)
    pieces = []
    for kv in range(WG_KV_HEADS):
        lo = kv * HEAD_DIM
        heads = range(kv * WG_GROUP, (kv + 1) * WG_GROUP)
        q_g = (jnp.concatenate([q_ref[:, h * HEAD_DIM:(h + 1) * HEAD_DIM] for h in heads], axis=0)
               * (QK_SCALE * LOG2E)).astype(BF16)
        sk = jnp.concatenate([jnp.broadcast_to(sink_ref[:, h:h + 1] * LOG2E, (tq, 1)) for h in heads],
                             axis=0)
        s = lax.dot_general(q_g, k_all[:, lo:lo + HEAD_DIM], NT_DIMS, preferred_element_type=F32)
        if latent:
            s_loc = s[:, 0:span] + bias
            s_ctx = s[:, span:]
            m = jnp.maximum(jnp.maximum(s_loc.max(axis=-1, keepdims=True),
                                        s_ctx.max(axis=-1, keepdims=True)), sk)
            e = jnp.concatenate([jnp.exp2(s_loc - m), jnp.exp2(s_ctx - m)], axis=-1).astype(BF16)
        else:
            m = jnp.maximum(s.max(axis=-1, keepdims=True), sk)
            e = jnp.exp2(s - m).astype(BF16)
        mine = (vlane >= lo) & (vlane < lo + HEAD_DIM)
        ov = _dot(e, jnp.where(mine, v_all, 1.0).astype(BF16))
        ones_lane = HEAD_DIM - lo
        d = ov[:, ones_lane:ones_lane + 1] + jnp.exp2(sk - m)
        on = ov * (1.0 / d)
        for g in range(WG_GROUP):
            pieces.append(on[g * tq:(g + 1) * tq, lo:lo + HEAD_DIM])
    o_ref[...] = jnp.concatenate(pieces, axis=-1)


def _win_call(z3, cache_k4, cache_v4, layer, sink_l, latent):
    b, t, _ = z3.shape
    tq = WINDOW if latent else t
    in_specs = [
        pl.BlockSpec((None, tq, BRANCH_W), lambda bi, n: (bi, n, COL_BQ // 4)),
        pl.BlockSpec((None, t, LANE), lambda bi, n: (bi, 0, COL_BK)),
        pl.BlockSpec((None, t, LANE), lambda bi, n: (bi, 0, COL_BV)),
    ]
    args = [z3, z3, z3]
    if latent:
        past = cache_k4.shape[2]
        in_specs += [
            pl.BlockSpec((None, None, past, LANE), lambda bi, n: (bi, layer, 0, 0)),
            pl.BlockSpec((None, None, past, LANE), lambda bi, n: (bi, layer, 0, 0)),
        ]
        args += [cache_k4, cache_v4]
    in_specs.append(pl.BlockSpec((1, WG_Q_HEADS), lambda bi, n: (0, 0)))
    args.append(sink_l)
    return pl.pallas_call(
        functools.partial(_win_kernel, latent=latent, t=t),
        grid=(b, t // tq),
        in_specs=in_specs,
        out_specs=pl.BlockSpec((None, tq, BRANCH_W), lambda bi, n: (bi, n, 0)),
        out_shape=jax.ShapeDtypeStruct((b, t, BRANCH_W), F32),
        compiler_params=_params(),
        name="win_lat" if latent else "win_ctx",
    )(*args)


def _log_sigmoid(x):
    return jnp.minimum(x, 0.0) - jnp.log1p(jnp.exp(-jnp.abs(x)))


def _ret_kernel(*refs, latent, t):
    if latent:
        q_ref, k_ref, v_ref, cg_ref, dec_ref, g_ref, s0_ref, o_ref, d_scr = refs
    else:
        q_ref, k_ref, v_ref, cg_ref, dec_ref, g_ref, o_ref, sfin_ref, d_scr = refs
    tq = q_ref.shape[0]
    q0 = pl.program_id(1) * tq
    lg = _log_sigmoid(dec_ref[...])

    @pl.when(pl.program_id(2) == 0)
    def _():
        tpos = q0 + lax.broadcasted_iota(jnp.int32, (tq, t), 0)
        spos = lax.broadcasted_iota(jnp.int32, (tq, t), 1)
        diff = (tpos - spos).astype(F32)
        for j in range(2):
            d_scr[j] = (jnp.where(diff >= 0, jnp.exp(jnp.maximum(diff, 0.0) * lg[0:1, j:j + 1]), 0.0)
                        + jnp.where(diff <= 0, jnp.exp(jnp.maximum(-diff, 0.0) * lg[1:2, j:j + 1]), 0.0))

    outs = []
    for j in range(2):
        lgf = lg[0:1, j:j + 1]
        lgb = lg[1:2, j:j + 1]
        q = q_ref[:, j * RT_DK:(j + 1) * RT_DK].astype(BF16)
        kf = k_ref[:, j * RT_DK:(j + 1) * RT_DK] * (RT_DK ** -0.5)
        k = kf.astype(BF16)
        v = v_ref[:, j * RT_DV:(j + 1) * RT_DV].astype(BF16)
        s = lax.dot_general(q, k, NT_DIMS, preferred_element_type=F32)
        y = _dot((s * d_scr[j]).astype(BF16), v)
        if latent:
            tq_pos = (q0 + lax.broadcasted_iota(jnp.int32, (tq, 1), 0)).astype(F32)
            cf = jnp.exp((tq_pos + 1.0) * lgf)
            cb = jnp.exp((float(t) - tq_pos) * lgb)
            y = (y + _dot(q, s0_ref[0, j].astype(BF16)) * cf
                 + _dot(q, s0_ref[1, j].astype(BF16)) * cb)
        else:
            sp = lax.broadcasted_iota(jnp.int32, (t, 1), 0).astype(F32)
            zf = jnp.exp((float(t) - 1.0 - sp) * lgf)
            zb = jnp.exp(sp * lgb)
            sfin_ref[0, j] = lax.dot_general((kf * zf).astype(BF16), v, TN_DIMS,
                                             preferred_element_type=F32)
            sfin_ref[1, j] = lax.dot_general((kf * zb).astype(BF16), v, TN_DIMS,
                                             preferred_element_type=F32)
        mu = jnp.mean(y, axis=-1, keepdims=True)
        yc = y - mu
        var = jnp.mean(yc * yc, axis=-1, keepdims=True)
        yn = yc * lax.rsqrt(var + LN_EPS) * g_ref[...]
        cg = cg_ref[:, j * RT_DV:(j + 1) * RT_DV]
        outs.append(yn * (cg * jax.nn.sigmoid(cg)))
    o_ref[...] = jnp.concatenate(outs, axis=-1)


def _ret_call(z3, state6, layer, decay_l, ret_norm_g_l, latent):
    b, t, _ = z3.shape
    tq = 256
    hp_n = RT_HEADS // 2
    dec3 = decay_l.reshape(2, hp_n, 2).transpose(1, 0, 2)
    in_specs = [
        pl.BlockSpec((None, tq, LANE), lambda hp, qi, bi: (bi, qi, COL_CQ + hp)),
        pl.BlockSpec((None, t, LANE), lambda hp, qi, bi: (bi, 0, COL_CK + hp)),
        pl.BlockSpec((None, t, 2 * LANE), lambda hp, qi, bi: (bi, 0, COL_CV // 2 + hp)),
        pl.BlockSpec((None, tq, 2 * LANE), lambda hp, qi, bi: (bi, qi, COL_CG // 2 + hp)),
        pl.BlockSpec((None, 2, 2), lambda hp, qi, bi: (hp, 0, 0)),
        pl.BlockSpec((1, RT_DV), lambda hp, qi, bi: (0, 0)),
    ]
    args = [z3, z3, z3, z3, dec3, ret_norm_g_l]
    o_spec = pl.BlockSpec((None, tq, 2 * LANE), lambda hp, qi, bi: (bi, qi, hp))
    o_shape = jax.ShapeDtypeStruct((b, t, BRANCH_W), F32)
    if latent:
        in_specs.append(pl.BlockSpec((None, None, 2, 2, RT_DK, RT_DV),
                                     lambda hp, qi, bi: (bi, layer, 0, hp, 0, 0)))
        args.append(state6)
        out_specs, out_shape = o_spec, o_shape
    else:
        out_specs = [o_spec, pl.BlockSpec((None, 2, 2, RT_DK, RT_DV),
                                          lambda hp, qi, bi: (bi, 0, hp, 0, 0))]
        out_shape = [o_shape, jax.ShapeDtypeStruct((b, 2, RT_HEADS, RT_DK, RT_DV), F32)]
    return pl.pallas_call(
        functools.partial(_ret_kernel, latent=latent, t=t),
        grid=(hp_n, t // tq, b),
        in_specs=in_specs,
        out_specs=out_specs,
        out_shape=out_shape,
        scratch_shapes=[pltpu.VMEM((2, tq, t), F32)],
        compiler_params=_params(),
        name="ret_lat" if latent else "ret_ctx",
    )(*args)


def _layer_norm(x, g, b):
    mu = jnp.mean(x, axis=-1, keepdims=True)
    xc = x - mu
    var = jnp.mean(xc * xc, axis=-1, keepdims=True)
    return xc * lax.rsqrt(var + LN_EPS) * g + b


def _tail_kernel(x_ref, mod_ref, oa_ref, ob_ref, oc_ref, wg_ref, bg_ref, wpa_ref, wpb_ref, wpc_ref,
                 wo_ref, ln1g_ref, ln1b_ref, w1_ref, w2_ref, ln2g_ref, ln2b_ref, o_ref):
    d = D_MODEL
    x = x_ref[...]
    sh1, sc1, g1 = mod_ref[:, 0:d], mod_ref[:, d:2 * d], mod_ref[:, 2 * d:3 * d]
    sh2, sc2, g2 = mod_ref[:, 3 * d:4 * d], mod_ref[:, 4 * d:5 * d], mod_ref[:, 5 * d:6 * d]
    h1 = (x * (1.0 + sc1) + sh1).astype(BF16)
    merged = None
    for i, (o_ref_i, wp_ref) in enumerate(((oa_ref, wpa_ref), (ob_ref, wpb_ref), (oc_ref, wpc_ref))):
        gate = jax.nn.sigmoid(_dot(h1, wg_ref[:, i * d:(i + 1) * d]) + bg_ref[:, i * d:(i + 1) * d])
        part = gate * _dot(o_ref_i[...].astype(BF16), wp_ref[...])
        merged = part if merged is None else merged + part
    y = _dot(merged.astype(BF16), wo_ref[...])
    x1 = _layer_norm(ALPHA * x + g1 * y, ln1g_ref[...], ln1b_ref[...])
    h2 = (x1 * (1.0 + sc2) + sh2).astype(BF16)
    u = jnp.maximum(_dot(h2, w1_ref[...]), 0.0)
    f = _dot((u * u).astype(BF16), w2_ref[...])
    o_ref[...] = _layer_norm(ALPHA * x1 + g2 * f, ln2g_ref[...], ln2b_ref[...])


def _tail_call(x2d, mod4, layer, oa, ob, oc, wg, bg, wpa, wpb, wpc, wo, ln1, w1, w2, ln2,
               tokens_per_seq, latent, tm=256):
    m = x2d.shape[0]
    tiles_per_seq = tokens_per_seq // tm

    def mod_idx(i):
        row = (1 + i // tiles_per_seq) if latent else 0
        return (layer, row, 0, 0)

    row_spec = lambda w: pl.BlockSpec((tm, w), lambda i: (i, 0))
    whole = lambda r, c: _resident((r, c), lambda i: (0, 0))
    return pl.pallas_call(
        _tail_kernel,
        grid=(m // tm,),
        in_specs=[
            row_spec(D_MODEL),
            pl.BlockSpec((None, None, 1, D_MOD), mod_idx),
            row_spec(BRANCH_W), row_spec(BRANCH_W), row_spec(BRANCH_W),
            whole(D_MODEL, D_GATE), whole(1, D_GATE),
            whole(BRANCH_W, D_MODEL), whole(BRANCH_W, D_MODEL), whole(BRANCH_W, D_MODEL),
            whole(D_MODEL, D_MODEL), whole(1, D_MODEL), whole(1, D_MODEL),
            whole(D_MODEL, D_FF), whole(D_FF, D_MODEL), whole(1, D_MODEL), whole(1, D_MODEL),
        ],
        out_specs=row_spec(D_MODEL),
        out_shape=jax.ShapeDtypeStruct((m, D_MODEL), F32),
        compiler_params=_params(),
        name="tail_lat" if latent else "tail_ctx",
    )(x2d, mod4, oa, ob, oc, wg, bg, wpa, wpb, wpc, wo, *ln1, w1, w2, *ln2)


def _rope_tables(n_tokens):
    rows = n_tokens // GRID_W
    r, col = jnp.meshgrid(jnp.arange(rows), jnp.arange(GRID_W), indexing="ij")
    r = r.reshape(-1).astype(F32)
    col = col.reshape(-1).astype(F32)
    nf = HEAD_DIM // 4
    inv = ROPE_BASE ** (-jnp.arange(nf, dtype=F32) / nf)
    ang_r = r[:, None] * inv[None, :]
    ang_c = col[:, None] * inv[None, :]
    zero = jnp.zeros_like(ang_r)
    cos = jnp.concatenate([jnp.cos(ang_r)] * 2 + [jnp.cos(ang_c)] * 2, axis=-1)
    s_next = jnp.concatenate([-jnp.sin(ang_r), zero, -jnp.sin(ang_c), zero], axis=-1)
    s_prev = jnp.concatenate([zero, jnp.sin(ang_r), zero, jnp.sin(ang_c)], axis=-1)
    return tuple(jnp.tile(t, (1, LANE // HEAD_DIM)) for t in (cos, s_next, s_prev))


def kernel(x_prompt, x_sample, c, cache_diff_k, cache_diff_v, cache_win_k, cache_win_v, state_ret,
           c_ctx, w_mod, b_mod, w_in, diff_lam, diff_norm_g, win_sink, ret_decay, ret_norm_g,
           w_pa, w_pb, w_pc, w_gate, b_gate, w_o, ln1_g, ln1_b, w_ff1, w_ff2, ln2_g, ln2_b):
    bp, tp, d = x_prompt.shape
    bs, ts, _ = x_sample.shape
    assert d == D_MODEL and w_in.shape == (DEPTH, D_MODEL, D_IN) and c.shape[0] + 1 <= MOD_ROWS
    past = cache_diff_k.shape[2]

    c_rows = jnp.concatenate(
        [c_ctx[None, :], c, jnp.zeros((MOD_ROWS - 1 - bs, d), F32)], axis=0)
    mod4 = _mod_call(c_rows, w_mod, b_mod).reshape(DEPTH, MOD_ROWS, 1, D_MOD)
    rope_tabs = _rope_tables(ts)

    ck_a = cache_diff_k.reshape(bs, DEPTH, past, DA_HEADS * 2 * HEAD_DIM)
    cv_a = cache_diff_v.reshape(bs, DEPTH, past, DA_HEADS * 2 * HEAD_DIM)
    ck_b = cache_win_k.reshape(bs, DEPTH, past, WG_KV_HEADS * HEAD_DIM)
    cv_b = cache_win_v.reshape(bs, DEPTH, past, WG_KV_HEADS * HEAD_DIM)

    xp = x_prompt.reshape(bp * tp, d)
    xs = x_sample.reshape(bs * ts, d)
    diff_k, diff_v, win_k, win_v, ret_s = [], [], [], [], []
    for l in range(DEPTH):
        lam_init = 0.8 - 0.6 * math.exp(-0.3 * l)
        w_in_b = w_in[l].astype(BF16)
        w_gate_b = w_gate[l].astype(BF16)
        wpa, wpb, wpc = w_pa[l].astype(BF16), w_pb[l].astype(BF16), w_pc[l].astype(BF16)
        wo = w_o[l].astype(BF16)
        w1, w2 = w_ff1[l].astype(BF16), w_ff2[l].astype(BF16)
        b_gate_l = b_gate[l].reshape(1, D_GATE)
        lam_l = diff_lam[l]
        dng_l = diff_norm_g[l].reshape(1, LANE)
        sink_l = win_sink[l].reshape(1, WG_Q_HEADS)
        rng_l = ret_norm_g[l].reshape(1, RT_DV)
        ln1 = (ln1_g[l].reshape(1, d), ln1_b[l].reshape(1, d))
        ln2 = (ln2_g[l].reshape(1, d), ln2_b[l].reshape(1, d))

        for latent in (False, True):
            x2d = xs if latent else xp
            b, t = (bs, ts) if latent else (bp, tp)
            z = _proj_call(x2d, mod4, l, w_in_b, rope_tabs, t, latent, tm=512)
            z3 = z.reshape(b, t, D_IN)
            oa = _diff_call(z3, ck_a, cv_a, l, lam_l, dng_l, lam_init, latent)
            ob = _win_call(z3, ck_b, cv_b, l, sink_l, latent)
            if latent:
                oc = _ret_call(z3, state_ret, l, ret_decay[l], rng_l, latent)
            else:
                oc, s_fin = _ret_call(z3, state_ret, l, ret_decay[l], rng_l, latent)
                diff_k.append(z3[:, :, COL_AK * LANE:COL_AV * LANE].reshape(b, t, DA_HEADS, 2 * HEAD_DIM))
                diff_v.append(z3[:, :, COL_AV * LANE:COL_BQ * LANE].reshape(b, t, DA_HEADS, 2 * HEAD_DIM))
                win_k.append(z3[:, :, COL_BK * LANE:COL_BV * LANE].reshape(b, t, WG_KV_HEADS, HEAD_DIM))
                win_v.append(z3[:, :, COL_BV * LANE:COL_CQ * LANE].reshape(b, t, WG_KV_HEADS, HEAD_DIM))
                ret_s.append(s_fin)
            x2 = _tail_call(x2d, mod4, l, oa.reshape(b * t, BRANCH_W), ob.reshape(b * t, BRANCH_W),
                            oc.reshape(b * t, BRANCH_W), w_gate_b, b_gate_l, wpa, wpb, wpc, wo, ln1,
                            w1, w2, ln2, t, latent)
            if latent:
                xs = x2
            else:
                xp = x2
    return (xp.reshape(bp, tp, d), xs.reshape(bs, ts, d),
            jnp.stack(diff_k, axis=1), jnp.stack(diff_v, axis=1),
            jnp.stack(win_k, axis=1), jnp.stack(win_v, axis=1), jnp.stack(ret_s, axis=1))
```

```python
import functools
import math

import jax
import jax.numpy as jnp
from jax import lax
from jax.experimental import pallas as pl
from jax.experimental.pallas import tpu as pltpu

F32 = jnp.float32
BF16 = jnp.bfloat16

D_MODEL = 1024
DEPTH = 2
GRID_W = 64
HEAD_DIM = 64
DA_HEADS = 4
WG_Q_HEADS = 8
WG_KV_HEADS = 2
WG_GROUP = WG_Q_HEADS // WG_KV_HEADS
WINDOW = 128
RT_HEADS = 4
RT_DK = 64
RT_DV = 128
BRANCH_W = 512
D_IN = 3840
D_GATE = 3 * D_MODEL
D_FF = 4 * D_MODEL
D_MOD = 6 * D_MODEL
ROPE_BASE = 10000.0
LN_EPS = 1e-5
ALPHA = (2 * DEPTH) ** 0.25
QK_SCALE = HEAD_DIM ** -0.5
LOG2E = math.log2(math.e)

LANE = 128
MOD_ROWS = 8
ONES_ROWS = 16

COL_AQ, COL_AK, COL_AV = 0, 4, 8
COL_BQ, COL_BK, COL_BV = 12, 16, 17
COL_CQ, COL_CK, COL_CV, COL_CG = 18, 20, 22, 26
ROPE_BLOCKS = tuple(range(0, 8)) + tuple(range(12, 17))

VMEM_LIMIT = 56 * 1024 * 1024
NT_DIMS = (((1,), (1,)), ((), ()))
TN_DIMS = (((0,), (0,)), ((), ()))


def _params():
    return pltpu.CompilerParams(vmem_limit_bytes=VMEM_LIMIT)


def _resident(shape, index_map):
    return pl.BlockSpec(shape, index_map, pipeline_mode=pl.Buffered(1))


def _dot(a, b):
    return jnp.dot(a, b, preferred_element_type=F32)


def _mod_kernel(c_ref, w_ref, b_ref, o_ref):
    c = c_ref[...]
    a = (c * jax.nn.sigmoid(c)).astype(BF16)
    o_ref[...] = _dot(a, w_ref[...].astype(BF16)) + b_ref[...]


def _mod_call(c_rows, w_mod, b_mod):
    tn = 1536
    return pl.pallas_call(
        _mod_kernel,
        grid=(DEPTH, D_MOD // tn),
        in_specs=[
            pl.BlockSpec((MOD_ROWS, D_MODEL), lambda l, n: (0, 0)),
            pl.BlockSpec((None, D_MODEL, tn), lambda l, n: (l, 0, n)),
            pl.BlockSpec((None, 1, tn), lambda l, n: (l, 0, n)),
        ],
        out_specs=pl.BlockSpec((None, MOD_ROWS, tn), lambda l, n: (l, 0, n)),
        out_shape=jax.ShapeDtypeStruct((DEPTH, MOD_ROWS, D_MOD), F32),
        compiler_params=_params(),
        name="mod_vectors",
    )(c_rows, w_mod, b_mod.reshape(DEPTH, 1, D_MOD))


def _proj_kernel(*refs, rope):
    if rope:
        x_ref, mod_ref, win_ref, cos_ref, sa_ref, sb_ref, z_ref = refs
    else:
        x_ref, mod_ref, win_ref, z_ref = refs
    sh1 = mod_ref[:, 0:D_MODEL]
    sc1 = mod_ref[:, D_MODEL:2 * D_MODEL]
    h = (x_ref[...] * (1.0 + sc1) + sh1).astype(BF16)
    nc = 768
    for c0 in range(0, D_IN, nc):
        z = _dot(h, win_ref[:, c0:c0 + nc])
        for j in range(nc // LANE):
            blk = c0 // LANE + j
            u = z[:, j * LANE:(j + 1) * LANE]
            if rope and blk in ROPE_BLOCKS:
                u = (u * cos_ref[...] + pltpu.roll(u, LANE - 16, 1) * sa_ref[...]
                     + pltpu.roll(u, 16, 1) * sb_ref[...])
            z_ref[:, blk * LANE:(blk + 1) * LANE] = u


def _proj_call(x2d, mod4, layer, w_in_b, rope_tabs, tokens_per_seq, latent, tm=256):
    m = x2d.shape[0]
    tiles_per_seq = tokens_per_seq // tm

    def mod_idx(i):
        row = (1 + i // tiles_per_seq) if latent else 0
        return (layer, row, 0, 0)

    in_specs = [
        pl.BlockSpec((tm, D_MODEL), lambda i: (i, 0)),
        pl.BlockSpec((None, None, 1, D_MOD), mod_idx),
        _resident((D_MODEL, D_IN), lambda i: (0, 0)),
    ]
    args = [x2d, mod4, w_in_b]
    if latent:
        for tab in rope_tabs:
            in_specs.append(pl.BlockSpec((tm, LANE), lambda i: (i % tiles_per_seq, 0)))
            args.append(tab)
    return pl.pallas_call(
        functools.partial(_proj_kernel, rope=latent),
        grid=(m // tm,),
        in_specs=in_specs,
        out_specs=pl.BlockSpec((tm, D_IN), lambda i: (i, 0)),
        out_shape=jax.ShapeDtypeStruct((m, D_IN), F32),
        compiler_params=_params(),
        name="proj_lat" if latent else "proj_ctx",
    )(*args)


def _values_t(v_parts):
    vt = jnp.concatenate([v.T for v in v_parts], axis=1)
    return jnp.concatenate([vt, jnp.ones((ONES_ROWS, vt.shape[1]), F32)], axis=0).astype(BF16)


def _diff_kernel(*refs, has_ctx, lam_init, heads, tq):
    if has_ctx:
        q_ref, k_ref, v_ref, kc_ref, vc_ref, lam_ref, g_ref, o_ref = refs
    else:
        q_ref, k_ref, v_ref, lam_ref, g_ref, o_ref = refs
    t = q_ref.shape[0]
    lv = lam_ref[...]
    lam = (jnp.exp(jnp.sum(lv[0:1] * lv[1:2], axis=-1, keepdims=True))
           - jnp.exp(jnp.sum(lv[2:3] * lv[3:4], axis=-1, keepdims=True)) + lam_init)
    lane = lax.broadcasted_iota(jnp.int32, (tq, LANE), 1)
    blocks = [(hh, i) for hh in range(heads) for i in range(t // tq)]
    kv = {}

    def scores(hh, i):
        cols = slice(hh * LANE, (hh + 1) * LANE)
        if hh not in kv:
            k_parts = [k_ref[:, cols]] + ([kc_ref[...]] if has_ctx else [])
            v_parts = [v_ref[:, cols]] + ([vc_ref[...]] if has_ctx else [])
            kv[hh] = (jnp.concatenate(k_parts, axis=0).astype(BF16), _values_t(v_parts))
        q = q_ref[i * tq:(i + 1) * tq, cols] * (QK_SCALE * LOG2E)
        qs = jnp.concatenate([jnp.where(lane < HEAD_DIM, q, 0.0),
                              jnp.where(lane >= HEAD_DIM, q, 0.0)], axis=0).astype(BF16)
        return lax.dot_general(kv[hh][0], qs, NT_DIMS, preferred_element_type=F32)

    def finish(hh, i, et):
        ot = _dot(kv[hh][1], et)
        o = ot[0:LANE] * (1.0 / ot[LANE:LANE + 1])
        od = o[:, 0:tq] - lam * o[:, tq:2 * tq]
        yt = od * lax.rsqrt(jnp.mean(od * od, axis=0, keepdims=True) + LN_EPS)
        o_ref[i * tq:(i + 1) * tq, hh * LANE:(hh + 1) * LANE] = yt.T * g_ref[...] * (1.0 - lam_init)

    st, et = {}, {}
    for n in range(len(blocks) + 2):
        if n < len(blocks):
            st[n] = scores(*blocks[n])
        if 1 <= n <= len(blocks):
            s = st.pop(n - 1)
            et[n - 1] = jnp.exp2(s - s.max(axis=0, keepdims=True)).astype(BF16)
        if n >= 2:
            finish(*blocks[n - 2], et.pop(n - 2))


def _diff_call(z3, cache_k4, cache_v4, layer, diff_lam_l, diff_norm_g_l, lam_init, latent):
    b, t, _ = z3.shape
    has_ctx = latent
    heads = 1 if latent else DA_HEADS
    w = heads * LANE
    in_specs = [
        pl.BlockSpec((None, t, w), lambda bi, h: (bi, 0, COL_AQ // heads + h)),
        pl.BlockSpec((None, t, w), lambda bi, h: (bi, 0, COL_AK // heads + h)),
        pl.BlockSpec((None, t, w), lambda bi, h: (bi, 0, COL_AV // heads + h)),
    ]
    args = [z3, z3, z3]
    if has_ctx:
        past = cache_k4.shape[2]
        in_specs += [
            pl.BlockSpec((None, None, past, LANE), lambda bi, h: (bi, layer, 0, h)),
            pl.BlockSpec((None, None, past, LANE), lambda bi, h: (bi, layer, 0, h)),
        ]
        args += [cache_k4, cache_v4]
    in_specs += [
        pl.BlockSpec((4, HEAD_DIM), lambda bi, h: (0, 0)),
        pl.BlockSpec((1, LANE), lambda bi, h: (0, 0)),
    ]
    args += [diff_lam_l, diff_norm_g_l]
    tq = 256
    nk = t + (cache_k4.shape[2] if has_ctx else 0)
    return pl.pallas_call(
        functools.partial(_diff_kernel, has_ctx=has_ctx, lam_init=lam_init, heads=heads, tq=tq),
        grid=(b, DA_HEADS // heads),
        in_specs=in_specs,
        out_specs=pl.BlockSpec((None, t, w), lambda bi, h: (bi, 0, h)),
        out_shape=jax.ShapeDtypeStruct((b, t, BRANCH_W), F32),
        compiler_params=_params(),
        name="diff_lat" if latent else "diff_ctx",
    )(*args)


def _win_kernel(*refs, latent, t):
    if latent:
        q_ref, k_ref, v_ref, kc_ref, vc_ref, sink_ref, o_ref = refs
    else:
        q_ref, k_ref, v_ref, sink_ref, o_ref = refs
    w = WINDOW
    if latent:
        tqb, heads_per_chain = w, WG_GROUP
    else:
        tqb, heads_per_chain = t, 2
    nb = t // tqb
    kb = k_ref[...].astype(BF16)
    vt = v_ref[...].T
    if latent:
        kcb = kc_ref[...].astype(BF16)
        vct = vc_ref[...].T
        jj = lax.broadcasted_iota(jnp.int32, (w, w), 0)
        ii = lax.broadcasted_iota(jnp.int32, (w, w), 1)
        bias_prev = jnp.concatenate([jnp.where(jj >= ii, 0.0, -1e30)] * heads_per_chain, axis=1)
        bias_next = jnp.concatenate([jnp.where(jj <= ii, 0.0, -1e30)] * heads_per_chain, axis=1)
    chains = [(kv, h0, n)
              for n in range(nb)
              for kv in range(WG_KV_HEADS)
              for h0 in range(kv * WG_GROUP, (kv + 1) * WG_GROUP, heads_per_chain)]
    sinks = {}

    def sink_row(h0):
        if h0 not in sinks:
            sinks[h0] = jnp.concatenate(
                [jnp.broadcast_to(sink_ref[:, h:h + 1] * LOG2E, (1, tqb))
                 for h in range(h0, h0 + heads_per_chain)], axis=1)
        return sinks[h0]

    def key_blocks(n):
        return (max(n - 1, 0), min(n + 1, nb - 1)) if latent else (0, 0)

    def scores(kv, h0, n):
        lo = kv * HEAD_DIM
        rows = slice(n * tqb, (n + 1) * tqb)
        q_g = (jnp.concatenate([q_ref[rows, h * HEAD_DIM:(h + 1) * HEAD_DIM]
                                for h in range(h0, h0 + heads_per_chain)], axis=0)
               * (QK_SCALE * LOG2E)).astype(BF16)
        b0, b1 = key_blocks(n)
        keys = kb[b0 * tqb:(b1 + 1) * tqb, lo:lo + HEAD_DIM]
        if latent:
            keys = jnp.concatenate([keys, kcb[:, lo:lo + HEAD_DIM]], axis=0)
        st = lax.dot_general(keys, q_g, NT_DIMS, preferred_element_type=F32)
        if latent:
            parts = []
            for blk in range(b0, b1 + 1):
                part = st[(blk - b0) * w:(blk - b0 + 1) * w]
                if blk == n - 1:
                    part = part + bias_prev
                elif blk == n + 1:
                    part = part + bias_next
                parts.append(part)
            parts.append(st[(b1 - b0 + 1) * w:])
            st = jnp.concatenate(parts, axis=0)
        return st

    def softmax_numerator(st, h0):
        m = jnp.maximum(st.max(axis=0, keepdims=True), sink_row(h0))
        return jnp.exp2(st - m).astype(BF16), m

    def finish(kv, h0, n, et, m):
        lo = kv * HEAD_DIM
        b0, b1 = key_blocks(n)
        vals = [vt[lo:lo + HEAD_DIM, b0 * tqb:(b1 + 1) * tqb]]
        if latent:
            vals.append(vct[lo:lo + HEAD_DIM])
        vals = jnp.concatenate(vals, axis=1)
        vals = jnp.concatenate([vals, jnp.ones((ONES_ROWS, vals.shape[1]), F32)], axis=0).astype(BF16)
        ot = _dot(vals, et)
        d = ot[HEAD_DIM:HEAD_DIM + 1] + jnp.exp2(sink_row(h0) - m)
        on = ot[0:HEAD_DIM] * (1.0 / d)
        for p in range(heads_per_chain // 2):
            pair = jnp.concatenate([on[:, (2 * p) * tqb:(2 * p + 1) * tqb],
                                    on[:, (2 * p + 1) * tqb:(2 * p + 2) * tqb]], axis=0)
            c0 = (h0 // 2 + p) * LANE
            o_ref[n * tqb:(n + 1) * tqb, c0:c0 + LANE] = pair.T

    st, et = {}, {}
    for c in range(len(chains) + 2):
        if c < len(chains):
            st[c] = scores(*chains[c])
        if 1 <= c <= len(chains):
            et[c - 1] = softmax_numerator(st.pop(c - 1), chains[c - 1][1])
        if c >= 2:
            finish(*chains[c - 2], *et.pop(c - 2))


def _win_call(z3, cache_k4, cache_v4, layer, sink_l, latent):
    b, t, _ = z3.shape
    in_specs = [
        pl.BlockSpec((None, t, BRANCH_W), lambda bi: (bi, 0, COL_BQ // 4)),
        pl.BlockSpec((None, t, LANE), lambda bi: (bi, 0, COL_BK)),
        pl.BlockSpec((None, t, LANE), lambda bi: (bi, 0, COL_BV)),
    ]
    args = [z3, z3, z3]
    if latent:
        past = cache_k4.shape[2]
        in_specs += [
            pl.BlockSpec((None, None, past, LANE), lambda bi: (bi, layer, 0, 0)),
            pl.BlockSpec((None, None, past, LANE), lambda bi: (bi, layer, 0, 0)),
        ]
        args += [cache_k4, cache_v4]
    in_specs.append(pl.BlockSpec((1, WG_Q_HEADS), lambda bi: (0, 0)))
    args.append(sink_l)
    return pl.pallas_call(
        functools.partial(_win_kernel, latent=latent, t=t),
        grid=(b,),
        in_specs=in_specs,
        out_specs=pl.BlockSpec((None, t, BRANCH_W), lambda bi: (bi, 0, 0)),
        out_shape=jax.ShapeDtypeStruct((b, t, BRANCH_W), F32),
        compiler_params=_params(),
        name="win_lat" if latent else "win_ctx",
    )(*args)


def _log_sigmoid(x):
    return jnp.minimum(x, 0.0) - jnp.log1p(jnp.exp(-jnp.abs(x)))


def _ret_kernel(*refs, latent, t):
    if latent:
        q_ref, k_ref, v_ref, cg_ref, dec_ref, g_ref, s0_ref, o_ref, d_scr = refs
    else:
        q_ref, k_ref, v_ref, cg_ref, dec_ref, g_ref, o_ref, sfin_ref, d_scr = refs
    tq = min(t, 512)
    lg = _log_sigmoid(dec_ref[...])

    @pl.when(pl.program_id(1) == 0)
    def _():
        spos = lax.broadcasted_iota(jnp.int32, (t, t), 0)
        tpos = lax.broadcasted_iota(jnp.int32, (t, t), 1)
        diff = (tpos - spos).astype(F32)
        for j in range(2):
            lgf, lgb = lg[0:1, j:j + 1], lg[1:2, j:j + 1]
            d_scr[j, 0:t, :] = (jnp.where(diff >= 0, jnp.exp(jnp.maximum(diff, 0.0) * lgf), 0.0)
                                + jnp.where(diff <= 0, jnp.exp(jnp.maximum(-diff, 0.0) * lgb), 0.0))
            if latent:
                tp = lax.broadcasted_iota(jnp.int32, (RT_DK, t), 1).astype(F32)
                d_scr[j, t:t + RT_DK, :] = jnp.exp((tp + 1.0) * lgf)
                d_scr[j, t + RT_DK:t + 2 * RT_DK, :] = jnp.exp((float(t) - tp) * lgb)

    if latent:
        eye = (lax.broadcasted_iota(jnp.int32, (RT_DK, RT_DK), 0)
               == lax.broadcasted_iota(jnp.int32, (RT_DK, RT_DK), 1)).astype(F32)
    heads = {}

    def head_operands(j):
        if j not in heads:
            kf = k_ref[:, j * RT_DK:(j + 1) * RT_DK] * (RT_DK ** -0.5)
            v = v_ref[:, j * RT_DV:(j + 1) * RT_DV]
            keys, vals_t = [kf], [v.T]
            if latent:
                keys += [eye, eye]
                vals_t.append(jnp.concatenate([s0_ref[0, j], s0_ref[1, j]], axis=0).T)
            else:
                sp = lax.broadcasted_iota(jnp.int32, (t, 1), 0).astype(F32)
                zf = jnp.exp((float(t) - 1.0 - sp) * lg[0:1, j:j + 1])
                zb = jnp.exp(sp * lg[1:2, j:j + 1])
                vb = v.astype(BF16)
                sfin_ref[0, j] = lax.dot_general((kf * zf).astype(BF16), vb, TN_DIMS,
                                                 preferred_element_type=F32)
                sfin_ref[1, j] = lax.dot_general((kf * zb).astype(BF16), vb, TN_DIMS,
                                                 preferred_element_type=F32)
            heads[j] = (jnp.concatenate(keys, axis=0).astype(BF16),
                        jnp.concatenate(vals_t, axis=1).astype(BF16))
        return heads[j]

    chains = [(j, i) for j in range(2) for i in range(t // tq)]

    def scores(j, i):
        q = q_ref[i * tq:(i + 1) * tq, j * RT_DK:(j + 1) * RT_DK].astype(BF16)
        return lax.dot_general(head_operands(j)[0], q, NT_DIMS, preferred_element_type=F32)

    def finish(j, i, at):
        yt = _dot(head_operands(j)[1], at)
        mu = jnp.mean(yt, axis=0, keepdims=True)
        yc = yt - mu
        var = jnp.mean(yc * yc, axis=0, keepdims=True)
        yn = (yc * lax.rsqrt(var + LN_EPS)).T * g_ref[...]
        cg = cg_ref[i * tq:(i + 1) * tq, j * RT_DV:(j + 1) * RT_DV]
        o_ref[i * tq:(i + 1) * tq, j * RT_DV:(j + 1) * RT_DV] = yn * (cg * jax.nn.sigmoid(cg))

    st, at = {}, {}
    for c in range(len(chains) + 2):
        if c < len(chains):
            st[c] = scores(*chains[c])
        if 1 <= c <= len(chains):
            j, i = chains[c - 1]
            at[c - 1] = (st.pop(c - 1) * d_scr[j, :, i * tq:(i + 1) * tq]).astype(BF16)
        if c >= 2:
            finish(*chains[c - 2], at.pop(c - 2))


def _ret_call(z3, state6, layer, decay_l, ret_norm_g_l, latent):
    b, t, _ = z3.shape
    hp_n = RT_HEADS // 2
    dec3 = decay_l.reshape(2, hp_n, 2).transpose(1, 0, 2)
    in_specs = [
        pl.BlockSpec((None, t, LANE), lambda hp, bi: (bi, 0, COL_CQ + hp)),
        pl.BlockSpec((None, t, LANE), lambda hp, bi: (bi, 0, COL_CK + hp)),
        pl.BlockSpec((None, t, 2 * LANE), lambda hp, bi: (bi, 0, COL_CV // 2 + hp)),
        pl.BlockSpec((None, t, 2 * LANE), lambda hp, bi: (bi, 0, COL_CG // 2 + hp)),
        pl.BlockSpec((None, 2, 2), lambda hp, bi: (hp, 0, 0)),
        pl.BlockSpec((1, RT_DV), lambda hp, bi: (0, 0)),
    ]
    args = [z3, z3, z3, z3, dec3, ret_norm_g_l]
    o_spec = pl.BlockSpec((None, t, 2 * LANE), lambda hp, bi: (bi, 0, hp))
    o_shape = jax.ShapeDtypeStruct((b, t, BRANCH_W), F32)
    if latent:
        in_specs.append(pl.BlockSpec((None, None, 2, 2, RT_DK, RT_DV),
                                     lambda hp, bi: (bi, layer, 0, hp, 0, 0)))
        args.append(state6)
        out_specs, out_shape = o_spec, o_shape
    else:
        out_specs = [o_spec, pl.BlockSpec((None, 2, 2, RT_DK, RT_DV),
                                          lambda hp, bi: (bi, 0, hp, 0, 0))]
        out_shape = [o_shape, jax.ShapeDtypeStruct((b, 2, RT_HEADS, RT_DK, RT_DV), F32)]
    n_keys = t + (2 * RT_DK if latent else 0)
    return pl.pallas_call(
        functools.partial(_ret_kernel, latent=latent, t=t),
        grid=(hp_n, b),
        in_specs=in_specs,
        out_specs=out_specs,
        out_shape=out_shape,
        scratch_shapes=[pltpu.VMEM((2, n_keys, t), F32)],
        compiler_params=_params(),
        name="ret_lat" if latent else "ret_ctx",
    )(*args)


def _layer_norm(x, g, b):
    mu = jnp.mean(x, axis=-1, keepdims=True)
    xc = x - mu
    var = jnp.mean(xc * xc, axis=-1, keepdims=True)
    return xc * lax.rsqrt(var + LN_EPS) * g + b


def _tail_kernel(x_ref, mod_ref, oa_ref, ob_ref, oc_ref, wg_ref, bg_ref, wpa_ref, wpb_ref, wpc_ref,
                 wo_ref, ln1g_ref, ln1b_ref, w1_ref, w2_ref, ln2g_ref, ln2b_ref, o_ref):
    d = D_MODEL
    x = x_ref[...]
    sh1, sc1, g1 = mod_ref[:, 0:d], mod_ref[:, d:2 * d], mod_ref[:, 2 * d:3 * d]
    sh2, sc2, g2 = mod_ref[:, 3 * d:4 * d], mod_ref[:, 4 * d:5 * d], mod_ref[:, 5 * d:6 * d]
    h1 = (x * (1.0 + sc1) + sh1).astype(BF16)
    merged = None
    for i, (o_ref_i, wp_ref) in enumerate(((oa_ref, wpa_ref), (ob_ref, wpb_ref), (oc_ref, wpc_ref))):
        gate = jax.nn.sigmoid(_dot(h1, wg_ref[:, i * d:(i + 1) * d]) + bg_ref[:, i * d:(i + 1) * d])
        part = gate * _dot(o_ref_i[...].astype(BF16), wp_ref[...])
        merged = part if merged is None else merged + part
    y = _dot(merged.astype(BF16), wo_ref[...])
    x1 = _layer_norm(ALPHA * x + g1 * y, ln1g_ref[...], ln1b_ref[...])
    h2 = (x1 * (1.0 + sc2) + sh2).astype(BF16)
    u = jnp.maximum(_dot(h2, w1_ref[...]), 0.0)
    f = _dot((u * u).astype(BF16), w2_ref[...])
    o_ref[...] = _layer_norm(ALPHA * x1 + g2 * f, ln2g_ref[...], ln2b_ref[...])


def _tail_call(x2d, mod4, layer, oa, ob, oc, wg, bg, wpa, wpb, wpc, wo, ln1, w1, w2, ln2,
               tokens_per_seq, latent, tm=256):
    m = x2d.shape[0]
    tiles_per_seq = tokens_per_seq // tm

    def mod_idx(i):
        row = (1 + i // tiles_per_seq) if latent else 0
        return (layer, row, 0, 0)

    row_spec = lambda w: pl.BlockSpec((tm, w), lambda i: (i, 0))
    whole = lambda r, c: _resident((r, c), lambda i: (0, 0))
    return pl.pallas_call(
        _tail_kernel,
        grid=(m // tm,),
        in_specs=[
            row_spec(D_MODEL),
            pl.BlockSpec((None, None, 1, D_MOD), mod_idx),
            row_spec(BRANCH_W), row_spec(BRANCH_W), row_spec(BRANCH_W),
            whole(D_MODEL, D_GATE), whole(1, D_GATE),
            whole(BRANCH_W, D_MODEL), whole(BRANCH_W, D_MODEL), whole(BRANCH_W, D_MODEL),
            whole(D_MODEL, D_MODEL), whole(1, D_MODEL), whole(1, D_MODEL),
            whole(D_MODEL, D_FF), whole(D_FF, D_MODEL), whole(1, D_MODEL), whole(1, D_MODEL),
        ],
        out_specs=row_spec(D_MODEL),
        out_shape=jax.ShapeDtypeStruct((m, D_MODEL), F32),
        compiler_params=_params(),
        name="tail_lat" if latent else "tail_ctx",
    )(x2d, mod4, oa, ob, oc, wg, bg, wpa, wpb, wpc, wo, *ln1, w1, w2, *ln2)


def _rope_tables(n_tokens):
    rows = n_tokens // GRID_W
    r, col = jnp.meshgrid(jnp.arange(rows), jnp.arange(GRID_W), indexing="ij")
    r = r.reshape(-1).astype(F32)
    col = col.reshape(-1).astype(F32)
    nf = HEAD_DIM // 4
    inv = ROPE_BASE ** (-jnp.arange(nf, dtype=F32) / nf)
    ang_r = r[:, None] * inv[None, :]
    ang_c = col[:, None] * inv[None, :]
    zero = jnp.zeros_like(ang_r)
    cos = jnp.concatenate([jnp.cos(ang_r)] * 2 + [jnp.cos(ang_c)] * 2, axis=-1)
    s_next = jnp.concatenate([-jnp.sin(ang_r), zero, -jnp.sin(ang_c), zero], axis=-1)
    s_prev = jnp.concatenate([zero, jnp.sin(ang_r), zero, jnp.sin(ang_c)], axis=-1)
    return tuple(jnp.tile(t, (1, LANE // HEAD_DIM)) for t in (cos, s_next, s_prev))


def kernel(x_prompt, x_sample, c, cache_diff_k, cache_diff_v, cache_win_k, cache_win_v, state_ret,
           c_ctx, w_mod, b_mod, w_in, diff_lam, diff_norm_g, win_sink, ret_decay, ret_norm_g,
           w_pa, w_pb, w_pc, w_gate, b_gate, w_o, ln1_g, ln1_b, w_ff1, w_ff2, ln2_g, ln2_b):
    bp, tp, d = x_prompt.shape
    bs, ts, _ = x_sample.shape
    assert d == D_MODEL and w_in.shape == (DEPTH, D_MODEL, D_IN) and c.shape[0] + 1 <= MOD_ROWS
    past = cache_diff_k.shape[2]

    c_rows = jnp.concatenate(
        [c_ctx[None, :], c, jnp.zeros((MOD_ROWS - 1 - bs, d), F32)], axis=0)
    mod4 = _mod_call(c_rows, w_mod, b_mod).reshape(DEPTH, MOD_ROWS, 1, D_MOD)
    rope_tabs = _rope_tables(ts)

    ck_a = cache_diff_k.reshape(bs, DEPTH, past, DA_HEADS * 2 * HEAD_DIM)
    cv_a = cache_diff_v.reshape(bs, DEPTH, past, DA_HEADS * 2 * HEAD_DIM)
    ck_b = cache_win_k.reshape(bs, DEPTH, past, WG_KV_HEADS * HEAD_DIM)
    cv_b = cache_win_v.reshape(bs, DEPTH, past, WG_KV_HEADS * HEAD_DIM)

    xp = x_prompt.reshape(bp * tp, d)
    xs = x_sample.reshape(bs * ts, d)
    diff_k, diff_v, win_k, win_v, ret_s = [], [], [], [], []
    for l in range(DEPTH):
        lam_init = 0.8 - 0.6 * math.exp(-0.3 * l)
        w_in_b = w_in[l].astype(BF16)
        w_gate_b = w_gate[l].astype(BF16)
        wpa, wpb, wpc = w_pa[l].astype(BF16), w_pb[l].astype(BF16), w_pc[l].astype(BF16)
        wo = w_o[l].astype(BF16)
        w1, w2 = w_ff1[l].astype(BF16), w_ff2[l].astype(BF16)
        b_gate_l = b_gate[l].reshape(1, D_GATE)
        lam_l = diff_lam[l]
        dng_l = diff_norm_g[l].reshape(1, LANE)
        sink_l = win_sink[l].reshape(1, WG_Q_HEADS)
        rng_l = ret_norm_g[l].reshape(1, RT_DV)
        ln1 = (ln1_g[l].reshape(1, d), ln1_b[l].reshape(1, d))
        ln2 = (ln2_g[l].reshape(1, d), ln2_b[l].reshape(1, d))

        for latent in (False, True):
            x2d = xs if latent else xp
            b, t = (bs, ts) if latent else (bp, tp)
            z = _proj_call(x2d, mod4, l, w_in_b, rope_tabs, t, latent, tm=512)
            z3 = z.reshape(b, t, D_IN)
            oa = _diff_call(z3, ck_a, cv_a, l, lam_l, dng_l, lam_init, latent)
            ob = _win_call(z3, ck_b, cv_b, l, sink_l, latent)
            if latent:
                oc = _ret_call(z3, state_ret, l, ret_decay[l], rng_l, latent)
            else:
                oc, s_fin = _ret_call(z3, state_ret, l, ret_decay[l], rng_l, latent)
                diff_k.append(z3[:, :, COL_AK * LANE:COL_AV * LANE].reshape(b, t, DA_HEADS, 2 * HEAD_DIM))
                diff_v.append(z3[:, :, COL_AV * LANE:COL_BQ * LANE].reshape(b, t, DA_HEADS, 2 * HEAD_DIM))
                win_k.append(z3[:, :, COL_BK * LANE:COL_BV * LANE].reshape(b, t, WG_KV_HEADS, HEAD_DIM))
                win_v.append(z3[:, :, COL_BV * LANE:COL_CQ * LANE].reshape(b, t, WG_KV_HEADS, HEAD_DIM))
                ret_s.append(s_fin)
            x2 = _tail_call(x2d, mod4, l, oa.reshape(b * t, BRANCH_W), ob.reshape(b * t, BRANCH_W),
                            oc.reshape(b * t, BRANCH_W), w_gate_b, b_gate_l, wpa, wpb, wpc, wo, ln1,
                            w1, w2, ln2, t, latent)
            if latent:
                xs = x2
            else:
                xp = x2
    return (xp.reshape(bp, tp, d), xs.reshape(bs, ts, d),
            jnp.stack(diff_k, axis=1), jnp.stack(diff_v, axis=1),
            jnp.stack(win_k, axis=1), jnp.stack(win_v, axis=1), jnp.stack(ret_s, axis=1))
```

```python
import functools
import math

import jax
import jax.numpy as jnp
from jax import lax
from jax.experimental import pallas as pl
from jax.experimental.pallas import tpu as pltpu

F32 = jnp.float32
BF16 = jnp.bfloat16

D_MODEL = 1024
DEPTH = 2
GRID_W = 64
HEAD_DIM = 64
DA_HEADS = 4
WG_Q_HEADS = 8
WG_KV_HEADS = 2
WG_GROUP = WG_Q_HEADS // WG_KV_HEADS
WINDOW = 128
RT_HEADS = 4
RT_DK = 64
RT_DV = 128
BRANCH_W = 512
D_IN = 3840
D_GATE = 3 * D_MODEL
D_FF = 4 * D_MODEL
D_MOD = 6 * D_MODEL
ROPE_BASE = 10000.0
LN_EPS = 1e-5
ALPHA = (2 * DEPTH) ** 0.25
QK_SCALE = HEAD_DIM ** -0.5
LOG2E = math.log2(math.e)

LANE = 128
MOD_ROWS = 8
ONES_ROWS = 16

COL_AQ, COL_AK, COL_AV = 0, 4, 8
COL_BQ, COL_BK, COL_BV = 12, 16, 17
COL_CQ, COL_CK, COL_CV, COL_CG = 18, 20, 22, 26
ROPE_BLOCKS = tuple(range(0, 8)) + tuple(range(12, 17))

VMEM_LIMIT = 56 * 1024 * 1024
NT_DIMS = (((1,), (1,)), ((), ()))
TN_DIMS = (((0,), (0,)), ((), ()))


def _params():
    return pltpu.CompilerParams(vmem_limit_bytes=VMEM_LIMIT)


def _resident(shape, index_map):
    return pl.BlockSpec(shape, index_map, pipeline_mode=pl.Buffered(1))


def _dot(a, b):
    return jnp.dot(a, b, preferred_element_type=F32)


def _mod_kernel(c_ref, w_ref, b_ref, o_ref):
    c = c_ref[...]
    a = (c * jax.nn.sigmoid(c)).astype(BF16)
    o_ref[...] = _dot(a, w_ref[...].astype(BF16)) + b_ref[...]


def _mod_call(c_rows, w_mod, b_mod):
    tn = 1536
    return pl.pallas_call(
        _mod_kernel,
        grid=(DEPTH, D_MOD // tn),
        in_specs=[
            pl.BlockSpec((MOD_ROWS, D_MODEL), lambda l, n: (0, 0)),
            pl.BlockSpec((None, D_MODEL, tn), lambda l, n: (l, 0, n)),
            pl.BlockSpec((None, 1, tn), lambda l, n: (l, 0, n)),
        ],
        out_specs=pl.BlockSpec((None, MOD_ROWS, tn), lambda l, n: (l, 0, n)),
        out_shape=jax.ShapeDtypeStruct((DEPTH, MOD_ROWS, D_MOD), F32),
        compiler_params=_params(),
        name="mod_vectors",
    )(c_rows, w_mod, b_mod.reshape(DEPTH, 1, D_MOD))


def _proj_kernel(xc_ref, xl_ref, mod_ref, win_ref, cos_ref, sa_ref, sb_ref,
                 zc_ref, zl_ref, dk_ref, dv_ref, wk_ref, wv_ref, *, n_ctx_tiles):
    is_ctx = pl.program_id(0) < n_ctx_tiles
    x = jnp.where(is_ctx, xc_ref[...], xl_ref[...])
    sh1 = mod_ref[:, 0:D_MODEL]
    sc1 = mod_ref[:, D_MODEL:2 * D_MODEL]
    h = (x * (1.0 + sc1) + sh1).astype(BF16)
    seqs, t_ctx = dk_ref.shape[0], dk_ref.shape[1]
    nc = 768
    for c0 in range(0, D_IN, nc):
        z = _dot(h, win_ref[:, c0:c0 + nc])
        blocks = [(c0 // LANE + j, z[:, j * LANE:(j + 1) * LANE]) for j in range(nc // LANE)]

        @pl.when(is_ctx)
        def _():
            zc_ref[:, c0:c0 + nc] = z
            for blk, u in blocks:
                for s in range(seqs):
                    us = u[s * t_ctx:(s + 1) * t_ctx]
                    if COL_AK <= blk < COL_AV:
                        dk_ref[s, :, blk - COL_AK, :] = us
                    elif COL_AV <= blk < COL_BQ:
                        dv_ref[s, :, blk - COL_AV, :] = us
                    elif blk in (COL_BK, COL_BV):
                        ref = wk_ref if blk == COL_BK else wv_ref
                        for kv in range(WG_KV_HEADS):
                            ref[s, :, kv, :] = us[:, kv * HEAD_DIM:(kv + 1) * HEAD_DIM]

        @pl.when(jnp.logical_not(is_ctx))
        def _():
            for blk, u in blocks:
                if blk in ROPE_BLOCKS:
                    u = (u * cos_ref[...] + pltpu.roll(u, LANE - 16, 1) * sa_ref[...]
                         + pltpu.roll(u, 16, 1) * sb_ref[...])
                zl_ref[:, blk * LANE:(blk + 1) * LANE] = u


def _proj_call(xc, xl, mod4, layer, w_in_b, rope_tabs, t_ctx, t_lat, tm=512):
    n_ctx, n_lat = xc.shape[0] // tm, xl.shape[0] // tm
    lat_tiles_per_seq = t_lat // tm
    seqs = tm // t_ctx
    b_ctx = xc.shape[0] // t_ctx

    def mod_idx(i):
        row = jnp.where(i < n_ctx, 0, 1 + (i - n_ctx) // lat_tiles_per_seq)
        return (layer, row, 0, 0)

    ctx_idx = lambda i: jnp.minimum(i, n_ctx - 1)
    lat_idx = lambda i: jnp.maximum(i - n_ctx, 0)
    rope_spec = pl.BlockSpec((tm, LANE), lambda i: (lat_idx(i) % lat_tiles_per_seq, 0))
    cache_a = pl.BlockSpec((seqs, t_ctx, DA_HEADS, 2 * HEAD_DIM), lambda i: (ctx_idx(i), 0, 0, 0))
    cache_b = pl.BlockSpec((seqs, t_ctx, WG_KV_HEADS, HEAD_DIM), lambda i: (ctx_idx(i), 0, 0, 0))
    return pl.pallas_call(
        functools.partial(_proj_kernel, n_ctx_tiles=n_ctx),
        grid=(n_ctx + n_lat,),
        in_specs=[
            pl.BlockSpec((tm, D_MODEL), lambda i: (ctx_idx(i), 0)),
            pl.BlockSpec((tm, D_MODEL), lambda i: (lat_idx(i), 0)),
            pl.BlockSpec((None, None, 1, D_MOD), mod_idx),
            _resident((D_MODEL, D_IN), lambda i: (0, 0)),
            rope_spec, rope_spec, rope_spec,
        ],
        out_specs=[
            pl.BlockSpec((tm, D_IN), lambda i: (ctx_idx(i), 0)),
            pl.BlockSpec((tm, D_IN), lambda i: (lat_idx(i), 0)),
            cache_a, cache_a, cache_b, cache_b,
        ],
        out_shape=[
            jax.ShapeDtypeStruct((xc.shape[0], D_IN), F32),
            jax.ShapeDtypeStruct((xl.shape[0], D_IN), F32),
            jax.ShapeDtypeStruct((b_ctx, t_ctx, DA_HEADS, 2 * HEAD_DIM), F32),
            jax.ShapeDtypeStruct((b_ctx, t_ctx, DA_HEADS, 2 * HEAD_DIM), F32),
            jax.ShapeDtypeStruct((b_ctx, t_ctx, WG_KV_HEADS, HEAD_DIM), F32),
            jax.ShapeDtypeStruct((b_ctx, t_ctx, WG_KV_HEADS, HEAD_DIM), F32),
        ],
        compiler_params=_params(),
        name="proj",
    )(xc, xl, mod4, w_in_b, *rope_tabs)


def _values_t(v_parts):
    vt = jnp.concatenate([v.T for v in v_parts], axis=1)
    return jnp.concatenate([vt, jnp.ones((ONES_ROWS, vt.shape[1]), F32)], axis=0).astype(BF16)


def _diff_kernel(*refs, has_ctx, lam_init, heads, tq):
    if has_ctx:
        q_ref, k_ref, v_ref, kc_ref, vc_ref, lam_ref, g_ref, o_ref = refs
    else:
        q_ref, k_ref, v_ref, lam_ref, g_ref, o_ref = refs
    t = q_ref.shape[0]
    lv = lam_ref[...]
    lam = (jnp.exp(jnp.sum(lv[0:1] * lv[1:2], axis=-1, keepdims=True))
           - jnp.exp(jnp.sum(lv[2:3] * lv[3:4], axis=-1, keepdims=True)) + lam_init)
    lane = lax.broadcasted_iota(jnp.int32, (tq, LANE), 1)
    blocks = [(hh, i) for hh in range(heads) for i in range(t // tq)]
    kv = {}

    def scores(hh, i):
        cols = slice(hh * LANE, (hh + 1) * LANE)
        if hh not in kv:
            k_parts = [k_ref[:, cols]] + ([kc_ref[...]] if has_ctx else [])
            v_parts = [v_ref[:, cols]] + ([vc_ref[...]] if has_ctx else [])
            kv[hh] = (jnp.concatenate(k_parts, axis=0).astype(BF16), _values_t(v_parts))
        q = q_ref[i * tq:(i + 1) * tq, cols] * (QK_SCALE * LOG2E)
        qs = jnp.concatenate([jnp.where(lane < HEAD_DIM, q, 0.0),
                              jnp.where(lane >= HEAD_DIM, q, 0.0)], axis=0).astype(BF16)
        return lax.dot_general(kv[hh][0], qs, NT_DIMS, preferred_element_type=F32)

    def finish(hh, i, et):
        ot = _dot(kv[hh][1], et)
        o = ot[0:LANE] * (1.0 / ot[LANE:LANE + 1])
        od = o[:, 0:tq] - lam * o[:, tq:2 * tq]
        yt = od * lax.rsqrt(jnp.mean(od * od, axis=0, keepdims=True) + LN_EPS)
        o_ref[i * tq:(i + 1) * tq, hh * LANE:(hh + 1) * LANE] = yt.T * g_ref[...] * (1.0 - lam_init)

    st, et = {}, {}
    for n in range(len(blocks) + 2):
        if n < len(blocks):
            st[n] = scores(*blocks[n])
        if 1 <= n <= len(blocks):
            s = st.pop(n - 1)
            et[n - 1] = jnp.exp2(s - s.max(axis=0, keepdims=True)).astype(BF16)
        if n >= 2:
            finish(*blocks[n - 2], et.pop(n - 2))


def _diff_call(z3, cache_k4, cache_v4, layer, diff_lam_l, diff_norm_g_l, lam_init, latent):
    b, t, _ = z3.shape
    has_ctx = latent
    heads = 1 if latent else DA_HEADS
    w = heads * LANE
    in_specs = [
        pl.BlockSpec((None, t, w), lambda bi, h: (bi, 0, COL_AQ // heads + h)),
        pl.BlockSpec((None, t, w), lambda bi, h: (bi, 0, COL_AK // heads + h)),
        pl.BlockSpec((None, t, w), lambda bi, h: (bi, 0, COL_AV // heads + h)),
    ]
    args = [z3, z3, z3]
    if has_ctx:
        past = cache_k4.shape[2]
        in_specs += [
            pl.BlockSpec((None, None, past, LANE), lambda bi, h: (bi, layer, 0, h)),
            pl.BlockSpec((None, None, past, LANE), lambda bi, h: (bi, layer, 0, h)),
        ]
        args += [cache_k4, cache_v4]
    in_specs += [
        pl.BlockSpec((4, HEAD_DIM), lambda bi, h: (0, 0)),
        pl.BlockSpec((1, LANE), lambda bi, h: (0, 0)),
    ]
    args += [diff_lam_l, diff_norm_g_l]
    tq = 256
    nk = t + (cache_k4.shape[2] if has_ctx else 0)
    return pl.pallas_call(
        functools.partial(_diff_kernel, has_ctx=has_ctx, lam_init=lam_init, heads=heads, tq=tq),
        grid=(b, DA_HEADS // heads),
        in_specs=in_specs,
        out_specs=pl.BlockSpec((None, t, w), lambda bi, h: (bi, 0, h)),
        out_shape=jax.ShapeDtypeStruct((b, t, BRANCH_W), F32),
        compiler_params=_params(),
        name="diff_lat" if latent else "diff_ctx",
    )(*args)


def _win_kernel(*refs, latent, t):
    if latent:
        q_ref, k_ref, v_ref, kc_ref, vc_ref, sink_ref, o_ref = refs
    else:
        q_ref, k_ref, v_ref, sink_ref, o_ref = refs
    w = WINDOW
    if latent:
        tqb, heads_per_chain = w, WG_GROUP
    else:
        tqb, heads_per_chain = t, 2
    nb = t // tqb
    kb = k_ref[...].astype(BF16)
    vt = v_ref[...].T
    if latent:
        kcb = kc_ref[...].astype(BF16)
        vct = vc_ref[...].T
        jj = lax.broadcasted_iota(jnp.int32, (w, w), 0)
        ii = lax.broadcasted_iota(jnp.int32, (w, w), 1)
        bias_prev = jnp.concatenate([jnp.where(jj >= ii, 0.0, -1e30)] * heads_per_chain, axis=1)
        bias_next = jnp.concatenate([jnp.where(jj <= ii, 0.0, -1e30)] * heads_per_chain, axis=1)
    chains = [(kv, h0, n)
              for n in range(nb)
              for kv in range(WG_KV_HEADS)
              for h0 in range(kv * WG_GROUP, (kv + 1) * WG_GROUP, heads_per_chain)]
    sinks = {}

    def sink_row(h0):
        if h0 not in sinks:
            sinks[h0] = jnp.concatenate(
                [jnp.broadcast_to(sink_ref[:, h:h + 1] * LOG2E, (1, tqb))
                 for h in range(h0, h0 + heads_per_chain)], axis=1)
        return sinks[h0]

    def key_blocks(n):
        return (max(n - 1, 0), min(n + 1, nb - 1)) if latent else (0, 0)

    def scores(kv, h0, n):
        lo = kv * HEAD_DIM
        rows = slice(n * tqb, (n + 1) * tqb)
        q_g = (jnp.concatenate([q_ref[rows, h * HEAD_DIM:(h + 1) * HEAD_DIM]
                                for h in range(h0, h0 + heads_per_chain)], axis=0)
               * (QK_SCALE * LOG2E)).astype(BF16)
        b0, b1 = key_blocks(n)
        keys = kb[b0 * tqb:(b1 + 1) * tqb, lo:lo + HEAD_DIM]
        if latent:
            keys = jnp.concatenate([keys, kcb[:, lo:lo + HEAD_DIM]], axis=0)
        st = lax.dot_general(keys, q_g, NT_DIMS, preferred_element_type=F32)
        if latent:
            parts = []
            for blk in range(b0, b1 + 1):
                part = st[(blk - b0) * w:(blk - b0 + 1) * w]
                if blk == n - 1:
                    part = part + bias_prev
                elif blk == n + 1:
                    part = part + bias_next
                parts.append(part)
            parts.append(st[(b1 - b0 + 1) * w:])
            st = jnp.concatenate(parts, axis=0)
        return st

    def softmax_numerator(st, h0):
        m = jnp.maximum(st.max(axis=0, keepdims=True), sink_row(h0))
        return jnp.exp2(st - m).astype(BF16), m

    def finish(kv, h0, n, et, m):
        lo = kv * HEAD_DIM
        b0, b1 = key_blocks(n)
        vals = [vt[lo:lo + HEAD_DIM, b0 * tqb:(b1 + 1) * tqb]]
        if latent:
            vals.append(vct[lo:lo + HEAD_DIM])
        vals = jnp.concatenate(vals, axis=1)
        vals = jnp.concatenate([vals, jnp.ones((ONES_ROWS, vals.shape[1]), F32)], axis=0).astype(BF16)
        ot = _dot(vals, et)
        d = ot[HEAD_DIM:HEAD_DIM + 1] + jnp.exp2(sink_row(h0) - m)
        on = ot[0:HEAD_DIM] * (1.0 / d)
        for p in range(heads_per_chain // 2):
            pair = jnp.concatenate([on[:, (2 * p) * tqb:(2 * p + 1) * tqb],
                                    on[:, (2 * p + 1) * tqb:(2 * p + 2) * tqb]], axis=0)
            c0 = (h0 // 2 + p) * LANE
            o_ref[n * tqb:(n + 1) * tqb, c0:c0 + LANE] = pair.T

    st, et = {}, {}
    for c in range(len(chains) + 2):
        if c < len(chains):
            st[c] = scores(*chains[c])
        if 1 <= c <= len(chains):
            et[c - 1] = softmax_numerator(st.pop(c - 1), chains[c - 1][1])
        if c >= 2:
            finish(*chains[c - 2], *et.pop(c - 2))


def _win_call(z3, cache_k4, cache_v4, layer, sink_l, latent):
    b, t, _ = z3.shape
    in_specs = [
        pl.BlockSpec((None, t, BRANCH_W), lambda bi: (bi, 0, COL_BQ // 4)),
        pl.BlockSpec((None, t, LANE), lambda bi: (bi, 0, COL_BK)),
        pl.BlockSpec((None, t, LANE), lambda bi: (bi, 0, COL_BV)),
    ]
    args = [z3, z3, z3]
    if latent:
        past = cache_k4.shape[2]
        in_specs += [
            pl.BlockSpec((None, None, past, LANE), lambda bi: (bi, layer, 0, 0)),
            pl.BlockSpec((None, None, past, LANE), lambda bi: (bi, layer, 0, 0)),
        ]
        args += [cache_k4, cache_v4]
    in_specs.append(pl.BlockSpec((1, WG_Q_HEADS), lambda bi: (0, 0)))
    args.append(sink_l)
    return pl.pallas_call(
        functools.partial(_win_kernel, latent=latent, t=t),
        grid=(b,),
        in_specs=in_specs,
        out_specs=pl.BlockSpec((None, t, BRANCH_W), lambda bi: (bi, 0, 0)),
        out_shape=jax.ShapeDtypeStruct((b, t, BRANCH_W), F32),
        compiler_params=_params(),
        name="win_lat" if latent else "win_ctx",
    )(*args)


def _log_sigmoid(x):
    return jnp.minimum(x, 0.0) - jnp.log1p(jnp.exp(-jnp.abs(x)))


def _ret_kernel(*refs, latent, t):
    if latent:
        q_ref, k_ref, v_ref, cg_ref, dec_ref, g_ref, s0_ref, o_ref, d_scr = refs
    else:
        q_ref, k_ref, v_ref, cg_ref, dec_ref, g_ref, o_ref, sfin_ref, d_scr = refs
    tq = min(t, 512)
    lg = _log_sigmoid(dec_ref[...])

    @pl.when(pl.program_id(1) == 0)
    def _():
        spos = lax.broadcasted_iota(jnp.int32, (t, t), 0)
        tpos = lax.broadcasted_iota(jnp.int32, (t, t), 1)
        diff = (tpos - spos).astype(F32)
        for j in range(2):
            lgf, lgb = lg[0:1, j:j + 1], lg[1:2, j:j + 1]
            d_scr[j, 0:t, :] = (jnp.where(diff >= 0, jnp.exp(jnp.maximum(diff, 0.0) * lgf), 0.0)
                                + jnp.where(diff <= 0, jnp.exp(jnp.maximum(-diff, 0.0) * lgb), 0.0))
            if latent:
                tp = lax.broadcasted_iota(jnp.int32, (RT_DK, t), 1).astype(F32)
                d_scr[j, t:t + RT_DK, :] = jnp.exp((tp + 1.0) * lgf)
                d_scr[j, t + RT_DK:t + 2 * RT_DK, :] = jnp.exp((float(t) - tp) * lgb)

    if latent:
        eye = (lax.broadcasted_iota(jnp.int32, (RT_DK, RT_DK), 0)
               == lax.broadcasted_iota(jnp.int32, (RT_DK, RT_DK), 1)).astype(F32)
    heads = {}

    def head_operands(j):
        if j not in heads:
            kf = k_ref[:, j * RT_DK:(j + 1) * RT_DK] * (RT_DK ** -0.5)
            v = v_ref[:, j * RT_DV:(j + 1) * RT_DV]
            keys, vals_t = [kf], [v.T]
            if latent:
                keys += [eye, eye]
                vals_t.append(jnp.concatenate([s0_ref[0, j], s0_ref[1, j]], axis=0).T)
            else:
                sp = lax.broadcasted_iota(jnp.int32, (t, 1), 0).astype(F32)
                zf = jnp.exp((float(t) - 1.0 - sp) * lg[0:1, j:j + 1])
                zb = jnp.exp(sp * lg[1:2, j:j + 1])
                vb = v.astype(BF16)
                sfin_ref[0, j] = lax.dot_general((kf * zf).astype(BF16), vb, TN_DIMS,
                                                 preferred_element_type=F32)
                sfin_ref[1, j] = lax.dot_general((kf * zb).astype(BF16), vb, TN_DIMS,
                                                 preferred_element_type=F32)
            heads[j] = (jnp.concatenate(keys, axis=0).astype(BF16),
                        jnp.concatenate(vals_t, axis=1).astype(BF16))
        return heads[j]

    chains = [(j, i) for j in range(2) for i in range(t // tq)]

    def scores(j, i):
        q = q_ref[i * tq:(i + 1) * tq, j * RT_DK:(j + 1) * RT_DK].astype(BF16)
        return lax.dot_general(head_operands(j)[0], q, NT_DIMS, preferred_element_type=F32)

    def finish(j, i, at):
        yt = _dot(head_operands(j)[1], at)
        mu = jnp.mean(yt, axis=0, keepdims=True)
        yc = yt - mu
        var = jnp.mean(yc * yc, axis=0, keepdims=True)
        yn = (yc * lax.rsqrt(var + LN_EPS)).T * g_ref[...]
        cg = cg_ref[i * tq:(i + 1) * tq, j * RT_DV:(j + 1) * RT_DV]
        o_ref[i * tq:(i + 1) * tq, j * RT_DV:(j + 1) * RT_DV] = yn * (cg * jax.nn.sigmoid(cg))

    st, at = {}, {}
    for c in range(len(chains) + 2):
        if c < len(chains):
            st[c] = scores(*chains[c])
        if 1 <= c <= len(chains):
            j, i = chains[c - 1]
            at[c - 1] = (st.pop(c - 1) * d_scr[j, :, i * tq:(i + 1) * tq]).astype(BF16)
        if c >= 2:
            finish(*chains[c - 2], at.pop(c - 2))


def _ret_call(z3, state6, layer, decay_l, ret_norm_g_l, latent):
    b, t, _ = z3.shape
    hp_n = RT_HEADS // 2
    dec3 = decay_l.reshape(2, hp_n, 2).transpose(1, 0, 2)
    in_specs = [
        pl.BlockSpec((None, t, LANE), lambda hp, bi: (bi, 0, COL_CQ + hp)),
        pl.BlockSpec((None, t, LANE), lambda hp, bi: (bi, 0, COL_CK + hp)),
        pl.BlockSpec((None, t, 2 * LANE), lambda hp, bi: (bi, 0, COL_CV // 2 + hp)),
        pl.BlockSpec((None, t, 2 * LANE), lambda hp, bi: (bi, 0, COL_CG // 2 + hp)),
        pl.BlockSpec((None, 2, 2), lambda hp, bi: (hp, 0, 0)),
        pl.BlockSpec((1, RT_DV), lambda hp, bi: (0, 0)),
    ]
    args = [z3, z3, z3, z3, dec3, ret_norm_g_l]
    o_spec = pl.BlockSpec((None, t, 2 * LANE), lambda hp, bi: (bi, 0, hp))
    o_shape = jax.ShapeDtypeStruct((b, t, BRANCH_W), F32)
    if latent:
        in_specs.append(pl.BlockSpec((None, None, 2, 2, RT_DK, RT_DV),
                                     lambda hp, bi: (bi, layer, 0, hp, 0, 0)))
        args.append(state6)
        out_specs, out_shape = o_spec, o_shape
    else:
        out_specs = [o_spec, pl.BlockSpec((None, 2, 2, RT_DK, RT_DV),
                                          lambda hp, bi: (bi, 0, hp, 0, 0))]
        out_shape = [o_shape, jax.ShapeDtypeStruct((b, 2, RT_HEADS, RT_DK, RT_DV), F32)]
    n_keys = t + (2 * RT_DK if latent else 0)
    return pl.pallas_call(
        functools.partial(_ret_kernel, latent=latent, t=t),
        grid=(hp_n, b),
        in_specs=in_specs,
        out_specs=out_specs,
        out_shape=out_shape,
        scratch_shapes=[pltpu.VMEM((2, n_keys, t), F32)],
        compiler_params=_params(),
        name="ret_lat" if latent else "ret_ctx",
    )(*args)


def _layer_norm(x, g, b):
    mu = jnp.mean(x, axis=-1, keepdims=True)
    xc = x - mu
    var = jnp.mean(xc * xc, axis=-1, keepdims=True)
    return xc * lax.rsqrt(var + LN_EPS) * g + b


def _tail_kernel(xc_ref, xl_ref, mod_ref, oac_ref, obc_ref, occ_ref, oal_ref, obl_ref, ocl_ref,
                 wg_ref, bg_ref, wpa_ref, wpb_ref, wpc_ref, wo_ref, ln1g_ref, ln1b_ref,
                 w1_ref, w2_ref, ln2g_ref, ln2b_ref, yc_ref, yl_ref, *, n_ctx_tiles):
    d = D_MODEL
    is_ctx = pl.program_id(0) < n_ctx_tiles
    x = jnp.where(is_ctx, xc_ref[...], xl_ref[...])
    sh1, sc1, g1 = mod_ref[:, 0:d], mod_ref[:, d:2 * d], mod_ref[:, 2 * d:3 * d]
    sh2, sc2, g2 = mod_ref[:, 3 * d:4 * d], mod_ref[:, 4 * d:5 * d], mod_ref[:, 5 * d:6 * d]
    h1 = (x * (1.0 + sc1) + sh1).astype(BF16)
    merged = None
    branches = ((oac_ref, oal_ref, wpa_ref), (obc_ref, obl_ref, wpb_ref), (occ_ref, ocl_ref, wpc_ref))
    for i, (oc_ref_i, ol_ref_i, wp_ref) in enumerate(branches):
        o = jnp.where(is_ctx, oc_ref_i[...], ol_ref_i[...]).astype(BF16)
        gate = jax.nn.sigmoid(_dot(h1, wg_ref[:, i * d:(i + 1) * d]) + bg_ref[:, i * d:(i + 1) * d])
        part = gate * _dot(o, wp_ref[...])
        merged = part if merged is None else merged + part
    y = _dot(merged.astype(BF16), wo_ref[...])
    x1 = _layer_norm(ALPHA * x + g1 * y, ln1g_ref[...], ln1b_ref[...])
    h2 = (x1 * (1.0 + sc2) + sh2).astype(BF16)
    u = jnp.maximum(_dot(h2, w1_ref[...]), 0.0)
    f = _dot((u * u).astype(BF16), w2_ref[...])
    x2 = _layer_norm(ALPHA * x1 + g2 * f, ln2g_ref[...], ln2b_ref[...])

    @pl.when(is_ctx)
    def _():
        yc_ref[...] = x2

    @pl.when(jnp.logical_not(is_ctx))
    def _():
        yl_ref[...] = x2


def _tail_call(xc, xl, mod4, layer, o_ctx, o_lat, wg, bg, wpa, wpb, wpc, wo, ln1, w1, w2, ln2,
               t_lat, tm=256):
    n_ctx, n_lat = xc.shape[0] // tm, xl.shape[0] // tm
    lat_tiles_per_seq = t_lat // tm

    def mod_idx(i):
        row = jnp.where(i < n_ctx, 0, 1 + (i - n_ctx) // lat_tiles_per_seq)
        return (layer, row, 0, 0)

    ctx_spec = lambda w: pl.BlockSpec((tm, w), lambda i: (jnp.minimum(i, n_ctx - 1), 0))
    lat_spec = lambda w: pl.BlockSpec((tm, w), lambda i: (jnp.maximum(i - n_ctx, 0), 0))
    whole = lambda r, c: _resident((r, c), lambda i: (0, 0))
    return pl.pallas_call(
        functools.partial(_tail_kernel, n_ctx_tiles=n_ctx),
        grid=(n_ctx + n_lat,),
        in_specs=[
            ctx_spec(D_MODEL), lat_spec(D_MODEL),
            pl.BlockSpec((None, None, 1, D_MOD), mod_idx),
            ctx_spec(BRANCH_W), ctx_spec(BRANCH_W), ctx_spec(BRANCH_W),
            lat_spec(BRANCH_W), lat_spec(BRANCH_W), lat_spec(BRANCH_W),
            whole(D_MODEL, D_GATE), whole(1, D_GATE),
            whole(BRANCH_W, D_MODEL), whole(BRANCH_W, D_MODEL), whole(BRANCH_W, D_MODEL),
            whole(D_MODEL, D_MODEL), whole(1, D_MODEL), whole(1, D_MODEL),
            whole(D_MODEL, D_FF), whole(D_FF, D_MODEL), whole(1, D_MODEL), whole(1, D_MODEL),
        ],
        out_specs=[ctx_spec(D_MODEL), lat_spec(D_MODEL)],
        out_shape=[jax.ShapeDtypeStruct(xc.shape, F32), jax.ShapeDtypeStruct(xl.shape, F32)],
        compiler_params=_params(),
        name="tail",
    )(xc, xl, mod4, *o_ctx, *o_lat, wg, bg, wpa, wpb, wpc, wo, *ln1, w1, w2, *ln2)


def _rope_tables(n_tokens):
    rows = n_tokens // GRID_W
    r, col = jnp.meshgrid(jnp.arange(rows), jnp.arange(GRID_W), indexing="ij")
    r = r.reshape(-1).astype(F32)
    col = col.reshape(-1).astype(F32)
    nf = HEAD_DIM // 4
    inv = ROPE_BASE ** (-jnp.arange(nf, dtype=F32) / nf)
    ang_r = r[:, None] * inv[None, :]
    ang_c = col[:, None] * inv[None, :]
    zero = jnp.zeros_like(ang_r)
    cos = jnp.concatenate([jnp.cos(ang_r)] * 2 + [jnp.cos(ang_c)] * 2, axis=-1)
    s_next = jnp.concatenate([-jnp.sin(ang_r), zero, -jnp.sin(ang_c), zero], axis=-1)
    s_prev = jnp.concatenate([zero, jnp.sin(ang_r), zero, jnp.sin(ang_c)], axis=-1)
    return tuple(jnp.tile(t, (1, LANE // HEAD_DIM)) for t in (cos, s_next, s_prev))


def kernel(x_prompt, x_sample, c, cache_diff_k, cache_diff_v, cache_win_k, cache_win_v, state_ret,
           c_ctx, w_mod, b_mod, w_in, diff_lam, diff_norm_g, win_sink, ret_decay, ret_norm_g,
           w_pa, w_pb, w_pc, w_gate, b_gate, w_o, ln1_g, ln1_b, w_ff1, w_ff2, ln2_g, ln2_b):
    bp, tp, d = x_prompt.shape
    bs, ts, _ = x_sample.shape
    assert d == D_MODEL and w_in.shape == (DEPTH, D_MODEL, D_IN) and c.shape[0] + 1 <= MOD_ROWS
    past = cache_diff_k.shape[2]

    c_rows = jnp.concatenate(
        [c_ctx[None, :], c, jnp.zeros((MOD_ROWS - 1 - bs, d), F32)], axis=0)
    mod4 = _mod_call(c_rows, w_mod, b_mod).reshape(DEPTH, MOD_ROWS, 1, D_MOD)
    rope_tabs = _rope_tables(ts)

    ck_a = cache_diff_k.reshape(bs, DEPTH, past, DA_HEADS * 2 * HEAD_DIM)
    cv_a = cache_diff_v.reshape(bs, DEPTH, past, DA_HEADS * 2 * HEAD_DIM)
    ck_b = cache_win_k.reshape(bs, DEPTH, past, WG_KV_HEADS * HEAD_DIM)
    cv_b = cache_win_v.reshape(bs, DEPTH, past, WG_KV_HEADS * HEAD_DIM)

    xp = x_prompt.reshape(bp * tp, d)
    xs = x_sample.reshape(bs * ts, d)
    diff_k, diff_v, win_k, win_v, ret_s = [], [], [], [], []
    for l in range(DEPTH):
        lam_init = 0.8 - 0.6 * math.exp(-0.3 * l)
        w_in_b = w_in[l].astype(BF16)
        w_gate_b = w_gate[l].astype(BF16)
        wpa, wpb, wpc = w_pa[l].astype(BF16), w_pb[l].astype(BF16), w_pc[l].astype(BF16)
        wo = w_o[l].astype(BF16)
        w1, w2 = w_ff1[l].astype(BF16), w_ff2[l].astype(BF16)
        b_gate_l = b_gate[l].reshape(1, D_GATE)
        lam_l = diff_lam[l]
        dng_l = diff_norm_g[l].reshape(1, LANE)
        sink_l = win_sink[l].reshape(1, WG_Q_HEADS)
        rng_l = ret_norm_g[l].reshape(1, RT_DV)
        ln1 = (ln1_g[l].reshape(1, d), ln1_b[l].reshape(1, d))
        ln2 = (ln2_g[l].reshape(1, d), ln2_b[l].reshape(1, d))

        zc, zl, dk, dv, wk, wv = _proj_call(xp, xs, mod4, l, w_in_b, rope_tabs, tp, ts)
        diff_k.append(dk)
        diff_v.append(dv)
        win_k.append(wk)
        win_v.append(wv)
        o_paths = []
        for latent, z, b, t in ((False, zc, bp, tp), (True, zl, bs, ts)):
            z3 = z.reshape(b, t, D_IN)
            oa = _diff_call(z3, ck_a, cv_a, l, lam_l, dng_l, lam_init, latent)
            ob = _win_call(z3, ck_b, cv_b, l, sink_l, latent)
            if latent:
                oc = _ret_call(z3, state_ret, l, ret_decay[l], rng_l, latent)
            else:
                oc, s_fin = _ret_call(z3, state_ret, l, ret_decay[l], rng_l, latent)
                ret_s.append(s_fin)
            o_paths.append([o.reshape(b * t, BRANCH_W) for o in (oa, ob, oc)])
        xp, xs = _tail_call(xp, xs, mod4, l, o_paths[0], o_paths[1], w_gate_b, b_gate_l,
                            wpa, wpb, wpc, wo, ln1, w1, w2, ln2, ts)
    return (xp.reshape(bp, tp, d), xs.reshape(bs, ts, d),
            jnp.stack(diff_k, axis=1), jnp.stack(diff_v, axis=1),
            jnp.stack(win_k, axis=1), jnp.stack(win_v, axis=1), jnp.stack(ret_s, axis=1))
```

```python
import functools
import math

import jax
import jax.numpy as jnp
from jax import lax
from jax.experimental import pallas as pl
from jax.experimental.pallas import tpu as pltpu

F32 = jnp.float32
BF16 = jnp.bfloat16

D_MODEL = 1024
DEPTH = 2
GRID_W = 64
HEAD_DIM = 64
DA_HEADS = 4
WG_Q_HEADS = 8
WG_KV_HEADS = 2
WG_GROUP = WG_Q_HEADS // WG_KV_HEADS
WINDOW = 128
RT_HEADS = 4
RT_DK = 64
RT_DV = 128
BRANCH_W = 512
D_IN = 3840
D_GATE = 3 * D_MODEL
D_FF = 4 * D_MODEL
D_MOD = 6 * D_MODEL
ROPE_BASE = 10000.0
LN_EPS = 1e-5
ALPHA = (2 * DEPTH) ** 0.25
QK_SCALE = HEAD_DIM ** -0.5
LOG2E = math.log2(math.e)

LANE = 128
MOD_ROWS = 8
ONES_ROWS = 16

COL_AQ, COL_AK, COL_AV = 0, 4, 8
COL_BQ, COL_BK, COL_BV = 12, 16, 17
COL_CQ, COL_CK, COL_CV, COL_CG = 18, 20, 22, 26
ROPE_BLOCKS = tuple(range(0, 8)) + tuple(range(12, 17))

VMEM_LIMIT = 56 * 1024 * 1024
NT_DIMS = (((1,), (1,)), ((), ()))
TN_DIMS = (((0,), (0,)), ((), ()))


def _params():
    return pltpu.CompilerParams(vmem_limit_bytes=VMEM_LIMIT)


def _resident(shape, index_map):
    return pl.BlockSpec(shape, index_map, pipeline_mode=pl.Buffered(1))


def _dot(a, b):
    return jnp.dot(a, b, preferred_element_type=F32)


def _mod_kernel(c_ref, w_ref, b_ref, o_ref):
    c = c_ref[...]
    a = (c * jax.nn.sigmoid(c)).astype(BF16)
    o_ref[...] = _dot(a, w_ref[...].astype(BF16)) + b_ref[...]


def _mod_call(c_rows, w_mod, b_mod):
    tn = 1536
    return pl.pallas_call(
        _mod_kernel,
        grid=(DEPTH, D_MOD // tn),
        in_specs=[
            pl.BlockSpec((MOD_ROWS, D_MODEL), lambda l, n: (0, 0)),
            pl.BlockSpec((None, D_MODEL, tn), lambda l, n: (l, 0, n)),
            pl.BlockSpec((None, 1, tn), lambda l, n: (l, 0, n)),
        ],
        out_specs=pl.BlockSpec((None, MOD_ROWS, tn), lambda l, n: (l, 0, n)),
        out_shape=jax.ShapeDtypeStruct((DEPTH, MOD_ROWS, D_MOD), F32),
        compiler_params=_params(),
        name="mod_vectors",
    )(c_rows, w_mod, b_mod.reshape(DEPTH, 1, D_MOD))


def _proj_kernel(xl_ref, xc_ref, mod_ref, win_ref, cos_ref, sa_ref, sb_ref,
                 dk_in, dv_in, wk_in, wv_in, z_ref, dk_ref, dv_ref, wk_ref, wv_ref, *, n_lat_tiles):
    del dk_in, dv_in, wk_in, wv_in
    is_lat = pl.program_id(0) < n_lat_tiles
    x = jnp.where(is_lat, xl_ref[...], xc_ref[...])
    sh1 = mod_ref[:, 0:D_MODEL]
    sc1 = mod_ref[:, D_MODEL:2 * D_MODEL]
    h = (x * (1.0 + sc1) + sh1).astype(BF16)
    seqs = dk_ref.shape[0]
    t_ctx = wk_ref.shape[-1]
    nc = 768
    for c0 in range(0, D_IN, nc):
        z = _dot(h, win_ref[:, c0:c0 + nc])
        for j in range(nc // LANE):
            blk = c0 // LANE + j
            u = z[:, j * LANE:(j + 1) * LANE]
            if blk in ROPE_BLOCKS:
                u = (u * cos_ref[...] + pltpu.roll(u, LANE - 16, 1) * sa_ref[...]
                     + pltpu.roll(u, 16, 1) * sb_ref[...])
            z_ref[:, blk * LANE:(blk + 1) * LANE] = u
            if COL_AK <= blk < COL_BQ:
                ref, head = (dk_ref, blk - COL_AK) if blk < COL_AV else (dv_ref, blk - COL_AV)
                for s in range(seqs):
                    ref[s, pl.ds(head, t_ctx, stride=DA_HEADS), :] = u[s * t_ctx:(s + 1) * t_ctx]
            elif blk in (COL_BK, COL_BV):
                ref = wk_ref if blk == COL_BK else wv_ref
                ut = u.T
                for s in range(seqs):
                    ref[s] = ut[:, s * t_ctx:(s + 1) * t_ctx]


def _proj_call(xl, xc, mod4, layer, w_in_b, rope_tabs, caches, t_lat, t_ctx, tm=512):
    n_lat, n_ctx = xl.shape[0] // tm, xc.shape[0] // tm
    lat_tiles_per_seq = t_lat // tm
    seqs = tm // t_ctx

    def mod_idx(i):
        row = jnp.where(i < n_lat, 1 + i // lat_tiles_per_seq, 0)
        return (layer, row, 0, 0)

    lat_idx = lambda i: jnp.minimum(i, n_lat - 1)
    ctx_idx = lambda i: jnp.maximum(i - n_lat, 0)
    rope_spec = pl.BlockSpec((tm, LANE), lambda i: (jnp.where(i < n_lat, 1 + i % lat_tiles_per_seq, 0), 0))
    cache_a = pl.BlockSpec((seqs, None, DA_HEADS * t_ctx, 2 * HEAD_DIM), lambda i: (ctx_idx(i), layer, 0, 0))
    cache_b = pl.BlockSpec((seqs, None, WG_KV_HEADS * HEAD_DIM, t_ctx), lambda i: (ctx_idx(i), layer, 0, 0))
    any_spec = pl.BlockSpec(memory_space=pl.ANY)
    return pl.pallas_call(
        functools.partial(_proj_kernel, n_lat_tiles=n_lat),
        grid=(n_lat + n_ctx,),
        in_specs=[
            pl.BlockSpec((tm, D_MODEL), lambda i: (lat_idx(i), 0)),
            pl.BlockSpec((tm, D_MODEL), lambda i: (ctx_idx(i), 0)),
            pl.BlockSpec((None, None, 1, D_MOD), mod_idx),
            _resident((D_MODEL, D_IN), lambda i: (0, 0)),
            rope_spec, rope_spec, rope_spec,
            any_spec, any_spec, any_spec, any_spec,
        ],
        out_specs=[pl.BlockSpec((tm, D_IN), lambda i: (i, 0)), cache_a, cache_a, cache_b, cache_b],
        out_shape=[jax.ShapeDtypeStruct((xl.shape[0] + xc.shape[0], D_IN), F32)]
        + [jax.ShapeDtypeStruct(a.shape, F32) for a in caches],
        input_output_aliases={7: 1, 8: 2, 9: 3, 10: 4},
        compiler_params=_params(),
        name="proj",
    )(xl, xc, mod4, w_in_b, *rope_tabs, *caches)


def _values_t(v_parts):
    vt = jnp.concatenate([v.T for v in v_parts], axis=1)
    return jnp.concatenate([vt, jnp.ones((ONES_ROWS, vt.shape[1]), F32)], axis=0).astype(BF16)


def _diff_kernel(*refs, has_ctx, lam_init, heads, tq):
    if has_ctx:
        q_ref, k_ref, v_ref, kc_ref, vc_ref, lam_ref, g_ref, o_ref = refs
    else:
        q_ref, k_ref, v_ref, lam_ref, g_ref, o_ref = refs
    t = q_ref.shape[0]
    lv = lam_ref[...]
    lam = (jnp.exp(jnp.sum(lv[0:1] * lv[1:2], axis=-1, keepdims=True))
           - jnp.exp(jnp.sum(lv[2:3] * lv[3:4], axis=-1, keepdims=True)) + lam_init)
    lane = lax.broadcasted_iota(jnp.int32, (tq, LANE), 1)
    blocks = [(hh, i) for hh in range(heads) for i in range(t // tq)]
    kv = {}

    def scores(hh, i):
        cols = slice(hh * LANE, (hh + 1) * LANE)
        if hh not in kv:
            k_parts = [k_ref[:, cols]] + ([kc_ref[...]] if has_ctx else [])
            v_parts = [v_ref[:, cols]] + ([vc_ref[...]] if has_ctx else [])
            kv[hh] = (jnp.concatenate(k_parts, axis=0).astype(BF16), _values_t(v_parts))
        q = q_ref[i * tq:(i + 1) * tq, cols] * (QK_SCALE * LOG2E)
        qs = jnp.concatenate([jnp.where(lane < HEAD_DIM, q, 0.0),
                              jnp.where(lane >= HEAD_DIM, q, 0.0)], axis=0).astype(BF16)
        return lax.dot_general(kv[hh][0], qs, NT_DIMS, preferred_element_type=F32)

    def finish(hh, i, et):
        ot = _dot(kv[hh][1], et)
        o = ot[0:LANE] * (1.0 / ot[LANE:LANE + 1])
        od = o[:, 0:tq] - lam * o[:, tq:2 * tq]
        yt = od * lax.rsqrt(jnp.mean(od * od, axis=0, keepdims=True) + LN_EPS)
        o_ref[i * tq:(i + 1) * tq, hh * LANE:(hh + 1) * LANE] = yt.T * g_ref[...] * (1.0 - lam_init)

    st, et = {}, {}
    for n in range(len(blocks) + 2):
        if n < len(blocks):
            st[n] = scores(*blocks[n])
        if 1 <= n <= len(blocks):
            s = st.pop(n - 1)
            et[n - 1] = jnp.exp2(s - s.max(axis=0, keepdims=True)).astype(BF16)
        if n >= 2:
            finish(*blocks[n - 2], et.pop(n - 2))


def _diff_call(z3, cache_k4, cache_v4, layer, diff_lam_l, diff_norm_g_l, lam_init, latent):
    z3, b0, b = z3
    t = z3.shape[1]
    has_ctx = latent
    heads = 1 if latent else DA_HEADS
    w = heads * LANE
    in_specs = [
        pl.BlockSpec((None, t, w), lambda bi, h: (bi + b0, 0, COL_AQ // heads + h)),
        pl.BlockSpec((None, t, w), lambda bi, h: (bi + b0, 0, COL_AK // heads + h)),
        pl.BlockSpec((None, t, w), lambda bi, h: (bi + b0, 0, COL_AV // heads + h)),
    ]
    args = [z3, z3, z3]
    if has_ctx:
        past = cache_k4.shape[2]
        in_specs += [
            pl.BlockSpec((None, None, past, LANE), lambda bi, h: (bi, layer, 0, h)),
            pl.BlockSpec((None, None, past, LANE), lambda bi, h: (bi, layer, 0, h)),
        ]
        args += [cache_k4, cache_v4]
    in_specs += [
        pl.BlockSpec((4, HEAD_DIM), lambda bi, h: (0, 0)),
        pl.BlockSpec((1, LANE), lambda bi, h: (0, 0)),
    ]
    args += [diff_lam_l, diff_norm_g_l]
    tq = 256
    nk = t + (cache_k4.shape[2] if has_ctx else 0)
    return pl.pallas_call(
        functools.partial(_diff_kernel, has_ctx=has_ctx, lam_init=lam_init, heads=heads, tq=tq),
        grid=(b, DA_HEADS // heads),
        in_specs=in_specs,
        out_specs=pl.BlockSpec((None, t, w), lambda bi, h: (bi, 0, h)),
        out_shape=jax.ShapeDtypeStruct((b, t, BRANCH_W), F32),
        compiler_params=_params(),
        name="diff_lat" if latent else "diff_ctx",
    )(*args)


def _win_kernel(*refs, latent, t):
    if latent:
        q_ref, k_ref, v_ref, kc_ref, vc_ref, sink_ref, o_ref = refs
    else:
        q_ref, k_ref, v_ref, sink_ref, o_ref = refs
    w = WINDOW
    if latent:
        tqb, heads_per_chain = w, WG_GROUP
    else:
        tqb, heads_per_chain = t, 2
    nb = t // tqb
    kb = k_ref[...].astype(BF16)
    vt = v_ref[...].T
    if latent:
        kcb = kc_ref[...].astype(BF16)
        vct = vc_ref[...].T
        jj = lax.broadcasted_iota(jnp.int32, (w, w), 0)
        ii = lax.broadcasted_iota(jnp.int32, (w, w), 1)
        bias_prev = jnp.concatenate([jnp.where(jj >= ii, 0.0, -1e30)] * heads_per_chain, axis=1)
        bias_next = jnp.concatenate([jnp.where(jj <= ii, 0.0, -1e30)] * heads_per_chain, axis=1)
    chains = [(kv, h0, n)
              for n in range(nb)
              for kv in range(WG_KV_HEADS)
              for h0 in range(kv * WG_GROUP, (kv + 1) * WG_GROUP, heads_per_chain)]
    sinks = {}

    def sink_row(h0):
        if h0 not in sinks:
            sinks[h0] = jnp.concatenate(
                [jnp.broadcast_to(sink_ref[:, h:h + 1] * LOG2E, (1, tqb))
                 for h in range(h0, h0 + heads_per_chain)], axis=1)
        return sinks[h0]

    def key_blocks(n):
        return (max(n - 1, 0), min(n + 1, nb - 1)) if latent else (0, 0)

    def scores(kv, h0, n):
        lo = kv * HEAD_DIM
        rows = slice(n * tqb, (n + 1) * tqb)
        q_g = (jnp.concatenate([q_ref[rows, h * HEAD_DIM:(h + 1) * HEAD_DIM]
                                for h in range(h0, h0 + heads_per_chain)], axis=0)
               * (QK_SCALE * LOG2E)).astype(BF16)
        b0, b1 = key_blocks(n)
        keys = kb[b0 * tqb:(b1 + 1) * tqb, lo:lo + HEAD_DIM]
        if latent:
            keys = jnp.concatenate([keys, kcb[:, lo:lo + HEAD_DIM]], axis=0)
        st = lax.dot_general(keys, q_g, NT_DIMS, preferred_element_type=F32)
        if latent:
            parts = []
            for blk in range(b0, b1 + 1):
                part = st[(blk - b0) * w:(blk - b0 + 1) * w]
                if blk == n - 1:
                    part = part + bias_prev
                elif blk == n + 1:
                    part = part + bias_next
                parts.append(part)
            parts.append(st[(b1 - b0 + 1) * w:])
            st = jnp.concatenate(parts, axis=0)
        return st

    def softmax_numerator(st, h0):
        m = jnp.maximum(st.max(axis=0, keepdims=True), sink_row(h0))
        return jnp.exp2(st - m).astype(BF16), m

    def finish(kv, h0, n, et, m):
        lo = kv * HEAD_DIM
        b0, b1 = key_blocks(n)
        vals = [vt[lo:lo + HEAD_DIM, b0 * tqb:(b1 + 1) * tqb]]
        if latent:
            vals.append(vct[lo:lo + HEAD_DIM])
        vals = jnp.concatenate(vals, axis=1)
        vals = jnp.concatenate([vals, jnp.ones((ONES_ROWS, vals.shape[1]), F32)], axis=0).astype(BF16)
        ot = _dot(vals, et)
        d = ot[HEAD_DIM:HEAD_DIM + 1] + jnp.exp2(sink_row(h0) - m)
        on = ot[0:HEAD_DIM] * (1.0 / d)
        for p in range(heads_per_chain // 2):
            pair = jnp.concatenate([on[:, (2 * p) * tqb:(2 * p + 1) * tqb],
                                    on[:, (2 * p + 1) * tqb:(2 * p + 2) * tqb]], axis=0)
            c0 = (h0 // 2 + p) * LANE
            o_ref[n * tqb:(n + 1) * tqb, c0:c0 + LANE] = pair.T

    st, et = {}, {}
    for c in range(len(chains) + 2):
        if c < len(chains):
            st[c] = scores(*chains[c])
        if 1 <= c <= len(chains):
            et[c - 1] = softmax_numerator(st.pop(c - 1), chains[c - 1][1])
        if c >= 2:
            finish(*chains[c - 2], *et.pop(c - 2))


def _win_call(z3, cache_k4, cache_v4, layer, sink_l, latent):
    z3, b0, b = z3
    t = z3.shape[1]
    in_specs = [
        pl.BlockSpec((None, t, BRANCH_W), lambda bi: (bi + b0, 0, COL_BQ // 4)),
        pl.BlockSpec((None, t, LANE), lambda bi: (bi + b0, 0, COL_BK)),
        pl.BlockSpec((None, t, LANE), lambda bi: (bi + b0, 0, COL_BV)),
    ]
    args = [z3, z3, z3]
    if latent:
        past = cache_k4.shape[2]
        in_specs += [
            pl.BlockSpec((None, None, past, LANE), lambda bi: (bi, layer, 0, 0)),
            pl.BlockSpec((None, None, past, LANE), lambda bi: (bi, layer, 0, 0)),
        ]
        args += [cache_k4, cache_v4]
    in_specs.append(pl.BlockSpec((1, WG_Q_HEADS), lambda bi: (0, 0)))
    args.append(sink_l)
    return pl.pallas_call(
        functools.partial(_win_kernel, latent=latent, t=t),
        grid=(b,),
        in_specs=in_specs,
        out_specs=pl.BlockSpec((None, t, BRANCH_W), lambda bi: (bi, 0, 0)),
        out_shape=jax.ShapeDtypeStruct((b, t, BRANCH_W), F32),
        compiler_params=_params(),
        name="win_lat" if latent else "win_ctx",
    )(*args)


def _log_sigmoid(x):
    return jnp.minimum(x, 0.0) - jnp.log1p(jnp.exp(-jnp.abs(x)))


def _ret_kernel(*refs, latent, t):
    if latent:
        q_ref, k_ref, v_ref, cg_ref, dec_ref, g_ref, s0_ref, o_ref, d_scr = refs
    else:
        q_ref, k_ref, v_ref, cg_ref, dec_ref, g_ref, o_ref, sfin_ref, d_scr = refs
    tq = min(t, 512)
    lg = _log_sigmoid(dec_ref[...])

    @pl.when(pl.program_id(1) == 0)
    def _():
        spos = lax.broadcasted_iota(jnp.int32, (t, t), 0)
        tpos = lax.broadcasted_iota(jnp.int32, (t, t), 1)
        diff = (tpos - spos).astype(F32)
        for j in range(2):
            lgf, lgb = lg[0:1, j:j + 1], lg[1:2, j:j + 1]
            d_scr[j, 0:t, :] = (jnp.where(diff >= 0, jnp.exp(jnp.maximum(diff, 0.0) * lgf), 0.0)
                                + jnp.where(diff <= 0, jnp.exp(jnp.maximum(-diff, 0.0) * lgb), 0.0))
            if latent:
                tp = lax.broadcasted_iota(jnp.int32, (RT_DK, t), 1).astype(F32)
                d_scr[j, t:t + RT_DK, :] = jnp.exp((tp + 1.0) * lgf)
                d_scr[j, t + RT_DK:t + 2 * RT_DK, :] = jnp.exp((float(t) - tp) * lgb)

    if latent:
        eye = (lax.broadcasted_iota(jnp.int32, (RT_DK, RT_DK), 0)
               == lax.broadcasted_iota(jnp.int32, (RT_DK, RT_DK), 1)).astype(F32)
    heads = {}

    def head_operands(j):
        if j not in heads:
            kf = k_ref[:, j * RT_DK:(j + 1) * RT_DK] * (RT_DK ** -0.5)
            v = v_ref[:, j * RT_DV:(j + 1) * RT_DV]
            keys, vals_t = [kf], [v.T]
            if latent:
                keys += [eye, eye]
                vals_t.append(jnp.concatenate([s0_ref[0, j], s0_ref[1, j]], axis=0).T)
            else:
                sp = lax.broadcasted_iota(jnp.int32, (t, 1), 0).astype(F32)
                zf = jnp.exp((float(t) - 1.0 - sp) * lg[0:1, j:j + 1])
                zb = jnp.exp(sp * lg[1:2, j:j + 1])
                vb = v.astype(BF16)
                sfin_ref[0, j] = lax.dot_general((kf * zf).astype(BF16), vb, TN_DIMS,
                                                 preferred_element_type=F32)
                sfin_ref[1, j] = lax.dot_general((kf * zb).astype(BF16), vb, TN_DIMS,
                                                 preferred_element_type=F32)
            heads[j] = (jnp.concatenate(keys, axis=0).astype(BF16),
                        jnp.concatenate(vals_t, axis=1).astype(BF16))
        return heads[j]

    chains = [(j, i) for j in range(2) for i in range(t // tq)]

    def scores(j, i):
        q = q_ref[i * tq:(i + 1) * tq, j * RT_DK:(j + 1) * RT_DK].astype(BF16)
        return lax.dot_general(head_operands(j)[0], q, NT_DIMS, preferred_element_type=F32)

    def finish(j, i, at):
        yt = _dot(head_operands(j)[1], at)
        mu = jnp.mean(yt, axis=0, keepdims=True)
        yc = yt - mu
        var = jnp.mean(yc * yc, axis=0, keepdims=True)
        yn = (yc * lax.rsqrt(var + LN_EPS)).T * g_ref[...]
        cg = cg_ref[i * tq:(i + 1) * tq, j * RT_DV:(j + 1) * RT_DV]
        o_ref[i * tq:(i + 1) * tq, j * RT_DV:(j + 1) * RT_DV] = yn * (cg * jax.nn.sigmoid(cg))

    st, at = {}, {}
    for c in range(len(chains) + 2):
        if c < len(chains):
            st[c] = scores(*chains[c])
        if 1 <= c <= len(chains):
            j, i = chains[c - 1]
            at[c - 1] = (st.pop(c - 1) * d_scr[j, :, i * tq:(i + 1) * tq]).astype(BF16)
        if c >= 2:
            finish(*chains[c - 2], at.pop(c - 2))


def _ret_call(z3, state6, layer, decay_l, ret_norm_g_l, latent):
    z3, b0, b = z3
    t = z3.shape[1]
    hp_n = RT_HEADS // 2
    dec3 = decay_l.reshape(2, hp_n, 2).transpose(1, 0, 2)
    in_specs = [
        pl.BlockSpec((None, t, LANE), lambda hp, bi: (bi + b0, 0, COL_CQ + hp)),
        pl.BlockSpec((None, t, LANE), lambda hp, bi: (bi + b0, 0, COL_CK + hp)),
        pl.BlockSpec((None, t, 2 * LANE), lambda hp, bi: (bi + b0, 0, COL_CV // 2 + hp)),
        pl.BlockSpec((None, t, 2 * LANE), lambda hp, bi: (bi + b0, 0, COL_CG // 2 + hp)),
        pl.BlockSpec((None, 2, 2), lambda hp, bi: (hp, 0, 0)),
        pl.BlockSpec((1, RT_DV), lambda hp, bi: (0, 0)),
    ]
    args = [z3, z3, z3, z3, dec3, ret_norm_g_l]
    o_spec = pl.BlockSpec((None, t, 2 * LANE), lambda hp, bi: (bi, 0, hp))
    o_shape = jax.ShapeDtypeStruct((b, t, BRANCH_W), F32)
    if latent:
        in_specs.append(pl.BlockSpec((None, None, 2, 2, RT_DK, RT_DV),
                                     lambda hp, bi: (bi, layer, 0, hp, 0, 0)))
        args.append(state6)
        out_specs, out_shape = o_spec, o_shape
    else:
        out_specs = [o_spec, pl.BlockSpec((None, 2, 2, RT_DK, RT_DV),
                                          lambda hp, bi: (bi, 0, hp, 0, 0))]
        out_shape = [o_shape, jax.ShapeDtypeStruct((b, 2, RT_HEADS, RT_DK, RT_DV), F32)]
    n_keys = t + (2 * RT_DK if latent else 0)
    return pl.pallas_call(
        functools.partial(_ret_kernel, latent=latent, t=t),
        grid=(hp_n, b),
        in_specs=in_specs,
        out_specs=out_specs,
        out_shape=out_shape,
        scratch_shapes=[pltpu.VMEM((2, n_keys, t), F32)],
        compiler_params=_params(),
        name="ret_lat" if latent else "ret_ctx",
    )(*args)


def _layer_norm(x, g, b):
    mu = jnp.mean(x, axis=-1, keepdims=True)
    xc = x - mu
    var = jnp.mean(xc * xc, axis=-1, keepdims=True)
    return xc * lax.rsqrt(var + LN_EPS) * g + b


def _tail_kernel(xc_ref, xl_ref, mod_ref, oac_ref, obc_ref, occ_ref, oal_ref, obl_ref, ocl_ref,
                 wg_ref, bg_ref, wpa_ref, wpb_ref, wpc_ref, wo_ref, ln1g_ref, ln1b_ref,
                 w1_ref, w2_ref, ln2g_ref, ln2b_ref, yc_ref, yl_ref, *, n_ctx_tiles):
    d = D_MODEL
    is_ctx = pl.program_id(0) < n_ctx_tiles
    x = jnp.where(is_ctx, xc_ref[...], xl_ref[...])
    sh1, sc1, g1 = mod_ref[:, 0:d], mod_ref[:, d:2 * d], mod_ref[:, 2 * d:3 * d]
    sh2, sc2, g2 = mod_ref[:, 3 * d:4 * d], mod_ref[:, 4 * d:5 * d], mod_ref[:, 5 * d:6 * d]
    h1 = (x * (1.0 + sc1) + sh1).astype(BF16)
    merged = None
    branches = ((oac_ref, oal_ref, wpa_ref), (obc_ref, obl_ref, wpb_ref), (occ_ref, ocl_ref, wpc_ref))
    for i, (oc_ref_i, ol_ref_i, wp_ref) in enumerate(branches):
        o = jnp.where(is_ctx, oc_ref_i[...], ol_ref_i[...]).astype(BF16)
        gate = jax.nn.sigmoid(_dot(h1, wg_ref[:, i * d:(i + 1) * d]) + bg_ref[:, i * d:(i + 1) * d])
        part = gate * _dot(o, wp_ref[...])
        merged = part if merged is None else merged + part
    y = _dot(merged.astype(BF16), wo_ref[...])
    x1 = _layer_norm(ALPHA * x + g1 * y, ln1g_ref[...], ln1b_ref[...])
    h2 = (x1 * (1.0 + sc2) + sh2).astype(BF16)
    u = jnp.maximum(_dot(h2, w1_ref[...]), 0.0)
    f = _dot((u * u).astype(BF16), w2_ref[...])
    x2 = _layer_norm(ALPHA * x1 + g2 * f, ln2g_ref[...], ln2b_ref[...])

    @pl.when(is_ctx)
    def _():
        yc_ref[...] = x2

    @pl.when(jnp.logical_not(is_ctx))
    def _():
        yl_ref[...] = x2


def _tail_call(xc, xl, mod4, layer, o_ctx, o_lat, wg, bg, wpa, wpb, wpc, wo, ln1, w1, w2, ln2,
               t_lat, tm=256):
    n_ctx, n_lat = xc.shape[0] // tm, xl.shape[0] // tm
    lat_tiles_per_seq = t_lat // tm

    def mod_idx(i):
        row = jnp.where(i < n_ctx, 0, 1 + (i - n_ctx) // lat_tiles_per_seq)
        return (layer, row, 0, 0)

    ctx_spec = lambda w: pl.BlockSpec((tm, w), lambda i: (jnp.minimum(i, n_ctx - 1), 0))
    lat_spec = lambda w: pl.BlockSpec((tm, w), lambda i: (jnp.maximum(i - n_ctx, 0), 0))
    whole = lambda r, c: _resident((r, c), lambda i: (0, 0))
    return pl.pallas_call(
        functools.partial(_tail_kernel, n_ctx_tiles=n_ctx),
        grid=(n_ctx + n_lat,),
        in_specs=[
            ctx_spec(D_MODEL), lat_spec(D_MODEL),
            pl.BlockSpec((None, None, 1, D_MOD), mod_idx),
            ctx_spec(BRANCH_W), ctx_spec(BRANCH_W), ctx_spec(BRANCH_W),
            lat_spec(BRANCH_W), lat_spec(BRANCH_W), lat_spec(BRANCH_W),
            whole(D_MODEL, D_GATE), whole(1, D_GATE),
            whole(BRANCH_W, D_MODEL), whole(BRANCH_W, D_MODEL), whole(BRANCH_W, D_MODEL),
            whole(D_MODEL, D_MODEL), whole(1, D_MODEL), whole(1, D_MODEL),
            whole(D_MODEL, D_FF), whole(D_FF, D_MODEL), whole(1, D_MODEL), whole(1, D_MODEL),
        ],
        out_specs=[ctx_spec(D_MODEL), lat_spec(D_MODEL)],
        out_shape=[jax.ShapeDtypeStruct(xc.shape, F32), jax.ShapeDtypeStruct(xl.shape, F32)],
        compiler_params=_params(),
        name="tail",
    )(xc, xl, mod4, *o_ctx, *o_lat, wg, bg, wpa, wpb, wpc, wo, *ln1, w1, w2, *ln2)


def _rope_tables(n_tokens, identity_rows):
    rows = n_tokens // GRID_W
    r, col = jnp.meshgrid(jnp.arange(rows), jnp.arange(GRID_W), indexing="ij")
    r = r.reshape(-1).astype(F32)
    col = col.reshape(-1).astype(F32)
    nf = HEAD_DIM // 4
    inv = ROPE_BASE ** (-jnp.arange(nf, dtype=F32) / nf)
    ang_r = r[:, None] * inv[None, :]
    ang_c = col[:, None] * inv[None, :]
    zero = jnp.zeros_like(ang_r)
    cos = jnp.concatenate([jnp.cos(ang_r)] * 2 + [jnp.cos(ang_c)] * 2, axis=-1)
    s_next = jnp.concatenate([-jnp.sin(ang_r), zero, -jnp.sin(ang_c), zero], axis=-1)
    s_prev = jnp.concatenate([zero, jnp.sin(ang_r), zero, jnp.sin(ang_c)], axis=-1)
    ident = (jnp.ones, jnp.zeros, jnp.zeros)
    return tuple(jnp.concatenate([fill((identity_rows, LANE), F32), jnp.tile(t, (1, LANE // HEAD_DIM))], axis=0)
                 for fill, t in zip(ident, (cos, s_next, s_prev)))


def kernel(x_prompt, x_sample, c, cache_diff_k, cache_diff_v, cache_win_k, cache_win_v, state_ret,
           c_ctx, w_mod, b_mod, w_in, diff_lam, diff_norm_g, win_sink, ret_decay, ret_norm_g,
           w_pa, w_pb, w_pc, w_gate, b_gate, w_o, ln1_g, ln1_b, w_ff1, w_ff2, ln2_g, ln2_b):
    bp, tp, d = x_prompt.shape
    bs, ts, _ = x_sample.shape
    assert d == D_MODEL and w_in.shape == (DEPTH, D_MODEL, D_IN) and c.shape[0] + 1 <= MOD_ROWS
    past = cache_diff_k.shape[2]

    c_rows = jnp.concatenate(
        [c_ctx[None, :], c, jnp.zeros((MOD_ROWS - 1 - bs, d), F32)], axis=0)
    mod4 = _mod_call(c_rows, w_mod, b_mod).reshape(DEPTH, MOD_ROWS, 1, D_MOD)
    proj_tm = 512
    rope_tabs = _rope_tables(ts, proj_tm)

    ck_a = cache_diff_k.reshape(bs, DEPTH, past, DA_HEADS * 2 * HEAD_DIM)
    cv_a = cache_diff_v.reshape(bs, DEPTH, past, DA_HEADS * 2 * HEAD_DIM)
    ck_b = cache_win_k.reshape(bs, DEPTH, past, WG_KV_HEADS * HEAD_DIM)
    cv_b = cache_win_v.reshape(bs, DEPTH, past, WG_KV_HEADS * HEAD_DIM)

    xp = x_prompt.reshape(bp * tp, d)
    xs = x_sample.reshape(bs * ts, d)
    new_caches = [jnp.zeros((bp, DEPTH, tp * DA_HEADS, 2 * HEAD_DIM), F32)] * 2 \
        + [jnp.zeros((bp, DEPTH, WG_KV_HEADS * HEAD_DIM, tp), F32)] * 2
    ret_s = []
    for l in range(DEPTH):
        lam_init = 0.8 - 0.6 * math.exp(-0.3 * l)
        w_in_b = w_in[l].astype(BF16)
        w_gate_b = w_gate[l].astype(BF16)
        wpa, wpb, wpc = w_pa[l].astype(BF16), w_pb[l].astype(BF16), w_pc[l].astype(BF16)
        wo = w_o[l].astype(BF16)
        w1, w2 = w_ff1[l].astype(BF16), w_ff2[l].astype(BF16)
        b_gate_l = b_gate[l].reshape(1, D_GATE)
        lam_l = diff_lam[l]
        dng_l = diff_norm_g[l].reshape(1, LANE)
        sink_l = win_sink[l].reshape(1, WG_Q_HEADS)
        rng_l = ret_norm_g[l].reshape(1, RT_DV)
        ln1 = (ln1_g[l].reshape(1, d), ln1_b[l].reshape(1, d))
        ln2 = (ln2_g[l].reshape(1, d), ln2_b[l].reshape(1, d))

        z_all, *new_caches = _proj_call(xs, xp, mod4, l, w_in_b, rope_tabs, new_caches, ts, tp,
                                        tm=proj_tm)
        o_paths = []
        for latent, b0, b, t in ((False, bs * ts // tp, bp, tp), (True, 0, bs, ts)):
            z3 = (z_all.reshape(-1, t, D_IN), b0, b)
            oa = _diff_call(z3, ck_a, cv_a, l, lam_l, dng_l, lam_init, latent)
            ob = _win_call(z3, ck_b, cv_b, l, sink_l, latent)
            if latent:
                oc = _ret_call(z3, state_ret, l, ret_decay[l], rng_l, latent)
            else:
                oc, s_fin = _ret_call(z3, state_ret, l, ret_decay[l], rng_l, latent)
                ret_s.append(s_fin)
            o_paths.append([o.reshape(b * t, BRANCH_W) for o in (oa, ob, oc)])
        xp, xs = _tail_call(xp, xs, mod4, l, o_paths[0], o_paths[1], w_gate_b, b_gate_l,
                            wpa, wpb, wpc, wo, ln1, w1, w2, ln2, ts)
    dk, dv, wk, wv = new_caches
    new_diff = [a.reshape(bp, DEPTH, tp, DA_HEADS, 2 * HEAD_DIM) for a in (dk, dv)]
    new_win = [a.reshape(bp, DEPTH, WG_KV_HEADS, HEAD_DIM, tp).transpose(0, 1, 4, 2, 3) for a in (wk, wv)]
    return (xp.reshape(bp, tp, d), xs.reshape(bs, ts, d), *new_diff, *new_win, jnp.stack(ret_s, axis=1))
```

```python
import functools
import math

import jax
import jax.numpy as jnp
from jax import lax
from jax.experimental import pallas as pl
from jax.experimental.pallas import tpu as pltpu

F32 = jnp.float32
BF16 = jnp.bfloat16

D_MODEL = 1024
DEPTH = 2
GRID_W = 64
HEAD_DIM = 64
DA_HEADS = 4
WG_Q_HEADS = 8
WG_KV_HEADS = 2
WG_GROUP = WG_Q_HEADS // WG_KV_HEADS
WINDOW = 128
RT_HEADS = 4
RT_DK = 64
RT_DV = 128
BRANCH_W = 512
D_IN = 3840
D_GATE = 3 * D_MODEL
D_FF = 4 * D_MODEL
D_MOD = 6 * D_MODEL
ROPE_BASE = 10000.0
LN_EPS = 1e-5
ALPHA = (2 * DEPTH) ** 0.25
QK_SCALE = HEAD_DIM ** -0.5
LOG2E = math.log2(math.e)

LANE = 128
MOD_ROWS = 8
ONES_ROWS = 16

COL_AQ, COL_AK, COL_AV = 0, 4, 8
COL_BQ, COL_BK, COL_BV = 12, 16, 17
COL_CQ, COL_CK, COL_CV, COL_CG = 18, 20, 22, 26
ROPE_BLOCKS = tuple(range(0, 8)) + tuple(range(12, 17))

VMEM_LIMIT = 56 * 1024 * 1024
NT_DIMS = (((1,), (1,)), ((), ()))
TN_DIMS = (((0,), (0,)), ((), ()))


def _params():
    return pltpu.CompilerParams(vmem_limit_bytes=VMEM_LIMIT)


def _resident(shape, index_map):
    return pl.BlockSpec(shape, index_map, pipeline_mode=pl.Buffered(1))


def _dot(a, b):
    return jnp.dot(a, b, preferred_element_type=F32)


def _mod_kernel(c_ref, w_ref, b_ref, o_ref):
    c = c_ref[...]
    a = (c * jax.nn.sigmoid(c)).astype(BF16)
    o_ref[...] = _dot(a, w_ref[...].astype(BF16)) + b_ref[...]


def _mod_call(c_rows, w_mod, b_mod):
    tn = 1536
    return pl.pallas_call(
        _mod_kernel,
        grid=(DEPTH, D_MOD // tn),
        in_specs=[
            pl.BlockSpec((MOD_ROWS, D_MODEL), lambda l, n: (0, 0)),
            pl.BlockSpec((None, D_MODEL, tn), lambda l, n: (l, 0, n)),
            pl.BlockSpec((None, 1, tn), lambda l, n: (l, 0, n)),
        ],
        out_specs=pl.BlockSpec((None, MOD_ROWS, tn), lambda l, n: (l, 0, n)),
        out_shape=jax.ShapeDtypeStruct((DEPTH, MOD_ROWS, D_MOD), F32),
        compiler_params=_params(),
        name="mod_vectors",
    )(c_rows, w_mod, b_mod.reshape(DEPTH, 1, D_MOD))


def _proj_kernel(xl_ref, xc_ref, mod_ref, win_ref, cos_ref, sa_ref, sb_ref, *refs, n_lat_tiles):
    z_ref, dk_ref, dv_ref, wk_ref, wv_ref = refs[-5:]
    is_lat = pl.program_id(0) < n_lat_tiles
    x = jnp.where(is_lat, xl_ref[...], xc_ref[...])
    sh1 = mod_ref[:, 0:D_MODEL]
    sc1 = mod_ref[:, D_MODEL:2 * D_MODEL]
    h = (x * (1.0 + sc1) + sh1).astype(BF16)
    seqs = dk_ref.shape[0]
    t_ctx = wk_ref.shape[-1]
    nc = 768
    for c0 in range(0, D_IN, nc):
        z = _dot(h, win_ref[:, c0:c0 + nc])
        for j in range(nc // LANE):
            blk = c0 // LANE + j
            u = z[:, j * LANE:(j + 1) * LANE]
            if blk in ROPE_BLOCKS:
                u = (u * cos_ref[...] + pltpu.roll(u, LANE - 16, 1) * sa_ref[...]
                     + pltpu.roll(u, 16, 1) * sb_ref[...])
            z_ref[:, blk * LANE:(blk + 1) * LANE] = u
            if COL_AK <= blk < COL_BQ:
                ref, head = (dk_ref, blk - COL_AK) if blk < COL_AV else (dv_ref, blk - COL_AV)
                for s in range(seqs):
                    ref[s, pl.ds(head, t_ctx, stride=DA_HEADS), :] = u[s * t_ctx:(s + 1) * t_ctx]
            elif blk in (COL_BK, COL_BV):
                ref = wk_ref if blk == COL_BK else wv_ref
                ut = u.T
                for s in range(seqs):
                    ref[s] = ut[:, s * t_ctx:(s + 1) * t_ctx]


def _proj_call(xl, xc, mod4, layer, w_in_b, rope_tabs, caches, t_lat, t_ctx, tm=512):
    n_lat, n_ctx = xl.shape[0] // tm, xc.shape[0] // tm
    lat_tiles_per_seq = t_lat // tm
    seqs = tm // t_ctx

    def mod_idx(i):
        row = jnp.where(i < n_lat, 1 + i // lat_tiles_per_seq, 0)
        return (layer, row, 0, 0)

    lat_idx = lambda i: jnp.minimum(i, n_lat - 1)
    ctx_idx = lambda i: jnp.maximum(i - n_lat, 0)
    rope_spec = pl.BlockSpec((tm, LANE), lambda i: (jnp.where(i < n_lat, 1 + i % lat_tiles_per_seq, 0), 0))
    cache_a = pl.BlockSpec((seqs, None, DA_HEADS * t_ctx, 2 * HEAD_DIM), lambda i: (ctx_idx(i), layer, 0, 0))
    cache_b = pl.BlockSpec((seqs, None, WG_KV_HEADS * HEAD_DIM, t_ctx), lambda i: (ctx_idx(i), layer, 0, 0))
    in_specs = [
        pl.BlockSpec((tm, D_MODEL), lambda i: (lat_idx(i), 0)),
        pl.BlockSpec((tm, D_MODEL), lambda i: (ctx_idx(i), 0)),
        pl.BlockSpec((None, None, 1, D_MOD), mod_idx),
        _resident((None, D_MODEL, D_IN), lambda i: (layer, 0, 0)),
        rope_spec, rope_spec, rope_spec,
    ]
    n_in = len(in_specs)
    in_specs += [pl.BlockSpec(memory_space=pl.ANY)] * len(caches)
    b_ctx = xc.shape[0] // t_ctx
    cache_shapes = [(b_ctx, DEPTH, DA_HEADS * t_ctx, 2 * HEAD_DIM)] * 2 \
        + [(b_ctx, DEPTH, WG_KV_HEADS * HEAD_DIM, t_ctx)] * 2
    return pl.pallas_call(
        functools.partial(_proj_kernel, n_lat_tiles=n_lat),
        grid=(n_lat + n_ctx,),
        in_specs=in_specs,
        out_specs=[pl.BlockSpec((tm, D_IN), lambda i: (i, 0)), cache_a, cache_a, cache_b, cache_b],
        out_shape=[jax.ShapeDtypeStruct((xl.shape[0] + xc.shape[0], D_IN), F32)]
        + [jax.ShapeDtypeStruct(s, F32) for s in cache_shapes],
        input_output_aliases={n_in + k: 1 + k for k in range(len(caches))},
        compiler_params=_params(),
        name="proj",
    )(xl, xc, mod4, w_in_b, *rope_tabs, *caches)


def _values_t(v_parts):
    vt = jnp.concatenate([v.T for v in v_parts], axis=1)
    return jnp.concatenate([vt, jnp.ones((ONES_ROWS, vt.shape[1]), F32)], axis=0).astype(BF16)


def _diff_kernel(*refs, has_ctx, lam_init, heads, tq):
    if has_ctx:
        q_ref, k_ref, v_ref, kc_ref, vc_ref, lam_ref, g_ref, o_ref = refs
    else:
        q_ref, k_ref, v_ref, lam_ref, g_ref, o_ref = refs
    t = q_ref.shape[0]
    lv = lam_ref[...]
    lam = (jnp.exp(jnp.sum(lv[0:1] * lv[1:2], axis=-1, keepdims=True))
           - jnp.exp(jnp.sum(lv[2:3] * lv[3:4], axis=-1, keepdims=True)) + lam_init)
    lane = lax.broadcasted_iota(jnp.int32, (tq, LANE), 1)
    blocks = [(hh, i) for hh in range(heads) for i in range(t // tq)]
    kv = {}

    def scores(hh, i):
        cols = slice(hh * LANE, (hh + 1) * LANE)
        if hh not in kv:
            k_parts, v_parts = [k_ref[:, cols]], [v_ref[:, cols]]
            if has_ctx:
                past = kc_ref.shape[0] // DA_HEADS
                mine = pl.ds(pl.program_id(1), past, stride=DA_HEADS)
                k_parts.append(kc_ref[mine, :])
                v_parts.append(vc_ref[mine, :])
            kv[hh] = (jnp.concatenate(k_parts, axis=0).astype(BF16), _values_t(v_parts))
        q = q_ref[i * tq:(i + 1) * tq, cols] * (QK_SCALE * LOG2E)
        qs = jnp.concatenate([jnp.where(lane < HEAD_DIM, q, 0.0),
                              jnp.where(lane >= HEAD_DIM, q, 0.0)], axis=0).astype(BF16)
        return lax.dot_general(kv[hh][0], qs, NT_DIMS, preferred_element_type=F32)

    def finish(hh, i, et):
        ot = _dot(kv[hh][1], et)
        o = ot[0:LANE] * (1.0 / ot[LANE:LANE + 1])
        od = o[:, 0:tq] - lam * o[:, tq:2 * tq]
        yt = od * lax.rsqrt(jnp.mean(od * od, axis=0, keepdims=True) + LN_EPS)
        o_ref[i * tq:(i + 1) * tq, hh * LANE:(hh + 1) * LANE] = yt.T * g_ref[...] * (1.0 - lam_init)

    st, et = {}, {}
    for n in range(len(blocks) + 2):
        if n < len(blocks):
            st[n] = scores(*blocks[n])
        if 1 <= n <= len(blocks):
            s = st.pop(n - 1)
            et[n - 1] = jnp.exp2(s - s.max(axis=0, keepdims=True)).astype(BF16)
        if n >= 2:
            finish(*blocks[n - 2], et.pop(n - 2))


def _diff_call(z3, cache_k4, cache_v4, layer, diff_lam_l, diff_norm_g_l, lam_init, latent):
    z3, b0, b = z3
    t = z3.shape[1]
    has_ctx = latent
    heads = 1 if latent else DA_HEADS
    w = heads * LANE
    in_specs = [
        pl.BlockSpec((None, t, w), lambda bi, h: (bi + b0, 0, COL_AQ // heads + h)),
        pl.BlockSpec((None, t, w), lambda bi, h: (bi + b0, 0, COL_AK // heads + h)),
        pl.BlockSpec((None, t, w), lambda bi, h: (bi + b0, 0, COL_AV // heads + h)),
    ]
    args = [z3, z3, z3]
    if has_ctx:
        rows = cache_k4.shape[2]
        in_specs += [
            pl.BlockSpec((None, None, rows, LANE), lambda bi, h: (bi, layer, 0, 0)),
            pl.BlockSpec((None, None, rows, LANE), lambda bi, h: (bi, layer, 0, 0)),
        ]
        args += [cache_k4, cache_v4]
    in_specs += [
        pl.BlockSpec((None, 4, HEAD_DIM), lambda bi, h: (layer, 0, 0)),
        pl.BlockSpec((None, 1, LANE), lambda bi, h: (layer, 0, 0)),
    ]
    args += [diff_lam_l, diff_norm_g_l]
    tq = 256
    return pl.pallas_call(
        functools.partial(_diff_kernel, has_ctx=has_ctx, lam_init=lam_init, heads=heads, tq=tq),
        grid=(b, DA_HEADS // heads),
        in_specs=in_specs,
        out_specs=pl.BlockSpec((None, t, w), lambda bi, h: (bi, 0, h)),
        out_shape=jax.ShapeDtypeStruct((b, t, BRANCH_W), F32),
        compiler_params=_params(),
        name="diff_lat" if latent else "diff_ctx",
    )(*args)


def _win_kernel(*refs, latent, t):
    if latent:
        q_ref, k_ref, v_ref, kc_ref, vc_ref, sink_ref, o_ref = refs
    else:
        q_ref, k_ref, v_ref, sink_ref, o_ref = refs
    w = WINDOW
    if latent:
        tqb, heads_per_chain = w, WG_GROUP
    else:
        tqb, heads_per_chain = t, 2
    nb = t // tqb
    kb = k_ref[...].astype(BF16)
    vt = v_ref[...].T
    if latent:
        kcb = kc_ref[...].T.astype(BF16)
        vct = vc_ref[...]
        jj = lax.broadcasted_iota(jnp.int32, (w, w), 0)
        ii = lax.broadcasted_iota(jnp.int32, (w, w), 1)
        bias_prev = jnp.concatenate([jnp.where(jj >= ii, 0.0, -1e30)] * heads_per_chain, axis=1)
        bias_next = jnp.concatenate([jnp.where(jj <= ii, 0.0, -1e30)] * heads_per_chain, axis=1)
    chains = [(kv, h0, n)
              for n in range(nb)
              for kv in range(WG_KV_HEADS)
              for h0 in range(kv * WG_GROUP, (kv + 1) * WG_GROUP, heads_per_chain)]
    sinks = {}

    def sink_row(h0):
        if h0 not in sinks:
            sinks[h0] = jnp.concatenate(
                [jnp.broadcast_to(sink_ref[:, h:h + 1] * LOG2E, (1, tqb))
                 for h in range(h0, h0 + heads_per_chain)], axis=1)
        return sinks[h0]

    def key_blocks(n):
        return (max(n - 1, 0), min(n + 1, nb - 1)) if latent else (0, 0)

    def scores(kv, h0, n):
        lo = kv * HEAD_DIM
        rows = slice(n * tqb, (n + 1) * tqb)
        q_g = (jnp.concatenate([q_ref[rows, h * HEAD_DIM:(h + 1) * HEAD_DIM]
                                for h in range(h0, h0 + heads_per_chain)], axis=0)
               * (QK_SCALE * LOG2E)).astype(BF16)
        b0, b1 = key_blocks(n)
        keys = kb[b0 * tqb:(b1 + 1) * tqb, lo:lo + HEAD_DIM]
        if latent:
            keys = jnp.concatenate([keys, kcb[:, lo:lo + HEAD_DIM]], axis=0)
        st = lax.dot_general(keys, q_g, NT_DIMS, preferred_element_type=F32)
        if latent:
            parts = []
            for blk in range(b0, b1 + 1):
                part = st[(blk - b0) * w:(blk - b0 + 1) * w]
                if blk == n - 1:
                    part = part + bias_prev
                elif blk == n + 1:
                    part = part + bias_next
                parts.append(part)
            parts.append(st[(b1 - b0 + 1) * w:])
            st = jnp.concatenate(parts, axis=0)
        return st

    def softmax_numerator(st, h0):
        m = jnp.maximum(st.max(axis=0, keepdims=True), sink_row(h0))
        return jnp.exp2(st - m).astype(BF16), m

    def finish(kv, h0, n, et, m):
        lo = kv * HEAD_DIM
        b0, b1 = key_blocks(n)
        vals = [vt[lo:lo + HEAD_DIM, b0 * tqb:(b1 + 1) * tqb]]
        if latent:
            vals.append(vct[lo:lo + HEAD_DIM])
        vals = jnp.concatenate(vals, axis=1)
        vals = jnp.concatenate([vals, jnp.ones((ONES_ROWS, vals.shape[1]), F32)], axis=0).astype(BF16)
        ot = _dot(vals, et)
        d = ot[HEAD_DIM:HEAD_DIM + 1] + jnp.exp2(sink_row(h0) - m)
        on = ot[0:HEAD_DIM] * (1.0 / d)
        for p in range(heads_per_chain // 2):
            pair = jnp.concatenate([on[:, (2 * p) * tqb:(2 * p + 1) * tqb],
                                    on[:, (2 * p + 1) * tqb:(2 * p + 2) * tqb]], axis=0)
            c0 = (h0 // 2 + p) * LANE
            o_ref[n * tqb:(n + 1) * tqb, c0:c0 + LANE] = pair.T

    st, et = {}, {}
    for c in range(len(chains) + 2):
        if c < len(chains):
            st[c] = scores(*chains[c])
        if 1 <= c <= len(chains):
            et[c - 1] = softmax_numerator(st.pop(c - 1), chains[c - 1][1])
        if c >= 2:
            finish(*chains[c - 2], *et.pop(c - 2))


def _win_call(z3, cache_k4, cache_v4, layer, sink_l, latent):
    z3, b0, b = z3
    t = z3.shape[1]
    in_specs = [
        pl.BlockSpec((None, t, BRANCH_W), lambda bi: (bi + b0, 0, COL_BQ // 4)),
        pl.BlockSpec((None, t, LANE), lambda bi: (bi + b0, 0, COL_BK)),
        pl.BlockSpec((None, t, LANE), lambda bi: (bi + b0, 0, COL_BV)),
    ]
    args = [z3, z3, z3]
    if latent:
        past = cache_k4.shape[3]
        in_specs += [
            pl.BlockSpec((None, None, LANE, past), lambda bi: (bi, layer, 0, 0)),
            pl.BlockSpec((None, None, LANE, past), lambda bi: (bi, layer, 0, 0)),
        ]
        args += [cache_k4, cache_v4]
    in_specs.append(pl.BlockSpec((None, 1, WG_Q_HEADS), lambda bi: (layer, 0, 0)))
    args.append(sink_l)
    return pl.pallas_call(
        functools.partial(_win_kernel, latent=latent, t=t),
        grid=(b,),
        in_specs=in_specs,
        out_specs=pl.BlockSpec((None, t, BRANCH_W), lambda bi: (bi, 0, 0)),
        out_shape=jax.ShapeDtypeStruct((b, t, BRANCH_W), F32),
        compiler_params=_params(),
        name="win_lat" if latent else "win_ctx",
    )(*args)


def _log_sigmoid(x):
    return jnp.minimum(x, 0.0) - jnp.log1p(jnp.exp(-jnp.abs(x)))


def _ret_kernel(*refs, latent, t):
    q_ref, k_ref, v_ref, cg_ref, dec_ref, g_ref = refs[:6]
    if latent:
        s0_ref, o_ref, d_scr = refs[6:]
    else:
        o_ref, sfin_ref, d_scr = refs[-3:]
    tq = min(t, 512)
    lg = _log_sigmoid(dec_ref[...])

    @pl.when(pl.program_id(1) == 0)
    def _():
        spos = lax.broadcasted_iota(jnp.int32, (t, t), 0)
        tpos = lax.broadcasted_iota(jnp.int32, (t, t), 1)
        diff = (tpos - spos).astype(F32)
        for j in range(2):
            lgf, lgb = lg[0:1, j:j + 1], lg[1:2, j:j + 1]
            d_scr[j, 0:t, :] = (jnp.where(diff >= 0, jnp.exp(jnp.maximum(diff, 0.0) * lgf), 0.0)
                                + jnp.where(diff <= 0, jnp.exp(jnp.maximum(-diff, 0.0) * lgb), 0.0))
            if latent:
                tp = lax.broadcasted_iota(jnp.int32, (RT_DK, t), 1).astype(F32)
                d_scr[j, t:t + RT_DK, :] = jnp.exp((tp + 1.0) * lgf)
                d_scr[j, t + RT_DK:t + 2 * RT_DK, :] = jnp.exp((float(t) - tp) * lgb)

    if latent:
        eye = (lax.broadcasted_iota(jnp.int32, (RT_DK, RT_DK), 0)
               == lax.broadcasted_iota(jnp.int32, (RT_DK, RT_DK), 1)).astype(F32)
    heads = {}

    def head_operands(j):
        if j not in heads:
            kf = k_ref[:, j * RT_DK:(j + 1) * RT_DK] * (RT_DK ** -0.5)
            v = v_ref[:, j * RT_DV:(j + 1) * RT_DV]
            keys, vals_t = [kf], [v.T]
            if latent:
                keys += [eye, eye]
                vals_t.append(jnp.concatenate([s0_ref[0, j], s0_ref[1, j]], axis=0).T)
            else:
                sp = lax.broadcasted_iota(jnp.int32, (t, 1), 0).astype(F32)
                zf = jnp.exp((float(t) - 1.0 - sp) * lg[0:1, j:j + 1])
                zb = jnp.exp(sp * lg[1:2, j:j + 1])
                vb = v.astype(BF16)
                sfin_ref[0, j] = lax.dot_general((kf * zf).astype(BF16), vb, TN_DIMS,
                                                 preferred_element_type=F32)
                sfin_ref[1, j] = lax.dot_general((kf * zb).astype(BF16), vb, TN_DIMS,
                                                 preferred_element_type=F32)
            heads[j] = (jnp.concatenate(keys, axis=0).astype(BF16),
                        jnp.concatenate(vals_t, axis=1).astype(BF16))
        return heads[j]

    chains = [(j, i) for j in range(2) for i in range(t // tq)]

    def scores(j, i):
        q = q_ref[i * tq:(i + 1) * tq, j * RT_DK:(j + 1) * RT_DK].astype(BF16)
        return lax.dot_general(head_operands(j)[0], q, NT_DIMS, preferred_element_type=F32)

    def finish(j, i, at):
        yt = _dot(head_operands(j)[1], at)
        mu = jnp.mean(yt, axis=0, keepdims=True)
        yc = yt - mu
        var = jnp.mean(yc * yc, axis=0, keepdims=True)
        yn = (yc * lax.rsqrt(var + LN_EPS)).T * g_ref[...]
        cg = cg_ref[i * tq:(i + 1) * tq, j * RT_DV:(j + 1) * RT_DV]
        o_ref[i * tq:(i + 1) * tq, j * RT_DV:(j + 1) * RT_DV] = yn * (cg * jax.nn.sigmoid(cg))

    st, at = {}, {}
    for c in range(len(chains) + 2):
        if c < len(chains):
            st[c] = scores(*chains[c])
        if 1 <= c <= len(chains):
            j, i = chains[c - 1]
            at[c - 1] = (st.pop(c - 1) * d_scr[j, :, i * tq:(i + 1) * tq]).astype(BF16)
        if c >= 2:
            finish(*chains[c - 2], at.pop(c - 2))


def _ret_call(z3, state6, layer, decay_l, ret_norm_g_l, latent):
    z3, b0, b = z3
    t = z3.shape[1]
    hp_n = RT_HEADS // 2
    in_specs = [
        pl.BlockSpec((None, t, LANE), lambda hp, bi: (bi + b0, 0, COL_CQ + hp)),
        pl.BlockSpec((None, t, LANE), lambda hp, bi: (bi + b0, 0, COL_CK + hp)),
        pl.BlockSpec((None, t, 2 * LANE), lambda hp, bi: (bi + b0, 0, COL_CV // 2 + hp)),
        pl.BlockSpec((None, t, 2 * LANE), lambda hp, bi: (bi + b0, 0, COL_CG // 2 + hp)),
        pl.BlockSpec((None, None, 2, 2), lambda hp, bi: (layer, hp, 0, 0)),
        pl.BlockSpec((None, 1, RT_DV), lambda hp, bi: (layer, 0, 0)),
    ]
    args = [z3, z3, z3, z3, decay_l, ret_norm_g_l]
    o_spec = pl.BlockSpec((None, t, 2 * LANE), lambda hp, bi: (bi, 0, hp))
    o_shape = jax.ShapeDtypeStruct((b, t, BRANCH_W), F32)
    state_spec = pl.BlockSpec((None, None, 2, 2, RT_DK, RT_DV), lambda hp, bi: (bi, layer, 0, hp, 0, 0))
    aliases = {}
    if latent:
        in_specs.append(state_spec)
        args.append(state6)
        out_specs, out_shape = o_spec, o_shape
    else:
        if state6 is not None:
            aliases = {len(in_specs): 1}
            in_specs.append(pl.BlockSpec(memory_space=pl.ANY))
            args.append(state6)
        out_specs = [o_spec, state_spec]
        out_shape = [o_shape, jax.ShapeDtypeStruct((b, DEPTH, 2, RT_HEADS, RT_DK, RT_DV), F32)]
    n_keys = t + (2 * RT_DK if latent else 0)
    return pl.pallas_call(
        functools.partial(_ret_kernel, latent=latent, t=t),
        grid=(hp_n, b),
        in_specs=in_specs,
        out_specs=out_specs,
        out_shape=out_shape,
        input_output_aliases=aliases,
        scratch_shapes=[pltpu.VMEM((2, n_keys, t), F32)],
        compiler_params=_params(),
        name="ret_lat" if latent else "ret_ctx",
    )(*args)


def _layer_norm(x, g, b):
    mu = jnp.mean(x, axis=-1, keepdims=True)
    xc = x - mu
    var = jnp.mean(xc * xc, axis=-1, keepdims=True)
    return xc * lax.rsqrt(var + LN_EPS) * g + b


def _tail_kernel(xc_ref, xl_ref, mod_ref, oac_ref, obc_ref, occ_ref, oal_ref, obl_ref, ocl_ref,
                 wg_ref, bg_ref, wpa_ref, wpb_ref, wpc_ref, wo_ref, ln1g_ref, ln1b_ref,
                 w1_ref, w2_ref, ln2g_ref, ln2b_ref, yc_ref, yl_ref, *, n_ctx_tiles):
    d = D_MODEL
    is_ctx = pl.program_id(0) < n_ctx_tiles
    x = jnp.where(is_ctx, xc_ref[...], xl_ref[...])
    sh1, sc1, g1 = mod_ref[:, 0:d], mod_ref[:, d:2 * d], mod_ref[:, 2 * d:3 * d]
    sh2, sc2, g2 = mod_ref[:, 3 * d:4 * d], mod_ref[:, 4 * d:5 * d], mod_ref[:, 5 * d:6 * d]
    h1 = (x * (1.0 + sc1) + sh1).astype(BF16)
    merged = None
    branches = ((oac_ref, oal_ref, wpa_ref), (obc_ref, obl_ref, wpb_ref), (occ_ref, ocl_ref, wpc_ref))
    for i, (oc_ref_i, ol_ref_i, wp_ref) in enumerate(branches):
        o = jnp.where(is_ctx, oc_ref_i[...], ol_ref_i[...]).astype(BF16)
        gate = jax.nn.sigmoid(_dot(h1, wg_ref[:, i * d:(i + 1) * d]) + bg_ref[:, i * d:(i + 1) * d])
        part = gate * _dot(o, wp_ref[...])
        merged = part if merged is None else merged + part
    y = _dot(merged.astype(BF16), wo_ref[...])
    x1 = _layer_norm(ALPHA * x + g1 * y, ln1g_ref[...], ln1b_ref[...])
    h2 = (x1 * (1.0 + sc2) + sh2).astype(BF16)
    u = jnp.maximum(_dot(h2, w1_ref[...]), 0.0)
    f = _dot((u * u).astype(BF16), w2_ref[...])
    x2 = _layer_norm(ALPHA * x1 + g2 * f, ln2g_ref[...], ln2b_ref[...])

    @pl.when(is_ctx)
    def _():
        yc_ref[...] = x2

    @pl.when(jnp.logical_not(is_ctx))
    def _():
        yl_ref[...] = x2


def _tail_call(xc, xl, mod4, layer, o_ctx, o_lat, wg, bg, wpa, wpb, wpc, wo, ln1, w1, w2, ln2,
               t_lat, tm=256):
    n_ctx, n_lat = xc.shape[0] // tm, xl.shape[0] // tm
    lat_tiles_per_seq = t_lat // tm

    def mod_idx(i):
        row = jnp.where(i < n_ctx, 0, 1 + (i - n_ctx) // lat_tiles_per_seq)
        return (layer, row, 0, 0)

    ctx_spec = lambda w: pl.BlockSpec((tm, w), lambda i: (jnp.minimum(i, n_ctx - 1), 0))
    lat_spec = lambda w: pl.BlockSpec((tm, w), lambda i: (jnp.maximum(i - n_ctx, 0), 0))
    whole = lambda r, c: _resident((None, r, c), lambda i: (layer, 0, 0))
    return pl.pallas_call(
        functools.partial(_tail_kernel, n_ctx_tiles=n_ctx),
        grid=(n_ctx + n_lat,),
        in_specs=[
            ctx_spec(D_MODEL), lat_spec(D_MODEL),
            pl.BlockSpec((None, None, 1, D_MOD), mod_idx),
            ctx_spec(BRANCH_W), ctx_spec(BRANCH_W), ctx_spec(BRANCH_W),
            lat_spec(BRANCH_W), lat_spec(BRANCH_W), lat_spec(BRANCH_W),
            whole(D_MODEL, D_GATE), whole(1, D_GATE),
            whole(BRANCH_W, D_MODEL), whole(BRANCH_W, D_MODEL), whole(BRANCH_W, D_MODEL),
            whole(D_MODEL, D_MODEL), whole(1, D_MODEL), whole(1, D_MODEL),
            whole(D_MODEL, D_FF), whole(D_FF, D_MODEL), whole(1, D_MODEL), whole(1, D_MODEL),
        ],
        out_specs=[ctx_spec(D_MODEL), lat_spec(D_MODEL)],
        out_shape=[jax.ShapeDtypeStruct(xc.shape, F32), jax.ShapeDtypeStruct(xl.shape, F32)],
        compiler_params=_params(),
        name="tail",
    )(xc, xl, mod4, *o_ctx, *o_lat, wg, bg, wpa, wpb, wpc, wo, *ln1, w1, w2, *ln2)


def _rope_tables(n_tokens, identity_rows):
    rows = n_tokens // GRID_W
    r, col = jnp.meshgrid(jnp.arange(rows), jnp.arange(GRID_W), indexing="ij")
    r = r.reshape(-1).astype(F32)
    col = col.reshape(-1).astype(F32)
    nf = HEAD_DIM // 4
    inv = ROPE_BASE ** (-jnp.arange(nf, dtype=F32) / nf)
    ang_r = r[:, None] * inv[None, :]
    ang_c = col[:, None] * inv[None, :]
    zero = jnp.zeros_like(ang_r)
    cos = jnp.concatenate([jnp.cos(ang_r)] * 2 + [jnp.cos(ang_c)] * 2, axis=-1)
    s_next = jnp.concatenate([-jnp.sin(ang_r), zero, -jnp.sin(ang_c), zero], axis=-1)
    s_prev = jnp.concatenate([zero, jnp.sin(ang_r), zero, jnp.sin(ang_c)], axis=-1)
    ident = (jnp.ones, jnp.zeros, jnp.zeros)
    return tuple(jnp.concatenate([fill((identity_rows, LANE), F32), jnp.tile(t, (1, LANE // HEAD_DIM))], axis=0)
                 for fill, t in zip(ident, (cos, s_next, s_prev)))


def kernel(x_prompt, x_sample, c, cache_diff_k, cache_diff_v, cache_win_k, cache_win_v, state_ret,
           c_ctx, w_mod, b_mod, w_in, diff_lam, diff_norm_g, win_sink, ret_decay, ret_norm_g,
           w_pa, w_pb, w_pc, w_gate, b_gate, w_o, ln1_g, ln1_b, w_ff1, w_ff2, ln2_g, ln2_b):
    bp, tp, d = x_prompt.shape
    bs, ts, _ = x_sample.shape
    assert d == D_MODEL and w_in.shape == (DEPTH, D_MODEL, D_IN) and c.shape[0] + 1 <= MOD_ROWS
    past = cache_diff_k.shape[2]

    c_rows = jnp.concatenate(
        [c_ctx[None, :], c, jnp.zeros((MOD_ROWS - 1 - bs, d), F32)], axis=0)
    mod4 = _mod_call(c_rows, w_mod, b_mod).reshape(DEPTH, MOD_ROWS, 1, D_MOD)
    proj_tm = 512
    rope_tabs = _rope_tables(ts, proj_tm)

    ck_a = cache_diff_k.reshape(bs, DEPTH, past * DA_HEADS, 2 * HEAD_DIM)
    cv_a = cache_diff_v.reshape(bs, DEPTH, past * DA_HEADS, 2 * HEAD_DIM)
    ck_b = cache_win_k.transpose(0, 1, 3, 4, 2).reshape(bs, DEPTH, WG_KV_HEADS * HEAD_DIM, past)
    cv_b = cache_win_v.transpose(0, 1, 3, 4, 2).reshape(bs, DEPTH, WG_KV_HEADS * HEAD_DIM, past)

    w_in_b, w_gate_b = w_in.astype(BF16), w_gate.astype(BF16)
    wpa, wpb, wpc, wo = (w.astype(BF16) for w in (w_pa, w_pb, w_pc, w_o))
    w1, w2 = w_ff1.astype(BF16), w_ff2.astype(BF16)
    row = lambda a: a.reshape(DEPTH, 1, a.shape[-1])
    ln1, ln2 = (row(ln1_g), row(ln1_b)), (row(ln2_g), row(ln2_b))
    dec4 = ret_decay.reshape(DEPTH, 2, RT_HEADS // 2, 2).transpose(0, 2, 1, 3)

    xp = x_prompt.reshape(bp * tp, d)
    xs = x_sample.reshape(bs * ts, d)
    new_caches, new_state = [], None
    for l in range(DEPTH):
        lam_init = 0.8 - 0.6 * math.exp(-0.3 * l)
        z_all, *new_caches = _proj_call(xs, xp, mod4, l, w_in_b, rope_tabs, new_caches, ts, tp,
                                        tm=proj_tm)
        o_paths = []
        for latent, b0, b, t in ((False, bs * ts // tp, bp, tp), (True, 0, bs, ts)):
            z3 = (z_all.reshape(-1, t, D_IN), b0, b)
            oa = _diff_call(z3, ck_a, cv_a, l, diff_lam, row(diff_norm_g), lam_init, latent)
            ob = _win_call(z3, ck_b, cv_b, l, row(win_sink), latent)
            if latent:
                oc = _ret_call(z3, state_ret, l, dec4, row(ret_norm_g), latent)
            else:
                oc, new_state = _ret_call(z3, new_state, l, dec4, row(ret_norm_g), latent)
            o_paths.append([o.reshape(b * t, BRANCH_W) for o in (oa, ob, oc)])
        xp, xs = _tail_call(xp, xs, mod4, l, o_paths[0], o_paths[1], w_gate_b, row(b_gate),
                            wpa, wpb, wpc, wo, ln1, w1, w2, ln2, ts)
    dk, dv, wk, wv = new_caches
    new_diff = [a.reshape(bp, DEPTH, tp, DA_HEADS, 2 * HEAD_DIM) for a in (dk, dv)]
    new_win = [a.reshape(bp, DEPTH, WG_KV_HEADS, HEAD_DIM, tp).transpose(0, 1, 4, 2, 3) for a in (wk, wv)]
    return (xp.reshape(bp, tp, d), xs.reshape(bs, ts, d), *new_diff, *new_win, new_state)
```

```python
import functools
import math

import jax
import jax.numpy as jnp
from jax import lax
from jax.experimental import pallas as pl
from jax.experimental.pallas import tpu as pltpu

F32 = jnp.float32
BF16 = jnp.bfloat16

D_MODEL = 1024
DEPTH = 2
GRID_W = 64
HEAD_DIM = 64
DA_HEADS = 4
WG_Q_HEADS = 8
WG_KV_HEADS = 2
WG_GROUP = WG_Q_HEADS // WG_KV_HEADS
WINDOW = 128
RT_HEADS = 4
RT_DK = 64
RT_DV = 128
BRANCH_W = 512
D_IN = 3840
D_GATE = 3 * D_MODEL
D_FF = 4 * D_MODEL
D_MOD = 6 * D_MODEL
ROPE_BASE = 10000.0
LN_EPS = 1e-5
ALPHA = (2 * DEPTH) ** 0.25
QK_SCALE = HEAD_DIM ** -0.5
LOG2E = math.log2(math.e)

LANE = 128
MOD_ROWS = 8
ONES_ROWS = 16

COL_AQ, COL_AK, COL_AV = 0, 4, 8
COL_BQ, COL_BK, COL_BV = 12, 16, 17
COL_CQ, COL_CK, COL_CV, COL_CG = 18, 20, 22, 26
ROPE_BLOCKS = tuple(range(0, 8)) + tuple(range(12, 17))

VMEM_LIMIT = 56 * 1024 * 1024
NT_DIMS = (((1,), (1,)), ((), ()))
TN_DIMS = (((0,), (0,)), ((), ()))


def _params():
    return pltpu.CompilerParams(vmem_limit_bytes=VMEM_LIMIT)


def _resident(shape, index_map):
    return pl.BlockSpec(shape, index_map, pipeline_mode=pl.Buffered(1))


def _dot(a, b):
    return jnp.dot(a, b, preferred_element_type=F32)


def _mod_kernel(c_ref, w_ref, b_ref, o_ref):
    c = c_ref[...]
    a = (c * jax.nn.sigmoid(c)).astype(BF16)
    o_ref[...] = _dot(a, w_ref[...].astype(BF16)) + b_ref[...]


def _mod_call(c_rows, w_mod, b_mod):
    tn = 1536
    return pl.pallas_call(
        _mod_kernel,
        grid=(DEPTH, D_MOD // tn),
        in_specs=[
            pl.BlockSpec((MOD_ROWS, D_MODEL), lambda l, n: (0, 0)),
            pl.BlockSpec((None, D_MODEL, tn), lambda l, n: (l, 0, n)),
            pl.BlockSpec((None, 1, tn), lambda l, n: (l, 0, n)),
        ],
        out_specs=pl.BlockSpec((None, MOD_ROWS, tn), lambda l, n: (l, 0, n)),
        out_shape=jax.ShapeDtypeStruct((DEPTH, MOD_ROWS, D_MOD), F32),
        compiler_params=_params(),
        name="mod_vectors",
    )(c_rows, w_mod, b_mod.reshape(DEPTH, 1, D_MOD))


def _proj_kernel(xl_ref, xc_ref, mod_ref, win_ref, cos_ref, sa_ref, sb_ref, *refs, n_lat_tiles):
    z_ref, dk_ref, dv_ref, wk_ref, wv_ref = refs[-5:]
    is_lat = pl.program_id(0) < n_lat_tiles
    x = jnp.where(is_lat, xl_ref[...], xc_ref[...])
    sh1 = mod_ref[:, 0:D_MODEL]
    sc1 = mod_ref[:, D_MODEL:2 * D_MODEL]
    h = (x * (1.0 + sc1) + sh1).astype(BF16)
    seqs = dk_ref.shape[0]
    t_ctx = wk_ref.shape[-1]
    nc = 768
    for c0 in range(0, D_IN, nc):
        z = _dot(h, win_ref[:, c0:c0 + nc])
        for j in range(nc // LANE):
            blk = c0 // LANE + j
            u = z[:, j * LANE:(j + 1) * LANE]
            if blk in ROPE_BLOCKS:
                u = (u * cos_ref[...] + pltpu.roll(u, LANE - 16, 1) * sa_ref[...]
                     + pltpu.roll(u, 16, 1) * sb_ref[...])
            z_ref[:, blk * LANE:(blk + 1) * LANE] = u
            if COL_AK <= blk < COL_BQ:
                ref, head = (dk_ref, blk - COL_AK) if blk < COL_AV else (dv_ref, blk - COL_AV)
                for s in range(seqs):
                    ref[s, pl.ds(head, t_ctx, stride=DA_HEADS), :] = u[s * t_ctx:(s + 1) * t_ctx]
            elif blk in (COL_BK, COL_BV):
                ref = wk_ref if blk == COL_BK else wv_ref
                ut = u.T
                for s in range(seqs):
                    ref[s] = ut[:, s * t_ctx:(s + 1) * t_ctx]


def _proj_call(xl, xc, mod4, layer, w_in_b, rope_tabs, caches, t_lat, t_ctx, tm=512):
    n_lat, n_ctx = xl.shape[0] // tm, xc.shape[0] // tm
    lat_tiles_per_seq = t_lat // tm
    seqs = tm // t_ctx

    def mod_idx(i):
        row = jnp.where(i < n_lat, 1 + i // lat_tiles_per_seq, 0)
        return (layer, row, 0, 0)

    lat_idx = lambda i: jnp.minimum(i, n_lat - 1)
    ctx_idx = lambda i: jnp.maximum(i - n_lat, 0)
    rope_spec = pl.BlockSpec((tm, LANE), lambda i: (jnp.where(i < n_lat, 1 + i % lat_tiles_per_seq, 0), 0))
    cache_a = pl.BlockSpec((seqs, None, DA_HEADS * t_ctx, 2 * HEAD_DIM), lambda i: (ctx_idx(i), layer, 0, 0))
    cache_b = pl.BlockSpec((seqs, None, WG_KV_HEADS * HEAD_DIM, t_ctx), lambda i: (ctx_idx(i), layer, 0, 0))
    in_specs = [
        pl.BlockSpec((tm, D_MODEL), lambda i: (lat_idx(i), 0)),
        pl.BlockSpec((tm, D_MODEL), lambda i: (ctx_idx(i), 0)),
        pl.BlockSpec((None, None, 1, D_MOD), mod_idx),
        _resident((None, D_MODEL, D_IN), lambda i: (layer, 0, 0)),
        rope_spec, rope_spec, rope_spec,
    ]
    n_in = len(in_specs)
    in_specs += [pl.BlockSpec(memory_space=pl.ANY)] * len(caches)
    b_ctx = xc.shape[0] // t_ctx
    cache_shapes = [(b_ctx, DEPTH, DA_HEADS * t_ctx, 2 * HEAD_DIM)] * 2 \
        + [(b_ctx, DEPTH, WG_KV_HEADS * HEAD_DIM, t_ctx)] * 2
    return pl.pallas_call(
        functools.partial(_proj_kernel, n_lat_tiles=n_lat),
        grid=(n_lat + n_ctx,),
        in_specs=in_specs,
        out_specs=[pl.BlockSpec((tm, D_IN), lambda i: (i, 0)), cache_a, cache_a, cache_b, cache_b],
        out_shape=[jax.ShapeDtypeStruct((xl.shape[0] + xc.shape[0], D_IN), F32)]
        + [jax.ShapeDtypeStruct(s, F32) for s in cache_shapes],
        input_output_aliases={n_in + k: 1 + k for k in range(len(caches))},
        compiler_params=_params(),
        name="proj",
    )(xl, xc, mod4, w_in_b, *rope_tabs, *caches)


def _values_t(v_parts):
    vt = jnp.concatenate([v.T for v in v_parts], axis=1)
    return jnp.concatenate([vt, jnp.ones((ONES_ROWS, vt.shape[1]), F32)], axis=0).astype(BF16)


def _diff_kernel(*refs, has_ctx, lam_init, heads, tq):
    if has_ctx:
        q_ref, k_ref, v_ref, kc_ref, vc_ref, lam_ref, g_ref, o_ref = refs
    else:
        q_ref, k_ref, v_ref, lam_ref, g_ref, o_ref = refs
    t = q_ref.shape[0]
    lv = lam_ref[...]
    lam = (jnp.exp(jnp.sum(lv[0:1] * lv[1:2], axis=-1, keepdims=True))
           - jnp.exp(jnp.sum(lv[2:3] * lv[3:4], axis=-1, keepdims=True)) + lam_init)
    lane = lax.broadcasted_iota(jnp.int32, (tq, LANE), 1)
    blocks = [(hh, i) for hh in range(heads) for i in range(t // tq)]
    kv = {}

    def scores(hh, i):
        cols = slice(hh * LANE, (hh + 1) * LANE)
        if hh not in kv:
            k_parts, v_parts = [k_ref[:, cols]], [v_ref[:, cols]]
            if has_ctx:
                past = kc_ref.shape[0] // DA_HEADS
                mine = pl.ds(pl.program_id(1), past, stride=DA_HEADS)
                k_parts.append(kc_ref[mine, :])
                v_parts.append(vc_ref[mine, :])
            kv[hh] = (jnp.concatenate(k_parts, axis=0).astype(BF16), _values_t(v_parts))
        q = q_ref[i * tq:(i + 1) * tq, cols] * (QK_SCALE * LOG2E)
        qs = jnp.concatenate([jnp.where(lane < HEAD_DIM, q, 0.0),
                              jnp.where(lane >= HEAD_DIM, q, 0.0)], axis=0).astype(BF16)
        return lax.dot_general(kv[hh][0], qs, NT_DIMS, preferred_element_type=F32)

    def finish(hh, i, et):
        ot = _dot(kv[hh][1], et)
        o = ot[0:LANE] * (1.0 / ot[LANE:LANE + 1])
        od = o[:, 0:tq] - lam * o[:, tq:2 * tq]
        yt = od * lax.rsqrt(jnp.mean(od * od, axis=0, keepdims=True) + LN_EPS)
        o_ref[i * tq:(i + 1) * tq, hh * LANE:(hh + 1) * LANE] = yt.T * g_ref[...] * (1.0 - lam_init)

    st, et = {}, {}
    for n in range(len(blocks) + 2):
        if n < len(blocks):
            st[n] = scores(*blocks[n])
        if 1 <= n <= len(blocks):
            s = st.pop(n - 1)
            et[n - 1] = jnp.exp2(s - s.max(axis=0, keepdims=True)).astype(BF16)
        if n >= 2:
            finish(*blocks[n - 2], et.pop(n - 2))


def _diff_call(z3, cache_k4, cache_v4, layer, diff_lam_l, diff_norm_g_l, lam_init, latent):
    z3, b0, b = z3
    t = z3.shape[1]
    has_ctx = latent
    heads = 1 if latent else DA_HEADS
    w = heads * LANE
    in_specs = [
        pl.BlockSpec((None, t, w), lambda bi, h: (bi + b0, 0, COL_AQ // heads + h)),
        pl.BlockSpec((None, t, w), lambda bi, h: (bi + b0, 0, COL_AK // heads + h)),
        pl.BlockSpec((None, t, w), lambda bi, h: (bi + b0, 0, COL_AV // heads + h)),
    ]
    args = [z3, z3, z3]
    if has_ctx:
        rows = cache_k4.shape[2]
        in_specs += [
            pl.BlockSpec((None, None, rows, LANE), lambda bi, h: (bi, layer, 0, 0)),
            pl.BlockSpec((None, None, rows, LANE), lambda bi, h: (bi, layer, 0, 0)),
        ]
        args += [cache_k4, cache_v4]
    in_specs += [
        pl.BlockSpec((None, 4, HEAD_DIM), lambda bi, h: (layer, 0, 0)),
        pl.BlockSpec((None, 1, LANE), lambda bi, h: (layer, 0, 0)),
    ]
    args += [diff_lam_l, diff_norm_g_l]
    tq = 256
    return pl.pallas_call(
        functools.partial(_diff_kernel, has_ctx=has_ctx, lam_init=lam_init, heads=heads, tq=tq),
        grid=(b, DA_HEADS // heads),
        in_specs=in_specs,
        out_specs=pl.BlockSpec((None, t, w), lambda bi, h: (bi, 0, h)),
        out_shape=jax.ShapeDtypeStruct((b, t, BRANCH_W), F32),
        compiler_params=_params(),
        name="diff_lat" if latent else "diff_ctx",
    )(*args)


def _win_kernel(*refs, latent, t):
    if latent:
        q_ref, k_ref, v_ref, kc_ref, vc_ref, sink_ref, o_ref = refs
    else:
        q_ref, k_ref, v_ref, sink_ref, o_ref = refs
    w = WINDOW
    if latent:
        tqb, heads_per_chain = w, WG_GROUP
    else:
        tqb, heads_per_chain = t, 2
    nb = t // tqb
    kb = k_ref[...].astype(BF16)
    vt = v_ref[...].T
    if latent:
        kcb = kc_ref[...].T.astype(BF16)
        vct = vc_ref[...]
        jj = lax.broadcasted_iota(jnp.int32, (w, w), 0)
        ii = lax.broadcasted_iota(jnp.int32, (w, w), 1)
        bias_prev = jnp.concatenate([jnp.where(jj >= ii, 0.0, -1e30)] * heads_per_chain, axis=1)
        bias_next = jnp.concatenate([jnp.where(jj <= ii, 0.0, -1e30)] * heads_per_chain, axis=1)
    chains = [(kv, h0, n)
              for n in range(nb)
              for kv in range(WG_KV_HEADS)
              for h0 in range(kv * WG_GROUP, (kv + 1) * WG_GROUP, heads_per_chain)]
    sinks = {}

    def sink_row(h0):
        if h0 not in sinks:
            sinks[h0] = jnp.concatenate(
                [jnp.broadcast_to(sink_ref[:, h:h + 1] * LOG2E, (1, tqb))
                 for h in range(h0, h0 + heads_per_chain)], axis=1)
        return sinks[h0]

    def key_blocks(n):
        return (max(n - 1, 0), min(n + 1, nb - 1)) if latent else (0, 0)

    def scores(kv, h0, n):
        lo = kv * HEAD_DIM
        rows = slice(n * tqb, (n + 1) * tqb)
        q_g = (jnp.concatenate([q_ref[rows, h * HEAD_DIM:(h + 1) * HEAD_DIM]
                                for h in range(h0, h0 + heads_per_chain)], axis=0)
               * (QK_SCALE * LOG2E)).astype(BF16)
        b0, b1 = key_blocks(n)
        keys = kb[b0 * tqb:(b1 + 1) * tqb, lo:lo + HEAD_DIM]
        if latent:
            keys = jnp.concatenate([keys, kcb[:, lo:lo + HEAD_DIM]], axis=0)
        st = lax.dot_general(keys, q_g, NT_DIMS, preferred_element_type=F32)
        if latent:
            parts = []
            for blk in range(b0, b1 + 1):
                part = st[(blk - b0) * w:(blk - b0 + 1) * w]
                if blk == n - 1:
                    part = part + bias_prev
                elif blk == n + 1:
                    part = part + bias_next
                parts.append(part)
            parts.append(st[(b1 - b0 + 1) * w:])
            st = jnp.concatenate(parts, axis=0)
        return st

    def softmax_numerator(st, h0):
        m = jnp.maximum(st.max(axis=0, keepdims=True), sink_row(h0))
        return jnp.exp2(st - m).astype(BF16), m

    def finish(kv, h0, n, et, m):
        lo = kv * HEAD_DIM
        b0, b1 = key_blocks(n)
        vals = [vt[lo:lo + HEAD_DIM, b0 * tqb:(b1 + 1) * tqb]]
        if latent:
            vals.append(vct[lo:lo + HEAD_DIM])
        vals = jnp.concatenate(vals, axis=1)
        vals = jnp.concatenate([vals, jnp.ones((ONES_ROWS, vals.shape[1]), F32)], axis=0).astype(BF16)
        ot = _dot(vals, et)
        d = ot[HEAD_DIM:HEAD_DIM + 1] + jnp.exp2(sink_row(h0) - m)
        on = ot[0:HEAD_DIM] * (1.0 / d)
        for p in range(heads_per_chain // 2):
            pair = jnp.concatenate([on[:, (2 * p) * tqb:(2 * p + 1) * tqb],
                                    on[:, (2 * p + 1) * tqb:(2 * p + 2) * tqb]], axis=0)
            c0 = (h0 // 2 + p) * LANE
            o_ref[n * tqb:(n + 1) * tqb, c0:c0 + LANE] = pair.T

    st, et = {}, {}
    for c in range(len(chains) + 2):
        if c < len(chains):
            st[c] = scores(*chains[c])
        if 1 <= c <= len(chains):
            et[c - 1] = softmax_numerator(st.pop(c - 1), chains[c - 1][1])
        if c >= 2:
            finish(*chains[c - 2], *et.pop(c - 2))


def _win_call(z3, cache_k4, cache_v4, layer, sink_l, latent):
    z3, b0, b = z3
    t = z3.shape[1]
    in_specs = [
        pl.BlockSpec((None, t, BRANCH_W), lambda bi: (bi + b0, 0, COL_BQ // 4)),
        pl.BlockSpec((None, t, LANE), lambda bi: (bi + b0, 0, COL_BK)),
        pl.BlockSpec((None, t, LANE), lambda bi: (bi + b0, 0, COL_BV)),
    ]
    args = [z3, z3, z3]
    if latent:
        past = cache_k4.shape[3]
        in_specs += [
            pl.BlockSpec((None, None, LANE, past), lambda bi: (bi, layer, 0, 0)),
            pl.BlockSpec((None, None, LANE, past), lambda bi: (bi, layer, 0, 0)),
        ]
        args += [cache_k4, cache_v4]
    in_specs.append(pl.BlockSpec((None, 1, WG_Q_HEADS), lambda bi: (layer, 0, 0)))
    args.append(sink_l)
    return pl.pallas_call(
        functools.partial(_win_kernel, latent=latent, t=t),
        grid=(b,),
        in_specs=in_specs,
        out_specs=pl.BlockSpec((None, t, BRANCH_W), lambda bi: (bi, 0, 0)),
        out_shape=jax.ShapeDtypeStruct((b, t, BRANCH_W), F32),
        compiler_params=_params(),
        name="win_lat" if latent else "win_ctx",
    )(*args)


def _log_sigmoid(x):
    return jnp.minimum(x, 0.0) - jnp.log1p(jnp.exp(-jnp.abs(x)))


def _ret_kernel(*refs, latent, t, pairs, seq_axis):
    q_ref, k_ref = refs[:2]
    v_refs, cg_refs = refs[2:2 + pairs], refs[2 + pairs:2 + 2 * pairs]
    dec_ref, g_ref = refs[2 + 2 * pairs:4 + 2 * pairs]
    if latent:
        s0_ref, o_ref, d_scr = refs[4 + 2 * pairs:]
    else:
        o_ref, sfin_ref, d_scr = refs[-3:]
    seqs = q_ref.shape[0]
    tq = min(t, 512)
    lg = [_log_sigmoid(dec_ref[p]) for p in range(pairs)]

    @pl.when(pl.program_id(seq_axis) == 0)
    def _():
        bw = min(t, LANE)
        nb = t // bw
        off = (lax.broadcasted_iota(jnp.int32, (bw, bw), 1)
               - lax.broadcasted_iota(jnp.int32, (bw, bw), 0)).astype(F32)
        for hh in range(2 * pairs):
            lgf, lgb = lg[hh // 2][0:1, hh % 2:hh % 2 + 1], lg[hh // 2][1:2, hh % 2:hh % 2 + 1]
            for o in range(-(nb - 1), nb):
                diff = off + float(o * bw)
                blk = (jnp.where(diff >= 0, jnp.exp(jnp.maximum(diff, 0.0) * lgf), 0.0)
                       + jnp.where(diff <= 0, jnp.exp(jnp.maximum(-diff, 0.0) * lgb), 0.0))
                for bs in range(max(0, -o), min(nb, nb - o)):
                    d_scr[hh, bs * bw:(bs + 1) * bw, (bs + o) * bw:(bs + o + 1) * bw] = blk
            if latent:
                tp = lax.broadcasted_iota(jnp.int32, (RT_DK, t), 1).astype(F32)
                d_scr[hh, t:t + RT_DK, :] = jnp.exp((tp + 1.0) * lgf)
                d_scr[hh, t + RT_DK:t + 2 * RT_DK, :] = jnp.exp((float(t) - tp) * lgb)

    if latent:
        eye = (lax.broadcasted_iota(jnp.int32, (RT_DK, RT_DK), 0)
               == lax.broadcasted_iota(jnp.int32, (RT_DK, RT_DK), 1)).astype(F32)
    heads = {}

    def head_operands(sq, hh):
        if (sq, hh) not in heads:
            p, j = hh // 2, hh % 2
            kf = k_ref[sq, :, hh * RT_DK:(hh + 1) * RT_DK] * (RT_DK ** -0.5)
            v = v_refs[p][sq, :, j * RT_DV:(j + 1) * RT_DV]
            keys, vals_t = [kf], [v.T]
            if latent:
                keys += [eye, eye]
                vals_t.append(jnp.concatenate([s0_ref[sq, 0, hh], s0_ref[sq, 1, hh]], axis=0).T)
            heads[(sq, hh)] = (jnp.concatenate(keys, axis=0).astype(BF16),
                               jnp.concatenate(vals_t, axis=1).astype(BF16))
            if not latent:
                sp = lax.broadcasted_iota(jnp.int32, (t, 1), 0).astype(F32)
                zf = jnp.exp((float(t) - 1.0 - sp) * lg[p][0:1, j:j + 1])
                zb = jnp.exp(sp * lg[p][1:2, j:j + 1])
                kz = jnp.concatenate([kf * zf, kf * zb], axis=1).astype(BF16)
                s_fb = _dot(heads[(sq, hh)][1], kz).T
                sfin_ref[sq, 0, hh] = s_fb[0:RT_DK]
                sfin_ref[sq, 1, hh] = s_fb[RT_DK:2 * RT_DK]
        return heads[(sq, hh)]

    chains = [(sq, hh, i) for sq in range(seqs) for hh in range(2 * pairs) for i in range(t // tq)]

    def scores(sq, hh, i):
        q = q_ref[sq, i * tq:(i + 1) * tq, hh * RT_DK:(hh + 1) * RT_DK].astype(BF16)
        return lax.dot_general(head_operands(sq, hh)[0], q, NT_DIMS, preferred_element_type=F32)

    def finish(sq, hh, i, at):
        yt = _dot(head_operands(sq, hh)[1], at)
        mu = jnp.mean(yt, axis=0, keepdims=True)
        yc = yt - mu
        var = jnp.mean(yc * yc, axis=0, keepdims=True)
        yn = (yc * lax.rsqrt(var + LN_EPS)).T * g_ref[...]
        cg = cg_refs[hh // 2][sq, i * tq:(i + 1) * tq, (hh % 2) * RT_DV:(hh % 2 + 1) * RT_DV]
        o_ref[sq, i * tq:(i + 1) * tq, hh * RT_DV:(hh + 1) * RT_DV] = yn * (cg * jax.nn.sigmoid(cg))

    st, at = {}, {}
    for c in range(len(chains) + 2):
        if c < len(chains):
            st[c] = scores(*chains[c])
        if 1 <= c <= len(chains):
            _, hh, i = chains[c - 1]
            at[c - 1] = (st.pop(c - 1) * d_scr[hh, :, i * tq:(i + 1) * tq]).astype(BF16)
        if c >= 2:
            finish(*chains[c - 2], at.pop(c - 2))


def _ret_call(z3, state6, layer, decay_l, ret_norm_g_l, latent):
    z3, b0, b = z3
    t = z3.shape[1]
    pairs, seqs = (1, 1) if latent else (RT_HEADS // 2, 2)
    hw = 2 * pairs
    imap = lambda f: (lambda hp, bi: f(hp, bi))
    in_specs = [
        pl.BlockSpec((seqs, t, hw * RT_DK), lambda hp, bi: (bi + b0 // seqs, 0, COL_CQ * LANE // (hw * RT_DK) + hp)),
        pl.BlockSpec((seqs, t, hw * RT_DK), lambda hp, bi: (bi + b0 // seqs, 0, COL_CK * LANE // (hw * RT_DK) + hp)),
    ]
    for col in (COL_CV, COL_CG):
        for p in range(pairs):
            in_specs.append(pl.BlockSpec(
                (seqs, t, 2 * RT_DV), imap(lambda hp, bi, c=col // 2 + p: (bi + b0 // seqs, 0, c + hp))))
    in_specs += [
        pl.BlockSpec((None, pairs, 2, 2), lambda hp, bi: (layer, hp, 0, 0)),
        pl.BlockSpec((None, 1, RT_DV), lambda hp, bi: (layer, 0, 0)),
    ]
    args = [z3] * (2 + 2 * pairs) + [decay_l, ret_norm_g_l]
    o_spec = pl.BlockSpec((seqs, t, hw * RT_DV), lambda hp, bi: (bi, 0, hp))
    o_shape = jax.ShapeDtypeStruct((b, t, BRANCH_W), F32)
    state_spec = pl.BlockSpec((seqs, None, 2, hw, RT_DK, RT_DV), lambda hp, bi: (bi, layer, 0, hp, 0, 0))
    aliases = {}
    if latent:
        in_specs.append(state_spec)
        args.append(state6)
        out_specs, out_shape = o_spec, o_shape
    else:
        if state6 is not None:
            aliases = {len(in_specs): 1}
            in_specs.append(pl.BlockSpec(memory_space=pl.ANY))
            args.append(state6)
        out_specs = [o_spec, state_spec]
        out_shape = [o_shape, jax.ShapeDtypeStruct((b, DEPTH, 2, RT_HEADS, RT_DK, RT_DV), F32)]
    n_keys = t + (2 * RT_DK if latent else 0)
    return pl.pallas_call(
        functools.partial(_ret_kernel, latent=latent, t=t, pairs=pairs, seq_axis=1),
        grid=(RT_HEADS // hw, b // seqs),
        in_specs=in_specs,
        out_specs=out_specs,
        out_shape=out_shape,
        input_output_aliases=aliases,
        scratch_shapes=[pltpu.VMEM((hw, n_keys, t), F32)],
        compiler_params=_params(),
        name="ret_lat" if latent else "ret_ctx",
    )(*args)


def _layer_norm(x, g, b):
    mu = jnp.mean(x, axis=-1, keepdims=True)
    xc = x - mu
    var = jnp.mean(xc * xc, axis=-1, keepdims=True)
    return xc * lax.rsqrt(var + LN_EPS) * g + b


def _tail_kernel(xc_ref, xl_ref, mod_ref, oac_ref, obc_ref, occ_ref, oal_ref, obl_ref, ocl_ref,
                 wg_ref, bg_ref, wpa_ref, wpb_ref, wpc_ref, wo_ref, ln1g_ref, ln1b_ref,
                 w1_ref, w2_ref, ln2g_ref, ln2b_ref, yc_ref, yl_ref, *, n_ctx_tiles):
    d = D_MODEL
    is_ctx = pl.program_id(0) < n_ctx_tiles
    x = jnp.where(is_ctx, xc_ref[...], xl_ref[...])
    sh1, sc1, g1 = mod_ref[:, 0:d], mod_ref[:, d:2 * d], mod_ref[:, 2 * d:3 * d]
    sh2, sc2, g2 = mod_ref[:, 3 * d:4 * d], mod_ref[:, 4 * d:5 * d], mod_ref[:, 5 * d:6 * d]
    h1 = (x * (1.0 + sc1) + sh1).astype(BF16)
    merged = None
    branches = ((oac_ref, oal_ref, wpa_ref), (obc_ref, obl_ref, wpb_ref), (occ_ref, ocl_ref, wpc_ref))
    for i, (oc_ref_i, ol_ref_i, wp_ref) in enumerate(branches):
        o = jnp.where(is_ctx, oc_ref_i[...], ol_ref_i[...]).astype(BF16)
        gate = jax.nn.sigmoid(_dot(h1, wg_ref[:, i * d:(i + 1) * d]) + bg_ref[:, i * d:(i + 1) * d])
        part = gate * _dot(o, wp_ref[...])
        merged = part if merged is None else merged + part
    y = _dot(merged.astype(BF16), wo_ref[...])
    x1 = _layer_norm(ALPHA * x + g1 * y, ln1g_ref[...], ln1b_ref[...])
    h2 = (x1 * (1.0 + sc2) + sh2).astype(BF16)
    u = jnp.maximum(_dot(h2, w1_ref[...]), 0.0)
    f = _dot((u * u).astype(BF16), w2_ref[...])
    x2 = _layer_norm(ALPHA * x1 + g2 * f, ln2g_ref[...], ln2b_ref[...])

    @pl.when(is_ctx)
    def _():
        yc_ref[...] = x2

    @pl.when(jnp.logical_not(is_ctx))
    def _():
        yl_ref[...] = x2


def _tail_call(xc, xl, mod4, layer, o_ctx, o_lat, wg, bg, wpa, wpb, wpc, wo, ln1, w1, w2, ln2,
               t_lat, tm=256):
    n_ctx, n_lat = xc.shape[0] // tm, xl.shape[0] // tm
    lat_tiles_per_seq = t_lat // tm

    def mod_idx(i):
        row = jnp.where(i < n_ctx, 0, 1 + (i - n_ctx) // lat_tiles_per_seq)
        return (layer, row, 0, 0)

    ctx_spec = lambda w: pl.BlockSpec((tm, w), lambda i: (jnp.minimum(i, n_ctx - 1), 0))
    lat_spec = lambda w: pl.BlockSpec((tm, w), lambda i: (jnp.maximum(i - n_ctx, 0), 0))
    whole = lambda r, c: _resident((None, r, c), lambda i: (layer, 0, 0))
    return pl.pallas_call(
        functools.partial(_tail_kernel, n_ctx_tiles=n_ctx),
        grid=(n_ctx + n_lat,),
        in_specs=[
            ctx_spec(D_MODEL), lat_spec(D_MODEL),
            pl.BlockSpec((None, None, 1, D_MOD), mod_idx),
            ctx_spec(BRANCH_W), ctx_spec(BRANCH_W), ctx_spec(BRANCH_W),
            lat_spec(BRANCH_W), lat_spec(BRANCH_W), lat_spec(BRANCH_W),
            whole(D_MODEL, D_GATE), whole(1, D_GATE),
            whole(BRANCH_W, D_MODEL), whole(BRANCH_W, D_MODEL), whole(BRANCH_W, D_MODEL),
            whole(D_MODEL, D_MODEL), whole(1, D_MODEL), whole(1, D_MODEL),
            whole(D_MODEL, D_FF), whole(D_FF, D_MODEL), whole(1, D_MODEL), whole(1, D_MODEL),
        ],
        out_specs=[ctx_spec(D_MODEL), lat_spec(D_MODEL)],
        out_shape=[jax.ShapeDtypeStruct(xc.shape, F32), jax.ShapeDtypeStruct(xl.shape, F32)],
        compiler_params=_params(),
        name="tail",
    )(xc, xl, mod4, *o_ctx, *o_lat, wg, bg, wpa, wpb, wpc, wo, *ln1, w1, w2, *ln2)


def _rope_tables(n_tokens, identity_rows):
    rows = n_tokens // GRID_W
    r, col = jnp.meshgrid(jnp.arange(rows), jnp.arange(GRID_W), indexing="ij")
    r = r.reshape(-1).astype(F32)
    col = col.reshape(-1).astype(F32)
    nf = HEAD_DIM // 4
    inv = ROPE_BASE ** (-jnp.arange(nf, dtype=F32) / nf)
    ang_r = r[:, None] * inv[None, :]
    ang_c = col[:, None] * inv[None, :]
    zero = jnp.zeros_like(ang_r)
    cos = jnp.concatenate([jnp.cos(ang_r)] * 2 + [jnp.cos(ang_c)] * 2, axis=-1)
    s_next = jnp.concatenate([-jnp.sin(ang_r), zero, -jnp.sin(ang_c), zero], axis=-1)
    s_prev = jnp.concatenate([zero, jnp.sin(ang_r), zero, jnp.sin(ang_c)], axis=-1)
    ident = (jnp.ones, jnp.zeros, jnp.zeros)
    return tuple(jnp.concatenate([fill((identity_rows, LANE), F32), jnp.tile(t, (1, LANE // HEAD_DIM))], axis=0)
                 for fill, t in zip(ident, (cos, s_next, s_prev)))


def kernel(x_prompt, x_sample, c, cache_diff_k, cache_diff_v, cache_win_k, cache_win_v, state_ret,
           c_ctx, w_mod, b_mod, w_in, diff_lam, diff_norm_g, win_sink, ret_decay, ret_norm_g,
           w_pa, w_pb, w_pc, w_gate, b_gate, w_o, ln1_g, ln1_b, w_ff1, w_ff2, ln2_g, ln2_b):
    bp, tp, d = x_prompt.shape
    bs, ts, _ = x_sample.shape
    assert d == D_MODEL and w_in.shape == (DEPTH, D_MODEL, D_IN) and c.shape[0] + 1 <= MOD_ROWS
    past = cache_diff_k.shape[2]

    c_rows = jnp.concatenate(
        [c_ctx[None, :], c, jnp.zeros((MOD_ROWS - 1 - bs, d), F32)], axis=0)
    mod4 = _mod_call(c_rows, w_mod, b_mod).reshape(DEPTH, MOD_ROWS, 1, D_MOD)
    proj_tm = 512
    rope_tabs = _rope_tables(ts, proj_tm)

    ck_a = cache_diff_k.reshape(bs, DEPTH, past * DA_HEADS, 2 * HEAD_DIM)
    cv_a = cache_diff_v.reshape(bs, DEPTH, past * DA_HEADS, 2 * HEAD_DIM)
    ck_b = cache_win_k.transpose(0, 1, 3, 4, 2).reshape(bs, DEPTH, WG_KV_HEADS * HEAD_DIM, past)
    cv_b = cache_win_v.transpose(0, 1, 3, 4, 2).reshape(bs, DEPTH, WG_KV_HEADS * HEAD_DIM, past)

    w_in_b, w_gate_b = w_in.astype(BF16), w_gate.astype(BF16)
    wpa, wpb, wpc, wo = (w.astype(BF16) for w in (w_pa, w_pb, w_pc, w_o))
    w1, w2 = w_ff1.astype(BF16), w_ff2.astype(BF16)
    row = lambda a: a.reshape(DEPTH, 1, a.shape[-1])
    ln1, ln2 = (row(ln1_g), row(ln1_b)), (row(ln2_g), row(ln2_b))
    dec4 = ret_decay.reshape(DEPTH, 2, RT_HEADS // 2, 2).transpose(0, 2, 1, 3)

    xp = x_prompt.reshape(bp * tp, d)
    xs = x_sample.reshape(bs * ts, d)
    new_caches, new_state = [], None
    for l in range(DEPTH):
        lam_init = 0.8 - 0.6 * math.exp(-0.3 * l)
        z_all, *new_caches = _proj_call(xs, xp, mod4, l, w_in_b, rope_tabs, new_caches, ts, tp,
                                        tm=proj_tm)
        o_paths = []
        for latent, b0, b, t in ((False, bs * ts // tp, bp, tp), (True, 0, bs, ts)):
            z3 = (z_all.reshape(-1, t, D_IN), b0, b)
            oa = _diff_call(z3, ck_a, cv_a, l, diff_lam, row(diff_norm_g), lam_init, latent)
            ob = _win_call(z3, ck_b, cv_b, l, row(win_sink), latent)
            if latent:
                oc = _ret_call(z3, state_ret, l, dec4, row(ret_norm_g), latent)
            else:
                oc, new_state = _ret_call(z3, new_state, l, dec4, row(ret_norm_g), latent)
            o_paths.append([o.reshape(b * t, BRANCH_W) for o in (oa, ob, oc)])
        xp, xs = _tail_call(xp, xs, mod4, l, o_paths[0], o_paths[1], w_gate_b, row(b_gate),
                            wpa, wpb, wpc, wo, ln1, w1, w2, ln2, ts)
    dk, dv, wk, wv = new_caches
    new_diff = [a.reshape(bp, DEPTH, tp, DA_HEADS, 2 * HEAD_DIM) for a in (dk, dv)]
    new_win = [a.reshape(bp, DEPTH, WG_KV_HEADS, HEAD_DIM, tp).transpose(0, 1, 4, 2, 3) for a in (wk, wv)]
    return (xp.reshape(bp, tp, d), xs.reshape(bs, ts, d), *new_diff, *new_win, new_state)
```

```python
import functools
import math

import jax
import jax.numpy as jnp
from jax import lax
from jax.experimental import pallas as pl
from jax.experimental.pallas import tpu as pltpu

F32 = jnp.float32
BF16 = jnp.bfloat16

D_MODEL = 1024
DEPTH = 2
GRID_W = 64
HEAD_DIM = 64
DA_HEADS = 4
WG_Q_HEADS = 8
WG_KV_HEADS = 2
WG_GROUP = WG_Q_HEADS // WG_KV_HEADS
WINDOW = 128
RT_HEADS = 4
RT_DK = 64
RT_DV = 128
BRANCH_W = 512
D_IN = 3840
D_GATE = 3 * D_MODEL
D_FF = 4 * D_MODEL
D_MOD = 6 * D_MODEL
ROPE_BASE = 10000.0
LN_EPS = 1e-5
ALPHA = (2 * DEPTH) ** 0.25
QK_SCALE = HEAD_DIM ** -0.5
LOG2E = math.log2(math.e)

LANE = 128
MOD_ROWS = 8
ONES_ROWS = 16

COL_AQ, COL_AK, COL_AV = 0, 4, 8
COL_BQ, COL_BK, COL_BV = 12, 16, 17
COL_CQ, COL_CK, COL_CV, COL_CG = 18, 20, 22, 26
ROPE_BLOCKS = tuple(range(0, 8)) + tuple(range(12, 17))

VMEM_LIMIT = 56 * 1024 * 1024
NT_DIMS = (((1,), (1,)), ((), ()))
TN_DIMS = (((0,), (0,)), ((), ()))


def _params():
    return pltpu.CompilerParams(vmem_limit_bytes=VMEM_LIMIT)


def _resident(shape, index_map):
    return pl.BlockSpec(shape, index_map, pipeline_mode=pl.Buffered(1))


def _dot(a, b):
    return jnp.dot(a, b, preferred_element_type=F32)


def _mod_kernel(c_ref, w_ref, b_ref, o_ref):
    c = c_ref[...]
    a = (c * jax.nn.sigmoid(c)).astype(BF16)
    o_ref[...] = _dot(a, w_ref[...].astype(BF16)) + b_ref[...]


def _mod_call(c_rows, w_mod, b_mod):
    tn = 1536
    return pl.pallas_call(
        _mod_kernel,
        grid=(DEPTH, D_MOD // tn),
        in_specs=[
            pl.BlockSpec((MOD_ROWS, D_MODEL), lambda l, n: (0, 0)),
            pl.BlockSpec((None, D_MODEL, tn), lambda l, n: (l, 0, n)),
            pl.BlockSpec((None, 1, tn), lambda l, n: (l, 0, n)),
        ],
        out_specs=pl.BlockSpec((None, MOD_ROWS, tn), lambda l, n: (l, 0, n)),
        out_shape=jax.ShapeDtypeStruct((DEPTH, MOD_ROWS, D_MOD), F32),
        compiler_params=_params(),
        name="mod_vectors",
    )(c_rows, w_mod, b_mod.reshape(DEPTH, 1, D_MOD))


def _proj_kernel(xl_ref, xc_ref, mod_ref, win_ref, cos_ref, sa_ref, sb_ref, *refs, n_lat_tiles, slot):
    z_ref, dk_ref, dv_ref, wk_ref, wv_ref = refs[-5:]
    seqs, n_slots = dk_ref.shape[0], dk_ref.shape[1]
    t_ctx = wk_ref.shape[-1]
    for ref in (dk_ref, dv_ref, wk_ref, wv_ref):
        for other in range(n_slots):
            if other != slot:
                ref[:, other] = jnp.zeros((seqs,) + ref.shape[2:], F32)
    dk_ref, dv_ref, wk_ref, wv_ref = (r.at[:, slot] for r in (dk_ref, dv_ref, wk_ref, wv_ref))
    is_lat = pl.program_id(0) < n_lat_tiles
    x = jnp.where(is_lat, xl_ref[...], xc_ref[...])
    sh1 = mod_ref[:, 0:D_MODEL]
    sc1 = mod_ref[:, D_MODEL:2 * D_MODEL]
    h = (x * (1.0 + sc1) + sh1).astype(BF16)
    nc = 768
    for c0 in range(0, D_IN, nc):
        z = _dot(h, win_ref[:, c0:c0 + nc])
        for j in range(nc // LANE):
            blk = c0 // LANE + j
            u = z[:, j * LANE:(j + 1) * LANE]
            if blk in ROPE_BLOCKS:
                u = (u * cos_ref[...] + pltpu.roll(u, LANE - 16, 1) * sa_ref[...]
                     + pltpu.roll(u, 16, 1) * sb_ref[...])
            z_ref[:, blk * LANE:(blk + 1) * LANE] = u
            if COL_AK <= blk < COL_BQ:
                ref, head = (dk_ref, blk - COL_AK) if blk < COL_AV else (dv_ref, blk - COL_AV)
                for s in range(seqs):
                    ref[s, pl.ds(head, t_ctx, stride=DA_HEADS), :] = u[s * t_ctx:(s + 1) * t_ctx]
            elif blk in (COL_BK, COL_BV):
                ref = wk_ref if blk == COL_BK else wv_ref
                ut = u.T
                for s in range(seqs):
                    ref[s] = ut[:, s * t_ctx:(s + 1) * t_ctx]


def _proj_call(xl, xc, mod4, layer, w_in_b, rope_tabs, caches, t_lat, t_ctx, tm=512):
    n_lat, n_ctx = xl.shape[0] // tm, xc.shape[0] // tm
    lat_tiles_per_seq = t_lat // tm
    seqs = tm // t_ctx

    def mod_idx(i):
        row = jnp.where(i < n_lat, 1 + i // lat_tiles_per_seq, 0)
        return (layer, row, 0, 0)

    lat_idx = lambda i: jnp.minimum(i, n_lat - 1)
    ctx_idx = lambda i: jnp.maximum(i - n_lat, 0)
    rope_spec = pl.BlockSpec((tm, LANE), lambda i: (jnp.where(i < n_lat, 1 + i % lat_tiles_per_seq, 0), 0))
    n_slots, slot, first = (1, 0, layer) if caches else (DEPTH, layer, 0)
    cache_a = pl.BlockSpec((seqs, n_slots, DA_HEADS * t_ctx, 2 * HEAD_DIM), lambda i: (ctx_idx(i), first, 0, 0))
    cache_b = pl.BlockSpec((seqs, n_slots, WG_KV_HEADS * HEAD_DIM, t_ctx), lambda i: (ctx_idx(i), first, 0, 0))
    in_specs = [
        pl.BlockSpec((tm, D_MODEL), lambda i: (lat_idx(i), 0)),
        pl.BlockSpec((tm, D_MODEL), lambda i: (ctx_idx(i), 0)),
        pl.BlockSpec((None, None, 1, D_MOD), mod_idx),
        _resident((None, D_MODEL, D_IN), lambda i: (layer, 0, 0)),
        rope_spec, rope_spec, rope_spec,
    ]
    n_in = len(in_specs)
    in_specs += [pl.BlockSpec(memory_space=pl.ANY)] * len(caches)
    b_ctx = xc.shape[0] // t_ctx
    cache_shapes = [(b_ctx, DEPTH, DA_HEADS * t_ctx, 2 * HEAD_DIM)] * 2 \
        + [(b_ctx, DEPTH, WG_KV_HEADS * HEAD_DIM, t_ctx)] * 2
    return pl.pallas_call(
        functools.partial(_proj_kernel, n_lat_tiles=n_lat, slot=slot),
        grid=(n_lat + n_ctx,),
        in_specs=in_specs,
        out_specs=[pl.BlockSpec((tm, D_IN), lambda i: (i, 0)), cache_a, cache_a, cache_b, cache_b],
        out_shape=[jax.ShapeDtypeStruct((xl.shape[0] + xc.shape[0], D_IN), F32)]
        + [jax.ShapeDtypeStruct(s, F32) for s in cache_shapes],
        input_output_aliases={n_in + k: 1 + k for k in range(len(caches))},
        compiler_params=_params(),
        name="proj",
    )(xl, xc, mod4, w_in_b, *rope_tabs, *caches)


def _values_t(v_parts):
    vt = jnp.concatenate([v.T for v in v_parts], axis=1)
    return jnp.concatenate([vt, jnp.ones((ONES_ROWS, vt.shape[1]), F32)], axis=0).astype(BF16)


def _diff_kernel(*refs, has_ctx, lam_init, heads, tq):
    if has_ctx:
        q_ref, k_ref, v_ref, kc_ref, vc_ref, lam_ref, g_ref, o_ref = refs
    else:
        q_ref, k_ref, v_ref, lam_ref, g_ref, o_ref = refs
    t = q_ref.shape[0]
    lv = lam_ref[...]
    lam = (jnp.exp(jnp.sum(lv[0:1] * lv[1:2], axis=-1, keepdims=True))
           - jnp.exp(jnp.sum(lv[2:3] * lv[3:4], axis=-1, keepdims=True)) + lam_init)
    lane = lax.broadcasted_iota(jnp.int32, (tq, LANE), 1)
    blocks = [(hh, i) for hh in range(heads) for i in range(t // tq)]
    kv = {}

    def scores(hh, i):
        cols = slice(hh * LANE, (hh + 1) * LANE)
        if hh not in kv:
            k_parts, v_parts = [k_ref[:, cols]], [v_ref[:, cols]]
            if has_ctx:
                past = kc_ref.shape[0] // DA_HEADS
                mine = pl.ds(pl.program_id(1), past, stride=DA_HEADS)
                k_parts.append(kc_ref[mine, :])
                v_parts.append(vc_ref[mine, :])
            kv[hh] = (jnp.concatenate(k_parts, axis=0).astype(BF16), _values_t(v_parts))
        q = q_ref[i * tq:(i + 1) * tq, cols] * (QK_SCALE * LOG2E)
        qs = jnp.concatenate([jnp.where(lane < HEAD_DIM, q, 0.0),
                              jnp.where(lane >= HEAD_DIM, q, 0.0)], axis=0).astype(BF16)
        return lax.dot_general(kv[hh][0], qs, NT_DIMS, preferred_element_type=F32)

    def finish(hh, i, et):
        ot = _dot(kv[hh][1], et)
        o = ot[0:LANE] * (1.0 / ot[LANE:LANE + 1])
        od = o[:, 0:tq] - lam * o[:, tq:2 * tq]
        yt = od * lax.rsqrt(jnp.mean(od * od, axis=0, keepdims=True) + LN_EPS)
        o_ref[i * tq:(i + 1) * tq, hh * LANE:(hh + 1) * LANE] = yt.T * g_ref[...] * (1.0 - lam_init)

    st, et = {}, {}
    for n in range(len(blocks) + 2):
        if n < len(blocks):
            st[n] = scores(*blocks[n])
        if 1 <= n <= len(blocks):
            s = st.pop(n - 1)
            et[n - 1] = jnp.exp2(s - s.max(axis=0, keepdims=True)).astype(BF16)
        if n >= 2:
            finish(*blocks[n - 2], et.pop(n - 2))


def _diff_call(z3, cache_k4, cache_v4, layer, diff_lam_l, diff_norm_g_l, lam_init, latent):
    z3, b0, b = z3
    t = z3.shape[1]
    has_ctx = latent
    heads = 1 if latent else DA_HEADS
    w = heads * LANE
    in_specs = [
        pl.BlockSpec((None, t, w), lambda bi, h: (bi + b0, 0, COL_AQ // heads + h)),
        pl.BlockSpec((None, t, w), lambda bi, h: (bi + b0, 0, COL_AK // heads + h)),
        pl.BlockSpec((None, t, w), lambda bi, h: (bi + b0, 0, COL_AV // heads + h)),
    ]
    args = [z3, z3, z3]
    if has_ctx:
        rows = cache_k4.shape[2]
        in_specs += [
            pl.BlockSpec((None, None, rows, LANE), lambda bi, h: (bi, layer, 0, 0)),
            pl.BlockSpec((None, None, rows, LANE), lambda bi, h: (bi, layer, 0, 0)),
        ]
        args += [cache_k4, cache_v4]
    in_specs += [
        pl.BlockSpec((None, 4, HEAD_DIM), lambda bi, h: (layer, 0, 0)),
        pl.BlockSpec((None, 1, LANE), lambda bi, h: (layer, 0, 0)),
    ]
    args += [diff_lam_l, diff_norm_g_l]
    tq = 256
    return pl.pallas_call(
        functools.partial(_diff_kernel, has_ctx=has_ctx, lam_init=lam_init, heads=heads, tq=tq),
        grid=(b, DA_HEADS // heads),
        in_specs=in_specs,
        out_specs=pl.BlockSpec((None, t, w), lambda bi, h: (bi, 0, h)),
        out_shape=jax.ShapeDtypeStruct((b, t, BRANCH_W), F32),
        compiler_params=_params(),
        name="diff_lat" if latent else "diff_ctx",
    )(*args)


def _win_kernel(*refs, latent, t):
    if latent:
        q_ref, k_ref, v_ref, kc_ref, vc_ref, sink_ref, o_ref = refs
    else:
        q_ref, k_ref, v_ref, sink_ref, o_ref = refs
    w = WINDOW
    if latent:
        tqb, heads_per_chain = w, WG_GROUP
    else:
        tqb, heads_per_chain = t, 2
    nb = t // tqb
    kb = k_ref[...].astype(BF16)
    vt = v_ref[...].T
    if latent:
        kcb = kc_ref[...].T.astype(BF16)
        vct = vc_ref[...]
        jj = lax.broadcasted_iota(jnp.int32, (w, w), 0)
        ii = lax.broadcasted_iota(jnp.int32, (w, w), 1)
        bias_prev = jnp.concatenate([jnp.where(jj >= ii, 0.0, -1e30)] * heads_per_chain, axis=1)
        bias_next = jnp.concatenate([jnp.where(jj <= ii, 0.0, -1e30)] * heads_per_chain, axis=1)
    chains = [(kv, h0, n)
              for n in range(nb)
              for kv in range(WG_KV_HEADS)
              for h0 in range(kv * WG_GROUP, (kv + 1) * WG_GROUP, heads_per_chain)]
    sinks = {}

    def sink_row(h0):
        if h0 not in sinks:
            sinks[h0] = jnp.concatenate(
                [jnp.broadcast_to(sink_ref[:, h:h + 1] * LOG2E, (1, tqb))
                 for h in range(h0, h0 + heads_per_chain)], axis=1)
        return sinks[h0]

    def key_blocks(n):
        return (max(n - 1, 0), min(n + 1, nb - 1)) if latent else (0, 0)

    def scores(kv, h0, n):
        lo = kv * HEAD_DIM
        rows = slice(n * tqb, (n + 1) * tqb)
        q_g = (jnp.concatenate([q_ref[rows, h * HEAD_DIM:(h + 1) * HEAD_DIM]
                                for h in range(h0, h0 + heads_per_chain)], axis=0)
               * (QK_SCALE * LOG2E)).astype(BF16)
        b0, b1 = key_blocks(n)
        keys = kb[b0 * tqb:(b1 + 1) * tqb, lo:lo + HEAD_DIM]
        if latent:
            keys = jnp.concatenate([keys, kcb[:, lo:lo + HEAD_DIM]], axis=0)
        st = lax.dot_general(keys, q_g, NT_DIMS, preferred_element_type=F32)
        if latent:
            parts = []
            for blk in range(b0, b1 + 1):
                part = st[(blk - b0) * w:(blk - b0 + 1) * w]
                if blk == n - 1:
                    part = part + bias_prev
                elif blk == n + 1:
                    part = part + bias_next
                parts.append(part)
            parts.append(st[(b1 - b0 + 1) * w:])
            st = jnp.concatenate(parts, axis=0)
        return st

    def softmax_numerator(st, h0):
        m = jnp.maximum(st.max(axis=0, keepdims=True), sink_row(h0))
        return jnp.exp2(st - m).astype(BF16), m

    def finish(kv, h0, n, et, m):
        lo = kv * HEAD_DIM
        b0, b1 = key_blocks(n)
        vals = [vt[lo:lo + HEAD_DIM, b0 * tqb:(b1 + 1) * tqb]]
        if latent:
            vals.append(vct[lo:lo + HEAD_DIM])
        vals = jnp.concatenate(vals, axis=1)
        vals = jnp.concatenate([vals, jnp.ones((ONES_ROWS, vals.shape[1]), F32)], axis=0).astype(BF16)
        ot = _dot(vals, et)
        d = ot[HEAD_DIM:HEAD_DIM + 1] + jnp.exp2(sink_row(h0) - m)
        on = ot[0:HEAD_DIM] * (1.0 / d)
        for p in range(heads_per_chain // 2):
            pair = jnp.concatenate([on[:, (2 * p) * tqb:(2 * p + 1) * tqb],
                                    on[:, (2 * p + 1) * tqb:(2 * p + 2) * tqb]], axis=0)
            c0 = (h0 // 2 + p) * LANE
            o_ref[n * tqb:(n + 1) * tqb, c0:c0 + LANE] = pair.T

    st, et = {}, {}
    for c in range(len(chains) + 2):
        if c < len(chains):
            st[c] = scores(*chains[c])
        if 1 <= c <= len(chains):
            et[c - 1] = softmax_numerator(st.pop(c - 1), chains[c - 1][1])
        if c >= 2:
            finish(*chains[c - 2], *et.pop(c - 2))


def _win_call(z3, cache_k4, cache_v4, layer, sink_l, latent):
    z3, b0, b = z3
    t = z3.shape[1]
    in_specs = [
        pl.BlockSpec((None, t, BRANCH_W), lambda bi: (bi + b0, 0, COL_BQ // 4)),
        pl.BlockSpec((None, t, LANE), lambda bi: (bi + b0, 0, COL_BK)),
        pl.BlockSpec((None, t, LANE), lambda bi: (bi + b0, 0, COL_BV)),
    ]
    args = [z3, z3, z3]
    if latent:
        past = cache_k4.shape[3]
        in_specs += [
            pl.BlockSpec((None, None, LANE, past), lambda bi: (bi, layer, 0, 0)),
            pl.BlockSpec((None, None, LANE, past), lambda bi: (bi, layer, 0, 0)),
        ]
        args += [cache_k4, cache_v4]
    in_specs.append(pl.BlockSpec((None, 1, WG_Q_HEADS), lambda bi: (layer, 0, 0)))
    args.append(sink_l)
    return pl.pallas_call(
        functools.partial(_win_kernel, latent=latent, t=t),
        grid=(b,),
        in_specs=in_specs,
        out_specs=pl.BlockSpec((None, t, BRANCH_W), lambda bi: (bi, 0, 0)),
        out_shape=jax.ShapeDtypeStruct((b, t, BRANCH_W), F32),
        compiler_params=_params(),
        name="win_lat" if latent else "win_ctx",
    )(*args)


def _log_sigmoid(x):
    return jnp.minimum(x, 0.0) - jnp.log1p(jnp.exp(-jnp.abs(x)))


def _ret_kernel(*refs, latent, t, pairs, seq_axis, slot):
    q_ref, k_ref = refs[:2]
    v_refs, cg_refs = refs[2:2 + pairs], refs[2 + pairs:2 + 2 * pairs]
    dec_ref, g_ref = refs[2 + 2 * pairs:4 + 2 * pairs]
    if latent:
        s0_ref, o_ref, d_scr = refs[4 + 2 * pairs:]
    else:
        o_ref, sfin_ref, d_scr = refs[-3:]
        for other in range(sfin_ref.shape[1]):
            if other != slot:
                sfin_ref[:, other] = jnp.zeros((sfin_ref.shape[0],) + sfin_ref.shape[2:], F32)
        sfin_ref = sfin_ref.at[:, slot]
    seqs = q_ref.shape[0]
    tq = min(t, 512)
    lg = [_log_sigmoid(dec_ref[p]) for p in range(pairs)]

    @pl.when(pl.program_id(seq_axis) == 0)
    def _():
        bw = min(t, LANE)
        nb = t // bw
        off = (lax.broadcasted_iota(jnp.int32, (bw, bw), 1)
               - lax.broadcasted_iota(jnp.int32, (bw, bw), 0)).astype(F32)
        for hh in range(2 * pairs):
            lgf, lgb = lg[hh // 2][0:1, hh % 2:hh % 2 + 1], lg[hh // 2][1:2, hh % 2:hh % 2 + 1]
            for o in range(-(nb - 1), nb):
                diff = off + float(o * bw)
                blk = (jnp.where(diff >= 0, jnp.exp(jnp.maximum(diff, 0.0) * lgf), 0.0)
                       + jnp.where(diff <= 0, jnp.exp(jnp.maximum(-diff, 0.0) * lgb), 0.0))
                for bs in range(max(0, -o), min(nb, nb - o)):
                    d_scr[hh, bs * bw:(bs + 1) * bw, (bs + o) * bw:(bs + o + 1) * bw] = blk
            if latent:
                tp = lax.broadcasted_iota(jnp.int32, (RT_DK, t), 1).astype(F32)
                d_scr[hh, t:t + RT_DK, :] = jnp.exp((tp + 1.0) * lgf)
                d_scr[hh, t + RT_DK:t + 2 * RT_DK, :] = jnp.exp((float(t) - tp) * lgb)

    if latent:
        eye = (lax.broadcasted_iota(jnp.int32, (RT_DK, RT_DK), 0)
               == lax.broadcasted_iota(jnp.int32, (RT_DK, RT_DK), 1)).astype(F32)
    heads = {}

    def head_operands(sq, hh):
        if (sq, hh) not in heads:
            p, j = hh // 2, hh % 2
            kf = k_ref[sq, :, hh * RT_DK:(hh + 1) * RT_DK] * (RT_DK ** -0.5)
            v = v_refs[p][sq, :, j * RT_DV:(j + 1) * RT_DV]
            keys, vals_t = [kf], [v.T]
            if latent:
                keys += [eye, eye]
                vals_t.append(jnp.concatenate([s0_ref[sq, 0, hh], s0_ref[sq, 1, hh]], axis=0).T)
            heads[(sq, hh)] = (jnp.concatenate(keys, axis=0).astype(BF16),
                               jnp.concatenate(vals_t, axis=1).astype(BF16))
            if not latent:
                sp = lax.broadcasted_iota(jnp.int32, (t, 1), 0).astype(F32)
                zf = jnp.exp((float(t) - 1.0 - sp) * lg[p][0:1, j:j + 1])
                zb = jnp.exp(sp * lg[p][1:2, j:j + 1])
                kz = jnp.concatenate([kf * zf, kf * zb], axis=1).astype(BF16)
                s_fb = _dot(heads[(sq, hh)][1], kz).T
                sfin_ref[sq, 0, hh] = s_fb[0:RT_DK]
                sfin_ref[sq, 1, hh] = s_fb[RT_DK:2 * RT_DK]
        return heads[(sq, hh)]

    chains = [(sq, hh, i) for sq in range(seqs) for hh in range(2 * pairs) for i in range(t // tq)]

    def scores(sq, hh, i):
        q = q_ref[sq, i * tq:(i + 1) * tq, hh * RT_DK:(hh + 1) * RT_DK].astype(BF16)
        return lax.dot_general(head_operands(sq, hh)[0], q, NT_DIMS, preferred_element_type=F32)

    def finish(sq, hh, i, at):
        yt = _dot(head_operands(sq, hh)[1], at)
        mu = jnp.mean(yt, axis=0, keepdims=True)
        yc = yt - mu
        var = jnp.mean(yc * yc, axis=0, keepdims=True)
        yn = (yc * lax.rsqrt(var + LN_EPS)).T * g_ref[...]
        cg = cg_refs[hh // 2][sq, i * tq:(i + 1) * tq, (hh % 2) * RT_DV:(hh % 2 + 1) * RT_DV]
        o_ref[sq, i * tq:(i + 1) * tq, hh * RT_DV:(hh + 1) * RT_DV] = yn * (cg * jax.nn.sigmoid(cg))

    st, at = {}, {}
    for c in range(len(chains) + 2):
        if c < len(chains):
            st[c] = scores(*chains[c])
        if 1 <= c <= len(chains):
            _, hh, i = chains[c - 1]
            at[c - 1] = (st.pop(c - 1) * d_scr[hh, :, i * tq:(i + 1) * tq]).astype(BF16)
        if c >= 2:
            finish(*chains[c - 2], at.pop(c - 2))


def _ret_call(z3, state6, layer, decay_l, ret_norm_g_l, latent):
    z3, b0, b = z3
    t = z3.shape[1]
    pairs, seqs = (1, 1) if latent else (RT_HEADS // 2, 2)
    hw = 2 * pairs
    imap = lambda f: (lambda hp, bi: f(hp, bi))
    in_specs = [
        pl.BlockSpec((seqs, t, hw * RT_DK), lambda hp, bi: (bi + b0 // seqs, 0, COL_CQ * LANE // (hw * RT_DK) + hp)),
        pl.BlockSpec((seqs, t, hw * RT_DK), lambda hp, bi: (bi + b0 // seqs, 0, COL_CK * LANE // (hw * RT_DK) + hp)),
    ]
    for col in (COL_CV, COL_CG):
        for p in range(pairs):
            in_specs.append(pl.BlockSpec(
                (seqs, t, 2 * RT_DV), imap(lambda hp, bi, c=col // 2 + p: (bi + b0 // seqs, 0, c + hp))))
    in_specs += [
        pl.BlockSpec((None, pairs, 2, 2), lambda hp, bi: (layer, hp, 0, 0)),
        pl.BlockSpec((None, 1, RT_DV), lambda hp, bi: (layer, 0, 0)),
    ]
    args = [z3] * (2 + 2 * pairs) + [decay_l, ret_norm_g_l]
    o_spec = pl.BlockSpec((seqs, t, hw * RT_DV), lambda hp, bi: (bi, 0, hp))
    o_shape = jax.ShapeDtypeStruct((b, t, BRANCH_W), F32)
    aliases, slot = {}, 0
    if latent:
        in_specs.append(pl.BlockSpec((seqs, None, 2, hw, RT_DK, RT_DV),
                                     lambda hp, bi: (bi, layer, 0, hp, 0, 0)))
        args.append(state6)
        out_specs, out_shape = o_spec, o_shape
    else:
        n_slots, slot, first = (DEPTH, layer, 0) if state6 is None else (1, 0, layer)
        if state6 is not None:
            aliases = {len(in_specs): 1}
            in_specs.append(pl.BlockSpec(memory_space=pl.ANY))
            args.append(state6)
        out_specs = [o_spec, pl.BlockSpec((seqs, n_slots, 2, hw, RT_DK, RT_DV),
                                          lambda hp, bi: (bi, first, 0, hp, 0, 0))]
        out_shape = [o_shape, jax.ShapeDtypeStruct((b, DEPTH, 2, RT_HEADS, RT_DK, RT_DV), F32)]
    n_keys = t + (2 * RT_DK if latent else 0)
    return pl.pallas_call(
        functools.partial(_ret_kernel, latent=latent, t=t, pairs=pairs, seq_axis=1, slot=slot),
        grid=(RT_HEADS // hw, b // seqs),
        in_specs=in_specs,
        out_specs=out_specs,
        out_shape=out_shape,
        input_output_aliases=aliases,
        scratch_shapes=[pltpu.VMEM((hw, n_keys, t), F32)],
        compiler_params=_params(),
        name="ret_lat" if latent else "ret_ctx",
    )(*args)


def _layer_norm(x, g, b):
    mu = jnp.mean(x, axis=-1, keepdims=True)
    xc = x - mu
    var = jnp.mean(xc * xc, axis=-1, keepdims=True)
    return xc * lax.rsqrt(var + LN_EPS) * g + b


def _tail_kernel(xc_ref, xl_ref, mod_ref, oac_ref, obc_ref, occ_ref, oal_ref, obl_ref, ocl_ref,
                 wg_ref, bg_ref, wpa_ref, wpb_ref, wpc_ref, wo_ref, ln1g_ref, ln1b_ref,
                 w1_ref, w2_ref, ln2g_ref, ln2b_ref, yc_ref, yl_ref, *, n_ctx_tiles):
    d = D_MODEL
    is_ctx = pl.program_id(0) < n_ctx_tiles
    x = jnp.where(is_ctx, xc_ref[...], xl_ref[...])
    sh1, sc1, g1 = mod_ref[:, 0:d], mod_ref[:, d:2 * d], mod_ref[:, 2 * d:3 * d]
    sh2, sc2, g2 = mod_ref[:, 3 * d:4 * d], mod_ref[:, 4 * d:5 * d], mod_ref[:, 5 * d:6 * d]
    h1 = (x * (1.0 + sc1) + sh1).astype(BF16)
    merged = None
    branches = ((oac_ref, oal_ref, wpa_ref), (obc_ref, obl_ref, wpb_ref), (occ_ref, ocl_ref, wpc_ref))
    for i, (oc_ref_i, ol_ref_i, wp_ref) in enumerate(branches):
        o = jnp.where(is_ctx, oc_ref_i[...], ol_ref_i[...]).astype(BF16)
        gate = jax.nn.sigmoid(_dot(h1, wg_ref[:, i * d:(i + 1) * d]) + bg_ref[:, i * d:(i + 1) * d])
        part = gate * _dot(o, wp_ref[...])
        merged = part if merged is None else merged + part
    y = _dot(merged.astype(BF16), wo_ref[...])
    x1 = _layer_norm(ALPHA * x + g1 * y, ln1g_ref[...], ln1b_ref[...])
    h2 = (x1 * (1.0 + sc2) + sh2).astype(BF16)
    u = jnp.maximum(_dot(h2, w1_ref[...]), 0.0)
    f = _dot((u * u).astype(BF16), w2_ref[...])
    x2 = _layer_norm(ALPHA * x1 + g2 * f, ln2g_ref[...], ln2b_ref[...])

    @pl.when(is_ctx)
    def _():
        yc_ref[...] = x2

    @pl.when(jnp.logical_not(is_ctx))
    def _():
        yl_ref[...] = x2


def _tail_call(xc, xl, mod4, layer, o_ctx, o_lat, wg, bg, wpa, wpb, wpc, wo, ln1, w1, w2, ln2,
               t_lat, tm=256):
    n_ctx, n_lat = xc.shape[0] // tm, xl.shape[0] // tm
    lat_tiles_per_seq = t_lat // tm

    def mod_idx(i):
        row = jnp.where(i < n_ctx, 0, 1 + (i - n_ctx) // lat_tiles_per_seq)
        return (layer, row, 0, 0)

    ctx_spec = lambda w: pl.BlockSpec((tm, w), lambda i: (jnp.minimum(i, n_ctx - 1), 0))
    lat_spec = lambda w: pl.BlockSpec((tm, w), lambda i: (jnp.maximum(i - n_ctx, 0), 0))
    whole = lambda r, c: _resident((None, r, c), lambda i: (layer, 0, 0))
    return pl.pallas_call(
        functools.partial(_tail_kernel, n_ctx_tiles=n_ctx),
        grid=(n_ctx + n_lat,),
        in_specs=[
            ctx_spec(D_MODEL), lat_spec(D_MODEL),
            pl.BlockSpec((None, None, 1, D_MOD), mod_idx),
            ctx_spec(BRANCH_W), ctx_spec(BRANCH_W), ctx_spec(BRANCH_W),
            lat_spec(BRANCH_W), lat_spec(BRANCH_W), lat_spec(BRANCH_W),
            whole(D_MODEL, D_GATE), whole(1, D_GATE),
            whole(BRANCH_W, D_MODEL), whole(BRANCH_W, D_MODEL), whole(BRANCH_W, D_MODEL),
            whole(D_MODEL, D_MODEL), whole(1, D_MODEL), whole(1, D_MODEL),
            whole(D_MODEL, D_FF), whole(D_FF, D_MODEL), whole(1, D_MODEL), whole(1, D_MODEL),
        ],
        out_specs=[ctx_spec(D_MODEL), lat_spec(D_MODEL)],
        out_shape=[jax.ShapeDtypeStruct(xc.shape, F32), jax.ShapeDtypeStruct(xl.shape, F32)],
        compiler_params=_params(),
        name="tail",
    )(xc, xl, mod4, *o_ctx, *o_lat, wg, bg, wpa, wpb, wpc, wo, *ln1, w1, w2, *ln2)


def _rope_tables(n_tokens, identity_rows):
    rows = n_tokens // GRID_W
    r, col = jnp.meshgrid(jnp.arange(rows), jnp.arange(GRID_W), indexing="ij")
    r = r.reshape(-1).astype(F32)
    col = col.reshape(-1).astype(F32)
    nf = HEAD_DIM // 4
    inv = ROPE_BASE ** (-jnp.arange(nf, dtype=F32) / nf)
    ang_r = r[:, None] * inv[None, :]
    ang_c = col[:, None] * inv[None, :]
    zero = jnp.zeros_like(ang_r)
    cos = jnp.concatenate([jnp.cos(ang_r)] * 2 + [jnp.cos(ang_c)] * 2, axis=-1)
    s_next = jnp.concatenate([-jnp.sin(ang_r), zero, -jnp.sin(ang_c), zero], axis=-1)
    s_prev = jnp.concatenate([zero, jnp.sin(ang_r), zero, jnp.sin(ang_c)], axis=-1)
    ident = (jnp.ones, jnp.zeros, jnp.zeros)
    return tuple(jnp.concatenate([fill((identity_rows, LANE), F32), jnp.tile(t, (1, LANE // HEAD_DIM))], axis=0)
                 for fill, t in zip(ident, (cos, s_next, s_prev)))


def kernel(x_prompt, x_sample, c, cache_diff_k, cache_diff_v, cache_win_k, cache_win_v, state_ret,
           c_ctx, w_mod, b_mod, w_in, diff_lam, diff_norm_g, win_sink, ret_decay, ret_norm_g,
           w_pa, w_pb, w_pc, w_gate, b_gate, w_o, ln1_g, ln1_b, w_ff1, w_ff2, ln2_g, ln2_b):
    bp, tp, d = x_prompt.shape
    bs, ts, _ = x_sample.shape
    assert d == D_MODEL and w_in.shape == (DEPTH, D_MODEL, D_IN) and c.shape[0] + 1 <= MOD_ROWS
    past = cache_diff_k.shape[2]

    c_rows = jnp.concatenate(
        [c_ctx[None, :], c, jnp.zeros((MOD_ROWS - 1 - bs, d), F32)], axis=0)
    mod4 = _mod_call(c_rows, w_mod, b_mod).reshape(DEPTH, MOD_ROWS, 1, D_MOD)
    proj_tm = 512
    rope_tabs = _rope_tables(ts, proj_tm)

    ck_a = cache_diff_k.reshape(bs, DEPTH, past * DA_HEADS, 2 * HEAD_DIM)
    cv_a = cache_diff_v.reshape(bs, DEPTH, past * DA_HEADS, 2 * HEAD_DIM)
    ck_b = cache_win_k.transpose(0, 1, 3, 4, 2).reshape(bs, DEPTH, WG_KV_HEADS * HEAD_DIM, past)
    cv_b = cache_win_v.transpose(0, 1, 3, 4, 2).reshape(bs, DEPTH, WG_KV_HEADS * HEAD_DIM, past)

    w_in_b, w_gate_b = w_in.astype(BF16), w_gate.astype(BF16)
    wpa, wpb, wpc, wo = (w.astype(BF16) for w in (w_pa, w_pb, w_pc, w_o))
    w1, w2 = w_ff1.astype(BF16), w_ff2.astype(BF16)
    row = lambda a: a.reshape(DEPTH, 1, a.shape[-1])
    ln1, ln2 = (row(ln1_g), row(ln1_b)), (row(ln2_g), row(ln2_b))
    dec4 = ret_decay.reshape(DEPTH, 2, RT_HEADS // 2, 2).transpose(0, 2, 1, 3)

    xp = x_prompt.reshape(bp * tp, d)
    xs = x_sample.reshape(bs * ts, d)
    new_caches, new_state = [], None
    for l in range(DEPTH):
        lam_init = 0.8 - 0.6 * math.exp(-0.3 * l)
        z_all, *new_caches = _proj_call(xs, xp, mod4, l, w_in_b, rope_tabs, new_caches, ts, tp,
                                        tm=proj_tm)
        o_paths = []
        for latent, b0, b, t in ((False, bs * ts // tp, bp, tp), (True, 0, bs, ts)):
            z3 = (z_all.reshape(-1, t, D_IN), b0, b)
            oa = _diff_call(z3, ck_a, cv_a, l, diff_lam, row(diff_norm_g), lam_init, latent)
            ob = _win_call(z3, ck_b, cv_b, l, row(win_sink), latent)
            if latent:
                oc = _ret_call(z3, state_ret, l, dec4, row(ret_norm_g), latent)
            else:
                oc, new_state = _ret_call(z3, new_state, l, dec4, row(ret_norm_g), latent)
            o_paths.append([o.reshape(b * t, BRANCH_W) for o in (oa, ob, oc)])
        xp, xs = _tail_call(xp, xs, mod4, l, o_paths[0], o_paths[1], w_gate_b, row(b_gate),
                            wpa, wpb, wpc, wo, ln1, w1, w2, ln2, ts)
    dk, dv, wk, wv = new_caches
    new_diff = [a.reshape(bp, DEPTH, tp, DA_HEADS, 2 * HEAD_DIM) for a in (dk, dv)]
    new_win = [a.reshape(bp, DEPTH, WG_KV_HEADS, HEAD_DIM, tp).transpose(0, 1, 4, 2, 3) for a in (wk, wv)]
    return (xp.reshape(bp, tp, d), xs.reshape(bs, ts, d), *new_diff, *new_win, new_state)
```

```python
import functools
import math

import jax
import jax.numpy as jnp
from jax import lax
from jax.experimental import pallas as pl
from jax.experimental.pallas import tpu as pltpu

F32 = jnp.float32
BF16 = jnp.bfloat16

D_MODEL = 1024
DEPTH = 2
GRID_W = 64
HEAD_DIM = 64
DA_HEADS = 4
WG_Q_HEADS = 8
WG_KV_HEADS = 2
WG_GROUP = WG_Q_HEADS // WG_KV_HEADS
WINDOW = 128
RT_HEADS = 4
RT_DK = 64
RT_DV = 128
BRANCH_W = 512
D_IN = 3840
D_GATE = 3 * D_MODEL
D_FF = 4 * D_MODEL
D_MOD = 6 * D_MODEL
ROPE_BASE = 10000.0
LN_EPS = 1e-5
ALPHA = (2 * DEPTH) ** 0.25
QK_SCALE = HEAD_DIM ** -0.5
LOG2E = math.log2(math.e)

LANE = 128
MOD_ROWS = 8
ONES_ROWS = 16
PROJ_COLS = 768

COL_AQ, COL_AK, COL_AV = 0, 4, 8
COL_BQ, COL_BK, COL_BV = 12, 16, 17
COL_CQ, COL_CK, COL_CV, COL_CG = 18, 20, 22, 26
ROPE_BLOCKS = tuple(range(0, 8)) + tuple(range(12, 17))

VMEM_LIMIT = 56 * 1024 * 1024
NT_DIMS = (((1,), (1,)), ((), ()))
TN_DIMS = (((0,), (0,)), ((), ()))


def _params():
    return pltpu.CompilerParams(vmem_limit_bytes=VMEM_LIMIT)


def _resident(shape, index_map):
    return pl.BlockSpec(shape, index_map, pipeline_mode=pl.Buffered(1))


def _dot(a, b):
    return jnp.dot(a, b, preferred_element_type=F32)


def _stage_weights_bf16(first_step, pieces, stage_ref, sem):
    def copy(c):
        src = pieces[c][0]
        return pltpu.make_async_copy(src, stage_ref.at[c % 2, 0:src.shape[0], 0:src.shape[1]], sem.at[c % 2])

    @pl.when(first_step)
    def _():
        for c in range(min(2, len(pieces))):
            copy(c).start()
        for c, (src, dst) in enumerate(pieces):
            copy(c).wait()
            dst[...] = stage_ref[c % 2, 0:src.shape[0], 0:src.shape[1]].astype(BF16)
            if c + 2 < len(pieces):
                copy(c + 2).start()


def _mod_kernel(c_ref, w_ref, b_ref, o_ref):
    c = c_ref[...]
    a = (c * jax.nn.sigmoid(c)).astype(BF16)
    o_ref[...] = _dot(a, w_ref[...].astype(BF16)) + b_ref[...]


def _mod_call(c_rows, w_mod, b_mod):
    tn = 1536
    return pl.pallas_call(
        _mod_kernel,
        grid=(DEPTH, D_MOD // tn),
        in_specs=[
            pl.BlockSpec((MOD_ROWS, D_MODEL), lambda l, n: (0, 0)),
            pl.BlockSpec((None, D_MODEL, tn), lambda l, n: (l, 0, n)),
            pl.BlockSpec((None, 1, tn), lambda l, n: (l, 0, n)),
        ],
        out_specs=pl.BlockSpec((None, MOD_ROWS, tn), lambda l, n: (l, 0, n)),
        out_shape=jax.ShapeDtypeStruct((DEPTH, MOD_ROWS, D_MOD), F32),
        compiler_params=_params(),
        name="mod_vectors",
    )(c_rows, w_mod, b_mod.reshape(DEPTH, 1, D_MOD))


def _proj_kernel(xl_ref, xc_ref, mod_ref, win_hbm, cos_ref, sa_ref, sb_ref, *refs, n_lat_tiles, slot, layer):
    z_ref, dk_ref, dv_ref, wk_ref, wv_ref, win_ref, stage_ref, sem = refs[-8:]
    nc = stage_ref.shape[2]
    _stage_weights_bf16(pl.program_id(0) == 0,
                        [(win_hbm.at[layer, :, c0:c0 + nc], win_ref.at[:, c0:c0 + nc]) for c0 in range(0, D_IN, nc)],
                        stage_ref, sem)
    seqs, n_slots = dk_ref.shape[0], dk_ref.shape[1]
    t_ctx = wk_ref.shape[-1]
    for ref in (dk_ref, dv_ref, wk_ref, wv_ref):
        for other in range(n_slots):
            if other != slot:
                ref[:, other] = jnp.zeros((seqs,) + ref.shape[2:], F32)
    dk_ref, dv_ref, wk_ref, wv_ref = (r.at[:, slot] for r in (dk_ref, dv_ref, wk_ref, wv_ref))
    is_lat = pl.program_id(0) < n_lat_tiles
    x = jnp.where(is_lat, xl_ref[...], xc_ref[...])
    sh1 = mod_ref[:, 0:D_MODEL]
    sc1 = mod_ref[:, D_MODEL:2 * D_MODEL]
    h = (x * (1.0 + sc1) + sh1).astype(BF16)
    for c0 in range(0, D_IN, nc):
        z = _dot(h, win_ref[:, c0:c0 + nc])
        for j in range(nc // LANE):
            blk = c0 // LANE + j
            u = z[:, j * LANE:(j + 1) * LANE]
            if blk in ROPE_BLOCKS:
                u = (u * cos_ref[...] + pltpu.roll(u, LANE - 16, 1) * sa_ref[...]
                     + pltpu.roll(u, 16, 1) * sb_ref[...])
            z_ref[:, blk * LANE:(blk + 1) * LANE] = u
            if COL_AK <= blk < COL_BQ:
                ref, head = (dk_ref, blk - COL_AK) if blk < COL_AV else (dv_ref, blk - COL_AV)
                for s in range(seqs):
                    ref[s, pl.ds(head, t_ctx, stride=DA_HEADS), :] = u[s * t_ctx:(s + 1) * t_ctx]
            elif blk in (COL_BK, COL_BV):
                ref = wk_ref if blk == COL_BK else wv_ref
                ut = u.T
                for s in range(seqs):
                    ref[s] = ut[:, s * t_ctx:(s + 1) * t_ctx]


def _proj_call(xl, xc, mod4, layer, w_in, rope_tabs, caches, t_lat, t_ctx, tm=512):
    n_lat, n_ctx = xl.shape[0] // tm, xc.shape[0] // tm
    lat_tiles_per_seq = t_lat // tm
    seqs = tm // t_ctx

    def mod_idx(i):
        row = jnp.where(i < n_lat, 1 + i // lat_tiles_per_seq, 0)
        return (layer, row, 0, 0)

    lat_idx = lambda i: jnp.minimum(i, n_lat - 1)
    ctx_idx = lambda i: jnp.maximum(i - n_lat, 0)
    rope_spec = pl.BlockSpec((tm, LANE), lambda i: (jnp.where(i < n_lat, 1 + i % lat_tiles_per_seq, 0), 0))
    n_slots, slot, first = (1, 0, layer) if caches else (DEPTH, layer, 0)
    cache_a = pl.BlockSpec((seqs, n_slots, DA_HEADS * t_ctx, 2 * HEAD_DIM), lambda i: (ctx_idx(i), first, 0, 0))
    cache_b = pl.BlockSpec((seqs, n_slots, WG_KV_HEADS * HEAD_DIM, t_ctx), lambda i: (ctx_idx(i), first, 0, 0))
    in_specs = [
        pl.BlockSpec((tm, D_MODEL), lambda i: (lat_idx(i), 0)),
        pl.BlockSpec((tm, D_MODEL), lambda i: (ctx_idx(i), 0)),
        pl.BlockSpec((None, None, 1, D_MOD), mod_idx),
        pl.BlockSpec(memory_space=pl.ANY),
        rope_spec, rope_spec, rope_spec,
    ]
    n_in = len(in_specs)
    in_specs += [pl.BlockSpec(memory_space=pl.ANY)] * len(caches)
    b_ctx = xc.shape[0] // t_ctx
    cache_shapes = [(b_ctx, DEPTH, DA_HEADS * t_ctx, 2 * HEAD_DIM)] * 2 \
        + [(b_ctx, DEPTH, WG_KV_HEADS * HEAD_DIM, t_ctx)] * 2
    return pl.pallas_call(
        functools.partial(_proj_kernel, n_lat_tiles=n_lat, slot=slot, layer=layer),
        grid=(n_lat + n_ctx,),
        in_specs=in_specs,
        out_specs=[pl.BlockSpec((tm, D_IN), lambda i: (i, 0)), cache_a, cache_a, cache_b, cache_b],
        out_shape=[jax.ShapeDtypeStruct((xl.shape[0] + xc.shape[0], D_IN), F32)]
        + [jax.ShapeDtypeStruct(s, F32) for s in cache_shapes],
        input_output_aliases={n_in + k: 1 + k for k in range(len(caches))},
        scratch_shapes=[
            pltpu.VMEM((D_MODEL, D_IN), BF16),
            pltpu.VMEM((2, D_MODEL, PROJ_COLS), F32),
            pltpu.SemaphoreType.DMA((2,)),
        ],
        compiler_params=_params(),
        name="proj",
    )(xl, xc, mod4, w_in, *rope_tabs, *caches)


def _values_t(v_parts):
    vt = jnp.concatenate([v.T for v in v_parts], axis=1)
    return jnp.concatenate([vt, jnp.ones((ONES_ROWS, vt.shape[1]), F32)], axis=0).astype(BF16)


def _diff_kernel(*refs, has_ctx, lam_init, heads, tq):
    if has_ctx:
        q_ref, k_ref, v_ref, kc_ref, vc_ref, lam_ref, g_ref, o_ref = refs
    else:
        q_ref, k_ref, v_ref, lam_ref, g_ref, o_ref = refs
    t = q_ref.shape[0]
    lv = lam_ref[...]
    lam = (jnp.exp(jnp.sum(lv[0:1] * lv[1:2], axis=-1, keepdims=True))
           - jnp.exp(jnp.sum(lv[2:3] * lv[3:4], axis=-1, keepdims=True)) + lam_init)
    lane = lax.broadcasted_iota(jnp.int32, (tq, LANE), 1)
    blocks = [(hh, i) for hh in range(heads) for i in range(t // tq)]
    kv = {}

    def scores(hh, i):
        cols = slice(hh * LANE, (hh + 1) * LANE)
        if hh not in kv:
            k_parts, v_parts = [k_ref[:, cols]], [v_ref[:, cols]]
            if has_ctx:
                past = kc_ref.shape[0] // DA_HEADS
                mine = pl.ds(pl.program_id(1), past, stride=DA_HEADS)
                k_parts.append(kc_ref[mine, :])
                v_parts.append(vc_ref[mine, :])
            kv[hh] = (jnp.concatenate(k_parts, axis=0).astype(BF16), _values_t(v_parts))
        q = q_ref[i * tq:(i + 1) * tq, cols] * (QK_SCALE * LOG2E)
        qs = jnp.concatenate([jnp.where(lane < HEAD_DIM, q, 0.0),
                              jnp.where(lane >= HEAD_DIM, q, 0.0)], axis=0).astype(BF16)
        return lax.dot_general(kv[hh][0], qs, NT_DIMS, preferred_element_type=F32)

    def finish(hh, i, et):
        ot = _dot(kv[hh][1], et)
        o = ot[0:LANE] * (1.0 / ot[LANE:LANE + 1])
        od = o[:, 0:tq] - lam * o[:, tq:2 * tq]
        yt = od * lax.rsqrt(jnp.mean(od * od, axis=0, keepdims=True) + LN_EPS)
        o_ref[i * tq:(i + 1) * tq, hh * LANE:(hh + 1) * LANE] = yt.T * g_ref[...] * (1.0 - lam_init)

    st, et = {}, {}
    for n in range(len(blocks) + 2):
        if n < len(blocks):
            st[n] = scores(*blocks[n])
        if 1 <= n <= len(blocks):
            s = st.pop(n - 1)
            et[n - 1] = jnp.exp2(s - s.max(axis=0, keepdims=True)).astype(BF16)
        if n >= 2:
            finish(*blocks[n - 2], et.pop(n - 2))


def _diff_call(z3, cache_k4, cache_v4, layer, diff_lam_l, diff_norm_g_l, lam_init, latent):
    z3, b0, b = z3
    t = z3.shape[1]
    has_ctx = latent
    heads = 1 if latent else DA_HEADS
    w = heads * LANE
    in_specs = [
        pl.BlockSpec((None, t, w), lambda bi, h: (bi + b0, 0, COL_AQ // heads + h)),
        pl.BlockSpec((None, t, w), lambda bi, h: (bi + b0, 0, COL_AK // heads + h)),
        pl.BlockSpec((None, t, w), lambda bi, h: (bi + b0, 0, COL_AV // heads + h)),
    ]
    args = [z3, z3, z3]
    if has_ctx:
        rows = cache_k4.shape[2]
        in_specs += [
            pl.BlockSpec((None, None, rows, LANE), lambda bi, h: (bi, layer, 0, 0)),
            pl.BlockSpec((None, None, rows, LANE), lambda bi, h: (bi, layer, 0, 0)),
        ]
        args += [cache_k4, cache_v4]
    in_specs += [
        pl.BlockSpec((None, 4, HEAD_DIM), lambda bi, h: (layer, 0, 0)),
        pl.BlockSpec((None, 1, LANE), lambda bi, h: (layer, 0, 0)),
    ]
    args += [diff_lam_l, diff_norm_g_l]
    tq = 256
    return pl.pallas_call(
        functools.partial(_diff_kernel, has_ctx=has_ctx, lam_init=lam_init, heads=heads, tq=tq),
        grid=(b, DA_HEADS // heads),
        in_specs=in_specs,
        out_specs=pl.BlockSpec((None, t, w), lambda bi, h: (bi, 0, h)),
        out_shape=jax.ShapeDtypeStruct((b, t, BRANCH_W), F32),
        compiler_params=_params(),
        name="diff_lat" if latent else "diff_ctx",
    )(*args)


def _win_kernel(*refs, latent, t):
    if latent:
        q_ref, k_ref, v_ref, kc_ref, vc_ref, sink_ref, o_ref = refs
    else:
        q_ref, k_ref, v_ref, sink_ref, o_ref = refs
    w = WINDOW
    if latent:
        tqb, heads_per_chain = w, WG_GROUP
    else:
        tqb, heads_per_chain = t, 2
    nb = t // tqb
    kb = k_ref[...].astype(BF16)
    vt = v_ref[...].T
    if latent:
        kcb = kc_ref[...].T.astype(BF16)
        vct = vc_ref[...]
        jj = lax.broadcasted_iota(jnp.int32, (w, w), 0)
        ii = lax.broadcasted_iota(jnp.int32, (w, w), 1)
        bias_prev = jnp.concatenate([jnp.where(jj >= ii, 0.0, -1e30)] * heads_per_chain, axis=1)
        bias_next = jnp.concatenate([jnp.where(jj <= ii, 0.0, -1e30)] * heads_per_chain, axis=1)
    chains = [(kv, h0, n)
              for n in range(nb)
              for kv in range(WG_KV_HEADS)
              for h0 in range(kv * WG_GROUP, (kv + 1) * WG_GROUP, heads_per_chain)]
    sinks = {}

    def sink_row(h0):
        if h0 not in sinks:
            sinks[h0] = jnp.concatenate(
                [jnp.broadcast_to(sink_ref[:, h:h + 1] * LOG2E, (1, tqb))
                 for h in range(h0, h0 + heads_per_chain)], axis=1)
        return sinks[h0]

    def key_blocks(n):
        return (max(n - 1, 0), min(n + 1, nb - 1)) if latent else (0, 0)

    def scores(kv, h0, n):
        lo = kv * HEAD_DIM
        rows = slice(n * tqb, (n + 1) * tqb)
        q_g = (jnp.concatenate([q_ref[rows, h * HEAD_DIM:(h + 1) * HEAD_DIM]
                                for h in range(h0, h0 + heads_per_chain)], axis=0)
               * (QK_SCALE * LOG2E)).astype(BF16)
        b0, b1 = key_blocks(n)
        keys = kb[b0 * tqb:(b1 + 1) * tqb, lo:lo + HEAD_DIM]
        if latent:
            keys = jnp.concatenate([keys, kcb[:, lo:lo + HEAD_DIM]], axis=0)
        st = lax.dot_general(keys, q_g, NT_DIMS, preferred_element_type=F32)
        if latent:
            parts = []
            for blk in range(b0, b1 + 1):
                part = st[(blk - b0) * w:(blk - b0 + 1) * w]
                if blk == n - 1:
                    part = part + bias_prev
                elif blk == n + 1:
                    part = part + bias_next
                parts.append(part)
            parts.append(st[(b1 - b0 + 1) * w:])
            st = jnp.concatenate(parts, axis=0)
        return st

    def softmax_numerator(st, h0):
        m = jnp.maximum(st.max(axis=0, keepdims=True), sink_row(h0))
        return jnp.exp2(st - m).astype(BF16), m

    def finish(kv, h0, n, et, m):
        lo = kv * HEAD_DIM
        b0, b1 = key_blocks(n)
        vals = [vt[lo:lo + HEAD_DIM, b0 * tqb:(b1 + 1) * tqb]]
        if latent:
            vals.append(vct[lo:lo + HEAD_DIM])
        vals = jnp.concatenate(vals, axis=1)
        vals = jnp.concatenate([vals, jnp.ones((ONES_ROWS, vals.shape[1]), F32)], axis=0).astype(BF16)
        ot = _dot(vals, et)
        d = ot[HEAD_DIM:HEAD_DIM + 1] + jnp.exp2(sink_row(h0) - m)
        on = ot[0:HEAD_DIM] * (1.0 / d)
        for p in range(heads_per_chain // 2):
            pair = jnp.concatenate([on[:, (2 * p) * tqb:(2 * p + 1) * tqb],
                                    on[:, (2 * p + 1) * tqb:(2 * p + 2) * tqb]], axis=0)
            c0 = (h0 // 2 + p) * LANE
            o_ref[n * tqb:(n + 1) * tqb, c0:c0 + LANE] = pair.T

    st, et = {}, {}
    for c in range(len(chains) + 2):
        if c < len(chains):
            st[c] = scores(*chains[c])
        if 1 <= c <= len(chains):
            et[c - 1] = softmax_numerator(st.pop(c - 1), chains[c - 1][1])
        if c >= 2:
            finish(*chains[c - 2], *et.pop(c - 2))


def _win_call(z3, cache_k4, cache_v4, layer, sink_l, latent):
    z3, b0, b = z3
    t = z3.shape[1]
    in_specs = [
        pl.BlockSpec((None, t, BRANCH_W), lambda bi: (bi + b0, 0, COL_BQ // 4)),
        pl.BlockSpec((None, t, LANE), lambda bi: (bi + b0, 0, COL_BK)),
        pl.BlockSpec((None, t, LANE), lambda bi: (bi + b0, 0, COL_BV)),
    ]
    args = [z3, z3, z3]
    if latent:
        past = cache_k4.shape[3]
        in_specs += [
            pl.BlockSpec((None, None, LANE, past), lambda bi: (bi, layer, 0, 0)),
            pl.BlockSpec((None, None, LANE, past), lambda bi: (bi, layer, 0, 0)),
        ]
        args += [cache_k4, cache_v4]
    in_specs.append(pl.BlockSpec((None, 1, WG_Q_HEADS), lambda bi: (layer, 0, 0)))
    args.append(sink_l)
    return pl.pallas_call(
        functools.partial(_win_kernel, latent=latent, t=t),
        grid=(b,),
        in_specs=in_specs,
        out_specs=pl.BlockSpec((None, t, BRANCH_W), lambda bi: (bi, 0, 0)),
        out_shape=jax.ShapeDtypeStruct((b, t, BRANCH_W), F32),
        compiler_params=_params(),
        name="win_lat" if latent else "win_ctx",
    )(*args)


def _log_sigmoid(x):
    return jnp.minimum(x, 0.0) - jnp.log1p(jnp.exp(-jnp.abs(x)))


def _ret_kernel(*refs, latent, t, pairs, seq_axis, slot):
    q_ref, k_ref = refs[:2]
    v_refs, cg_refs = refs[2:2 + pairs], refs[2 + pairs:2 + 2 * pairs]
    dec_ref, g_ref = refs[2 + 2 * pairs:4 + 2 * pairs]
    if latent:
        s0_ref, o_ref, d_scr = refs[4 + 2 * pairs:]
    else:
        o_ref, sfin_ref, d_scr = refs[-3:]
        for other in range(sfin_ref.shape[1]):
            if other != slot:
                sfin_ref[:, other] = jnp.zeros((sfin_ref.shape[0],) + sfin_ref.shape[2:], F32)
        sfin_ref = sfin_ref.at[:, slot]
    seqs = q_ref.shape[0]
    tq = min(t, 512)
    lg = [_log_sigmoid(dec_ref[p]) for p in range(pairs)]

    @pl.when(pl.program_id(seq_axis) == 0)
    def _():
        bw = min(t, LANE)
        nb = t // bw
        off = (lax.broadcasted_iota(jnp.int32, (bw, bw), 1)
               - lax.broadcasted_iota(jnp.int32, (bw, bw), 0)).astype(F32)
        for hh in range(2 * pairs):
            lgf, lgb = lg[hh // 2][0:1, hh % 2:hh % 2 + 1], lg[hh // 2][1:2, hh % 2:hh % 2 + 1]
            for o in range(-(nb - 1), nb):
                diff = off + float(o * bw)
                blk = (jnp.where(diff >= 0, jnp.exp(jnp.maximum(diff, 0.0) * lgf), 0.0)
                       + jnp.where(diff <= 0, jnp.exp(jnp.maximum(-diff, 0.0) * lgb), 0.0))
                for bs in range(max(0, -o), min(nb, nb - o)):
                    d_scr[hh, bs * bw:(bs + 1) * bw, (bs + o) * bw:(bs + o + 1) * bw] = blk
            if latent:
                tp = lax.broadcasted_iota(jnp.int32, (RT_DK, t), 1).astype(F32)
                d_scr[hh, t:t + RT_DK, :] = jnp.exp((tp + 1.0) * lgf)
                d_scr[hh, t + RT_DK:t + 2 * RT_DK, :] = jnp.exp((float(t) - tp) * lgb)

    if latent:
        eye = (lax.broadcasted_iota(jnp.int32, (RT_DK, RT_DK), 0)
               == lax.broadcasted_iota(jnp.int32, (RT_DK, RT_DK), 1)).astype(F32)
    heads = {}

    def head_operands(sq, hh):
        if (sq, hh) not in heads:
            p, j = hh // 2, hh % 2
            kf = k_ref[sq, :, hh * RT_DK:(hh + 1) * RT_DK] * (RT_DK ** -0.5)
            v = v_refs[p][sq, :, j * RT_DV:(j + 1) * RT_DV]
            keys, vals_t = [kf], [v.T]
            if latent:
                keys += [eye, eye]
                vals_t.append(jnp.concatenate([s0_ref[sq, 0, hh], s0_ref[sq, 1, hh]], axis=0).T)
            heads[(sq, hh)] = (jnp.concatenate(keys, axis=0).astype(BF16),
                               jnp.concatenate(vals_t, axis=1).astype(BF16))
            if not latent:
                sp = lax.broadcasted_iota(jnp.int32, (t, 1), 0).astype(F32)
                zf = jnp.exp((float(t) - 1.0 - sp) * lg[p][0:1, j:j + 1])
                zb = jnp.exp(sp * lg[p][1:2, j:j + 1])
                kz = jnp.concatenate([kf * zf, kf * zb], axis=1).astype(BF16)
                s_fb = _dot(heads[(sq, hh)][1], kz).T
                sfin_ref[sq, 0, hh] = s_fb[0:RT_DK]
                sfin_ref[sq, 1, hh] = s_fb[RT_DK:2 * RT_DK]
        return heads[(sq, hh)]

    chains = [(sq, hh, i) for sq in range(seqs) for hh in range(2 * pairs) for i in range(t // tq)]

    def scores(sq, hh, i):
        q = q_ref[sq, i * tq:(i + 1) * tq, hh * RT_DK:(hh + 1) * RT_DK].astype(BF16)
        return lax.dot_general(head_operands(sq, hh)[0], q, NT_DIMS, preferred_element_type=F32)

    def finish(sq, hh, i, at):
        yt = _dot(head_operands(sq, hh)[1], at)
        mu = jnp.mean(yt, axis=0, keepdims=True)
        yc = yt - mu
        var = jnp.mean(yc * yc, axis=0, keepdims=True)
        yn = (yc * lax.rsqrt(var + LN_EPS)).T * g_ref[...]
        cg = cg_refs[hh // 2][sq, i * tq:(i + 1) * tq, (hh % 2) * RT_DV:(hh % 2 + 1) * RT_DV]
        o_ref[sq, i * tq:(i + 1) * tq, hh * RT_DV:(hh + 1) * RT_DV] = yn * (cg * jax.nn.sigmoid(cg))

    st, at = {}, {}
    for c in range(len(chains) + 2):
        if c < len(chains):
            st[c] = scores(*chains[c])
        if 1 <= c <= len(chains):
            _, hh, i = chains[c - 1]
            at[c - 1] = (st.pop(c - 1) * d_scr[hh, :, i * tq:(i + 1) * tq]).astype(BF16)
        if c >= 2:
            finish(*chains[c - 2], at.pop(c - 2))


def _ret_call(z3, state6, layer, decay_l, ret_norm_g_l, latent):
    z3, b0, b = z3
    t = z3.shape[1]
    pairs, seqs = (1, 1) if latent else (RT_HEADS // 2, 2)
    hw = 2 * pairs
    imap = lambda f: (lambda hp, bi: f(hp, bi))
    in_specs = [
        pl.BlockSpec((seqs, t, hw * RT_DK), lambda hp, bi: (bi + b0 // seqs, 0, COL_CQ * LANE // (hw * RT_DK) + hp)),
        pl.BlockSpec((seqs, t, hw * RT_DK), lambda hp, bi: (bi + b0 // seqs, 0, COL_CK * LANE // (hw * RT_DK) + hp)),
    ]
    for col in (COL_CV, COL_CG):
        for p in range(pairs):
            in_specs.append(pl.BlockSpec(
                (seqs, t, 2 * RT_DV), imap(lambda hp, bi, c=col // 2 + p: (bi + b0 // seqs, 0, c + hp))))
    in_specs += [
        pl.BlockSpec((None, pairs, 2, 2), lambda hp, bi: (layer, hp, 0, 0)),
        pl.BlockSpec((None, 1, RT_DV), lambda hp, bi: (layer, 0, 0)),
    ]
    args = [z3] * (2 + 2 * pairs) + [decay_l, ret_norm_g_l]
    o_spec = pl.BlockSpec((seqs, t, hw * RT_DV), lambda hp, bi: (bi, 0, hp))
    o_shape = jax.ShapeDtypeStruct((b, t, BRANCH_W), F32)
    aliases, slot = {}, 0
    if latent:
        in_specs.append(pl.BlockSpec((seqs, None, 2, hw, RT_DK, RT_DV),
                                     lambda hp, bi: (bi, layer, 0, hp, 0, 0)))
        args.append(state6)
        out_specs, out_shape = o_spec, o_shape
    else:
        n_slots, slot, first = (DEPTH, layer, 0) if state6 is None else (1, 0, layer)
        if state6 is not None:
            aliases = {len(in_specs): 1}
            in_specs.append(pl.BlockSpec(memory_space=pl.ANY))
            args.append(state6)
        out_specs = [o_spec, pl.BlockSpec((seqs, n_slots, 2, hw, RT_DK, RT_DV),
                                          lambda hp, bi: (bi, first, 0, hp, 0, 0))]
        out_shape = [o_shape, jax.ShapeDtypeStruct((b, DEPTH, 2, RT_HEADS, RT_DK, RT_DV), F32)]
    n_keys = t + (2 * RT_DK if latent else 0)
    return pl.pallas_call(
        functools.partial(_ret_kernel, latent=latent, t=t, pairs=pairs, seq_axis=1, slot=slot),
        grid=(RT_HEADS // hw, b // seqs),
        in_specs=in_specs,
        out_specs=out_specs,
        out_shape=out_shape,
        input_output_aliases=aliases,
        scratch_shapes=[pltpu.VMEM((hw, n_keys, t), F32)],
        compiler_params=_params(),
        name="ret_lat" if latent else "ret_ctx",
    )(*args)


def _layer_norm(x, g, b):
    mu = jnp.mean(x, axis=-1, keepdims=True)
    xc = x - mu
    var = jnp.mean(xc * xc, axis=-1, keepdims=True)
    return xc * lax.rsqrt(var + LN_EPS) * g + b


def _tail_kernel(xc_ref, xl_ref, mod_ref, oac_ref, obc_ref, occ_ref, oal_ref, obl_ref, ocl_ref,
                 bg_ref, ln1g_ref, ln1b_ref, ln2g_ref, ln2b_ref,
                 wg_hbm, wpa_hbm, wpb_hbm, wpc_hbm, wo_hbm, w1_hbm, w2_hbm, yc_ref, yl_ref,
                 wg_ref, wpa_ref, wpb_ref, wpc_ref, wo_ref, w1_ref, w2_ref, stage_ref, sem,
                 *, n_ctx_tiles, layer):
    d = D_MODEL
    pieces = [(wg_hbm.at[layer, :, i * d:(i + 1) * d], wg_ref.at[:, i * d:(i + 1) * d]) for i in range(3)]
    pieces += [(w.at[layer], r) for w, r in ((wpa_hbm, wpa_ref), (wpb_hbm, wpb_ref), (wpc_hbm, wpc_ref),
                                             (wo_hbm, wo_ref))]
    pieces += [(w1_hbm.at[layer, :, i * d:(i + 1) * d], w1_ref.at[:, i * d:(i + 1) * d]) for i in range(4)]
    pieces += [(w2_hbm.at[layer, i * d:(i + 1) * d, :], w2_ref.at[i * d:(i + 1) * d, :]) for i in range(4)]
    _stage_weights_bf16(pl.program_id(0) == 0, pieces, stage_ref, sem)

    is_ctx = pl.program_id(0) < n_ctx_tiles
    x = jnp.where(is_ctx, xc_ref[...], xl_ref[...])
    sh1, sc1, g1 = mod_ref[:, 0:d], mod_ref[:, d:2 * d], mod_ref[:, 2 * d:3 * d]
    sh2, sc2, g2 = mod_ref[:, 3 * d:4 * d], mod_ref[:, 4 * d:5 * d], mod_ref[:, 5 * d:6 * d]
    h1 = (x * (1.0 + sc1) + sh1).astype(BF16)
    merged = None
    branches = ((oac_ref, oal_ref, wpa_ref), (obc_ref, obl_ref, wpb_ref), (occ_ref, ocl_ref, wpc_ref))
    for i, (oc_ref_i, ol_ref_i, wp_ref) in enumerate(branches):
        o = jnp.where(is_ctx, oc_ref_i[...], ol_ref_i[...]).astype(BF16)
        gate = jax.nn.sigmoid(_dot(h1, wg_ref[:, i * d:(i + 1) * d]) + bg_ref[:, i * d:(i + 1) * d])
        part = gate * _dot(o, wp_ref[...])
        merged = part if merged is None else merged + part
    y = _dot(merged.astype(BF16), wo_ref[...])
    x1 = _layer_norm(ALPHA * x + g1 * y, ln1g_ref[...], ln1b_ref[...])
    h2 = (x1 * (1.0 + sc2) + sh2).astype(BF16)
    u = jnp.maximum(_dot(h2, w1_ref[...]), 0.0)
    f = _dot((u * u).astype(BF16), w2_ref[...])
    x2 = _layer_norm(ALPHA * x1 + g2 * f, ln2g_ref[...], ln2b_ref[...])

    @pl.when(is_ctx)
    def _():
        yc_ref[...] = x2

    @pl.when(jnp.logical_not(is_ctx))
    def _():
        yl_ref[...] = x2


def _tail_call(xc, xl, mod4, layer, o_ctx, o_lat, wg, bg, wpa, wpb, wpc, wo, ln1, w1, w2, ln2,
               t_lat, tm=256):
    n_ctx, n_lat = xc.shape[0] // tm, xl.shape[0] // tm
    lat_tiles_per_seq = t_lat // tm

    def mod_idx(i):
        row = jnp.where(i < n_ctx, 0, 1 + (i - n_ctx) // lat_tiles_per_seq)
        return (layer, row, 0, 0)

    ctx_spec = lambda w: pl.BlockSpec((tm, w), lambda i: (jnp.minimum(i, n_ctx - 1), 0))
    lat_spec = lambda w: pl.BlockSpec((tm, w), lambda i: (jnp.maximum(i - n_ctx, 0), 0))
    vec = lambda c: _resident((None, 1, c), lambda i: (layer, 0, 0))
    in_hbm = pl.BlockSpec(memory_space=pl.ANY)
    return pl.pallas_call(
        functools.partial(_tail_kernel, n_ctx_tiles=n_ctx, layer=layer),
        grid=(n_ctx + n_lat,),
        in_specs=[
            ctx_spec(D_MODEL), lat_spec(D_MODEL),
            pl.BlockSpec((None, None, 1, D_MOD), mod_idx),
            ctx_spec(BRANCH_W), ctx_spec(BRANCH_W), ctx_spec(BRANCH_W),
            lat_spec(BRANCH_W), lat_spec(BRANCH_W), lat_spec(BRANCH_W),
            vec(D_GATE), vec(D_MODEL), vec(D_MODEL), vec(D_MODEL), vec(D_MODEL),
        ] + [in_hbm] * 7,
        out_specs=[ctx_spec(D_MODEL), lat_spec(D_MODEL)],
        out_shape=[jax.ShapeDtypeStruct(xc.shape, F32), jax.ShapeDtypeStruct(xl.shape, F32)],
        scratch_shapes=[
            pltpu.VMEM((D_MODEL, D_GATE), BF16),
            pltpu.VMEM((BRANCH_W, D_MODEL), BF16), pltpu.VMEM((BRANCH_W, D_MODEL), BF16),
            pltpu.VMEM((BRANCH_W, D_MODEL), BF16),
            pltpu.VMEM((D_MODEL, D_MODEL), BF16),
            pltpu.VMEM((D_MODEL, D_FF), BF16), pltpu.VMEM((D_FF, D_MODEL), BF16),
            pltpu.VMEM((2, D_MODEL, D_MODEL), F32),
            pltpu.SemaphoreType.DMA((2,)),
        ],
        compiler_params=_params(),
        name="tail",
    )(xc, xl, mod4, *o_ctx, *o_lat, bg, *ln1, *ln2, wg, wpa, wpb, wpc, wo, w1, w2)


def _rope_tables(n_tokens, identity_rows):
    rows = n_tokens // GRID_W
    r, col = jnp.meshgrid(jnp.arange(rows), jnp.arange(GRID_W), indexing="ij")
    r = r.reshape(-1).astype(F32)
    col = col.reshape(-1).astype(F32)
    nf = HEAD_DIM // 4
    inv = ROPE_BASE ** (-jnp.arange(nf, dtype=F32) / nf)
    ang_r = r[:, None] * inv[None, :]
    ang_c = col[:, None] * inv[None, :]
    zero = jnp.zeros_like(ang_r)
    cos = jnp.concatenate([jnp.cos(ang_r)] * 2 + [jnp.cos(ang_c)] * 2, axis=-1)
    s_next = jnp.concatenate([-jnp.sin(ang_r), zero, -jnp.sin(ang_c), zero], axis=-1)
    s_prev = jnp.concatenate([zero, jnp.sin(ang_r), zero, jnp.sin(ang_c)], axis=-1)
    ident = (jnp.ones, jnp.zeros, jnp.zeros)
    return tuple(jnp.concatenate([fill((identity_rows, LANE), F32), jnp.tile(t, (1, LANE // HEAD_DIM))], axis=0)
                 for fill, t in zip(ident, (cos, s_next, s_prev)))


def kernel(x_prompt, x_sample, c, cache_diff_k, cache_diff_v, cache_win_k, cache_win_v, state_ret,
           c_ctx, w_mod, b_mod, w_in, diff_lam, diff_norm_g, win_sink, ret_decay, ret_norm_g,
           w_pa, w_pb, w_pc, w_gate, b_gate, w_o, ln1_g, ln1_b, w_ff1, w_ff2, ln2_g, ln2_b):
    bp, tp, d = x_prompt.shape
    bs, ts, _ = x_sample.shape
    assert d == D_MODEL and w_in.shape == (DEPTH, D_MODEL, D_IN) and c.shape[0] + 1 <= MOD_ROWS
    past = cache_diff_k.shape[2]

    c_rows = jnp.concatenate(
        [c_ctx[None, :], c, jnp.zeros((MOD_ROWS - 1 - bs, d), F32)], axis=0)
    mod4 = _mod_call(c_rows, w_mod, b_mod).reshape(DEPTH, MOD_ROWS, 1, D_MOD)
    proj_tm = 512
    rope_tabs = _rope_tables(ts, proj_tm)

    ck_a = cache_diff_k.reshape(bs, DEPTH, past * DA_HEADS, 2 * HEAD_DIM)
    cv_a = cache_diff_v.reshape(bs, DEPTH, past * DA_HEADS, 2 * HEAD_DIM)
    ck_b = cache_win_k.transpose(0, 1, 3, 4, 2).reshape(bs, DEPTH, WG_KV_HEADS * HEAD_DIM, past)
    cv_b = cache_win_v.transpose(0, 1, 3, 4, 2).reshape(bs, DEPTH, WG_KV_HEADS * HEAD_DIM, past)

    row = lambda a: a.reshape(DEPTH, 1, a.shape[-1])
    ln1, ln2 = (row(ln1_g), row(ln1_b)), (row(ln2_g), row(ln2_b))
    dec4 = ret_decay.reshape(DEPTH, 2, RT_HEADS // 2, 2).transpose(0, 2, 1, 3)

    xp = x_prompt.reshape(bp * tp, d)
    xs = x_sample.reshape(bs * ts, d)
    new_caches, new_state = [], None
    for l in range(DEPTH):
        lam_init = 0.8 - 0.6 * math.exp(-0.3 * l)
        z_all, *new_caches = _proj_call(xs, xp, mod4, l, w_in, rope_tabs, new_caches, ts, tp,
                                        tm=proj_tm)
        o_paths = []
        for latent, b0, b, t in ((False, bs * ts // tp, bp, tp), (True, 0, bs, ts)):
            z3 = (z_all.reshape(-1, t, D_IN), b0, b)
            oa = _diff_call(z3, ck_a, cv_a, l, diff_lam, row(diff_norm_g), lam_init, latent)
            ob = _win_call(z3, ck_b, cv_b, l, row(win_sink), latent)
            if latent:
                oc = _ret_call(z3, state_ret, l, dec4, row(ret_norm_g), latent)
            else:
                oc, new_state = _ret_call(z3, new_state, l, dec4, row(ret_norm_g), latent)
            o_paths.append([o.reshape(b * t, BRANCH_W) for o in (oa, ob, oc)])
        xp, xs = _tail_call(xp, xs, mod4, l, o_paths[0], o_paths[1], w_gate, row(b_gate),
                            w_pa, w_pb, w_pc, w_o, ln1, w_ff1, w_ff2, ln2, ts)
    dk, dv, wk, wv = new_caches
    new_diff = [a.reshape(bp, DEPTH, tp, DA_HEADS, 2 * HEAD_DIM) for a in (dk, dv)]
    new_win = [a.reshape(bp, DEPTH, WG_KV_HEADS, HEAD_DIM, tp).transpose(0, 1, 4, 2, 3) for a in (wk, wv)]
    return (xp.reshape(bp, tp, d), xs.reshape(bs, ts, d), *new_diff, *new_win, new_state)
```

```python
import functools
import math

import jax
import jax.numpy as jnp
from jax import lax
from jax.experimental import pallas as pl
from jax.experimental.pallas import tpu as pltpu

F32 = jnp.float32
BF16 = jnp.bfloat16

D_MODEL = 1024
DEPTH = 2
GRID_W = 64
HEAD_DIM = 64
DA_HEADS = 4
WG_Q_HEADS = 8
WG_KV_HEADS = 2
WG_GROUP = WG_Q_HEADS // WG_KV_HEADS
WINDOW = 128
RT_HEADS = 4
RT_DK = 64
RT_DV = 128
BRANCH_W = 512
D_IN = 3840
D_GATE = 3 * D_MODEL
D_FF = 4 * D_MODEL
D_MOD = 6 * D_MODEL
ROPE_BASE = 10000.0
LN_EPS = 1e-5
ALPHA = (2 * DEPTH) ** 0.25
QK_SCALE = HEAD_DIM ** -0.5
LOG2E = math.log2(math.e)

LANE = 128
MOD_ROWS = 8
ONES_ROWS = 16
PROJ_COLS = 768

COL_AQ, COL_AK, COL_AV = 0, 4, 8
COL_BQ, COL_BK, COL_BV = 12, 16, 17
COL_CQ, COL_CK, COL_CV, COL_CG = 18, 20, 22, 26
ROPE_BLOCKS = tuple(range(0, 8)) + tuple(range(12, 17))

VMEM_LIMIT = 56 * 1024 * 1024
NT_DIMS = (((1,), (1,)), ((), ()))
TN_DIMS = (((0,), (0,)), ((), ()))


def _params():
    return pltpu.CompilerParams(vmem_limit_bytes=VMEM_LIMIT)


def _resident(shape, index_map):
    return pl.BlockSpec(shape, index_map, pipeline_mode=pl.Buffered(1))


def _dot(a, b):
    return jnp.dot(a, b, preferred_element_type=F32)


def _stage_weights_bf16(first_step, pieces, stage_ref, sem):
    def copy(c):
        src = pieces[c][0]
        return pltpu.make_async_copy(src, stage_ref.at[c % 2, 0:src.shape[0], 0:src.shape[1]], sem.at[c % 2])

    @pl.when(first_step)
    def _():
        for c in range(min(2, len(pieces))):
            copy(c).start()
        for c, (src, dst) in enumerate(pieces):
            copy(c).wait()
            dst[...] = stage_ref[c % 2, 0:src.shape[0], 0:src.shape[1]].astype(BF16)
            if c + 2 < len(pieces):
                copy(c + 2).start()


def _mod_kernel(c_ref, w_ref, b_ref, o_ref):
    c = c_ref[...]
    a = (c * jax.nn.sigmoid(c)).astype(BF16)
    o_ref[...] = _dot(a, w_ref[...].astype(BF16)) + b_ref[...]


def _mod_call(c_rows, w_mod, b_mod):
    tn = 1536
    return pl.pallas_call(
        _mod_kernel,
        grid=(DEPTH, D_MOD // tn),
        in_specs=[
            pl.BlockSpec((MOD_ROWS, D_MODEL), lambda l, n: (0, 0)),
            pl.BlockSpec((None, D_MODEL, tn), lambda l, n: (l, 0, n)),
            pl.BlockSpec((None, 1, tn), lambda l, n: (l, 0, n)),
        ],
        out_specs=pl.BlockSpec((None, MOD_ROWS, tn), lambda l, n: (l, 0, n)),
        out_shape=jax.ShapeDtypeStruct((DEPTH, MOD_ROWS, D_MOD), F32),
        compiler_params=_params(),
        name="mod_vectors",
    )(c_rows, w_mod, b_mod.reshape(DEPTH, 1, D_MOD))


def _proj_kernel(xl_ref, xc_ref, mod_ref, win_hbm, cos_ref, sa_ref, sb_ref, *refs, n_lat_tiles, slot, layer):
    z_ref, dk_ref, dv_ref, wk_ref, wv_ref, win_ref, stage_ref, sem = refs[-8:]
    nc = stage_ref.shape[2]
    _stage_weights_bf16(pl.program_id(0) == 0,
                        [(win_hbm.at[layer, :, c0:c0 + nc], win_ref.at[:, c0:c0 + nc]) for c0 in range(0, D_IN, nc)],
                        stage_ref, sem)
    seqs, n_slots = dk_ref.shape[0], dk_ref.shape[1]
    t_ctx = wk_ref.shape[-1]
    for ref in (dk_ref, dv_ref, wk_ref, wv_ref):
        for other in range(n_slots):
            if other != slot:
                ref[:, other] = jnp.zeros((seqs,) + ref.shape[2:], F32)
    dk_ref, dv_ref, wk_ref, wv_ref = (r.at[:, slot] for r in (dk_ref, dv_ref, wk_ref, wv_ref))
    is_lat = pl.program_id(0) < n_lat_tiles
    x = jnp.where(is_lat, xl_ref[...], xc_ref[...])
    sh1 = mod_ref[:, 0:D_MODEL]
    sc1 = mod_ref[:, D_MODEL:2 * D_MODEL]
    h = (x * (1.0 + sc1) + sh1).astype(BF16)
    for c0 in range(0, D_IN, nc):
        z = _dot(h, win_ref[:, c0:c0 + nc])
        for j in range(nc // LANE):
            blk = c0 // LANE + j
            u = z[:, j * LANE:(j + 1) * LANE]
            if blk in ROPE_BLOCKS:
                u = (u * cos_ref[...] + pltpu.roll(u, LANE - 16, 1) * sa_ref[...]
                     + pltpu.roll(u, 16, 1) * sb_ref[...])
            z_ref[:, blk * LANE:(blk + 1) * LANE] = u
            if COL_AK <= blk < COL_BQ:
                ref, head = (dk_ref, blk - COL_AK) if blk < COL_AV else (dv_ref, blk - COL_AV)
                for s in range(seqs):
                    ref[s, pl.ds(head, t_ctx, stride=DA_HEADS), :] = u[s * t_ctx:(s + 1) * t_ctx]
            elif blk in (COL_BK, COL_BV):
                ref = wk_ref if blk == COL_BK else wv_ref
                ut = u.T
                for s in range(seqs):
                    ref[s] = ut[:, s * t_ctx:(s + 1) * t_ctx]


def _proj_call(xl, xc, mod4, layer, w_in, rope_tabs, caches, t_lat, t_ctx, tm=512):
    n_lat, n_ctx = xl.shape[0] // tm, xc.shape[0] // tm
    lat_tiles_per_seq = t_lat // tm
    seqs = tm // t_ctx

    def mod_idx(i):
        row = jnp.where(i < n_lat, 1 + i // lat_tiles_per_seq, 0)
        return (layer, row, 0, 0)

    lat_idx = lambda i: jnp.minimum(i, n_lat - 1)
    ctx_idx = lambda i: jnp.maximum(i - n_lat, 0)
    rope_spec = pl.BlockSpec((tm, LANE), lambda i: (jnp.where(i < n_lat, 1 + i % lat_tiles_per_seq, 0), 0))
    n_slots, slot, first = (1, 0, layer) if caches else (DEPTH, layer, 0)
    cache_a = pl.BlockSpec((seqs, n_slots, DA_HEADS * t_ctx, 2 * HEAD_DIM), lambda i: (ctx_idx(i), first, 0, 0))
    cache_b = pl.BlockSpec((seqs, n_slots, WG_KV_HEADS * HEAD_DIM, t_ctx), lambda i: (ctx_idx(i), first, 0, 0))
    in_specs = [
        pl.BlockSpec((tm, D_MODEL), lambda i: (lat_idx(i), 0)),
        pl.BlockSpec((tm, D_MODEL), lambda i: (ctx_idx(i), 0)),
        pl.BlockSpec((None, None, 1, D_MOD), mod_idx),
        pl.BlockSpec(memory_space=pl.ANY),
        rope_spec, rope_spec, rope_spec,
    ]
    n_in = len(in_specs)
    in_specs += [pl.BlockSpec(memory_space=pl.ANY)] * len(caches)
    b_ctx = xc.shape[0] // t_ctx
    cache_shapes = [(b_ctx, DEPTH, DA_HEADS * t_ctx, 2 * HEAD_DIM)] * 2 \
        + [(b_ctx, DEPTH, WG_KV_HEADS * HEAD_DIM, t_ctx)] * 2
    return pl.pallas_call(
        functools.partial(_proj_kernel, n_lat_tiles=n_lat, slot=slot, layer=layer),
        grid=(n_lat + n_ctx,),
        in_specs=in_specs,
        out_specs=[pl.BlockSpec((tm, D_IN), lambda i: (i, 0)), cache_a, cache_a, cache_b, cache_b],
        out_shape=[jax.ShapeDtypeStruct((xl.shape[0] + xc.shape[0], D_IN), F32)]
        + [jax.ShapeDtypeStruct(s, F32) for s in cache_shapes],
        input_output_aliases={n_in + k: 1 + k for k in range(len(caches))},
        scratch_shapes=[
            pltpu.VMEM((D_MODEL, D_IN), BF16),
            pltpu.VMEM((2, D_MODEL, PROJ_COLS), F32),
            pltpu.SemaphoreType.DMA((2,)),
        ],
        compiler_params=_params(),
        name="proj",
    )(xl, xc, mod4, w_in, *rope_tabs, *caches)


def _values_t(v_parts):
    vt = jnp.concatenate([v.T for v in v_parts], axis=1)
    return jnp.concatenate([vt, jnp.ones((ONES_ROWS, vt.shape[1]), F32)], axis=0).astype(BF16)


def _diff_kernel(*refs, has_ctx, lam_init, heads, tq):
    if has_ctx:
        q_ref, k_ref, v_ref, kc_ref, vc_ref, lam_ref, g_ref, o_ref = refs
    else:
        q_ref, k_ref, v_ref, lam_ref, g_ref, o_ref = refs
    t = q_ref.shape[0]
    lv = lam_ref[...]
    lam = (jnp.exp(jnp.sum(lv[0:1] * lv[1:2], axis=-1, keepdims=True))
           - jnp.exp(jnp.sum(lv[2:3] * lv[3:4], axis=-1, keepdims=True)) + lam_init)
    lane = lax.broadcasted_iota(jnp.int32, (tq, LANE), 1)
    blocks = [(hh, i) for hh in range(heads) for i in range(t // tq)]
    kv = {}

    def scores(hh, i):
        cols = slice(hh * LANE, (hh + 1) * LANE)
        if hh not in kv:
            k_parts, v_parts = [k_ref[:, cols]], [v_ref[:, cols]]
            if has_ctx:
                past = kc_ref.shape[0] // DA_HEADS
                mine = pl.ds(pl.program_id(1), past, stride=DA_HEADS)
                k_parts.append(kc_ref[mine, :])
                v_parts.append(vc_ref[mine, :])
            kv[hh] = (jnp.concatenate(k_parts, axis=0).astype(BF16), _values_t(v_parts))
        q = q_ref[i * tq:(i + 1) * tq, cols] * (QK_SCALE * LOG2E)
        qs = jnp.concatenate([jnp.where(lane < HEAD_DIM, q, 0.0),
                              jnp.where(lane >= HEAD_DIM, q, 0.0)], axis=0).astype(BF16)
        return lax.dot_general(kv[hh][0], qs, NT_DIMS, preferred_element_type=F32)

    def finish(hh, i, et):
        ot = _dot(kv[hh][1], et)
        o = ot[0:LANE] * (1.0 / ot[LANE:LANE + 1])
        od = o[:, 0:tq] - lam * o[:, tq:2 * tq]
        yt = od * lax.rsqrt(jnp.mean(od * od, axis=0, keepdims=True) + LN_EPS)
        o_ref[i * tq:(i + 1) * tq, hh * LANE:(hh + 1) * LANE] = yt.T * g_ref[...] * (1.0 - lam_init)

    st, et = {}, {}
    for n in range(len(blocks) + 2):
        if n < len(blocks):
            st[n] = scores(*blocks[n])
        if 1 <= n <= len(blocks):
            s = st.pop(n - 1)
            et[n - 1] = jnp.exp2(s - s.max(axis=0, keepdims=True)).astype(BF16)
        if n >= 2:
            finish(*blocks[n - 2], et.pop(n - 2))


def _diff_call(z3, cache_k4, cache_v4, layer, diff_lam_l, diff_norm_g_l, lam_init, latent):
    z3, b0, b = z3
    t = z3.shape[1]
    has_ctx = latent
    heads = 1 if latent else DA_HEADS
    w = heads * LANE
    in_specs = [
        pl.BlockSpec((None, t, w), lambda bi, h: (bi + b0, 0, COL_AQ // heads + h)),
        pl.BlockSpec((None, t, w), lambda bi, h: (bi + b0, 0, COL_AK // heads + h)),
        pl.BlockSpec((None, t, w), lambda bi, h: (bi + b0, 0, COL_AV // heads + h)),
    ]
    args = [z3, z3, z3]
    if has_ctx:
        rows = cache_k4.shape[2]
        in_specs += [
            pl.BlockSpec((None, None, rows, LANE), lambda bi, h: (bi, layer, 0, 0)),
            pl.BlockSpec((None, None, rows, LANE), lambda bi, h: (bi, layer, 0, 0)),
        ]
        args += [cache_k4, cache_v4]
    in_specs += [
        pl.BlockSpec((None, 4, HEAD_DIM), lambda bi, h: (layer, 0, 0)),
        pl.BlockSpec((None, 1, LANE), lambda bi, h: (layer, 0, 0)),
    ]
    args += [diff_lam_l, diff_norm_g_l]
    tq = 256
    return pl.pallas_call(
        functools.partial(_diff_kernel, has_ctx=has_ctx, lam_init=lam_init, heads=heads, tq=tq),
        grid=(b, DA_HEADS // heads),
        in_specs=in_specs,
        out_specs=pl.BlockSpec((None, t, w), lambda bi, h: (bi, 0, h)),
        out_shape=jax.ShapeDtypeStruct((b, t, BRANCH_W), F32),
        compiler_params=_params(),
        name="diff_lat" if latent else "diff_ctx",
    )(*args)


def _win_kernel(*refs, latent, t):
    if latent:
        q_ref, k_ref, v_ref, kc_ref, vc_ref, sink_ref, o_ref = refs
    else:
        q_ref, k_ref, v_ref, sink_ref, o_ref = refs
    w = WINDOW
    if latent:
        tqb, heads_per_chain = w, WG_GROUP
    else:
        tqb, heads_per_chain = t, 2
    nb = t // tqb
    kb = k_ref[...].astype(BF16)
    vt = v_ref[...].T
    if latent:
        kcb = kc_ref[...].T.astype(BF16)
        vct = vc_ref[...]
        jj = lax.broadcasted_iota(jnp.int32, (w, w), 0)
        ii = lax.broadcasted_iota(jnp.int32, (w, w), 1)
        bias_prev = jnp.concatenate([jnp.where(jj >= ii, 0.0, -1e30)] * heads_per_chain, axis=1)
        bias_next = jnp.concatenate([jnp.where(jj <= ii, 0.0, -1e30)] * heads_per_chain, axis=1)
    chains = [(kv, h0, n)
              for n in range(nb)
              for kv in range(WG_KV_HEADS)
              for h0 in range(kv * WG_GROUP, (kv + 1) * WG_GROUP, heads_per_chain)]
    sinks = {}

    def sink_row(h0):
        if h0 not in sinks:
            sinks[h0] = jnp.concatenate(
                [jnp.broadcast_to(sink_ref[:, h:h + 1] * LOG2E, (1, tqb))
                 for h in range(h0, h0 + heads_per_chain)], axis=1)
        return sinks[h0]

    def key_blocks(n):
        return (max(n - 1, 0), min(n + 1, nb - 1)) if latent else (0, 0)

    def scores(kv, h0, n):
        lo = kv * HEAD_DIM
        rows = slice(n * tqb, (n + 1) * tqb)
        q_g = (jnp.concatenate([q_ref[rows, h * HEAD_DIM:(h + 1) * HEAD_DIM]
                                for h in range(h0, h0 + heads_per_chain)], axis=0)
               * (QK_SCALE * LOG2E)).astype(BF16)
        b0, b1 = key_blocks(n)
        keys = kb[b0 * tqb:(b1 + 1) * tqb, lo:lo + HEAD_DIM]
        if latent:
            keys = jnp.concatenate([keys, kcb[:, lo:lo + HEAD_DIM]], axis=0)
        st = lax.dot_general(keys, q_g, NT_DIMS, preferred_element_type=F32)
        if latent:
            parts = []
            for blk in range(b0, b1 + 1):
                part = st[(blk - b0) * w:(blk - b0 + 1) * w]
                if blk == n - 1:
                    part = part + bias_prev
                elif blk == n + 1:
                    part = part + bias_next
                parts.append(part)
            parts.append(st[(b1 - b0 + 1) * w:])
            st = jnp.concatenate(parts, axis=0)
        return st

    def softmax_numerator(st, h0):
        m = jnp.maximum(st.max(axis=0, keepdims=True), sink_row(h0))
        return jnp.exp2(st - m).astype(BF16), m

    def finish(kv, h0, n, et, m):
        lo = kv * HEAD_DIM
        b0, b1 = key_blocks(n)
        vals = [vt[lo:lo + HEAD_DIM, b0 * tqb:(b1 + 1) * tqb]]
        if latent:
            vals.append(vct[lo:lo + HEAD_DIM])
        vals = jnp.concatenate(vals, axis=1)
        vals = jnp.concatenate([vals, jnp.ones((ONES_ROWS, vals.shape[1]), F32)], axis=0).astype(BF16)
        ot = _dot(vals, et)
        d = ot[HEAD_DIM:HEAD_DIM + 1] + jnp.exp2(sink_row(h0) - m)
        on = ot[0:HEAD_DIM] * (1.0 / d)
        for p in range(heads_per_chain // 2):
            pair = jnp.concatenate([on[:, (2 * p) * tqb:(2 * p + 1) * tqb],
                                    on[:, (2 * p + 1) * tqb:(2 * p + 2) * tqb]], axis=0)
            c0 = (h0 // 2 + p) * LANE
            o_ref[n * tqb:(n + 1) * tqb, c0:c0 + LANE] = pair.T

    st, et = {}, {}
    for c in range(len(chains) + 2):
        if c < len(chains):
            st[c] = scores(*chains[c])
        if 1 <= c <= len(chains):
            et[c - 1] = softmax_numerator(st.pop(c - 1), chains[c - 1][1])
        if c >= 2:
            finish(*chains[c - 2], *et.pop(c - 2))


def _win_call(z3, cache_k4, cache_v4, layer, sink_l, latent):
    z3, b0, b = z3
    t = z3.shape[1]
    in_specs = [
        pl.BlockSpec((None, t, BRANCH_W), lambda bi: (bi + b0, 0, COL_BQ // 4)),
        pl.BlockSpec((None, t, LANE), lambda bi: (bi + b0, 0, COL_BK)),
        pl.BlockSpec((None, t, LANE), lambda bi: (bi + b0, 0, COL_BV)),
    ]
    args = [z3, z3, z3]
    if latent:
        past = cache_k4.shape[3]
        in_specs += [
            pl.BlockSpec((None, None, LANE, past), lambda bi: (bi, layer, 0, 0)),
            pl.BlockSpec((None, None, LANE, past), lambda bi: (bi, layer, 0, 0)),
        ]
        args += [cache_k4, cache_v4]
    in_specs.append(pl.BlockSpec((None, 1, WG_Q_HEADS), lambda bi: (layer, 0, 0)))
    args.append(sink_l)
    return pl.pallas_call(
        functools.partial(_win_kernel, latent=latent, t=t),
        grid=(b,),
        in_specs=in_specs,
        out_specs=pl.BlockSpec((None, t, BRANCH_W), lambda bi: (bi, 0, 0)),
        out_shape=jax.ShapeDtypeStruct((b, t, BRANCH_W), F32),
        compiler_params=_params(),
        name="win_lat" if latent else "win_ctx",
    )(*args)


def _log_sigmoid(x):
    return jnp.minimum(x, 0.0) - jnp.log1p(jnp.exp(-jnp.abs(x)))


def _ret_kernel(*refs, latent, t, pairs, seq_axis, slot):
    q_ref, k_ref = refs[:2]
    v_refs, cg_refs = refs[2:2 + pairs], refs[2 + pairs:2 + 2 * pairs]
    dec_ref, g_ref = refs[2 + 2 * pairs:4 + 2 * pairs]
    if latent:
        s0_ref, o_ref, d_scr = refs[4 + 2 * pairs:]
    else:
        o_ref, sfin_ref, d_scr = refs[-3:]
        for other in range(sfin_ref.shape[1]):
            if other != slot:
                sfin_ref[:, other] = jnp.zeros((sfin_ref.shape[0],) + sfin_ref.shape[2:], F32)
        sfin_ref = sfin_ref.at[:, slot]
    seqs = q_ref.shape[0]
    tq = min(t, 512)
    lg = [_log_sigmoid(dec_ref[p]) for p in range(pairs)]

    @pl.when(pl.program_id(seq_axis) == 0)
    def _():
        bw = min(t, LANE)
        nb = t // bw
        off = (lax.broadcasted_iota(jnp.int32, (bw, bw), 1)
               - lax.broadcasted_iota(jnp.int32, (bw, bw), 0)).astype(F32)
        for hh in range(2 * pairs):
            lgf, lgb = lg[hh // 2][0:1, hh % 2:hh % 2 + 1], lg[hh // 2][1:2, hh % 2:hh % 2 + 1]
            for o in range(-(nb - 1), nb):
                diff = off + float(o * bw)
                blk = (jnp.where(diff >= 0, jnp.exp(jnp.maximum(diff, 0.0) * lgf), 0.0)
                       + jnp.where(diff <= 0, jnp.exp(jnp.maximum(-diff, 0.0) * lgb), 0.0))
                for bs in range(max(0, -o), min(nb, nb - o)):
                    d_scr[hh, bs * bw:(bs + 1) * bw, (bs + o) * bw:(bs + o + 1) * bw] = blk
            if latent:
                tp = lax.broadcasted_iota(jnp.int32, (RT_DK, t), 1).astype(F32)
                d_scr[hh, t:t + RT_DK, :] = jnp.exp((tp + 1.0) * lgf)
                d_scr[hh, t + RT_DK:t + 2 * RT_DK, :] = jnp.exp((float(t) - tp) * lgb)

    if latent:
        eye = (lax.broadcasted_iota(jnp.int32, (RT_DK, RT_DK), 0)
               == lax.broadcasted_iota(jnp.int32, (RT_DK, RT_DK), 1)).astype(F32)
    heads = {}

    def head_operands(sq, hh):
        if (sq, hh) not in heads:
            p, j = hh // 2, hh % 2
            kf = k_ref[sq, :, hh * RT_DK:(hh + 1) * RT_DK] * (RT_DK ** -0.5)
            v = v_refs[p][sq, :, j * RT_DV:(j + 1) * RT_DV]
            keys, vals_t = [kf], [v.T]
            if latent:
                keys += [eye, eye]
                vals_t.append(jnp.concatenate([s0_ref[sq, 0, hh], s0_ref[sq, 1, hh]], axis=0).T)
            heads[(sq, hh)] = (jnp.concatenate(keys, axis=0).astype(BF16),
                               jnp.concatenate(vals_t, axis=1).astype(BF16))
            if not latent:
                sp = lax.broadcasted_iota(jnp.int32, (t, 1), 0).astype(F32)
                zf = jnp.exp((float(t) - 1.0 - sp) * lg[p][0:1, j:j + 1])
                zb = jnp.exp(sp * lg[p][1:2, j:j + 1])
                kz = jnp.concatenate([kf * zf, kf * zb], axis=1).astype(BF16)
                s_fb = _dot(heads[(sq, hh)][1], kz).T
                sfin_ref[sq, 0, hh] = s_fb[0:RT_DK]
                sfin_ref[sq, 1, hh] = s_fb[RT_DK:2 * RT_DK]
        return heads[(sq, hh)]

    chains = [(sq, hh, i) for sq in range(seqs) for hh in range(2 * pairs) for i in range(t // tq)]

    def scores(sq, hh, i):
        q = q_ref[sq, i * tq:(i + 1) * tq, hh * RT_DK:(hh + 1) * RT_DK].astype(BF16)
        return lax.dot_general(head_operands(sq, hh)[0], q, NT_DIMS, preferred_element_type=F32)

    def finish(sq, hh, i, at):
        yt = _dot(head_operands(sq, hh)[1], at)
        mu = jnp.mean(yt, axis=0, keepdims=True)
        yc = yt - mu
        var = jnp.mean(yc * yc, axis=0, keepdims=True)
        yn = (yc * lax.rsqrt(var + LN_EPS)).T * g_ref[...]
        cg = cg_refs[hh // 2][sq, i * tq:(i + 1) * tq, (hh % 2) * RT_DV:(hh % 2 + 1) * RT_DV]
        o_ref[sq, i * tq:(i + 1) * tq, hh * RT_DV:(hh + 1) * RT_DV] = yn * (cg * jax.nn.sigmoid(cg))

    st, at = {}, {}
    for c in range(len(chains) + 2):
        if c < len(chains):
            st[c] = scores(*chains[c])
        if 1 <= c <= len(chains):
            _, hh, i = chains[c - 1]
            at[c - 1] = (st.pop(c - 1) * d_scr[hh, :, i * tq:(i + 1) * tq]).astype(BF16)
        if c >= 2:
            finish(*chains[c - 2], at.pop(c - 2))


def _ret_call(z3, state6, layer, decay_l, ret_norm_g_l, latent):
    z3, b0, b = z3
    t = z3.shape[1]
    pairs, seqs = (1, 1) if latent else (RT_HEADS // 2, 2)
    hw = 2 * pairs
    imap = lambda f: (lambda hp, bi: f(hp, bi))
    in_specs = [
        pl.BlockSpec((seqs, t, hw * RT_DK), lambda hp, bi: (bi + b0 // seqs, 0, COL_CQ * LANE // (hw * RT_DK) + hp)),
        pl.BlockSpec((seqs, t, hw * RT_DK), lambda hp, bi: (bi + b0 // seqs, 0, COL_CK * LANE // (hw * RT_DK) + hp)),
    ]
    for col in (COL_CV, COL_CG):
        for p in range(pairs):
            in_specs.append(pl.BlockSpec(
                (seqs, t, 2 * RT_DV), imap(lambda hp, bi, c=col // 2 + p: (bi + b0 // seqs, 0, c + hp))))
    in_specs += [
        pl.BlockSpec((None, pairs, 2, 2), lambda hp, bi: (layer, hp, 0, 0)),
        pl.BlockSpec((None, 1, RT_DV), lambda hp, bi: (layer, 0, 0)),
    ]
    args = [z3] * (2 + 2 * pairs) + [decay_l, ret_norm_g_l]
    o_spec = pl.BlockSpec((seqs, t, hw * RT_DV), lambda hp, bi: (bi, 0, hp))
    o_shape = jax.ShapeDtypeStruct((b, t, BRANCH_W), F32)
    aliases, slot = {}, 0
    if latent:
        in_specs.append(pl.BlockSpec((seqs, None, 2, hw, RT_DK, RT_DV),
                                     lambda hp, bi: (bi, layer, 0, hp, 0, 0)))
        args.append(state6)
        out_specs, out_shape = o_spec, o_shape
    else:
        n_slots, slot, first = (DEPTH, layer, 0) if state6 is None else (1, 0, layer)
        if state6 is not None:
            aliases = {len(in_specs): 1}
            in_specs.append(pl.BlockSpec(memory_space=pl.ANY))
            args.append(state6)
        out_specs = [o_spec, pl.BlockSpec((seqs, n_slots, 2, hw, RT_DK, RT_DV),
                                          lambda hp, bi: (bi, first, 0, hp, 0, 0))]
        out_shape = [o_shape, jax.ShapeDtypeStruct((b, DEPTH, 2, RT_HEADS, RT_DK, RT_DV), F32)]
    n_keys = t + (2 * RT_DK if latent else 0)
    return pl.pallas_call(
        functools.partial(_ret_kernel, latent=latent, t=t, pairs=pairs, seq_axis=1, slot=slot),
        grid=(RT_HEADS // hw, b // seqs),
        in_specs=in_specs,
        out_specs=out_specs,
        out_shape=out_shape,
        input_output_aliases=aliases,
        scratch_shapes=[pltpu.VMEM((hw, n_keys, t), F32)],
        compiler_params=_params(),
        name="ret_lat" if latent else "ret_ctx",
    )(*args)


def _layer_norm(x, g, b):
    mu = jnp.mean(x, axis=-1, keepdims=True)
    xc = x - mu
    var = jnp.mean(xc * xc, axis=-1, keepdims=True)
    return xc * lax.rsqrt(var + LN_EPS) * g + b


def _tail_kernel(xc_ref, xl_ref, mod_ref, oac_ref, obc_ref, occ_ref, oal_ref, obl_ref, ocl_ref,
                 bg_ref, ln1g_ref, ln1b_ref, ln2g_ref, ln2b_ref,
                 wg_hbm, wpa_hbm, wpb_hbm, wpc_hbm, wo_hbm, w1_hbm, w2_hbm, yc_ref, yl_ref,
                 wg_ref, wpa_ref, wpb_ref, wpc_ref, wo_ref, w1_ref, w2_ref, stage_ref, sem,
                 x1_scr, f_scr, g2_scr, *, n_ctx_tiles, n_tiles, layer):
    d = D_MODEL
    step = pl.program_id(0)
    pieces = [(wg_hbm.at[layer, :, i * d:(i + 1) * d], wg_ref.at[:, i * d:(i + 1) * d]) for i in range(3)]
    pieces += [(w.at[layer], r) for w, r in ((wpa_hbm, wpa_ref), (wpb_hbm, wpb_ref), (wpc_hbm, wpc_ref),
                                             (wo_hbm, wo_ref))]
    pieces += [(w1_hbm.at[layer, :, i * d:(i + 1) * d], w1_ref.at[:, i * d:(i + 1) * d]) for i in range(4)]
    pieces += [(w2_hbm.at[layer, i * d:(i + 1) * d, :], w2_ref.at[i * d:(i + 1) * d, :]) for i in range(4)]
    _stage_weights_bf16(step == 0, pieces, stage_ref, sem)

    @pl.when(step == 0)
    def _():
        x1_scr[...] = jnp.zeros(x1_scr.shape, F32)
        f_scr[...] = jnp.zeros(f_scr.shape, F32)
        g2_scr[...] = jnp.zeros(g2_scr.shape, F32)

    def finish_previous():
        return _layer_norm(ALPHA * x1_scr[...] + g2_scr[...] * f_scr[...], ln2g_ref[...], ln2b_ref[...])

    def store_previous(x2):
        prev_is_ctx = step - 1 < n_ctx_tiles

        @pl.when(prev_is_ctx)
        def _():
            yc_ref[...] = x2

        @pl.when(jnp.logical_not(prev_is_ctx))
        def _():
            yl_ref[...] = x2

    @pl.when(step < n_tiles)
    def _():
        is_ctx = step < n_ctx_tiles
        x = jnp.where(is_ctx, xc_ref[...], xl_ref[...])
        sh1, sc1, g1 = mod_ref[:, 0:d], mod_ref[:, d:2 * d], mod_ref[:, 2 * d:3 * d]
        sh2, sc2, g2 = mod_ref[:, 3 * d:4 * d], mod_ref[:, 4 * d:5 * d], mod_ref[:, 5 * d:6 * d]
        h1 = (x * (1.0 + sc1) + sh1).astype(BF16)
        merged = None
        branches = ((oac_ref, oal_ref, wpa_ref), (obc_ref, obl_ref, wpb_ref), (occ_ref, ocl_ref, wpc_ref))
        for i, (oc_ref_i, ol_ref_i, wp_ref) in enumerate(branches):
            o = jnp.where(is_ctx, oc_ref_i[...], ol_ref_i[...]).astype(BF16)
            gate = jax.nn.sigmoid(_dot(h1, wg_ref[:, i * d:(i + 1) * d]) + bg_ref[:, i * d:(i + 1) * d])
            part = gate * _dot(o, wp_ref[...])
            merged = part if merged is None else merged + part
            if i == 0:
                x2_prev = finish_previous()
        y = _dot(merged.astype(BF16), wo_ref[...])
        x1 = _layer_norm(ALPHA * x + g1 * y, ln1g_ref[...], ln1b_ref[...])
        h2 = (x1 * (1.0 + sc2) + sh2).astype(BF16)
        u = jnp.maximum(_dot(h2, w1_ref[...]), 0.0)
        f = _dot((u * u).astype(BF16), w2_ref[...])
        x1_scr[...] = x1
        f_scr[...] = f
        g2_scr[...] = g2
        store_previous(x2_prev)

    @pl.when(step == n_tiles)
    def _():
        store_previous(finish_previous())


def _tail_call(xc, xl, mod4, layer, o_ctx, o_lat, wg, bg, wpa, wpb, wpc, wo, ln1, w1, w2, ln2,
               t_lat, tm=256):
    n_ctx, n_lat = xc.shape[0] // tm, xl.shape[0] // tm
    lat_tiles_per_seq = t_lat // tm

    n_tiles = n_ctx + n_lat
    cur = lambda i: jnp.minimum(i, n_tiles - 1)
    prev = lambda i: jnp.maximum(i - 1, 0)

    def mod_idx(i):
        row = jnp.where(cur(i) < n_ctx, 0, 1 + (cur(i) - n_ctx) // lat_tiles_per_seq)
        return (layer, row, 0, 0)

    def row_spec(w, which, lat):
        if lat:
            return pl.BlockSpec((tm, w), lambda i: (jnp.maximum(which(i) - n_ctx, 0), 0))
        return pl.BlockSpec((tm, w), lambda i: (jnp.minimum(which(i), n_ctx - 1), 0))

    ctx_spec = lambda w: row_spec(w, cur, False)
    lat_spec = lambda w: row_spec(w, cur, True)
    vec = lambda c: _resident((None, 1, c), lambda i: (layer, 0, 0))
    in_hbm = pl.BlockSpec(memory_space=pl.ANY)
    return pl.pallas_call(
        functools.partial(_tail_kernel, n_ctx_tiles=n_ctx, n_tiles=n_tiles, layer=layer),
        grid=(n_tiles + 1,),
        in_specs=[
            ctx_spec(D_MODEL), lat_spec(D_MODEL),
            pl.BlockSpec((None, None, 1, D_MOD), mod_idx),
            ctx_spec(BRANCH_W), ctx_spec(BRANCH_W), ctx_spec(BRANCH_W),
            lat_spec(BRANCH_W), lat_spec(BRANCH_W), lat_spec(BRANCH_W),
            vec(D_GATE), vec(D_MODEL), vec(D_MODEL), vec(D_MODEL), vec(D_MODEL),
        ] + [in_hbm] * 7,
        out_specs=[row_spec(D_MODEL, prev, False), row_spec(D_MODEL, prev, True)],
        out_shape=[jax.ShapeDtypeStruct(xc.shape, F32), jax.ShapeDtypeStruct(xl.shape, F32)],
        scratch_shapes=[
            pltpu.VMEM((D_MODEL, D_GATE), BF16),
            pltpu.VMEM((BRANCH_W, D_MODEL), BF16), pltpu.VMEM((BRANCH_W, D_MODEL), BF16),
            pltpu.VMEM((BRANCH_W, D_MODEL), BF16),
            pltpu.VMEM((D_MODEL, D_MODEL), BF16),
            pltpu.VMEM((D_MODEL, D_FF), BF16), pltpu.VMEM((D_FF, D_MODEL), BF16),
            pltpu.VMEM((2, D_MODEL, D_MODEL), F32),
            pltpu.SemaphoreType.DMA((2,)),
            pltpu.VMEM((tm, D_MODEL), F32), pltpu.VMEM((tm, D_MODEL), F32), pltpu.VMEM((1, D_MODEL), F32),
        ],
        compiler_params=_params(),
        name="tail",
    )(xc, xl, mod4, *o_ctx, *o_lat, bg, *ln1, *ln2, wg, wpa, wpb, wpc, wo, w1, w2)


def _rope_tables(n_tokens, identity_rows):
    rows = n_tokens // GRID_W
    r, col = jnp.meshgrid(jnp.arange(rows), jnp.arange(GRID_W), indexing="ij")
    r = r.reshape(-1).astype(F32)
    col = col.reshape(-1).astype(F32)
    nf = HEAD_DIM // 4
    inv = ROPE_BASE ** (-jnp.arange(nf, dtype=F32) / nf)
    ang_r = r[:, None] * inv[None, :]
    ang_c = col[:, None] * inv[None, :]
    zero = jnp.zeros_like(ang_r)
    cos = jnp.concatenate([jnp.cos(ang_r)] * 2 + [jnp.cos(ang_c)] * 2, axis=-1)
    s_next = jnp.concatenate([-jnp.sin(ang_r), zero, -jnp.sin(ang_c), zero], axis=-1)
    s_prev = jnp.concatenate([zero, jnp.sin(ang_r), zero, jnp.sin(ang_c)], axis=-1)
    ident = (jnp.ones, jnp.zeros, jnp.zeros)
    return tuple(jnp.concatenate([fill((identity_rows, LANE), F32), jnp.tile(t, (1, LANE // HEAD_DIM))], axis=0)
                 for fill, t in zip(ident, (cos, s_next, s_prev)))


def kernel(x_prompt, x_sample, c, cache_diff_k, cache_diff_v, cache_win_k, cache_win_v, state_ret,
           c_ctx, w_mod, b_mod, w_in, diff_lam, diff_norm_g, win_sink, ret_decay, ret_norm_g,
           w_pa, w_pb, w_pc, w_gate, b_gate, w_o, ln1_g, ln1_b, w_ff1, w_ff2, ln2_g, ln2_b):
    bp, tp, d = x_prompt.shape
    bs, ts, _ = x_sample.shape
    assert d == D_MODEL and w_in.shape == (DEPTH, D_MODEL, D_IN) and c.shape[0] + 1 <= MOD_ROWS
    past = cache_diff_k.shape[2]

    c_rows = jnp.concatenate(
        [c_ctx[None, :], c, jnp.zeros((MOD_ROWS - 1 - bs, d), F32)], axis=0)
    mod4 = _mod_call(c_rows, w_mod, b_mod).reshape(DEPTH, MOD_ROWS, 1, D_MOD)
    proj_tm = 512
    rope_tabs = _rope_tables(ts, proj_tm)

    ck_a = cache_diff_k.reshape(bs, DEPTH, past * DA_HEADS, 2 * HEAD_DIM)
    cv_a = cache_diff_v.reshape(bs, DEPTH, past * DA_HEADS, 2 * HEAD_DIM)
    ck_b = cache_win_k.transpose(0, 1, 3, 4, 2).reshape(bs, DEPTH, WG_KV_HEADS * HEAD_DIM, past)
    cv_b = cache_win_v.transpose(0, 1, 3, 4, 2).reshape(bs, DEPTH, WG_KV_HEADS * HEAD_DIM, past)

    row = lambda a: a.reshape(DEPTH, 1, a.shape[-1])
    ln1, ln2 = (row(ln1_g), row(ln1_b)), (row(ln2_g), row(ln2_b))
    dec4 = ret_decay.reshape(DEPTH, 2, RT_HEADS // 2, 2).transpose(0, 2, 1, 3)

    xp = x_prompt.reshape(bp * tp, d)
    xs = x_sample.reshape(bs * ts, d)
    new_caches, new_state = [], None
    for l in range(DEPTH):
        lam_init = 0.8 - 0.6 * math.exp(-0.3 * l)
        z_all, *new_caches = _proj_call(xs, xp, mod4, l, w_in, rope_tabs, new_caches, ts, tp,
                                        tm=proj_tm)
        o_paths = []
        for latent, b0, b, t in ((False, bs * ts // tp, bp, tp), (True, 0, bs, ts)):
            z3 = (z_all.reshape(-1, t, D_IN), b0, b)
            oa = _diff_call(z3, ck_a, cv_a, l, diff_lam, row(diff_norm_g), lam_init, latent)
            ob = _win_call(z3, ck_b, cv_b, l, row(win_sink), latent)
            if latent:
                oc = _ret_call(z3, state_ret, l, dec4, row(ret_norm_g), latent)
            else:
                oc, new_state = _ret_call(z3, new_state, l, dec4, row(ret_norm_g), latent)
            o_paths.append([o.reshape(b * t, BRANCH_W) for o in (oa, ob, oc)])
        xp, xs = _tail_call(xp, xs, mod4, l, o_paths[0], o_paths[1], w_gate, row(b_gate),
                            w_pa, w_pb, w_pc, w_o, ln1, w_ff1, w_ff2, ln2, ts)
    dk, dv, wk, wv = new_caches
    new_diff = [a.reshape(bp, DEPTH, tp, DA_HEADS, 2 * HEAD_DIM) for a in (dk, dv)]
    new_win = [a.reshape(bp, DEPTH, WG_KV_HEADS, HEAD_DIM, tp).transpose(0, 1, 4, 2, 3) for a in (wk, wv)]
    return (xp.reshape(bp, tp, d), xs.reshape(bs, ts, d), *new_diff, *new_win, new_state)
```

```python
import functools
import math

import jax
import jax.numpy as jnp
from jax import lax
from jax.experimental import pallas as pl
from jax.experimental.pallas import tpu as pltpu

F32 = jnp.float32
BF16 = jnp.bfloat16

D_MODEL = 1024
DEPTH = 2
GRID_W = 64
HEAD_DIM = 64
DA_HEADS = 4
WG_Q_HEADS = 8
WG_KV_HEADS = 2
WG_GROUP = WG_Q_HEADS // WG_KV_HEADS
WINDOW = 128
RT_HEADS = 4
RT_DK = 64
RT_DV = 128
BRANCH_W = 512
D_IN = 3840
D_GATE = 3 * D_MODEL
D_FF = 4 * D_MODEL
D_MOD = 6 * D_MODEL
ROPE_BASE = 10000.0
LN_EPS = 1e-5
ALPHA = (2 * DEPTH) ** 0.25
QK_SCALE = HEAD_DIM ** -0.5
LOG2E = math.log2(math.e)

LANE = 128
MOD_ROWS = 8
ONES_ROWS = 16
PROJ_COLS = 768

COL_AQ, COL_AK, COL_AV = 0, 4, 8
COL_BQ, COL_BK, COL_BV = 12, 16, 17
COL_CQ, COL_CK, COL_CV, COL_CG = 18, 20, 22, 26
ROPE_BLOCKS = tuple(range(0, 8)) + tuple(range(12, 17))

VMEM_LIMIT = 56 * 1024 * 1024
NT_DIMS = (((1,), (1,)), ((), ()))
TN_DIMS = (((0,), (0,)), ((), ()))


def _params():
    return pltpu.CompilerParams(vmem_limit_bytes=VMEM_LIMIT)


def _resident(shape, index_map):
    return pl.BlockSpec(shape, index_map, pipeline_mode=pl.Buffered(1))


def _dot(a, b):
    return jnp.dot(a, b, preferred_element_type=F32)


def _zero_other_layers(ref, slot):
    for other in range(ref.shape[1]):
        if other != slot:
            ref[:, other] = jnp.zeros((ref.shape[0],) + ref.shape[2:], F32)
    return ref.at[:, slot]


def _stage_weights_bf16(first_step, pieces, stage_ref, sem):
    def copy(c):
        src = pieces[c][0]
        return pltpu.make_async_copy(src, stage_ref.at[c % 2, 0:src.shape[0], 0:src.shape[1]], sem.at[c % 2])

    @pl.when(first_step)
    def _():
        for c in range(min(2, len(pieces))):
            copy(c).start()
        for c, (src, dst) in enumerate(pieces):
            copy(c).wait()
            dst[...] = stage_ref[c % 2, 0:src.shape[0], 0:src.shape[1]].astype(BF16)
            if c + 2 < len(pieces):
                copy(c + 2).start()


def _mod_kernel(c_ref, w_ref, b_ref, o_ref):
    c = c_ref[...]
    a = (c * jax.nn.sigmoid(c)).astype(BF16)
    o_ref[...] = _dot(a, w_ref[...].astype(BF16)) + b_ref[...]


def _mod_call(c_rows, w_mod, b_mod):
    tn = 1536
    return pl.pallas_call(
        _mod_kernel,
        grid=(DEPTH, D_MOD // tn),
        in_specs=[
            pl.BlockSpec((MOD_ROWS, D_MODEL), lambda l, n: (0, 0)),
            pl.BlockSpec((None, D_MODEL, tn), lambda l, n: (l, 0, n)),
            pl.BlockSpec((None, 1, tn), lambda l, n: (l, 0, n)),
        ],
        out_specs=pl.BlockSpec((None, MOD_ROWS, tn), lambda l, n: (l, 0, n)),
        out_shape=jax.ShapeDtypeStruct((DEPTH, MOD_ROWS, D_MOD), F32),
        compiler_params=_params(),
        name="mod_vectors",
    )(c_rows, w_mod, b_mod.reshape(DEPTH, 1, D_MOD))


def _proj_kernel(xl_ref, xc_ref, mod_ref, win_hbm, cos_ref, sa_ref, sb_ref, *refs, n_lat_tiles, slot, layer):
    z_ref, dk_ref, dv_ref, wk_ref, wv_ref, win_ref, stage_ref, sem = refs[-8:]
    nc = stage_ref.shape[2]
    _stage_weights_bf16(pl.program_id(0) == 0,
                        [(win_hbm.at[layer, :, c0:c0 + nc], win_ref.at[:, c0:c0 + nc]) for c0 in range(0, D_IN, nc)],
                        stage_ref, sem)
    seqs, t_ctx = dk_ref.shape[0], wk_ref.shape[-1]
    dk_ref, dv_ref, wk_ref, wv_ref = (_zero_other_layers(r, slot) for r in (dk_ref, dv_ref, wk_ref, wv_ref))
    is_lat = pl.program_id(0) < n_lat_tiles
    x = jnp.where(is_lat, xl_ref[...], xc_ref[...])
    sh1 = mod_ref[:, 0:D_MODEL]
    sc1 = mod_ref[:, D_MODEL:2 * D_MODEL]
    h = (x * (1.0 + sc1) + sh1).astype(BF16)
    for c0 in range(0, D_IN, nc):
        z = _dot(h, win_ref[:, c0:c0 + nc])
        for j in range(nc // LANE):
            blk = c0 // LANE + j
            u = z[:, j * LANE:(j + 1) * LANE]
            if blk in ROPE_BLOCKS:
                u = (u * cos_ref[...] + pltpu.roll(u, LANE - 16, 1) * sa_ref[...]
                     + pltpu.roll(u, 16, 1) * sb_ref[...])
            z_ref[:, blk * LANE:(blk + 1) * LANE] = u
            if COL_AK <= blk < COL_BQ:
                ref, head = (dk_ref, blk - COL_AK) if blk < COL_AV else (dv_ref, blk - COL_AV)
                for s in range(seqs):
                    ref[s, pl.ds(head, t_ctx, stride=DA_HEADS), :] = u[s * t_ctx:(s + 1) * t_ctx]
            elif blk in (COL_BK, COL_BV):
                ref = wk_ref if blk == COL_BK else wv_ref
                ut = u.T
                for s in range(seqs):
                    ref[s] = ut[:, s * t_ctx:(s + 1) * t_ctx]


def _proj_call(xl, xc, mod4, layer, w_in, rope_tabs, caches, t_lat, t_ctx, tm=512):
    n_lat, n_ctx = xl.shape[0] // tm, xc.shape[0] // tm
    lat_tiles_per_seq = t_lat // tm
    seqs = tm // t_ctx

    def mod_idx(i):
        row = jnp.where(i < n_lat, 1 + i // lat_tiles_per_seq, 0)
        return (layer, row, 0, 0)

    lat_idx = lambda i: jnp.minimum(i, n_lat - 1)
    ctx_idx = lambda i: jnp.maximum(i - n_lat, 0)
    rope_spec = pl.BlockSpec((tm, LANE), lambda i: (jnp.where(i < n_lat, 1 + i % lat_tiles_per_seq, 0), 0))
    n_slots, slot, first = (1, 0, layer) if caches else (DEPTH, layer, 0)
    cache_a = pl.BlockSpec((seqs, n_slots, DA_HEADS * t_ctx, 2 * HEAD_DIM), lambda i: (ctx_idx(i), first, 0, 0))
    cache_b = pl.BlockSpec((seqs, n_slots, WG_KV_HEADS * HEAD_DIM, t_ctx), lambda i: (ctx_idx(i), first, 0, 0))
    in_specs = [
        pl.BlockSpec((tm, D_MODEL), lambda i: (lat_idx(i), 0)),
        pl.BlockSpec((tm, D_MODEL), lambda i: (ctx_idx(i), 0)),
        pl.BlockSpec((None, None, 1, D_MOD), mod_idx),
        pl.BlockSpec(memory_space=pl.ANY),
        rope_spec, rope_spec, rope_spec,
    ]
    n_in = len(in_specs)
    in_specs += [pl.BlockSpec(memory_space=pl.ANY)] * len(caches)
    b_ctx = xc.shape[0] // t_ctx
    cache_shapes = [(b_ctx, DEPTH, DA_HEADS * t_ctx, 2 * HEAD_DIM)] * 2 \
        + [(b_ctx, DEPTH, WG_KV_HEADS * HEAD_DIM, t_ctx)] * 2
    return pl.pallas_call(
        functools.partial(_proj_kernel, n_lat_tiles=n_lat, slot=slot, layer=layer),
        grid=(n_lat + n_ctx,),
        in_specs=in_specs,
        out_specs=[pl.BlockSpec((tm, D_IN), lambda i: (i, 0)), cache_a, cache_a, cache_b, cache_b],
        out_shape=[jax.ShapeDtypeStruct((xl.shape[0] + xc.shape[0], D_IN), F32)]
        + [jax.ShapeDtypeStruct(s, F32) for s in cache_shapes],
        input_output_aliases={n_in + k: 1 + k for k in range(len(caches))},
        scratch_shapes=[
            pltpu.VMEM((D_MODEL, D_IN), BF16),
            pltpu.VMEM((2, D_MODEL, PROJ_COLS), F32),
            pltpu.SemaphoreType.DMA((2,)),
        ],
        compiler_params=_params(),
        name="proj",
    )(xl, xc, mod4, w_in, *rope_tabs, *caches)


def _values_t(v_parts):
    vt = jnp.concatenate([v.T for v in v_parts], axis=1)
    return jnp.concatenate([vt, jnp.ones((ONES_ROWS, vt.shape[1]), F32)], axis=0).astype(BF16)


def _run_chains(chains):
    a, b = {}, {}
    n = len(chains)
    for c in range(n + 2):
        if c < n:
            a[c] = chains[c][0]()
        if 1 <= c <= n:
            b[c - 1] = chains[c - 1][1](a.pop(c - 1))
        if c >= 2:
            chains[c - 2][2](b.pop(c - 2))


def _diff_chains(q_ref, k_ref, v_ref, o_ref, lam_ref, g_ref, cached_kv, *, lam_init, tq):
    t, heads = q_ref.shape[0], q_ref.shape[1] // LANE
    lv = lam_ref[...]
    lam = (jnp.exp(jnp.sum(lv[0:1] * lv[1:2], axis=-1, keepdims=True))
           - jnp.exp(jnp.sum(lv[2:3] * lv[3:4], axis=-1, keepdims=True)) + lam_init)
    lane = lax.broadcasted_iota(jnp.int32, (tq, LANE), 1)
    kv = {}

    def scores(hh, i):
        cols = slice(hh * LANE, (hh + 1) * LANE)
        if hh not in kv:
            k_parts, v_parts = [k_ref[:, cols]], [v_ref[:, cols]]
            extra = cached_kv(hh)
            if extra is not None:
                k_parts.append(extra[0])
                v_parts.append(extra[1])
            kv[hh] = (jnp.concatenate(k_parts, axis=0).astype(BF16), _values_t(v_parts))
        q = q_ref[i * tq:(i + 1) * tq, cols] * (QK_SCALE * LOG2E)
        qs = jnp.concatenate([jnp.where(lane < HEAD_DIM, q, 0.0),
                              jnp.where(lane >= HEAD_DIM, q, 0.0)], axis=0).astype(BF16)
        return lax.dot_general(kv[hh][0], qs, NT_DIMS, preferred_element_type=F32)

    def weights(s):
        return jnp.exp2(s - s.max(axis=0, keepdims=True)).astype(BF16)

    def finish(hh, i, et):
        ot = _dot(kv[hh][1], et)
        o = ot[0:LANE] * (1.0 / ot[LANE:LANE + 1])
        od = o[:, 0:tq] - lam * o[:, tq:2 * tq]
        yt = od * lax.rsqrt(jnp.mean(od * od, axis=0, keepdims=True) + LN_EPS)
        o_ref[i * tq:(i + 1) * tq, hh * LANE:(hh + 1) * LANE] = yt.T * g_ref[...] * (1.0 - lam_init)

    return [(functools.partial(scores, hh, i), weights, functools.partial(finish, hh, i))
            for hh in range(heads) for i in range(t // tq)]


def _diff_kernel(q_ref, k_ref, v_ref, kc_ref, vc_ref, lam_ref, g_ref, o_ref, *, lam_init, tq):
    def cached_kv(_):
        mine = pl.ds(pl.program_id(1), kc_ref.shape[0] // DA_HEADS, stride=DA_HEADS)
        return kc_ref[mine, :], vc_ref[mine, :]

    _run_chains(_diff_chains(q_ref, k_ref, v_ref, o_ref, lam_ref, g_ref, cached_kv,
                             lam_init=lam_init, tq=tq))


def _diff_call(z3, cache_k4, cache_v4, layer, diff_lam, diff_norm_g, lam_init):
    z3, b0, b = z3
    t = z3.shape[1]
    rows = cache_k4.shape[2]
    return pl.pallas_call(
        functools.partial(_diff_kernel, lam_init=lam_init, tq=256),
        grid=(b, DA_HEADS),
        in_specs=[
            pl.BlockSpec((None, t, LANE), lambda bi, h: (bi + b0, 0, COL_AQ + h)),
            pl.BlockSpec((None, t, LANE), lambda bi, h: (bi + b0, 0, COL_AK + h)),
            pl.BlockSpec((None, t, LANE), lambda bi, h: (bi + b0, 0, COL_AV + h)),
            pl.BlockSpec((None, None, rows, LANE), lambda bi, h: (bi, layer, 0, 0)),
            pl.BlockSpec((None, None, rows, LANE), lambda bi, h: (bi, layer, 0, 0)),
            pl.BlockSpec((None, 4, HEAD_DIM), lambda bi, h: (layer, 0, 0)),
            pl.BlockSpec((None, 1, LANE), lambda bi, h: (layer, 0, 0)),
        ],
        out_specs=pl.BlockSpec((None, t, LANE), lambda bi, h: (bi, 0, h)),
        out_shape=jax.ShapeDtypeStruct((b, t, BRANCH_W), F32),
        compiler_params=_params(),
        name="diff_lat",
    )(z3, z3, z3, cache_k4, cache_v4, diff_lam, diff_norm_g)


def _win_chains(q_ref, k_ref, v_ref, o_ref, sink_ref, kc_ref=None, vc_ref=None):
    latent = kc_ref is not None
    t = q_ref.shape[0]
    w = WINDOW
    if latent:
        tqb, heads_per_chain = w, WG_GROUP
    else:
        tqb, heads_per_chain = t, 2
    nb = t // tqb
    kb = k_ref[...].astype(BF16)
    vt = v_ref[...].T
    if latent:
        kcb = kc_ref[...].T.astype(BF16)
        vct = vc_ref[...]
        jj = lax.broadcasted_iota(jnp.int32, (w, w), 0)
        ii = lax.broadcasted_iota(jnp.int32, (w, w), 1)
        bias_prev = jnp.concatenate([jnp.where(jj >= ii, 0.0, -1e30)] * heads_per_chain, axis=1)
        bias_next = jnp.concatenate([jnp.where(jj <= ii, 0.0, -1e30)] * heads_per_chain, axis=1)
    chains = [(kv, h0, n)
              for n in range(nb)
              for kv in range(WG_KV_HEADS)
              for h0 in range(kv * WG_GROUP, (kv + 1) * WG_GROUP, heads_per_chain)]
    sinks = {}

    def sink_row(h0):
        if h0 not in sinks:
            sinks[h0] = jnp.concatenate(
                [jnp.broadcast_to(sink_ref[:, h:h + 1] * LOG2E, (1, tqb))
                 for h in range(h0, h0 + heads_per_chain)], axis=1)
        return sinks[h0]

    def key_blocks(n):
        return (max(n - 1, 0), min(n + 1, nb - 1)) if latent else (0, 0)

    def scores(kv, h0, n):
        lo = kv * HEAD_DIM
        rows = slice(n * tqb, (n + 1) * tqb)
        q_g = (jnp.concatenate([q_ref[rows, h * HEAD_DIM:(h + 1) * HEAD_DIM]
                                for h in range(h0, h0 + heads_per_chain)], axis=0)
               * (QK_SCALE * LOG2E)).astype(BF16)
        b0, b1 = key_blocks(n)
        keys = kb[b0 * tqb:(b1 + 1) * tqb, lo:lo + HEAD_DIM]
        if latent:
            keys = jnp.concatenate([keys, kcb[:, lo:lo + HEAD_DIM]], axis=0)
        st = lax.dot_general(keys, q_g, NT_DIMS, preferred_element_type=F32)
        if latent:
            parts = []
            for blk in range(b0, b1 + 1):
                part = st[(blk - b0) * w:(blk - b0 + 1) * w]
                if blk == n - 1:
                    part = part + bias_prev
                elif blk == n + 1:
                    part = part + bias_next
                parts.append(part)
            parts.append(st[(b1 - b0 + 1) * w:])
            st = jnp.concatenate(parts, axis=0)
        return st

    def weights(h0, st):
        m = jnp.maximum(st.max(axis=0, keepdims=True), sink_row(h0))
        return jnp.exp2(st - m).astype(BF16), m

    def finish(kv, h0, n, et_m):
        et, m = et_m
        lo = kv * HEAD_DIM
        b0, b1 = key_blocks(n)
        vals = [vt[lo:lo + HEAD_DIM, b0 * tqb:(b1 + 1) * tqb]]
        if latent:
            vals.append(vct[lo:lo + HEAD_DIM])
        vals = jnp.concatenate(vals, axis=1)
        vals = jnp.concatenate([vals, jnp.ones((ONES_ROWS, vals.shape[1]), F32)], axis=0).astype(BF16)
        ot = _dot(vals, et)
        d = ot[HEAD_DIM:HEAD_DIM + 1] + jnp.exp2(sink_row(h0) - m)
        on = ot[0:HEAD_DIM] * (1.0 / d)
        for p in range(heads_per_chain // 2):
            pair = jnp.concatenate([on[:, (2 * p) * tqb:(2 * p + 1) * tqb],
                                    on[:, (2 * p + 1) * tqb:(2 * p + 2) * tqb]], axis=0)
            c0 = (h0 // 2 + p) * LANE
            o_ref[n * tqb:(n + 1) * tqb, c0:c0 + LANE] = pair.T

    return [(functools.partial(scores, kv, h0, n), functools.partial(weights, h0),
             functools.partial(finish, kv, h0, n)) for kv, h0, n in chains]


def _win_kernel(q_ref, k_ref, v_ref, kc_ref, vc_ref, sink_ref, o_ref):
    _run_chains(_win_chains(q_ref, k_ref, v_ref, o_ref, sink_ref, kc_ref, vc_ref))


def _win_call(z3, cache_k4, cache_v4, layer, sink):
    z3, b0, b = z3
    t = z3.shape[1]
    past = cache_k4.shape[3]
    return pl.pallas_call(
        _win_kernel,
        grid=(b,),
        in_specs=[
            pl.BlockSpec((None, t, BRANCH_W), lambda bi: (bi + b0, 0, COL_BQ // 4)),
            pl.BlockSpec((None, t, LANE), lambda bi: (bi + b0, 0, COL_BK)),
            pl.BlockSpec((None, t, LANE), lambda bi: (bi + b0, 0, COL_BV)),
            pl.BlockSpec((None, None, LANE, past), lambda bi: (bi, layer, 0, 0)),
            pl.BlockSpec((None, None, LANE, past), lambda bi: (bi, layer, 0, 0)),
            pl.BlockSpec((None, 1, WG_Q_HEADS), lambda bi: (layer, 0, 0)),
        ],
        out_specs=pl.BlockSpec((None, t, BRANCH_W), lambda bi: (bi, 0, 0)),
        out_shape=jax.ShapeDtypeStruct((b, t, BRANCH_W), F32),
        compiler_params=_params(),
        name="win_lat",
    )(z3, z3, z3, cache_k4, cache_v4, sink)


def _log_sigmoid(x):
    return jnp.minimum(x, 0.0) - jnp.log1p(jnp.exp(-jnp.abs(x)))


def _ret_fill_decay(d_scr, lg, t, latent):
    bw = min(t, LANE)
    nb = t // bw
    off = (lax.broadcasted_iota(jnp.int32, (bw, bw), 1)
           - lax.broadcasted_iota(jnp.int32, (bw, bw), 0)).astype(F32)
    for hh in range(d_scr.shape[0]):
        lgf, lgb = lg[hh // 2][0:1, hh % 2:hh % 2 + 1], lg[hh // 2][1:2, hh % 2:hh % 2 + 1]
        for o in range(-(nb - 1), nb):
            diff = off + float(o * bw)
            blk = (jnp.where(diff >= 0, jnp.exp(jnp.maximum(diff, 0.0) * lgf), 0.0)
                   + jnp.where(diff <= 0, jnp.exp(jnp.maximum(-diff, 0.0) * lgb), 0.0))
            for bs in range(max(0, -o), min(nb, nb - o)):
                d_scr[hh, bs * bw:(bs + 1) * bw, (bs + o) * bw:(bs + o + 1) * bw] = blk
        if latent:
            tp = lax.broadcasted_iota(jnp.int32, (RT_DK, t), 1).astype(F32)
            d_scr[hh, t:t + RT_DK, :] = jnp.exp((tp + 1.0) * lgf)
            d_scr[hh, t + RT_DK:t + 2 * RT_DK, :] = jnp.exp((float(t) - tp) * lgb)


def _ret_chains(q_ref, k_ref, v_refs, cg_refs, lg, g_ref, o_ref, d_scr, s0_ref=None, sfin_ref=None):
    latent = s0_ref is not None
    seqs, t = q_ref.shape[0], q_ref.shape[1]
    pairs = len(v_refs)
    tq = min(t, 512)
    if latent:
        eye = (lax.broadcasted_iota(jnp.int32, (RT_DK, RT_DK), 0)
               == lax.broadcasted_iota(jnp.int32, (RT_DK, RT_DK), 1)).astype(F32)
    heads = {}

    def head_operands(sq, hh):
        if (sq, hh) not in heads:
            p, j = hh // 2, hh % 2
            kf = k_ref[sq, :, hh * RT_DK:(hh + 1) * RT_DK] * (RT_DK ** -0.5)
            v = v_refs[p][sq, :, j * RT_DV:(j + 1) * RT_DV]
            keys, vals_t = [kf], [v.T]
            if latent:
                keys += [eye, eye]
                vals_t.append(jnp.concatenate([s0_ref[sq, 0, hh], s0_ref[sq, 1, hh]], axis=0).T)
            heads[(sq, hh)] = (jnp.concatenate(keys, axis=0).astype(BF16),
                               jnp.concatenate(vals_t, axis=1).astype(BF16))
            if not latent:
                sp = lax.broadcasted_iota(jnp.int32, (t, 1), 0).astype(F32)
                zf = jnp.exp((float(t) - 1.0 - sp) * lg[p][0:1, j:j + 1])
                zb = jnp.exp(sp * lg[p][1:2, j:j + 1])
                kz = jnp.concatenate([kf * zf, kf * zb], axis=1).astype(BF16)
                s_fb = _dot(heads[(sq, hh)][1], kz).T
                sfin_ref[sq, 0, hh] = s_fb[0:RT_DK]
                sfin_ref[sq, 1, hh] = s_fb[RT_DK:2 * RT_DK]
        return heads[(sq, hh)]

    chains = [(sq, hh, i) for sq in range(seqs) for hh in range(2 * pairs) for i in range(t // tq)]

    def scores(sq, hh, i):
        q = q_ref[sq, i * tq:(i + 1) * tq, hh * RT_DK:(hh + 1) * RT_DK].astype(BF16)
        return lax.dot_general(head_operands(sq, hh)[0], q, NT_DIMS, preferred_element_type=F32)

    def finish(sq, hh, i, at):
        yt = _dot(head_operands(sq, hh)[1], at)
        mu = jnp.mean(yt, axis=0, keepdims=True)
        yc = yt - mu
        var = jnp.mean(yc * yc, axis=0, keepdims=True)
        yn = (yc * lax.rsqrt(var + LN_EPS)).T * g_ref[...]
        cg = cg_refs[hh // 2][sq, i * tq:(i + 1) * tq, (hh % 2) * RT_DV:(hh % 2 + 1) * RT_DV]
        o_ref[sq, i * tq:(i + 1) * tq, hh * RT_DV:(hh + 1) * RT_DV] = yn * (cg * jax.nn.sigmoid(cg))

    def weights(hh, i, st):
        return (st * d_scr[hh, :, i * tq:(i + 1) * tq]).astype(BF16)

    return [(functools.partial(scores, sq, hh, i), functools.partial(weights, hh, i),
             functools.partial(finish, sq, hh, i)) for sq, hh, i in chains]


def _ret_kernel(q_ref, k_ref, v_ref, cg_ref, dec_ref, g_ref, s0_ref, o_ref, d_scr):
    lg = [_log_sigmoid(dec_ref[0])]

    @pl.when(pl.program_id(1) == 0)
    def _():
        _ret_fill_decay(d_scr, lg, q_ref.shape[1], True)

    _run_chains(_ret_chains(q_ref, k_ref, [v_ref], [cg_ref], lg, g_ref, o_ref, d_scr, s0_ref=s0_ref))


def _ret_call(z3, state6, layer, dec4, ret_norm_g):
    z3, b0, b = z3
    t = z3.shape[1]
    return pl.pallas_call(
        _ret_kernel,
        grid=(RT_HEADS // 2, b),
        in_specs=[
            pl.BlockSpec((1, t, LANE), lambda hp, bi: (bi + b0, 0, COL_CQ + hp)),
            pl.BlockSpec((1, t, LANE), lambda hp, bi: (bi + b0, 0, COL_CK + hp)),
            pl.BlockSpec((1, t, 2 * RT_DV), lambda hp, bi: (bi + b0, 0, COL_CV // 2 + hp)),
            pl.BlockSpec((1, t, 2 * RT_DV), lambda hp, bi: (bi + b0, 0, COL_CG // 2 + hp)),
            pl.BlockSpec((None, 1, 2, 2), lambda hp, bi: (layer, hp, 0, 0)),
            pl.BlockSpec((None, 1, RT_DV), lambda hp, bi: (layer, 0, 0)),
            pl.BlockSpec((1, None, 2, 2, RT_DK, RT_DV), lambda hp, bi: (bi, layer, 0, hp, 0, 0)),
        ],
        out_specs=pl.BlockSpec((1, t, 2 * RT_DV), lambda hp, bi: (bi, 0, hp)),
        out_shape=jax.ShapeDtypeStruct((b, t, BRANCH_W), F32),
        scratch_shapes=[pltpu.VMEM((2, t + 2 * RT_DK, t), F32)],
        compiler_params=_params(),
        name="ret_lat",
    )(z3, z3, z3, z3, dec4, ret_norm_g, state6)


def _ctx_mix_kernel(z_ref, lam_ref, dng_ref, sink_ref, dec_ref, rng_ref, *refs, lam_init, slot):
    oa_ref, ob_ref, oc_ref, sfin_ref, d_scr = refs[-5:]
    sfin_ref = _zero_other_layers(sfin_ref, slot)
    seqs, t = z_ref.shape[0], z_ref.shape[1]
    cols = lambda c0, c1: slice(c0 * LANE, c1 * LANE)
    lg = [_log_sigmoid(dec_ref[p]) for p in range(RT_HEADS // 2)]

    @pl.when(pl.program_id(0) == 0)
    def _():
        _ret_fill_decay(d_scr, lg, t, False)

    chains = []
    for sq in range(seqs):
        chains += _diff_chains(z_ref.at[sq, :, cols(COL_AQ, COL_AK)], z_ref.at[sq, :, cols(COL_AK, COL_AV)],
                               z_ref.at[sq, :, cols(COL_AV, COL_BQ)], oa_ref.at[sq], lam_ref, dng_ref,
                               lambda _: None, lam_init=lam_init, tq=t)
        chains += _win_chains(z_ref.at[sq, :, cols(COL_BQ, COL_BK)], z_ref.at[sq, :, cols(COL_BK, COL_BV)],
                              z_ref.at[sq, :, cols(COL_BV, COL_CQ)], ob_ref.at[sq], sink_ref)
    chains += _ret_chains(z_ref.at[:, :, cols(COL_CQ, COL_CK)], z_ref.at[:, :, cols(COL_CK, COL_CV)],
                          [z_ref.at[:, :, cols(COL_CV + 2 * p, COL_CV + 2 * p + 2)] for p in range(2)],
                          [z_ref.at[:, :, cols(COL_CG + 2 * p, COL_CG + 2 * p + 2)] for p in range(2)],
                          lg, rng_ref, oc_ref, d_scr, sfin_ref=sfin_ref)
    _run_chains(chains)


def _ctx_mix_call(z3, new_state, layer, diff_lam, diff_norm_g, sink, dec4, ret_norm_g, lam_init, seqs=2):
    z3, b0, b = z3
    t = z3.shape[1]
    layer_spec = lambda *s: pl.BlockSpec((None,) + s, lambda bi: (layer,) + (0,) * len(s))
    in_specs = [
        pl.BlockSpec((seqs, t, D_IN), lambda bi: (bi + b0 // seqs, 0, 0)),
        layer_spec(4, HEAD_DIM), layer_spec(1, LANE), layer_spec(1, WG_Q_HEADS),
        layer_spec(RT_HEADS // 2, 2, 2), layer_spec(1, RT_DV),
    ]
    args = [z3, diff_lam, diff_norm_g, sink, dec4, ret_norm_g]
    n_slots, slot, first = (DEPTH, layer, 0) if new_state is None else (1, 0, layer)
    aliases = {}
    if new_state is not None:
        aliases = {len(in_specs): 3}
        in_specs.append(pl.BlockSpec(memory_space=pl.ANY))
        args.append(new_state)
    o_spec = pl.BlockSpec((seqs, t, BRANCH_W), lambda bi: (bi, 0, 0))
    o_shape = jax.ShapeDtypeStruct((b, t, BRANCH_W), F32)
    return pl.pallas_call(
        functools.partial(_ctx_mix_kernel, lam_init=lam_init, slot=slot),
        grid=(b // seqs,),
        in_specs=in_specs,
        out_specs=[o_spec, o_spec, o_spec,
                   pl.BlockSpec((seqs, n_slots, 2, RT_HEADS, RT_DK, RT_DV), lambda bi: (bi, first, 0, 0, 0, 0))],
        out_shape=[o_shape, o_shape, o_shape,
                   jax.ShapeDtypeStruct((b, DEPTH, 2, RT_HEADS, RT_DK, RT_DV), F32)],
        input_output_aliases=aliases,
        scratch_shapes=[pltpu.VMEM((RT_HEADS, t, t), F32)],
        compiler_params=_params(),
        name="mix_ctx",
    )(*args)


def _layer_norm(x, g, b):
    mu = jnp.mean(x, axis=-1, keepdims=True)
    xc = x - mu
    var = jnp.mean(xc * xc, axis=-1, keepdims=True)
    return xc * lax.rsqrt(var + LN_EPS) * g + b


def _tail_kernel(xc_ref, xl_ref, mod_ref, oac_ref, obc_ref, occ_ref, oal_ref, obl_ref, ocl_ref,
                 bg_ref, ln1g_ref, ln1b_ref, ln2g_ref, ln2b_ref,
                 wg_hbm, wpa_hbm, wpb_hbm, wpc_hbm, wo_hbm, w1_hbm, w2_hbm, yc_ref, yl_ref,
                 wg_ref, wpa_ref, wpb_ref, wpc_ref, wo_ref, w1_ref, w2_ref, stage_ref, sem,
                 *, n_ctx_tiles, layer):
    d = D_MODEL
    pieces = [(wg_hbm.at[layer, :, i * d:(i + 1) * d], wg_ref.at[:, i * d:(i + 1) * d]) for i in range(3)]
    pieces += [(w.at[layer], r) for w, r in ((wpa_hbm, wpa_ref), (wpb_hbm, wpb_ref), (wpc_hbm, wpc_ref),
                                             (wo_hbm, wo_ref))]
    pieces += [(w1_hbm.at[layer, :, i * d:(i + 1) * d], w1_ref.at[:, i * d:(i + 1) * d]) for i in range(4)]
    pieces += [(w2_hbm.at[layer, i * d:(i + 1) * d, :], w2_ref.at[i * d:(i + 1) * d, :]) for i in range(4)]
    _stage_weights_bf16(pl.program_id(0) == 0, pieces, stage_ref, sem)

    is_ctx = pl.program_id(0) < n_ctx_tiles
    x = jnp.where(is_ctx, xc_ref[...], xl_ref[...])
    sh1, sc1, g1 = mod_ref[:, 0:d], mod_ref[:, d:2 * d], mod_ref[:, 2 * d:3 * d]
    sh2, sc2, g2 = mod_ref[:, 3 * d:4 * d], mod_ref[:, 4 * d:5 * d], mod_ref[:, 5 * d:6 * d]
    h1 = (x * (1.0 + sc1) + sh1).astype(BF16)
    merged = None
    branches = ((oac_ref, oal_ref, wpa_ref), (obc_ref, obl_ref, wpb_ref), (occ_ref, ocl_ref, wpc_ref))
    for i, (oc_ref_i, ol_ref_i, wp_ref) in enumerate(branches):
        o = jnp.where(is_ctx, oc_ref_i[...], ol_ref_i[...]).astype(BF16)
        gate = jax.nn.sigmoid(_dot(h1, wg_ref[:, i * d:(i + 1) * d]) + bg_ref[:, i * d:(i + 1) * d])
        part = gate * _dot(o, wp_ref[...])
        merged = part if merged is None else merged + part
    y = _dot(merged.astype(BF16), wo_ref[...])
    x1 = _layer_norm(ALPHA * x + g1 * y, ln1g_ref[...], ln1b_ref[...])
    h2 = (x1 * (1.0 + sc2) + sh2).astype(BF16)
    u = jnp.maximum(_dot(h2, w1_ref[...]), 0.0)
    f = _dot((u * u).astype(BF16), w2_ref[...])
    x2 = _layer_norm(ALPHA * x1 + g2 * f, ln2g_ref[...], ln2b_ref[...])

    @pl.when(is_ctx)
    def _():
        yc_ref[...] = x2

    @pl.when(jnp.logical_not(is_ctx))
    def _():
        yl_ref[...] = x2


def _tail_call(xc, xl, mod4, layer, o_ctx, o_lat, wg, bg, wpa, wpb, wpc, wo, ln1, w1, w2, ln2,
               t_lat, tm=256):
    n_ctx, n_lat = xc.shape[0] // tm, xl.shape[0] // tm
    lat_tiles_per_seq = t_lat // tm

    def mod_idx(i):
        row = jnp.where(i < n_ctx, 0, 1 + (i - n_ctx) // lat_tiles_per_seq)
        return (layer, row, 0, 0)

    ctx_spec = lambda w: pl.BlockSpec((tm, w), lambda i: (jnp.minimum(i, n_ctx - 1), 0))
    lat_spec = lambda w: pl.BlockSpec((tm, w), lambda i: (jnp.maximum(i - n_ctx, 0), 0))
    vec = lambda c: _resident((None, 1, c), lambda i: (layer, 0, 0))
    in_hbm = pl.BlockSpec(memory_space=pl.ANY)
    return pl.pallas_call(
        functools.partial(_tail_kernel, n_ctx_tiles=n_ctx, layer=layer),
        grid=(n_ctx + n_lat,),
        in_specs=[
            ctx_spec(D_MODEL), lat_spec(D_MODEL),
            pl.BlockSpec((None, None, 1, D_MOD), mod_idx),
            ctx_spec(BRANCH_W), ctx_spec(BRANCH_W), ctx_spec(BRANCH_W),
            lat_spec(BRANCH_W), lat_spec(BRANCH_W), lat_spec(BRANCH_W),
            vec(D_GATE), vec(D_MODEL), vec(D_MODEL), vec(D_MODEL), vec(D_MODEL),
        ] + [in_hbm] * 7,
        out_specs=[ctx_spec(D_MODEL), lat_spec(D_MODEL)],
        out_shape=[jax.ShapeDtypeStruct(xc.shape, F32), jax.ShapeDtypeStruct(xl.shape, F32)],
        scratch_shapes=[
            pltpu.VMEM((D_MODEL, D_GATE), BF16),
            pltpu.VMEM((BRANCH_W, D_MODEL), BF16), pltpu.VMEM((BRANCH_W, D_MODEL), BF16),
            pltpu.VMEM((BRANCH_W, D_MODEL), BF16),
            pltpu.VMEM((D_MODEL, D_MODEL), BF16),
            pltpu.VMEM((D_MODEL, D_FF), BF16), pltpu.VMEM((D_FF, D_MODEL), BF16),
            pltpu.VMEM((2, D_MODEL, D_MODEL), F32),
            pltpu.SemaphoreType.DMA((2,)),
        ],
        compiler_params=_params(),
        name="tail",
    )(xc, xl, mod4, *o_ctx, *o_lat, bg, *ln1, *ln2, wg, wpa, wpb, wpc, wo, w1, w2)


def _rope_tables(n_tokens, identity_rows):
    rows = n_tokens // GRID_W
    r, col = jnp.meshgrid(jnp.arange(rows), jnp.arange(GRID_W), indexing="ij")
    r = r.reshape(-1).astype(F32)
    col = col.reshape(-1).astype(F32)
    nf = HEAD_DIM // 4
    inv = ROPE_BASE ** (-jnp.arange(nf, dtype=F32) / nf)
    ang_r = r[:, None] * inv[None, :]
    ang_c = col[:, None] * inv[None, :]
    zero = jnp.zeros_like(ang_r)
    cos = jnp.concatenate([jnp.cos(ang_r)] * 2 + [jnp.cos(ang_c)] * 2, axis=-1)
    s_next = jnp.concatenate([-jnp.sin(ang_r), zero, -jnp.sin(ang_c), zero], axis=-1)
    s_prev = jnp.concatenate([zero, jnp.sin(ang_r), zero, jnp.sin(ang_c)], axis=-1)
    ident = (jnp.ones, jnp.zeros, jnp.zeros)
    return tuple(jnp.concatenate([fill((identity_rows, LANE), F32), jnp.tile(t, (1, LANE // HEAD_DIM))], axis=0)
                 for fill, t in zip(ident, (cos, s_next, s_prev)))


def kernel(x_prompt, x_sample, c, cache_diff_k, cache_diff_v, cache_win_k, cache_win_v, state_ret,
           c_ctx, w_mod, b_mod, w_in, diff_lam, diff_norm_g, win_sink, ret_decay, ret_norm_g,
           w_pa, w_pb, w_pc, w_gate, b_gate, w_o, ln1_g, ln1_b, w_ff1, w_ff2, ln2_g, ln2_b):
    bp, tp, d = x_prompt.shape
    bs, ts, _ = x_sample.shape
    assert d == D_MODEL and w_in.shape == (DEPTH, D_MODEL, D_IN) and c.shape[0] + 1 <= MOD_ROWS
    past = cache_diff_k.shape[2]

    c_rows = jnp.concatenate(
        [c_ctx[None, :], c, jnp.zeros((MOD_ROWS - 1 - bs, d), F32)], axis=0)
    mod4 = _mod_call(c_rows, w_mod, b_mod).reshape(DEPTH, MOD_ROWS, 1, D_MOD)
    proj_tm = 512
    rope_tabs = _rope_tables(ts, proj_tm)

    ck_a = cache_diff_k.reshape(bs, DEPTH, past * DA_HEADS, 2 * HEAD_DIM)
    cv_a = cache_diff_v.reshape(bs, DEPTH, past * DA_HEADS, 2 * HEAD_DIM)
    ck_b = cache_win_k.transpose(0, 1, 3, 4, 2).reshape(bs, DEPTH, WG_KV_HEADS * HEAD_DIM, past)
    cv_b = cache_win_v.transpose(0, 1, 3, 4, 2).reshape(bs, DEPTH, WG_KV_HEADS * HEAD_DIM, past)

    row = lambda a: a.reshape(DEPTH, 1, a.shape[-1])
    ln1, ln2 = (row(ln1_g), row(ln1_b)), (row(ln2_g), row(ln2_b))
    dec4 = ret_decay.reshape(DEPTH, 2, RT_HEADS // 2, 2).transpose(0, 2, 1, 3)

    xp = x_prompt.reshape(bp * tp, d)
    xs = x_sample.reshape(bs * ts, d)
    new_caches, new_state = [], None
    for l in range(DEPTH):
        lam_init = 0.8 - 0.6 * math.exp(-0.3 * l)
        z_all, *new_caches = _proj_call(xs, xp, mod4, l, w_in, rope_tabs, new_caches, ts, tp,
                                        tm=proj_tm)
        z_ctx = (z_all.reshape(-1, tp, D_IN), bs * ts // tp, bp)
        z_lat = (z_all.reshape(-1, ts, D_IN), 0, bs)
        *o_ctx, new_state = _ctx_mix_call(z_ctx, new_state, l, diff_lam, row(diff_norm_g), row(win_sink),
                                          dec4, row(ret_norm_g), lam_init)
        o_lat = [_diff_call(z_lat, ck_a, cv_a, l, diff_lam, row(diff_norm_g), lam_init),
                 _win_call(z_lat, ck_b, cv_b, l, row(win_sink)),
                 _ret_call(z_lat, state_ret, l, dec4, row(ret_norm_g))]
        xp, xs = _tail_call(xp, xs, mod4, l, [o.reshape(bp * tp, BRANCH_W) for o in o_ctx],
                            [o.reshape(bs * ts, BRANCH_W) for o in o_lat], w_gate, row(b_gate),
                            w_pa, w_pb, w_pc, w_o, ln1, w_ff1, w_ff2, ln2, ts)
    dk, dv, wk, wv = new_caches
    new_diff = [a.reshape(bp, DEPTH, tp, DA_HEADS, 2 * HEAD_DIM) for a in (dk, dv)]
    new_win = [a.reshape(bp, DEPTH, WG_KV_HEADS, HEAD_DIM, tp).transpose(0, 1, 4, 2, 3) for a in (wk, wv)]
    return (xp.reshape(bp, tp, d), xs.reshape(bs, ts, d), *new_diff, *new_win, new_state)
```

```python
import functools
import math

import jax
import jax.numpy as jnp
from jax import lax
from jax.experimental import pallas as pl
from jax.experimental.pallas import tpu as pltpu

F32 = jnp.float32
BF16 = jnp.bfloat16

D_MODEL = 1024
DEPTH = 2
GRID_W = 64
HEAD_DIM = 64
DA_HEADS = 4
WG_Q_HEADS = 8
WG_KV_HEADS = 2
WG_GROUP = WG_Q_HEADS // WG_KV_HEADS
WINDOW = 128
RT_HEADS = 4
RT_DK = 64
RT_DV = 128
BRANCH_W = 512
D_IN = 3840
D_GATE = 3 * D_MODEL
D_FF = 4 * D_MODEL
D_MOD = 6 * D_MODEL
ROPE_BASE = 10000.0
LN_EPS = 1e-5
ALPHA = (2 * DEPTH) ** 0.25
QK_SCALE = HEAD_DIM ** -0.5
LOG2E = math.log2(math.e)

LANE = 128
MOD_ROWS = 8
ONES_ROWS = 16
PROJ_COLS = 768

COL_AQ, COL_AK, COL_AV = 0, 4, 8
COL_BQ, COL_BK, COL_BV = 12, 16, 17
COL_CQ, COL_CK, COL_CV, COL_CG = 18, 20, 22, 26
ROPE_BLOCKS = tuple(range(0, 8)) + tuple(range(12, 17))

VMEM_LIMIT = 56 * 1024 * 1024
NT_DIMS = (((1,), (1,)), ((), ()))
TN_DIMS = (((0,), (0,)), ((), ()))


def _params():
    return pltpu.CompilerParams(vmem_limit_bytes=VMEM_LIMIT)


def _resident(shape, index_map):
    return pl.BlockSpec(shape, index_map, pipeline_mode=pl.Buffered(1))


def _dot(a, b):
    return jnp.dot(a, b, preferred_element_type=F32)


def _zero_other_layers(ref, slot):
    for other in range(ref.shape[1]):
        if other != slot:
            ref[:, other] = jnp.zeros((ref.shape[0],) + ref.shape[2:], F32)
    return ref.at[:, slot]


def _stage_weights_bf16(first_step, pieces, stage_ref, sem):
    def copy(c):
        src = pieces[c][0]
        return pltpu.make_async_copy(src, stage_ref.at[c % 2, 0:src.shape[0], 0:src.shape[1]], sem.at[c % 2])

    @pl.when(first_step)
    def _():
        for c in range(min(2, len(pieces))):
            copy(c).start()
        for c, (src, dst) in enumerate(pieces):
            copy(c).wait()
            dst[...] = stage_ref[c % 2, 0:src.shape[0], 0:src.shape[1]].astype(BF16)
            if c + 2 < len(pieces):
                copy(c + 2).start()


def _mod_kernel(c_ref, w_ref, b_ref, o_ref):
    c = c_ref[...]
    a = (c * jax.nn.sigmoid(c)).astype(BF16)
    o_ref[...] = _dot(a, w_ref[...].astype(BF16)) + b_ref[...]


def _mod_call(c_rows, w_mod, b_mod):
    tn = 1536
    return pl.pallas_call(
        _mod_kernel,
        grid=(DEPTH, D_MOD // tn),
        in_specs=[
            pl.BlockSpec((MOD_ROWS, D_MODEL), lambda l, n: (0, 0)),
            pl.BlockSpec((None, D_MODEL, tn), lambda l, n: (l, 0, n)),
            pl.BlockSpec((None, 1, tn), lambda l, n: (l, 0, n)),
        ],
        out_specs=pl.BlockSpec((None, MOD_ROWS, tn), lambda l, n: (l, 0, n)),
        out_shape=jax.ShapeDtypeStruct((DEPTH, MOD_ROWS, D_MOD), F32),
        compiler_params=_params(),
        name="mod_vectors",
    )(c_rows, w_mod, b_mod.reshape(DEPTH, 1, D_MOD))


def _proj_kernel(xl_ref, xc_ref, mod_ref, win_hbm, cos_ref, sa_ref, sb_ref, *refs, n_lat_tiles, slot, layer):
    z_ref, dk_ref, dv_ref, wk_ref, wv_ref, win_ref, stage_ref, sem = refs[-8:]
    nc = stage_ref.shape[2]
    _stage_weights_bf16(pl.program_id(0) == 0,
                        [(win_hbm.at[layer, :, c0:c0 + nc], win_ref.at[:, c0:c0 + nc]) for c0 in range(0, D_IN, nc)],
                        stage_ref, sem)
    seqs, t_ctx = dk_ref.shape[0], wk_ref.shape[-1]
    dk_ref, dv_ref, wk_ref, wv_ref = (_zero_other_layers(r, slot) for r in (dk_ref, dv_ref, wk_ref, wv_ref))
    is_lat = pl.program_id(0) < n_lat_tiles
    x = jnp.where(is_lat, xl_ref[...], xc_ref[...])
    sh1 = mod_ref[:, 0:D_MODEL]
    sc1 = mod_ref[:, D_MODEL:2 * D_MODEL]
    h = (x * (1.0 + sc1) + sh1).astype(BF16)
    for c0 in range(0, D_IN, nc):
        z = _dot(h, win_ref[:, c0:c0 + nc])
        for j in range(nc // LANE):
            blk = c0 // LANE + j
            u = z[:, j * LANE:(j + 1) * LANE]
            if blk in ROPE_BLOCKS:
                u = (u * cos_ref[...] + pltpu.roll(u, LANE - 16, 1) * sa_ref[...]
                     + pltpu.roll(u, 16, 1) * sb_ref[...])
            z_ref[:, blk * LANE:(blk + 1) * LANE] = u
            if COL_AK <= blk < COL_BQ:
                ref, head = (dk_ref, blk - COL_AK) if blk < COL_AV else (dv_ref, blk - COL_AV)
                for s in range(seqs):
                    ref[s, pl.ds(head, t_ctx, stride=DA_HEADS), :] = u[s * t_ctx:(s + 1) * t_ctx]
            elif blk in (COL_BK, COL_BV):
                ref = wk_ref if blk == COL_BK else wv_ref
                ut = u.T
                for s in range(seqs):
                    ref[s] = ut[:, s * t_ctx:(s + 1) * t_ctx]


def _proj_call(xl, xc, mod4, layer, w_in, rope_tabs, caches, t_lat, t_ctx, tm=512):
    n_lat, n_ctx = xl.shape[0] // tm, xc.shape[0] // tm
    lat_tiles_per_seq = t_lat // tm
    seqs = tm // t_ctx

    def mod_idx(i):
        row = jnp.where(i < n_lat, 1 + i // lat_tiles_per_seq, 0)
        return (layer, row, 0, 0)

    lat_idx = lambda i: jnp.minimum(i, n_lat - 1)
    ctx_idx = lambda i: jnp.maximum(i - n_lat, 0)
    rope_spec = pl.BlockSpec((tm, LANE), lambda i: (jnp.where(i < n_lat, 1 + i % lat_tiles_per_seq, 0), 0))
    n_slots, slot, first = (1, 0, layer) if caches else (DEPTH, layer, 0)
    cache_a = pl.BlockSpec((seqs, n_slots, DA_HEADS * t_ctx, 2 * HEAD_DIM), lambda i: (ctx_idx(i), first, 0, 0))
    cache_b = pl.BlockSpec((seqs, n_slots, WG_KV_HEADS * HEAD_DIM, t_ctx), lambda i: (ctx_idx(i), first, 0, 0))
    in_specs = [
        pl.BlockSpec((tm, D_MODEL), lambda i: (lat_idx(i), 0)),
        pl.BlockSpec((tm, D_MODEL), lambda i: (ctx_idx(i), 0)),
        pl.BlockSpec((None, None, 1, D_MOD), mod_idx),
        pl.BlockSpec(memory_space=pl.ANY),
        rope_spec, rope_spec, rope_spec,
    ]
    n_in = len(in_specs)
    in_specs += [pl.BlockSpec(memory_space=pl.ANY)] * len(caches)
    b_ctx = xc.shape[0] // t_ctx
    cache_shapes = [(b_ctx, DEPTH, DA_HEADS * t_ctx, 2 * HEAD_DIM)] * 2 \
        + [(b_ctx, DEPTH, WG_KV_HEADS * HEAD_DIM, t_ctx)] * 2
    return pl.pallas_call(
        functools.partial(_proj_kernel, n_lat_tiles=n_lat, slot=slot, layer=layer),
        grid=(n_lat + n_ctx,),
        in_specs=in_specs,
        out_specs=[pl.BlockSpec((tm, D_IN), lambda i: (i, 0)), cache_a, cache_a, cache_b, cache_b],
        out_shape=[jax.ShapeDtypeStruct((xl.shape[0] + xc.shape[0], D_IN), F32)]
        + [jax.ShapeDtypeStruct(s, F32) for s in cache_shapes],
        input_output_aliases={n_in + k: 1 + k for k in range(len(caches))},
        scratch_shapes=[
            pltpu.VMEM((D_MODEL, D_IN), BF16),
            pltpu.VMEM((2, D_MODEL, PROJ_COLS), F32),
            pltpu.SemaphoreType.DMA((2,)),
        ],
        compiler_params=_params(),
        name="proj",
    )(xl, xc, mod4, w_in, *rope_tabs, *caches)


def _values_t(v_parts):
    vt = jnp.concatenate([v.T for v in v_parts], axis=1)
    return jnp.concatenate([vt, jnp.ones((ONES_ROWS, vt.shape[1]), F32)], axis=0).astype(BF16)


def _run_chains(chains):
    a, b = {}, {}
    n = len(chains)
    for c in range(n + 2):
        if c < n:
            a[c] = chains[c][0]()
        if 1 <= c <= n:
            b[c - 1] = chains[c - 1][1](a.pop(c - 1))
        if c >= 2:
            chains[c - 2][2](b.pop(c - 2))


def _diff_chains(q_ref, k_ref, v_ref, o_ref, lam_ref, g_ref, cached_kv, *, lam_init, tq):
    t, heads = q_ref.shape[0], q_ref.shape[1] // LANE
    lv = lam_ref[...]
    lam = (jnp.exp(jnp.sum(lv[0:1] * lv[1:2], axis=-1, keepdims=True))
           - jnp.exp(jnp.sum(lv[2:3] * lv[3:4], axis=-1, keepdims=True)) + lam_init)
    lane = lax.broadcasted_iota(jnp.int32, (tq, LANE), 1)
    kv = {}

    def scores(hh, i):
        cols = slice(hh * LANE, (hh + 1) * LANE)
        if hh not in kv:
            k_parts, v_parts = [k_ref[:, cols]], [v_ref[:, cols]]
            extra = cached_kv(hh)
            if extra is not None:
                k_parts.append(extra[0])
                v_parts.append(extra[1])
            kv[hh] = (jnp.concatenate(k_parts, axis=0).astype(BF16), _values_t(v_parts))
        q = q_ref[i * tq:(i + 1) * tq, cols] * (QK_SCALE * LOG2E)
        qs = jnp.concatenate([jnp.where(lane < HEAD_DIM, q, 0.0),
                              jnp.where(lane >= HEAD_DIM, q, 0.0)], axis=0).astype(BF16)
        return lax.dot_general(kv[hh][0], qs, NT_DIMS, preferred_element_type=F32)

    def weights(s):
        return jnp.exp2(s - s.max(axis=0, keepdims=True)).astype(BF16)

    def finish(hh, i, et):
        ot = _dot(kv[hh][1], et)
        o = ot[0:LANE] * (1.0 / ot[LANE:LANE + 1])
        od = o[:, 0:tq] - lam * o[:, tq:2 * tq]
        yt = od * lax.rsqrt(jnp.mean(od * od, axis=0, keepdims=True) + LN_EPS)
        o_ref[i * tq:(i + 1) * tq, hh * LANE:(hh + 1) * LANE] = yt.T * g_ref[...] * (1.0 - lam_init)

    return [(functools.partial(scores, hh, i), weights, functools.partial(finish, hh, i))
            for hh in range(heads) for i in range(t // tq)]


def _win_chains(q_ref, k_ref, v_ref, o_ref, sink_ref, kc_ref=None, vc_ref=None, kv_heads=range(WG_KV_HEADS)):
    latent = kc_ref is not None
    head0 = kv_heads[0] * WG_GROUP
    t = q_ref.shape[0]
    w = WINDOW
    if latent:
        tqb, heads_per_chain = w, WG_GROUP
    else:
        tqb, heads_per_chain = t, 2
    nb = t // tqb
    kb = k_ref[...].astype(BF16)
    vt = v_ref[...].T
    if latent:
        kcb = kc_ref[...].T.astype(BF16)
        vct = vc_ref[...]
        jj = lax.broadcasted_iota(jnp.int32, (w, w), 0)
        ii = lax.broadcasted_iota(jnp.int32, (w, w), 1)
        bias_prev = jnp.concatenate([jnp.where(jj >= ii, 0.0, -1e30)] * heads_per_chain, axis=1)
        bias_next = jnp.concatenate([jnp.where(jj <= ii, 0.0, -1e30)] * heads_per_chain, axis=1)
    chains = [(kv, h0, n)
              for n in range(nb)
              for kv in kv_heads
              for h0 in range(kv * WG_GROUP, (kv + 1) * WG_GROUP, heads_per_chain)]
    sinks = {}

    def sink_row(h0):
        if h0 not in sinks:
            sinks[h0] = jnp.concatenate(
                [jnp.broadcast_to(sink_ref[:, h:h + 1] * LOG2E, (1, tqb))
                 for h in range(h0, h0 + heads_per_chain)], axis=1)
        return sinks[h0]

    def key_blocks(n):
        return (max(n - 1, 0), min(n + 1, nb - 1)) if latent else (0, 0)

    def scores(kv, h0, n):
        lo = kv * HEAD_DIM
        rows = slice(n * tqb, (n + 1) * tqb)
        q_g = (jnp.concatenate([q_ref[rows, (h - head0) * HEAD_DIM:(h - head0 + 1) * HEAD_DIM]
                                for h in range(h0, h0 + heads_per_chain)], axis=0)
               * (QK_SCALE * LOG2E)).astype(BF16)
        b0, b1 = key_blocks(n)
        keys = kb[b0 * tqb:(b1 + 1) * tqb, lo:lo + HEAD_DIM]
        if latent:
            keys = jnp.concatenate([keys, kcb[:, lo:lo + HEAD_DIM]], axis=0)
        st = lax.dot_general(keys, q_g, NT_DIMS, preferred_element_type=F32)
        if latent:
            parts = []
            for blk in range(b0, b1 + 1):
                part = st[(blk - b0) * w:(blk - b0 + 1) * w]
                if blk == n - 1:
                    part = part + bias_prev
                elif blk == n + 1:
                    part = part + bias_next
                parts.append(part)
            parts.append(st[(b1 - b0 + 1) * w:])
            st = jnp.concatenate(parts, axis=0)
        return st

    def weights(h0, st):
        m = jnp.maximum(st.max(axis=0, keepdims=True), sink_row(h0))
        return jnp.exp2(st - m).astype(BF16), m

    def finish(kv, h0, n, et_m):
        et, m = et_m
        lo = kv * HEAD_DIM
        b0, b1 = key_blocks(n)
        vals = [vt[lo:lo + HEAD_DIM, b0 * tqb:(b1 + 1) * tqb]]
        if latent:
            vals.append(vct[lo:lo + HEAD_DIM])
        vals = jnp.concatenate(vals, axis=1)
        vals = jnp.concatenate([vals, jnp.ones((ONES_ROWS, vals.shape[1]), F32)], axis=0).astype(BF16)
        ot = _dot(vals, et)
        d = ot[HEAD_DIM:HEAD_DIM + 1] + jnp.exp2(sink_row(h0) - m)
        on = ot[0:HEAD_DIM] * (1.0 / d)
        for p in range(heads_per_chain // 2):
            pair = jnp.concatenate([on[:, (2 * p) * tqb:(2 * p + 1) * tqb],
                                    on[:, (2 * p + 1) * tqb:(2 * p + 2) * tqb]], axis=0)
            c0 = ((h0 - head0) // 2 + p) * LANE
            o_ref[n * tqb:(n + 1) * tqb, c0:c0 + LANE] = pair.T

    return [(functools.partial(scores, kv, h0, n), functools.partial(weights, h0),
             functools.partial(finish, kv, h0, n)) for kv, h0, n in chains]


def _log_sigmoid(x):
    return jnp.minimum(x, 0.0) - jnp.log1p(jnp.exp(-jnp.abs(x)))


def _ret_fill_decay(d_scr, lg, t, latent):
    bw = min(t, LANE)
    nb = t // bw
    off = (lax.broadcasted_iota(jnp.int32, (bw, bw), 1)
           - lax.broadcasted_iota(jnp.int32, (bw, bw), 0)).astype(F32)
    for hh in range(d_scr.shape[0]):
        lgf, lgb = lg[hh // 2][0:1, hh % 2:hh % 2 + 1], lg[hh // 2][1:2, hh % 2:hh % 2 + 1]
        for o in range(-(nb - 1), nb):
            diff = off + float(o * bw)
            blk = (jnp.where(diff >= 0, jnp.exp(jnp.maximum(diff, 0.0) * lgf), 0.0)
                   + jnp.where(diff <= 0, jnp.exp(jnp.maximum(-diff, 0.0) * lgb), 0.0))
            for bs in range(max(0, -o), min(nb, nb - o)):
                d_scr[hh, bs * bw:(bs + 1) * bw, (bs + o) * bw:(bs + o + 1) * bw] = blk
        if latent:
            tp = lax.broadcasted_iota(jnp.int32, (RT_DK, t), 1).astype(F32)
            d_scr[hh, t:t + RT_DK, :] = jnp.exp((tp + 1.0) * lgf)
            d_scr[hh, t + RT_DK:t + 2 * RT_DK, :] = jnp.exp((float(t) - tp) * lgb)


def _ret_chains(q_ref, k_ref, v_refs, cg_refs, lg, g_ref, o_ref, d_scr, s0_ref=None, sfin_ref=None):
    latent = s0_ref is not None
    seqs, t = q_ref.shape[0], q_ref.shape[1]
    pairs = len(v_refs)
    tq = min(t, 512)
    if latent:
        eye = (lax.broadcasted_iota(jnp.int32, (RT_DK, RT_DK), 0)
               == lax.broadcasted_iota(jnp.int32, (RT_DK, RT_DK), 1)).astype(F32)
    heads = {}

    def head_operands(sq, hh):
        if (sq, hh) not in heads:
            p, j = hh // 2, hh % 2
            kf = k_ref[sq, :, hh * RT_DK:(hh + 1) * RT_DK] * (RT_DK ** -0.5)
            v = v_refs[p][sq, :, j * RT_DV:(j + 1) * RT_DV]
            keys, vals_t = [kf], [v.T]
            if latent:
                keys += [eye, eye]
                vals_t.append(jnp.concatenate([s0_ref[sq, 0, hh], s0_ref[sq, 1, hh]], axis=0).T)
            heads[(sq, hh)] = (jnp.concatenate(keys, axis=0).astype(BF16),
                               jnp.concatenate(vals_t, axis=1).astype(BF16))
            if not latent:
                sp = lax.broadcasted_iota(jnp.int32, (t, 1), 0).astype(F32)
                zf = jnp.exp((float(t) - 1.0 - sp) * lg[p][0:1, j:j + 1])
                zb = jnp.exp(sp * lg[p][1:2, j:j + 1])
                kz = jnp.concatenate([kf * zf, kf * zb], axis=1).astype(BF16)
                s_fb = _dot(heads[(sq, hh)][1], kz).T
                sfin_ref[sq, 0, hh] = s_fb[0:RT_DK]
                sfin_ref[sq, 1, hh] = s_fb[RT_DK:2 * RT_DK]
        return heads[(sq, hh)]

    chains = [(sq, hh, i) for sq in range(seqs) for hh in range(2 * pairs) for i in range(t // tq)]

    def scores(sq, hh, i):
        q = q_ref[sq, i * tq:(i + 1) * tq, hh * RT_DK:(hh + 1) * RT_DK].astype(BF16)
        return lax.dot_general(head_operands(sq, hh)[0], q, NT_DIMS, preferred_element_type=F32)

    def finish(sq, hh, i, at):
        yt = _dot(head_operands(sq, hh)[1], at)
        mu = jnp.mean(yt, axis=0, keepdims=True)
        yc = yt - mu
        var = jnp.mean(yc * yc, axis=0, keepdims=True)
        yn = (yc * lax.rsqrt(var + LN_EPS)).T * g_ref[...]
        cg = cg_refs[hh // 2][sq, i * tq:(i + 1) * tq, (hh % 2) * RT_DV:(hh % 2 + 1) * RT_DV]
        o_ref[sq, i * tq:(i + 1) * tq, hh * RT_DV:(hh + 1) * RT_DV] = yn * (cg * jax.nn.sigmoid(cg))

    def weights(hh, i, st):
        return (st * d_scr[hh, :, i * tq:(i + 1) * tq]).astype(BF16)

    return [(functools.partial(scores, sq, hh, i), functools.partial(weights, hh, i),
             functools.partial(finish, sq, hh, i)) for sq, hh, i in chains]


def _lat_mix_kernel(aq_ref, ak_ref, av_ref, bq_ref, bk_ref, bv_ref, cq_ref, ck_ref, cv_ref, cg_ref,
                    akc_ref, avc_ref, bkc_ref, bvc_ref, s0_ref, lam_ref, dng_ref, sink_ref, dec_ref, rng_ref,
                    oa_ref, ob_ref, oc_ref, d_scr, *, lam_init, half):
    t = aq_ref.shape[1]
    lg = [_log_sigmoid(dec_ref[0])]

    @pl.when(pl.program_id(0) == 0)
    def _():
        _ret_fill_decay(d_scr, lg, t, True)

    def cached_kv(hh):
        mine = pl.ds(2 * half + hh, akc_ref.shape[0] // DA_HEADS, stride=DA_HEADS)
        return akc_ref[mine, :], avc_ref[mine, :]

    chains = _diff_chains(aq_ref.at[0], ak_ref.at[0], av_ref.at[0], oa_ref.at[0], lam_ref, dng_ref,
                          cached_kv, lam_init=lam_init, tq=256)
    chains += _win_chains(bq_ref.at[0], bk_ref.at[0], bv_ref.at[0], ob_ref.at[0], sink_ref,
                          bkc_ref, bvc_ref, kv_heads=(half,))
    chains += _ret_chains(cq_ref, ck_ref, [cv_ref], [cg_ref], lg, rng_ref, oc_ref, d_scr, s0_ref=s0_ref)
    _run_chains(chains)


def _lat_mix_call(z3, half, caches, state6, layer, diff_lam, diff_norm_g, sink, dec4, ret_norm_g, lam_init):
    z3, b0, b = z3
    t = z3.shape[1]
    ck_a, cv_a, ck_b, cv_b = caches
    zcol = lambda blk0, nblk: pl.BlockSpec((1, t, nblk * LANE), lambda bi: (bi + b0, 0, blk0 // nblk))
    layer_spec = lambda *s: pl.BlockSpec((None,) + s, lambda bi: (layer,) + (0,) * len(s))
    cache_spec = lambda a: pl.BlockSpec((None, None) + a.shape[2:], lambda bi: (bi, layer, 0, 0))
    o_spec = pl.BlockSpec((1, t, BRANCH_W // 2), lambda bi: (bi, 0, 0))
    o_shape = jax.ShapeDtypeStruct((b, t, BRANCH_W // 2), F32)
    return pl.pallas_call(
        functools.partial(_lat_mix_kernel, lam_init=lam_init, half=half),
        grid=(b,),
        in_specs=[
            zcol(COL_AQ + 2 * half, 2), zcol(COL_AK + 2 * half, 2), zcol(COL_AV + 2 * half, 2),
            zcol(COL_BQ + 2 * half, 2), zcol(COL_BK, 1), zcol(COL_BV, 1),
            zcol(COL_CQ + half, 1), zcol(COL_CK + half, 1), zcol(COL_CV + 2 * half, 2), zcol(COL_CG + 2 * half, 2),
            cache_spec(ck_a), cache_spec(cv_a), cache_spec(ck_b), cache_spec(cv_b),
            pl.BlockSpec((1, None, 2, 2, RT_DK, RT_DV), lambda bi: (bi, layer, 0, half, 0, 0)),
            layer_spec(4, HEAD_DIM), layer_spec(1, LANE), layer_spec(1, WG_Q_HEADS),
            pl.BlockSpec((None, 1, 2, 2), lambda bi: (layer, half, 0, 0)), layer_spec(1, RT_DV),
        ],
        out_specs=[o_spec, o_spec, o_spec],
        out_shape=[o_shape, o_shape, o_shape],
        scratch_shapes=[pltpu.VMEM((2, t + 2 * RT_DK, t), F32)],
        compiler_params=_params(),
        name="mix_lat",
    )(*([z3] * 10), ck_a, cv_a, ck_b, cv_b, state6, diff_lam, diff_norm_g, sink, dec4, ret_norm_g)


def _ctx_mix_kernel(z_ref, lam_ref, dng_ref, sink_ref, dec_ref, rng_ref, *refs, lam_init, slot):
    oa_ref, ob_ref, oc_ref, sfin_ref, d_scr = refs[-5:]
    sfin_ref = _zero_other_layers(sfin_ref, slot)
    seqs, t = z_ref.shape[0], z_ref.shape[1]
    cols = lambda c0, c1: slice(c0 * LANE, c1 * LANE)
    lg = [_log_sigmoid(dec_ref[p]) for p in range(RT_HEADS // 2)]

    @pl.when(pl.program_id(0) == 0)
    def _():
        _ret_fill_decay(d_scr, lg, t, False)

    chains = []
    for sq in range(seqs):
        chains += _diff_chains(z_ref.at[sq, :, cols(COL_AQ, COL_AK)], z_ref.at[sq, :, cols(COL_AK, COL_AV)],
                               z_ref.at[sq, :, cols(COL_AV, COL_BQ)], oa_ref.at[sq], lam_ref, dng_ref,
                               lambda _: None, lam_init=lam_init, tq=t)
        chains += _win_chains(z_ref.at[sq, :, cols(COL_BQ, COL_BK)], z_ref.at[sq, :, cols(COL_BK, COL_BV)],
                              z_ref.at[sq, :, cols(COL_BV, COL_CQ)], ob_ref.at[sq], sink_ref)
    chains += _ret_chains(z_ref.at[:, :, cols(COL_CQ, COL_CK)], z_ref.at[:, :, cols(COL_CK, COL_CV)],
                          [z_ref.at[:, :, cols(COL_CV + 2 * p, COL_CV + 2 * p + 2)] for p in range(2)],
                          [z_ref.at[:, :, cols(COL_CG + 2 * p, COL_CG + 2 * p + 2)] for p in range(2)],
                          lg, rng_ref, oc_ref, d_scr, sfin_ref=sfin_ref)
    _run_chains(chains)


def _ctx_mix_call(z3, new_state, layer, diff_lam, diff_norm_g, sink, dec4, ret_norm_g, lam_init, seqs=2):
    z3, b0, b = z3
    t = z3.shape[1]
    layer_spec = lambda *s: pl.BlockSpec((None,) + s, lambda bi: (layer,) + (0,) * len(s))
    in_specs = [
        pl.BlockSpec((seqs, t, D_IN), lambda bi: (bi + b0 // seqs, 0, 0)),
        layer_spec(4, HEAD_DIM), layer_spec(1, LANE), layer_spec(1, WG_Q_HEADS),
        layer_spec(RT_HEADS // 2, 2, 2), layer_spec(1, RT_DV),
    ]
    args = [z3, diff_lam, diff_norm_g, sink, dec4, ret_norm_g]
    n_slots, slot, first = (DEPTH, layer, 0) if new_state is None else (1, 0, layer)
    aliases = {}
    if new_state is not None:
        aliases = {len(in_specs): 3}
        in_specs.append(pl.BlockSpec(memory_space=pl.ANY))
        args.append(new_state)
    o_spec = pl.BlockSpec((seqs, t, BRANCH_W), lambda bi: (bi, 0, 0))
    o_shape = jax.ShapeDtypeStruct((b, t, BRANCH_W), F32)
    return pl.pallas_call(
        functools.partial(_ctx_mix_kernel, lam_init=lam_init, slot=slot),
        grid=(b // seqs,),
        in_specs=in_specs,
        out_specs=[o_spec, o_spec, o_spec,
                   pl.BlockSpec((seqs, n_slots, 2, RT_HEADS, RT_DK, RT_DV), lambda bi: (bi, first, 0, 0, 0, 0))],
        out_shape=[o_shape, o_shape, o_shape,
                   jax.ShapeDtypeStruct((b, DEPTH, 2, RT_HEADS, RT_DK, RT_DV), F32)],
        input_output_aliases=aliases,
        scratch_shapes=[pltpu.VMEM((RT_HEADS, t, t), F32)],
        compiler_params=_params(),
        name="mix_ctx",
    )(*args)


def _layer_norm(x, g, b):
    mu = jnp.mean(x, axis=-1, keepdims=True)
    xc = x - mu
    var = jnp.mean(xc * xc, axis=-1, keepdims=True)
    return xc * lax.rsqrt(var + LN_EPS) * g + b


def _tail_kernel(xc_ref, xl_ref, mod_ref, oac_ref, obc_ref, occ_ref,
                 oal0_ref, oal1_ref, obl0_ref, obl1_ref, ocl0_ref, ocl1_ref,
                 bg_ref, ln1g_ref, ln1b_ref, ln2g_ref, ln2b_ref,
                 wg_hbm, wpa_hbm, wpb_hbm, wpc_hbm, wo_hbm, w1_hbm, w2_hbm, yc_ref, yl_ref,
                 wg_ref, wpa_ref, wpb_ref, wpc_ref, wo_ref, w1_ref, w2_ref, stage_ref, sem,
                 *, n_ctx_tiles, layer):
    d = D_MODEL
    pieces = [(wg_hbm.at[layer, :, i * d:(i + 1) * d], wg_ref.at[:, i * d:(i + 1) * d]) for i in range(3)]
    pieces += [(w.at[layer], r) for w, r in ((wpa_hbm, wpa_ref), (wpb_hbm, wpb_ref), (wpc_hbm, wpc_ref),
                                             (wo_hbm, wo_ref))]
    pieces += [(w1_hbm.at[layer, :, i * d:(i + 1) * d], w1_ref.at[:, i * d:(i + 1) * d]) for i in range(4)]
    pieces += [(w2_hbm.at[layer, i * d:(i + 1) * d, :], w2_ref.at[i * d:(i + 1) * d, :]) for i in range(4)]
    _stage_weights_bf16(pl.program_id(0) == 0, pieces, stage_ref, sem)

    is_ctx = pl.program_id(0) < n_ctx_tiles
    x = jnp.where(is_ctx, xc_ref[...], xl_ref[...])
    sh1, sc1, g1 = mod_ref[:, 0:d], mod_ref[:, d:2 * d], mod_ref[:, 2 * d:3 * d]
    sh2, sc2, g2 = mod_ref[:, 3 * d:4 * d], mod_ref[:, 4 * d:5 * d], mod_ref[:, 5 * d:6 * d]
    h1 = (x * (1.0 + sc1) + sh1).astype(BF16)
    merged = None
    branches = ((oac_ref, (oal0_ref, oal1_ref), wpa_ref), (obc_ref, (obl0_ref, obl1_ref), wpb_ref),
                (occ_ref, (ocl0_ref, ocl1_ref), wpc_ref))
    for i, (oc_ref_i, ol_refs_i, wp_ref) in enumerate(branches):
        o_lat = jnp.concatenate([r[...] for r in ol_refs_i], axis=1)
        o = jnp.where(is_ctx, oc_ref_i[...], o_lat).astype(BF16)
        gate = jax.nn.sigmoid(_dot(h1, wg_ref[:, i * d:(i + 1) * d]) + bg_ref[:, i * d:(i + 1) * d])
        part = gate * _dot(o, wp_ref[...])
        merged = part if merged is None else merged + part
    y = _dot(merged.astype(BF16), wo_ref[...])
    x1 = _layer_norm(ALPHA * x + g1 * y, ln1g_ref[...], ln1b_ref[...])
    h2 = (x1 * (1.0 + sc2) + sh2).astype(BF16)
    u = jnp.maximum(_dot(h2, w1_ref[...]), 0.0)
    f = _dot((u * u).astype(BF16), w2_ref[...])
    x2 = _layer_norm(ALPHA * x1 + g2 * f, ln2g_ref[...], ln2b_ref[...])

    @pl.when(is_ctx)
    def _():
        yc_ref[...] = x2

    @pl.when(jnp.logical_not(is_ctx))
    def _():
        yl_ref[...] = x2


def _tail_call(xc, xl, mod4, layer, o_ctx, o_lat, wg, bg, wpa, wpb, wpc, wo, ln1, w1, w2, ln2,
               t_lat, tm=256):
    n_ctx, n_lat = xc.shape[0] // tm, xl.shape[0] // tm
    lat_tiles_per_seq = t_lat // tm

    def mod_idx(i):
        row = jnp.where(i < n_ctx, 0, 1 + (i - n_ctx) // lat_tiles_per_seq)
        return (layer, row, 0, 0)

    ctx_spec = lambda w: pl.BlockSpec((tm, w), lambda i: (jnp.minimum(i, n_ctx - 1), 0))
    lat_spec = lambda w: pl.BlockSpec((tm, w), lambda i: (jnp.maximum(i - n_ctx, 0), 0))
    vec = lambda c: _resident((None, 1, c), lambda i: (layer, 0, 0))
    in_hbm = pl.BlockSpec(memory_space=pl.ANY)
    return pl.pallas_call(
        functools.partial(_tail_kernel, n_ctx_tiles=n_ctx, layer=layer),
        grid=(n_ctx + n_lat,),
        in_specs=[
            ctx_spec(D_MODEL), lat_spec(D_MODEL),
            pl.BlockSpec((None, None, 1, D_MOD), mod_idx),
            ctx_spec(BRANCH_W), ctx_spec(BRANCH_W), ctx_spec(BRANCH_W),
            *[lat_spec(BRANCH_W // 2)] * 6,
            vec(D_GATE), vec(D_MODEL), vec(D_MODEL), vec(D_MODEL), vec(D_MODEL),
        ] + [in_hbm] * 7,
        out_specs=[ctx_spec(D_MODEL), lat_spec(D_MODEL)],
        out_shape=[jax.ShapeDtypeStruct(xc.shape, F32), jax.ShapeDtypeStruct(xl.shape, F32)],
        scratch_shapes=[
            pltpu.VMEM((D_MODEL, D_GATE), BF16),
            pltpu.VMEM((BRANCH_W, D_MODEL), BF16), pltpu.VMEM((BRANCH_W, D_MODEL), BF16),
            pltpu.VMEM((BRANCH_W, D_MODEL), BF16),
            pltpu.VMEM((D_MODEL, D_MODEL), BF16),
            pltpu.VMEM((D_MODEL, D_FF), BF16), pltpu.VMEM((D_FF, D_MODEL), BF16),
            pltpu.VMEM((2, D_MODEL, D_MODEL), F32),
            pltpu.SemaphoreType.DMA((2,)),
        ],
        compiler_params=_params(),
        name="tail",
    )(xc, xl, mod4, *o_ctx, *o_lat, bg, *ln1, *ln2, wg, wpa, wpb, wpc, wo, w1, w2)


def _rope_tables(n_tokens, identity_rows):
    rows = n_tokens // GRID_W
    r, col = jnp.meshgrid(jnp.arange(rows), jnp.arange(GRID_W), indexing="ij")
    r = r.reshape(-1).astype(F32)
    col = col.reshape(-1).astype(F32)
    nf = HEAD_DIM // 4
    inv = ROPE_BASE ** (-jnp.arange(nf, dtype=F32) / nf)
    ang_r = r[:, None] * inv[None, :]
    ang_c = col[:, None] * inv[None, :]
    zero = jnp.zeros_like(ang_r)
    cos = jnp.concatenate([jnp.cos(ang_r)] * 2 + [jnp.cos(ang_c)] * 2, axis=-1)
    s_next = jnp.concatenate([-jnp.sin(ang_r), zero, -jnp.sin(ang_c), zero], axis=-1)
    s_prev = jnp.concatenate([zero, jnp.sin(ang_r), zero, jnp.sin(ang_c)], axis=-1)
    ident = (jnp.ones, jnp.zeros, jnp.zeros)
    return tuple(jnp.concatenate([fill((identity_rows, LANE), F32), jnp.tile(t, (1, LANE // HEAD_DIM))], axis=0)
                 for fill, t in zip(ident, (cos, s_next, s_prev)))


def kernel(x_prompt, x_sample, c, cache_diff_k, cache_diff_v, cache_win_k, cache_win_v, state_ret,
           c_ctx, w_mod, b_mod, w_in, diff_lam, diff_norm_g, win_sink, ret_decay, ret_norm_g,
           w_pa, w_pb, w_pc, w_gate, b_gate, w_o, ln1_g, ln1_b, w_ff1, w_ff2, ln2_g, ln2_b):
    bp, tp, d = x_prompt.shape
    bs, ts, _ = x_sample.shape
    assert d == D_MODEL and w_in.shape == (DEPTH, D_MODEL, D_IN) and c.shape[0] + 1 <= MOD_ROWS
    past = cache_diff_k.shape[2]

    c_rows = jnp.concatenate(
        [c_ctx[None, :], c, jnp.zeros((MOD_ROWS - 1 - bs, d), F32)], axis=0)
    mod4 = _mod_call(c_rows, w_mod, b_mod).reshape(DEPTH, MOD_ROWS, 1, D_MOD)
    proj_tm = 512
    rope_tabs = _rope_tables(ts, proj_tm)

    ck_a = cache_diff_k.reshape(bs, DEPTH, past * DA_HEADS, 2 * HEAD_DIM)
    cv_a = cache_diff_v.reshape(bs, DEPTH, past * DA_HEADS, 2 * HEAD_DIM)
    ck_b = cache_win_k.transpose(0, 1, 3, 4, 2).reshape(bs, DEPTH, WG_KV_HEADS * HEAD_DIM, past)
    cv_b = cache_win_v.transpose(0, 1, 3, 4, 2).reshape(bs, DEPTH, WG_KV_HEADS * HEAD_DIM, past)

    row = lambda a: a.reshape(DEPTH, 1, a.shape[-1])
    ln1, ln2 = (row(ln1_g), row(ln1_b)), (row(ln2_g), row(ln2_b))
    dec4 = ret_decay.reshape(DEPTH, 2, RT_HEADS // 2, 2).transpose(0, 2, 1, 3)

    xp = x_prompt.reshape(bp * tp, d)
    xs = x_sample.reshape(bs * ts, d)
    new_caches, new_state = [], None
    for l in range(DEPTH):
        lam_init = 0.8 - 0.6 * math.exp(-0.3 * l)
        z_all, *new_caches = _proj_call(xs, xp, mod4, l, w_in, rope_tabs, new_caches, ts, tp,
                                        tm=proj_tm)
        z_ctx = (z_all.reshape(-1, tp, D_IN), bs * ts // tp, bp)
        z_lat = (z_all.reshape(-1, ts, D_IN), 0, bs)
        *o_ctx, new_state = _ctx_mix_call(z_ctx, new_state, l, diff_lam, row(diff_norm_g), row(win_sink),
                                          dec4, row(ret_norm_g), lam_init)
        halves = [_lat_mix_call(z_lat, half, (ck_a, cv_a, ck_b, cv_b), state_ret, l, diff_lam,
                                row(diff_norm_g), row(win_sink), dec4, row(ret_norm_g), lam_init)
                  for half in range(2)]
        o_lat = [o.reshape(bs * ts, BRANCH_W // 2) for pair in zip(*halves) for o in pair]
        xp, xs = _tail_call(xp, xs, mod4, l, [o.reshape(bp * tp, BRANCH_W) for o in o_ctx],
                            o_lat, w_gate, row(b_gate),
                            w_pa, w_pb, w_pc, w_o, ln1, w_ff1, w_ff2, ln2, ts)
    dk, dv, wk, wv = new_caches
    new_diff = [a.reshape(bp, DEPTH, tp, DA_HEADS, 2 * HEAD_DIM) for a in (dk, dv)]
    new_win = [a.reshape(bp, DEPTH, WG_KV_HEADS, HEAD_DIM, tp).transpose(0, 1, 4, 2, 3) for a in (wk, wv)]
    return (xp.reshape(bp, tp, d), xs.reshape(bs, ts, d), *new_diff, *new_win, new_state)
```

```python
import functools
import math

import jax
import jax.numpy as jnp
from jax import lax
from jax.experimental import pallas as pl
from jax.experimental.pallas import tpu as pltpu

F32 = jnp.float32
BF16 = jnp.bfloat16

D_MODEL = 1024
DEPTH = 2
GRID_W = 64
HEAD_DIM = 64
DA_HEADS = 4
WG_Q_HEADS = 8
WG_KV_HEADS = 2
WG_GROUP = WG_Q_HEADS // WG_KV_HEADS
WINDOW = 128
RT_HEADS = 4
RT_DK = 64
RT_DV = 128
BRANCH_W = 512
D_IN = 3840
D_GATE = 3 * D_MODEL
D_FF = 4 * D_MODEL
D_MOD = 6 * D_MODEL
ROPE_BASE = 10000.0
LN_EPS = 1e-5
ALPHA = (2 * DEPTH) ** 0.25
QK_SCALE = HEAD_DIM ** -0.5
LOG2E = math.log2(math.e)

LANE = 128
MOD_ROWS = 8
ONES_ROWS = 16
PROJ_COLS = 768
STAGE_ROWS = 512

COL_AQ, COL_AK, COL_AV = 0, 4, 8
COL_BQ, COL_BK, COL_BV = 12, 16, 17
COL_CQ, COL_CK, COL_CV, COL_CG = 18, 20, 22, 26
ROPE_BLOCKS = tuple(range(0, 8)) + tuple(range(12, 17))

VMEM_LIMIT = 56 * 1024 * 1024
NT_DIMS = (((1,), (1,)), ((), ()))
TN_DIMS = (((0,), (0,)), ((), ()))


def _params():
    return pltpu.CompilerParams(vmem_limit_bytes=VMEM_LIMIT)


def _resident(shape, index_map):
    return pl.BlockSpec(shape, index_map, pipeline_mode=pl.Buffered(1))


def _dot(a, b):
    return jnp.dot(a, b, preferred_element_type=F32)


def _zero_other_layers(ref, slot):
    for other in range(ref.shape[1]):
        if other != slot:
            ref[:, other] = jnp.zeros((ref.shape[0],) + ref.shape[2:], F32)
    return ref.at[:, slot]


class _WeightStager:
    def __init__(self, pieces, stage_ref, sem):
        self.stage_ref, self.sem = stage_ref, sem
        slot_rows = stage_ref.shape[1]
        self.chunks, self.groups = [], []
        for src, dst in pieces:
            first = len(self.chunks)
            for r0 in range(0, src.shape[0], slot_rows):
                r1 = min(r0 + slot_rows, src.shape[0])
                self.chunks.append((src.at[r0:r1, :], dst.at[r0:r1, :]))
            self.groups.append(range(first, len(self.chunks)))

    def _slot(self, k):
        rows, cols = self.chunks[k][0].shape
        return self.stage_ref.at[k % 2, 0:rows, 0:cols]

    def _copy(self, k):
        return pltpu.make_async_copy(self.chunks[k][0], self._slot(k), self.sem.at[k % 2])

    def start(self):
        for k in range(min(2, len(self.chunks))):
            self._copy(k).start()

    def ready(self, c):
        for k in self.groups[c]:
            self._copy(k).wait()
            self.chunks[k][1][...] = self._slot(k)[...].astype(BF16)
            if k + 2 < len(self.chunks):
                self._copy(k + 2).start()


def _mod_kernel(c_ref, w_ref, b_ref, o_ref):
    c = c_ref[...]
    a = (c * jax.nn.sigmoid(c)).astype(BF16)
    o_ref[...] = _dot(a, w_ref[...].astype(BF16)) + b_ref[...]


def _mod_call(c_rows, w_mod, b_mod):
    tn = 1536
    return pl.pallas_call(
        _mod_kernel,
        grid=(DEPTH, D_MOD // tn),
        in_specs=[
            pl.BlockSpec((MOD_ROWS, D_MODEL), lambda l, n: (0, 0)),
            pl.BlockSpec((None, D_MODEL, tn), lambda l, n: (l, 0, n)),
            pl.BlockSpec((None, 1, tn), lambda l, n: (l, 0, n)),
        ],
        out_specs=pl.BlockSpec((None, MOD_ROWS, tn), lambda l, n: (l, 0, n)),
        out_shape=jax.ShapeDtypeStruct((DEPTH, MOD_ROWS, D_MOD), F32),
        compiler_params=_params(),
        name="mod_vectors",
    )(c_rows, w_mod, b_mod.reshape(DEPTH, 1, D_MOD))


def _proj_kernel(xl_ref, xc_ref, mod_ref, win_hbm, cos_ref, sa_ref, sb_ref, *refs, n_lat_tiles, slot, layer):
    z_ref, dk_ref, dv_ref, wk_ref, wv_ref, win_ref, stage_ref, sem = refs[-8:]
    nc = stage_ref.shape[2]
    stager = _WeightStager(
        [(win_hbm.at[layer, :, c0:c0 + nc], win_ref.at[:, c0:c0 + nc]) for c0 in range(0, D_IN, nc)],
        stage_ref, sem)
    seqs, t_ctx = dk_ref.shape[0], wk_ref.shape[-1]
    dk_ref, dv_ref, wk_ref, wv_ref = (_zero_other_layers(r, slot) for r in (dk_ref, dv_ref, wk_ref, wv_ref))

    def tile(ready):
        is_lat = pl.program_id(0) < n_lat_tiles
        x = jnp.where(is_lat, xl_ref[...], xc_ref[...])
        sh1 = mod_ref[:, 0:D_MODEL]
        sc1 = mod_ref[:, D_MODEL:2 * D_MODEL]
        h = (x * (1.0 + sc1) + sh1).astype(BF16)
        for c0 in range(0, D_IN, nc):
            if ready:
                ready(c0 // nc)
            z = _dot(h, win_ref[:, c0:c0 + nc])
            for j in range(nc // LANE):
                blk = c0 // LANE + j
                u = z[:, j * LANE:(j + 1) * LANE]
                if blk in ROPE_BLOCKS:
                    u = (u * cos_ref[...] + pltpu.roll(u, LANE - 16, 1) * sa_ref[...]
                         + pltpu.roll(u, 16, 1) * sb_ref[...])
                z_ref[:, blk * LANE:(blk + 1) * LANE] = u
                if COL_AK <= blk < COL_BQ:
                    ref, head = (dk_ref, blk - COL_AK) if blk < COL_AV else (dv_ref, blk - COL_AV)
                    for s in range(seqs):
                        ref[s, pl.ds(head, t_ctx, stride=DA_HEADS), :] = u[s * t_ctx:(s + 1) * t_ctx]
                elif blk in (COL_BK, COL_BV):
                    ref = wk_ref if blk == COL_BK else wv_ref
                    ut = u.T
                    for s in range(seqs):
                        ref[s] = ut[:, s * t_ctx:(s + 1) * t_ctx]

    @pl.when(pl.program_id(0) == 0)
    def _():
        stager.start()
        tile(stager.ready)

    @pl.when(pl.program_id(0) > 0)
    def _():
        tile(None)


def _proj_call(xl, xc, mod4, layer, w_in, rope_tabs, caches, t_lat, t_ctx, tm=512):
    n_lat, n_ctx = xl.shape[0] // tm, xc.shape[0] // tm
    lat_tiles_per_seq = t_lat // tm
    seqs = tm // t_ctx

    def mod_idx(i):
        row = jnp.where(i < n_lat, 1 + i // lat_tiles_per_seq, 0)
        return (layer, row, 0, 0)

    lat_idx = lambda i: jnp.minimum(i, n_lat - 1)
    ctx_idx = lambda i: jnp.maximum(i - n_lat, 0)
    rope_spec = pl.BlockSpec((tm, LANE), lambda i: (jnp.where(i < n_lat, 1 + i % lat_tiles_per_seq, 0), 0))
    n_slots, slot, first = (1, 0, layer) if caches else (DEPTH, layer, 0)
    cache_a = pl.BlockSpec((seqs, n_slots, DA_HEADS * t_ctx, 2 * HEAD_DIM), lambda i: (ctx_idx(i), first, 0, 0))
    cache_b = pl.BlockSpec((seqs, n_slots, WG_KV_HEADS * HEAD_DIM, t_ctx), lambda i: (ctx_idx(i), first, 0, 0))
    in_specs = [
        pl.BlockSpec((tm, D_MODEL), lambda i: (lat_idx(i), 0)),
        pl.BlockSpec((tm, D_MODEL), lambda i: (ctx_idx(i), 0)),
        pl.BlockSpec((None, None, 1, D_MOD), mod_idx),
        pl.BlockSpec(memory_space=pl.ANY),
        rope_spec, rope_spec, rope_spec,
    ]
    n_in = len(in_specs)
    in_specs += [pl.BlockSpec(memory_space=pl.ANY)] * len(caches)
    b_ctx = xc.shape[0] // t_ctx
    cache_shapes = [(b_ctx, DEPTH, DA_HEADS * t_ctx, 2 * HEAD_DIM)] * 2 \
        + [(b_ctx, DEPTH, WG_KV_HEADS * HEAD_DIM, t_ctx)] * 2
    return pl.pallas_call(
        functools.partial(_proj_kernel, n_lat_tiles=n_lat, slot=slot, layer=layer),
        grid=(n_lat + n_ctx,),
        in_specs=in_specs,
        out_specs=[pl.BlockSpec((tm, D_IN), lambda i: (i, 0)), cache_a, cache_a, cache_b, cache_b],
        out_shape=[jax.ShapeDtypeStruct((xl.shape[0] + xc.shape[0], D_IN), F32)]
        + [jax.ShapeDtypeStruct(s, F32) for s in cache_shapes],
        input_output_aliases={n_in + k: 1 + k for k in range(len(caches))},
        scratch_shapes=[
            pltpu.VMEM((D_MODEL, D_IN), BF16),
            pltpu.VMEM((2, STAGE_ROWS, PROJ_COLS), F32),
            pltpu.SemaphoreType.DMA((2,)),
        ],
        compiler_params=_params(),
        name="proj",
    )(xl, xc, mod4, w_in, *rope_tabs, *caches)


def _values_t(v_parts):
    vt = jnp.concatenate([v.T for v in v_parts], axis=1)
    return jnp.concatenate([vt, jnp.ones((ONES_ROWS, vt.shape[1]), F32)], axis=0).astype(BF16)


def _run_chains(chains):
    a, b = {}, {}
    n = len(chains)
    for c in range(n + 2):
        if c < n:
            a[c] = chains[c][0]()
        if 1 <= c <= n:
            b[c - 1] = chains[c - 1][1](a.pop(c - 1))
        if c >= 2:
            chains[c - 2][2](b.pop(c - 2))


def _diff_chains(q_ref, k_ref, v_ref, o_ref, lam_ref, g_ref, cached_kv, *, lam_init, tq):
    t, heads = q_ref.shape[0], q_ref.shape[1] // LANE
    lv = lam_ref[...]
    lam = (jnp.exp(jnp.sum(lv[0:1] * lv[1:2], axis=-1, keepdims=True))
           - jnp.exp(jnp.sum(lv[2:3] * lv[3:4], axis=-1, keepdims=True)) + lam_init)
    lane = lax.broadcasted_iota(jnp.int32, (tq, LANE), 1)
    kv = {}

    def scores(hh, i):
        cols = slice(hh * LANE, (hh + 1) * LANE)
        if hh not in kv:
            k_parts, v_parts = [k_ref[:, cols]], [v_ref[:, cols]]
            extra = cached_kv(hh)
            if extra is not None:
                k_parts.append(extra[0])
                v_parts.append(extra[1])
            kv[hh] = (jnp.concatenate(k_parts, axis=0).astype(BF16), _values_t(v_parts))
        q = q_ref[i * tq:(i + 1) * tq, cols] * (QK_SCALE * LOG2E)
        qs = jnp.concatenate([jnp.where(lane < HEAD_DIM, q, 0.0),
                              jnp.where(lane >= HEAD_DIM, q, 0.0)], axis=0).astype(BF16)
        return lax.dot_general(kv[hh][0], qs, NT_DIMS, preferred_element_type=F32)

    def weights(s):
        return jnp.exp2(s - s.max(axis=0, keepdims=True)).astype(BF16)

    def finish(hh, i, et):
        ot = _dot(kv[hh][1], et)
        o = ot[0:LANE] * (1.0 / ot[LANE:LANE + 1])
        od = o[:, 0:tq] - lam * o[:, tq:2 * tq]
        yt = od * lax.rsqrt(jnp.mean(od * od, axis=0, keepdims=True) + LN_EPS)
        o_ref[i * tq:(i + 1) * tq, hh * LANE:(hh + 1) * LANE] = yt.T * g_ref[...] * (1.0 - lam_init)

    return [(functools.partial(scores, hh, i), weights, functools.partial(finish, hh, i))
            for hh in range(heads) for i in range(t // tq)]


def _win_chains(q_ref, k_ref, v_ref, o_ref, sink_ref, kc_ref=None, vc_ref=None, kv_heads=range(WG_KV_HEADS)):
    latent = kc_ref is not None
    head0 = kv_heads[0] * WG_GROUP
    t = q_ref.shape[0]
    w = WINDOW
    if latent:
        tqb, heads_per_chain = w, WG_GROUP
    else:
        tqb, heads_per_chain = t, 2
    nb = t // tqb
    kb = k_ref[...].astype(BF16)
    vt = v_ref[...].T
    if latent:
        kcb = kc_ref[...].T.astype(BF16)
        vct = vc_ref[...]
        jj = lax.broadcasted_iota(jnp.int32, (w, w), 0)
        ii = lax.broadcasted_iota(jnp.int32, (w, w), 1)
        bias_prev = jnp.concatenate([jnp.where(jj >= ii, 0.0, -1e30)] * heads_per_chain, axis=1)
        bias_next = jnp.concatenate([jnp.where(jj <= ii, 0.0, -1e30)] * heads_per_chain, axis=1)
    chains = [(kv, h0, n)
              for n in range(nb)
              for kv in kv_heads
              for h0 in range(kv * WG_GROUP, (kv + 1) * WG_GROUP, heads_per_chain)]
    sinks = {}

    def sink_row(h0):
        if h0 not in sinks:
            sinks[h0] = jnp.concatenate(
                [jnp.broadcast_to(sink_ref[:, h:h + 1] * LOG2E, (1, tqb))
                 for h in range(h0, h0 + heads_per_chain)], axis=1)
        return sinks[h0]

    def key_blocks(n):
        return (max(n - 1, 0), min(n + 1, nb - 1)) if latent else (0, 0)

    def scores(kv, h0, n):
        lo = kv * HEAD_DIM
        rows = slice(n * tqb, (n + 1) * tqb)
        q_g = (jnp.concatenate([q_ref[rows, (h - head0) * HEAD_DIM:(h - head0 + 1) * HEAD_DIM]
                                for h in range(h0, h0 + heads_per_chain)], axis=0)
               * (QK_SCALE * LOG2E)).astype(BF16)
        b0, b1 = key_blocks(n)
        keys = kb[b0 * tqb:(b1 + 1) * tqb, lo:lo + HEAD_DIM]
        if latent:
            keys = jnp.concatenate([keys, kcb[:, lo:lo + HEAD_DIM]], axis=0)
        st = lax.dot_general(keys, q_g, NT_DIMS, preferred_element_type=F32)
        if latent:
            parts = []
            for blk in range(b0, b1 + 1):
                part = st[(blk - b0) * w:(blk - b0 + 1) * w]
                if blk == n - 1:
                    part = part + bias_prev
                elif blk == n + 1:
                    part = part + bias_next
                parts.append(part)
            parts.append(st[(b1 - b0 + 1) * w:])
            st = jnp.concatenate(parts, axis=0)
        return st

    def weights(h0, st):
        m = jnp.maximum(st.max(axis=0, keepdims=True), sink_row(h0))
        return jnp.exp2(st - m).astype(BF16), m

    def finish(kv, h0, n, et_m):
        et, m = et_m
        lo = kv * HEAD_DIM
        b0, b1 = key_blocks(n)
        vals = [vt[lo:lo + HEAD_DIM, b0 * tqb:(b1 + 1) * tqb]]
        if latent:
            vals.append(vct[lo:lo + HEAD_DIM])
        vals = jnp.concatenate(vals, axis=1)
        vals = jnp.concatenate([vals, jnp.ones((ONES_ROWS, vals.shape[1]), F32)], axis=0).astype(BF16)
        ot = _dot(vals, et)
        d = ot[HEAD_DIM:HEAD_DIM + 1] + jnp.exp2(sink_row(h0) - m)
        on = ot[0:HEAD_DIM] * (1.0 / d)
        for p in range(heads_per_chain // 2):
            pair = jnp.concatenate([on[:, (2 * p) * tqb:(2 * p + 1) * tqb],
                                    on[:, (2 * p + 1) * tqb:(2 * p + 2) * tqb]], axis=0)
            c0 = ((h0 - head0) // 2 + p) * LANE
            o_ref[n * tqb:(n + 1) * tqb, c0:c0 + LANE] = pair.T

    return [(functools.partial(scores, kv, h0, n), functools.partial(weights, h0),
             functools.partial(finish, kv, h0, n)) for kv, h0, n in chains]


def _log_sigmoid(x):
    return jnp.minimum(x, 0.0) - jnp.log1p(jnp.exp(-jnp.abs(x)))


def _ret_fill_decay(d_scr, lg, t, latent):
    bw = min(t, LANE)
    nb = t // bw
    off = (lax.broadcasted_iota(jnp.int32, (bw, bw), 1)
           - lax.broadcasted_iota(jnp.int32, (bw, bw), 0)).astype(F32)
    for hh in range(d_scr.shape[0]):
        lgf, lgb = lg[hh // 2][0:1, hh % 2:hh % 2 + 1], lg[hh // 2][1:2, hh % 2:hh % 2 + 1]
        for o in range(-(nb - 1), nb):
            diff = off + float(o * bw)
            blk = (jnp.where(diff >= 0, jnp.exp(jnp.maximum(diff, 0.0) * lgf), 0.0)
                   + jnp.where(diff <= 0, jnp.exp(jnp.maximum(-diff, 0.0) * lgb), 0.0))
            for bs in range(max(0, -o), min(nb, nb - o)):
                d_scr[hh, bs * bw:(bs + 1) * bw, (bs + o) * bw:(bs + o + 1) * bw] = blk
        if latent:
            tp = lax.broadcasted_iota(jnp.int32, (RT_DK, t), 1).astype(F32)
            d_scr[hh, t:t + RT_DK, :] = jnp.exp((tp + 1.0) * lgf)
            d_scr[hh, t + RT_DK:t + 2 * RT_DK, :] = jnp.exp((float(t) - tp) * lgb)


def _ret_chains(q_ref, k_ref, v_refs, cg_refs, lg, g_ref, o_ref, d_scr, s0_ref=None, sfin_ref=None):
    latent = s0_ref is not None
    seqs, t = q_ref.shape[0], q_ref.shape[1]
    pairs = len(v_refs)
    tq = min(t, 512)
    if latent:
        eye = (lax.broadcasted_iota(jnp.int32, (RT_DK, RT_DK), 0)
               == lax.broadcasted_iota(jnp.int32, (RT_DK, RT_DK), 1)).astype(F32)
    heads = {}

    def head_operands(sq, hh):
        if (sq, hh) not in heads:
            p, j = hh // 2, hh % 2
            kf = k_ref[sq, :, hh * RT_DK:(hh + 1) * RT_DK] * (RT_DK ** -0.5)
            v = v_refs[p][sq, :, j * RT_DV:(j + 1) * RT_DV]
            keys, vals_t = [kf], [v.T]
            if latent:
                keys += [eye, eye]
                vals_t.append(jnp.concatenate([s0_ref[sq, 0, hh], s0_ref[sq, 1, hh]], axis=0).T)
            heads[(sq, hh)] = (jnp.concatenate(keys, axis=0).astype(BF16),
                               jnp.concatenate(vals_t, axis=1).astype(BF16))
            if not latent:
                sp = lax.broadcasted_iota(jnp.int32, (t, 1), 0).astype(F32)
                zf = jnp.exp((float(t) - 1.0 - sp) * lg[p][0:1, j:j + 1])
                zb = jnp.exp(sp * lg[p][1:2, j:j + 1])
                kz = jnp.concatenate([kf * zf, kf * zb], axis=1).astype(BF16)
                s_fb = _dot(heads[(sq, hh)][1], kz).T
                sfin_ref[sq, 0, hh] = s_fb[0:RT_DK]
                sfin_ref[sq, 1, hh] = s_fb[RT_DK:2 * RT_DK]
        return heads[(sq, hh)]

    chains = [(sq, hh, i) for sq in range(seqs) for hh in range(2 * pairs) for i in range(t // tq)]

    def scores(sq, hh, i):
        q = q_ref[sq, i * tq:(i + 1) * tq, hh * RT_DK:(hh + 1) * RT_DK].astype(BF16)
        return lax.dot_general(head_operands(sq, hh)[0], q, NT_DIMS, preferred_element_type=F32)

    def finish(sq, hh, i, at):
        yt = _dot(head_operands(sq, hh)[1], at)
        mu = jnp.mean(yt, axis=0, keepdims=True)
        yc = yt - mu
        var = jnp.mean(yc * yc, axis=0, keepdims=True)
        yn = (yc * lax.rsqrt(var + LN_EPS)).T * g_ref[...]
        cg = cg_refs[hh // 2][sq, i * tq:(i + 1) * tq, (hh % 2) * RT_DV:(hh % 2 + 1) * RT_DV]
        o_ref[sq, i * tq:(i + 1) * tq, hh * RT_DV:(hh + 1) * RT_DV] = yn * (cg * jax.nn.sigmoid(cg))

    def weights(hh, i, st):
        return (st * d_scr[hh, :, i * tq:(i + 1) * tq]).astype(BF16)

    return [(functools.partial(scores, sq, hh, i), functools.partial(weights, hh, i),
             functools.partial(finish, sq, hh, i)) for sq, hh, i in chains]


def _lat_mix_kernel(aq_ref, ak_ref, av_ref, bq_ref, bk_ref, bv_ref, cq_ref, ck_ref, cv_ref, cg_ref,
                    akc_ref, avc_ref, bkc_ref, bvc_ref, s0_ref, lam_ref, dng_ref, sink_ref, dec_ref, rng_ref,
                    oa_ref, ob_ref, oc_ref, d_scr, *, lam_init, half):
    t = aq_ref.shape[1]
    lg = [_log_sigmoid(dec_ref[0])]

    @pl.when(pl.program_id(0) == 0)
    def _():
        _ret_fill_decay(d_scr, lg, t, True)

    def cached_kv(hh):
        mine = pl.ds(2 * half + hh, akc_ref.shape[0] // DA_HEADS, stride=DA_HEADS)
        return akc_ref[mine, :], avc_ref[mine, :]

    chains = _diff_chains(aq_ref.at[0], ak_ref.at[0], av_ref.at[0], oa_ref.at[0], lam_ref, dng_ref,
                          cached_kv, lam_init=lam_init, tq=256)
    chains += _win_chains(bq_ref.at[0], bk_ref.at[0], bv_ref.at[0], ob_ref.at[0], sink_ref,
                          bkc_ref, bvc_ref, kv_heads=(half,))
    chains += _ret_chains(cq_ref, ck_ref, [cv_ref], [cg_ref], lg, rng_ref, oc_ref, d_scr, s0_ref=s0_ref)
    _run_chains(chains)


def _lat_mix_call(z3, half, caches, state6, layer, diff_lam, diff_norm_g, sink, dec4, ret_norm_g, lam_init):
    z3, b0, b = z3
    t = z3.shape[1]
    ck_a, cv_a, ck_b, cv_b = caches
    zcol = lambda blk0, nblk: pl.BlockSpec((1, t, nblk * LANE), lambda bi: (bi + b0, 0, blk0 // nblk))
    layer_spec = lambda *s: pl.BlockSpec((None,) + s, lambda bi: (layer,) + (0,) * len(s))
    cache_spec = lambda a: pl.BlockSpec((None, None) + a.shape[2:], lambda bi: (bi, layer, 0, 0))
    o_spec = pl.BlockSpec((1, t, BRANCH_W // 2), lambda bi: (bi, 0, 0))
    o_shape = jax.ShapeDtypeStruct((b, t, BRANCH_W // 2), F32)
    return pl.pallas_call(
        functools.partial(_lat_mix_kernel, lam_init=lam_init, half=half),
        grid=(b,),
        in_specs=[
            zcol(COL_AQ + 2 * half, 2), zcol(COL_AK + 2 * half, 2), zcol(COL_AV + 2 * half, 2),
            zcol(COL_BQ + 2 * half, 2), zcol(COL_BK, 1), zcol(COL_BV, 1),
            zcol(COL_CQ + half, 1), zcol(COL_CK + half, 1), zcol(COL_CV + 2 * half, 2), zcol(COL_CG + 2 * half, 2),
            cache_spec(ck_a), cache_spec(cv_a), cache_spec(ck_b), cache_spec(cv_b),
            pl.BlockSpec((1, None, 2, 2, RT_DK, RT_DV), lambda bi: (bi, layer, 0, half, 0, 0)),
            layer_spec(4, HEAD_DIM), layer_spec(1, LANE), layer_spec(1, WG_Q_HEADS),
            pl.BlockSpec((None, 1, 2, 2), lambda bi: (layer, half, 0, 0)), layer_spec(1, RT_DV),
        ],
        out_specs=[o_spec, o_spec, o_spec],
        out_shape=[o_shape, o_shape, o_shape],
        scratch_shapes=[pltpu.VMEM((2, t + 2 * RT_DK, t), F32)],
        compiler_params=_params(),
        name="mix_lat",
    )(*([z3] * 10), ck_a, cv_a, ck_b, cv_b, state6, diff_lam, diff_norm_g, sink, dec4, ret_norm_g)


def _ctx_mix_kernel(z_ref, lam_ref, dng_ref, sink_ref, dec_ref, rng_ref, *refs, lam_init, slot):
    oa_ref, ob_ref, oc_ref, sfin_ref, d_scr = refs[-5:]
    sfin_ref = _zero_other_layers(sfin_ref, slot)
    seqs, t = z_ref.shape[0], z_ref.shape[1]
    cols = lambda c0, c1: slice(c0 * LANE, c1 * LANE)
    lg = [_log_sigmoid(dec_ref[p]) for p in range(RT_HEADS // 2)]

    @pl.when(pl.program_id(0) == 0)
    def _():
        _ret_fill_decay(d_scr, lg, t, False)

    chains = []
    for sq in range(seqs):
        chains += _diff_chains(z_ref.at[sq, :, cols(COL_AQ, COL_AK)], z_ref.at[sq, :, cols(COL_AK, COL_AV)],
                               z_ref.at[sq, :, cols(COL_AV, COL_BQ)], oa_ref.at[sq], lam_ref, dng_ref,
                               lambda _: None, lam_init=lam_init, tq=t)
        chains += _win_chains(z_ref.at[sq, :, cols(COL_BQ, COL_BK)], z_ref.at[sq, :, cols(COL_BK, COL_BV)],
                              z_ref.at[sq, :, cols(COL_BV, COL_CQ)], ob_ref.at[sq], sink_ref)
    chains += _ret_chains(z_ref.at[:, :, cols(COL_CQ, COL_CK)], z_ref.at[:, :, cols(COL_CK, COL_CV)],
                          [z_ref.at[:, :, cols(COL_CV + 2 * p, COL_CV + 2 * p + 2)] for p in range(2)],
                          [z_ref.at[:, :, cols(COL_CG + 2 * p, COL_CG + 2 * p + 2)] for p in range(2)],
                          lg, rng_ref, oc_ref, d_scr, sfin_ref=sfin_ref)
    _run_chains(chains)


def _ctx_mix_call(z3, new_state, layer, diff_lam, diff_norm_g, sink, dec4, ret_norm_g, lam_init, seqs=2):
    z3, b0, b = z3
    t = z3.shape[1]
    layer_spec = lambda *s: pl.BlockSpec((None,) + s, lambda bi: (layer,) + (0,) * len(s))
    in_specs = [
        pl.BlockSpec((seqs, t, D_IN), lambda bi: (bi + b0 // seqs, 0, 0)),
        layer_spec(4, HEAD_DIM), layer_spec(1, LANE), layer_spec(1, WG_Q_HEADS),
        layer_spec(RT_HEADS // 2, 2, 2), layer_spec(1, RT_DV),
    ]
    args = [z3, diff_lam, diff_norm_g, sink, dec4, ret_norm_g]
    n_slots, slot, first = (DEPTH, layer, 0) if new_state is None else (1, 0, layer)
    aliases = {}
    if new_state is not None:
        aliases = {len(in_specs): 3}
        in_specs.append(pl.BlockSpec(memory_space=pl.ANY))
        args.append(new_state)
    o_spec = pl.BlockSpec((seqs, t, BRANCH_W), lambda bi: (bi, 0, 0))
    o_shape = jax.ShapeDtypeStruct((b, t, BRANCH_W), F32)
    return pl.pallas_call(
        functools.partial(_ctx_mix_kernel, lam_init=lam_init, slot=slot),
        grid=(b // seqs,),
        in_specs=in_specs,
        out_specs=[o_spec, o_spec, o_spec,
                   pl.BlockSpec((seqs, n_slots, 2, RT_HEADS, RT_DK, RT_DV), lambda bi: (bi, first, 0, 0, 0, 0))],
        out_shape=[o_shape, o_shape, o_shape,
                   jax.ShapeDtypeStruct((b, DEPTH, 2, RT_HEADS, RT_DK, RT_DV), F32)],
        input_output_aliases=aliases,
        scratch_shapes=[pltpu.VMEM((RT_HEADS, t, t), F32)],
        compiler_params=_params(),
        name="mix_ctx",
    )(*args)


def _layer_norm(x, g, b):
    mu = jnp.mean(x, axis=-1, keepdims=True)
    xc = x - mu
    var = jnp.mean(xc * xc, axis=-1, keepdims=True)
    return xc * lax.rsqrt(var + LN_EPS) * g + b


def _tail_kernel(xc_ref, xl_ref, mod_ref, oac_ref, obc_ref, occ_ref,
                 oal0_ref, oal1_ref, obl0_ref, obl1_ref, ocl0_ref, ocl1_ref,
                 bg_ref, ln1g_ref, ln1b_ref, ln2g_ref, ln2b_ref,
                 wg_hbm, wpa_hbm, wpb_hbm, wpc_hbm, wo_hbm, w1_hbm, w2_hbm, yc_ref, yl_ref,
                 wg_ref, wpa_ref, wpb_ref, wpc_ref, wo_ref, w1_ref, w2_ref, stage_ref, sem,
                 *, n_ctx_tiles, layer):
    d = D_MODEL
    col = lambda i: slice(i * d, (i + 1) * d)
    wp_hbm, wp_refs = (wpa_hbm, wpb_hbm, wpc_hbm), (wpa_ref, wpb_ref, wpc_ref)
    pieces = []
    for i in range(3):
        pieces += [(wg_hbm.at[layer, :, col(i)], wg_ref.at[:, col(i)]), (wp_hbm[i].at[layer], wp_refs[i])]
    pieces.append((wo_hbm.at[layer], wo_ref))
    n_ff = D_FF // d
    pieces += [(w1_hbm.at[layer, :, col(i)], w1_ref.at[:, col(i)]) for i in range(n_ff)]
    pieces += [(w2_hbm.at[layer, col(i), :], w2_ref.at[col(i), :]) for i in range(n_ff)]
    stager = _WeightStager(pieces, stage_ref, sem)

    def tile(ready):
        is_ctx = pl.program_id(0) < n_ctx_tiles
        x = jnp.where(is_ctx, xc_ref[...], xl_ref[...])
        sh1, sc1, g1 = mod_ref[:, 0:d], mod_ref[:, d:2 * d], mod_ref[:, 2 * d:3 * d]
        sh2, sc2, g2 = mod_ref[:, 3 * d:4 * d], mod_ref[:, 4 * d:5 * d], mod_ref[:, 5 * d:6 * d]
        h1 = (x * (1.0 + sc1) + sh1).astype(BF16)
        merged = None
        branches = ((oac_ref, (oal0_ref, oal1_ref)), (obc_ref, (obl0_ref, obl1_ref)),
                    (occ_ref, (ocl0_ref, ocl1_ref)))
        for i, (oc_ref_i, ol_refs_i) in enumerate(branches):
            o_lat = jnp.concatenate([r[...] for r in ol_refs_i], axis=1)
            o = jnp.where(is_ctx, oc_ref_i[...], o_lat).astype(BF16)
            if ready:
                ready(2 * i)
            gate = jax.nn.sigmoid(_dot(h1, wg_ref[:, col(i)]) + bg_ref[:, col(i)])
            if ready:
                ready(2 * i + 1)
            part = gate * _dot(o, wp_refs[i][...])
            merged = part if merged is None else merged + part
        if ready:
            ready(6)
        y = _dot(merged.astype(BF16), wo_ref[...])
        x1 = _layer_norm(ALPHA * x + g1 * y, ln1g_ref[...], ln1b_ref[...])
        h2 = (x1 * (1.0 + sc2) + sh2).astype(BF16)
        if ready:
            sq = []
            for c in range(n_ff):
                ready(7 + c)
                u = jnp.maximum(_dot(h2, w1_ref[:, col(c)]), 0.0)
                sq.append((u * u).astype(BF16))
            f = None
            for c in range(n_ff):
                ready(7 + n_ff + c)
                part = _dot(sq[c], w2_ref[col(c), :])
                f = part if f is None else f + part
        else:
            u = jnp.maximum(_dot(h2, w1_ref[...]), 0.0)
            f = _dot((u * u).astype(BF16), w2_ref[...])
        x2 = _layer_norm(ALPHA * x1 + g2 * f, ln2g_ref[...], ln2b_ref[...])

        @pl.when(is_ctx)
        def _():
            yc_ref[...] = x2

        @pl.when(jnp.logical_not(is_ctx))
        def _():
            yl_ref[...] = x2

    @pl.when(pl.program_id(0) == 0)
    def _():
        stager.start()
        tile(stager.ready)

    @pl.when(pl.program_id(0) > 0)
    def _():
        tile(None)


def _tail_call(xc, xl, mod4, layer, o_ctx, o_lat, wg, bg, wpa, wpb, wpc, wo, ln1, w1, w2, ln2,
               t_lat, tm=256):
    n_ctx, n_lat = xc.shape[0] // tm, xl.shape[0] // tm
    lat_tiles_per_seq = t_lat // tm

    def mod_idx(i):
        row = jnp.where(i < n_ctx, 0, 1 + (i - n_ctx) // lat_tiles_per_seq)
        return (layer, row, 0, 0)

    ctx_spec = lambda w: pl.BlockSpec((tm, w), lambda i: (jnp.minimum(i, n_ctx - 1), 0))
    lat_spec = lambda w: pl.BlockSpec((tm, w), lambda i: (jnp.maximum(i - n_ctx, 0), 0))
    vec = lambda c: _resident((None, 1, c), lambda i: (layer, 0, 0))
    in_hbm = pl.BlockSpec(memory_space=pl.ANY)
    return pl.pallas_call(
        functools.partial(_tail_kernel, n_ctx_tiles=n_ctx, layer=layer),
        grid=(n_ctx + n_lat,),
        in_specs=[
            ctx_spec(D_MODEL), lat_spec(D_MODEL),
            pl.BlockSpec((None, None, 1, D_MOD), mod_idx),
            ctx_spec(BRANCH_W), ctx_spec(BRANCH_W), ctx_spec(BRANCH_W),
            *[lat_spec(BRANCH_W // 2)] * 6,
            vec(D_GATE), vec(D_MODEL), vec(D_MODEL), vec(D_MODEL), vec(D_MODEL),
        ] + [in_hbm] * 7,
        out_specs=[ctx_spec(D_MODEL), lat_spec(D_MODEL)],
        out_shape=[jax.ShapeDtypeStruct(xc.shape, F32), jax.ShapeDtypeStruct(xl.shape, F32)],
        scratch_shapes=[
            pltpu.VMEM((D_MODEL, D_GATE), BF16),
            pltpu.VMEM((BRANCH_W, D_MODEL), BF16), pltpu.VMEM((BRANCH_W, D_MODEL), BF16),
            pltpu.VMEM((BRANCH_W, D_MODEL), BF16),
            pltpu.VMEM((D_MODEL, D_MODEL), BF16),
            pltpu.VMEM((D_MODEL, D_FF), BF16), pltpu.VMEM((D_FF, D_MODEL), BF16),
            pltpu.VMEM((2, STAGE_ROWS, D_MODEL), F32),
            pltpu.SemaphoreType.DMA((2,)),
        ],
        compiler_params=_params(),
        name="tail",
    )(xc, xl, mod4, *o_ctx, *o_lat, bg, *ln1, *ln2, wg, wpa, wpb, wpc, wo, w1, w2)


def _rope_tables(n_tokens, identity_rows):
    rows = n_tokens // GRID_W
    r, col = jnp.meshgrid(jnp.arange(rows), jnp.arange(GRID_W), indexing="ij")
    r = r.reshape(-1).astype(F32)
    col = col.reshape(-1).astype(F32)
    nf = HEAD_DIM // 4
    inv = ROPE_BASE ** (-jnp.arange(nf, dtype=F32) / nf)
    ang_r = r[:, None] * inv[None, :]
    ang_c = col[:, None] * inv[None, :]
    zero = jnp.zeros_like(ang_r)
    cos = jnp.concatenate([jnp.cos(ang_r)] * 2 + [jnp.cos(ang_c)] * 2, axis=-1)
    s_next = jnp.concatenate([-jnp.sin(ang_r), zero, -jnp.sin(ang_c), zero], axis=-1)
    s_prev = jnp.concatenate([zero, jnp.sin(ang_r), zero, jnp.sin(ang_c)], axis=-1)
    ident = (jnp.ones, jnp.zeros, jnp.zeros)
    return tuple(jnp.concatenate([fill((identity_rows, LANE), F32), jnp.tile(t, (1, LANE // HEAD_DIM))], axis=0)
                 for fill, t in zip(ident, (cos, s_next, s_prev)))


def kernel(x_prompt, x_sample, c, cache_diff_k, cache_diff_v, cache_win_k, cache_win_v, state_ret,
           c_ctx, w_mod, b_mod, w_in, diff_lam, diff_norm_g, win_sink, ret_decay, ret_norm_g,
           w_pa, w_pb, w_pc, w_gate, b_gate, w_o, ln1_g, ln1_b, w_ff1, w_ff2, ln2_g, ln2_b):
    bp, tp, d = x_prompt.shape
    bs, ts, _ = x_sample.shape
    assert d == D_MODEL and w_in.shape == (DEPTH, D_MODEL, D_IN) and c.shape[0] + 1 <= MOD_ROWS
    past = cache_diff_k.shape[2]

    c_rows = jnp.concatenate(
        [c_ctx[None, :], c, jnp.zeros((MOD_ROWS - 1 - bs, d), F32)], axis=0)
    mod4 = _mod_call(c_rows, w_mod, b_mod).reshape(DEPTH, MOD_ROWS, 1, D_MOD)
    proj_tm = 512
    rope_tabs = _rope_tables(ts, proj_tm)

    ck_a = cache_diff_k.reshape(bs, DEPTH, past * DA_HEADS, 2 * HEAD_DIM)
    cv_a = cache_diff_v.reshape(bs, DEPTH, past * DA_HEADS, 2 * HEAD_DIM)
    ck_b = cache_win_k.transpose(0, 1, 3, 4, 2).reshape(bs, DEPTH, WG_KV_HEADS * HEAD_DIM, past)
    cv_b = cache_win_v.transpose(0, 1, 3, 4, 2).reshape(bs, DEPTH, WG_KV_HEADS * HEAD_DIM, past)

    row = lambda a: a.reshape(DEPTH, 1, a.shape[-1])
    ln1, ln2 = (row(ln1_g), row(ln1_b)), (row(ln2_g), row(ln2_b))
    dec4 = ret_decay.reshape(DEPTH, 2, RT_HEADS // 2, 2).transpose(0, 2, 1, 3)

    xp = x_prompt.reshape(bp * tp, d)
    xs = x_sample.reshape(bs * ts, d)
    new_caches, new_state = [], None
    for l in range(DEPTH):
        lam_init = 0.8 - 0.6 * math.exp(-0.3 * l)
        z_all, *new_caches = _proj_call(xs, xp, mod4, l, w_in, rope_tabs, new_caches, ts, tp,
                                        tm=proj_tm)
        z_ctx = (z_all.reshape(-1, tp, D_IN), bs * ts // tp, bp)
        z_lat = (z_all.reshape(-1, ts, D_IN), 0, bs)
        *o_ctx, new_state = _ctx_mix_call(z_ctx, new_state, l, diff_lam, row(diff_norm_g), row(win_sink),
                                          dec4, row(ret_norm_g), lam_init)
        halves = [_lat_mix_call(z_lat, half, (ck_a, cv_a, ck_b, cv_b), state_ret, l, diff_lam,
                                row(diff_norm_g), row(win_sink), dec4, row(ret_norm_g), lam_init)
                  for half in range(2)]
        o_lat = [o.reshape(bs * ts, BRANCH_W // 2) for pair in zip(*halves) for o in pair]
        xp, xs = _tail_call(xp, xs, mod4, l, [o.reshape(bp * tp, BRANCH_W) for o in o_ctx],
                            o_lat, w_gate, row(b_gate),
                            w_pa, w_pb, w_pc, w_o, ln1, w_ff1, w_ff2, ln2, ts)
    dk, dv, wk, wv = new_caches
    new_diff = [a.reshape(bp, DEPTH, tp, DA_HEADS, 2 * HEAD_DIM) for a in (dk, dv)]
    new_win = [a.reshape(bp, DEPTH, WG_KV_HEADS, HEAD_DIM, tp).transpose(0, 1, 4, 2, 3) for a in (wk, wv)]
    return (xp.reshape(bp, tp, d), xs.reshape(bs, ts, d), *new_diff, *new_win, new_state)
```

```python
import functools
import math

import jax
import jax.numpy as jnp
from jax import lax
from jax.experimental import pallas as pl
from jax.experimental.pallas import tpu as pltpu

F32 = jnp.float32
BF16 = jnp.bfloat16

D_MODEL = 1024
DEPTH = 2
GRID_W = 64
HEAD_DIM = 64
DA_HEADS = 4
WG_Q_HEADS = 8
WG_KV_HEADS = 2
WG_GROUP = WG_Q_HEADS // WG_KV_HEADS
WINDOW = 128
RT_HEADS = 4
RT_DK = 64
RT_DV = 128
BRANCH_W = 512
D_IN = 3840
D_GATE = 3 * D_MODEL
D_FF = 4 * D_MODEL
D_MOD = 6 * D_MODEL
ROPE_BASE = 10000.0
LN_EPS = 1e-5
ALPHA = (2 * DEPTH) ** 0.25
QK_SCALE = HEAD_DIM ** -0.5
LOG2E = math.log2(math.e)

LANE = 128
MOD_ROWS = 8
ONES_ROWS = 16
PROJ_COLS = 768
PROJ_ROWS = 512
TAIL_ROWS = 256
MOD_COLS = 1536
DIFF_Q_BLOCK = 256

COL_AQ, COL_AK, COL_AV = 0, 4, 8
COL_BQ, COL_BK, COL_BV = 12, 16, 17
COL_CQ, COL_CK, COL_CV, COL_CG = 18, 20, 22, 26
ROPE_BLOCKS = tuple(range(0, 8)) + tuple(range(12, 17))

VMEM_LIMIT = 56 * 1024 * 1024
NT_DIMS = (((1,), (1,)), ((), ()))
TN_DIMS = (((0,), (0,)), ((), ()))


def _params():
    return pltpu.CompilerParams(vmem_limit_bytes=VMEM_LIMIT)


def _resident(shape, index_map):
    return pl.BlockSpec(shape, index_map, pipeline_mode=pl.Buffered(1))


def _dot(a, b):
    return jnp.dot(a, b, preferred_element_type=F32)


def _all_layers(param):
    return _resident(param.shape, lambda *_: (0, 0))


def _layer_row(ref, layer):
    return ref.at[layer:layer + 1, :]


def _zero_other_layers(ref, slot):
    for other in range(ref.shape[1]):
        if other != slot:
            ref[:, other] = jnp.zeros((ref.shape[0],) + ref.shape[2:], F32)
    return ref.at[:, slot]


def _stage_weights_bf16(first_step, pieces, stage_ref, sem):
    def copy(c):
        src = pieces[c][0]
        return pltpu.make_async_copy(src, stage_ref.at[c % 2, 0:src.shape[0], 0:src.shape[1]], sem.at[c % 2])

    @pl.when(first_step)
    def _():
        for c in range(min(2, len(pieces))):
            copy(c).start()
        for c, (src, dst) in enumerate(pieces):
            copy(c).wait()
            dst[...] = stage_ref[c % 2, 0:src.shape[0], 0:src.shape[1]].astype(BF16)
            if c + 2 < len(pieces):
                copy(c + 2).start()


def _mod_kernel(c_ref, w_ref, b_ref, o_ref):
    c = c_ref[...]
    a = (c * jax.nn.sigmoid(c)).astype(BF16)
    o_ref[...] = _dot(a, w_ref[...].astype(BF16)) + b_ref[...]


def _mod_call(c_rows, w_mod, b_mod):
    tn = MOD_COLS
    return pl.pallas_call(
        _mod_kernel,
        grid=(DEPTH, D_MOD // tn),
        in_specs=[
            pl.BlockSpec((MOD_ROWS, D_MODEL), lambda l, n: (0, 0)),
            pl.BlockSpec((None, D_MODEL, tn), lambda l, n: (l, 0, n)),
            pl.BlockSpec((None, 1, tn), lambda l, n: (l, 0, n)),
        ],
        out_specs=pl.BlockSpec((None, MOD_ROWS, tn), lambda l, n: (l, 0, n)),
        out_shape=jax.ShapeDtypeStruct((DEPTH, MOD_ROWS, D_MOD), F32),
        compiler_params=_params(),
        name="mod_vectors",
    )(c_rows, w_mod, b_mod.reshape(DEPTH, 1, D_MOD))


def _proj_kernel(xl_ref, xc_ref, mod_ref, win_hbm, cos_ref, sa_ref, sb_ref, *refs, n_lat_tiles, slot, layer):
    z_ref, dk_ref, dv_ref, wk_ref, wv_ref, win_ref, stage_ref, sem = refs[-8:]
    nc = stage_ref.shape[2]
    _stage_weights_bf16(pl.program_id(0) == 0,
                        [(win_hbm.at[layer, :, c0:c0 + nc], win_ref.at[:, c0:c0 + nc]) for c0 in range(0, D_IN, nc)],
                        stage_ref, sem)
    seqs, t_ctx = dk_ref.shape[0], wk_ref.shape[-1]
    dk_ref, dv_ref, wk_ref, wv_ref = (_zero_other_layers(r, slot) for r in (dk_ref, dv_ref, wk_ref, wv_ref))
    is_lat = pl.program_id(0) < n_lat_tiles
    x = jnp.where(is_lat, xl_ref[...], xc_ref[...])
    sh1 = mod_ref[:, 0:D_MODEL]
    sc1 = mod_ref[:, D_MODEL:2 * D_MODEL]
    h = (x * (1.0 + sc1) + sh1).astype(BF16)
    for c0 in range(0, D_IN, nc):
        z = _dot(h, win_ref[:, c0:c0 + nc])
        for j in range(nc // LANE):
            blk = c0 // LANE + j
            u = z[:, j * LANE:(j + 1) * LANE]
            if blk in ROPE_BLOCKS:
                u = (u * cos_ref[...] + pltpu.roll(u, LANE - 16, 1) * sa_ref[...]
                     + pltpu.roll(u, 16, 1) * sb_ref[...])
            z_ref[:, blk * LANE:(blk + 1) * LANE] = u
            if COL_AK <= blk < COL_BQ:
                ref, head = (dk_ref, blk - COL_AK) if blk < COL_AV else (dv_ref, blk - COL_AV)
                for s in range(seqs):
                    ref[s, pl.ds(head, t_ctx, stride=DA_HEADS), :] = u[s * t_ctx:(s + 1) * t_ctx]
            elif blk in (COL_BK, COL_BV):
                ref = wk_ref if blk == COL_BK else wv_ref
                ut = u.T
                for s in range(seqs):
                    ref[s] = ut[:, s * t_ctx:(s + 1) * t_ctx]


def _proj_call(xl, xc, mod4, layer, w_in, rope_tabs, caches, t_lat, t_ctx, tm=PROJ_ROWS):
    n_lat, n_ctx = xl.shape[0] // tm, xc.shape[0] // tm
    lat_tiles_per_seq = t_lat // tm
    seqs = tm // t_ctx

    def mod_idx(i):
        row = jnp.where(i < n_lat, 1 + i // lat_tiles_per_seq, 0)
        return (layer, row, 0, 0)

    lat_idx = lambda i: jnp.minimum(i, n_lat - 1)
    ctx_idx = lambda i: jnp.maximum(i - n_lat, 0)
    rope_spec = pl.BlockSpec((tm, LANE), lambda i: (jnp.where(i < n_lat, 1 + i % lat_tiles_per_seq, 0), 0))
    n_slots, slot, first = (1, 0, layer) if caches else (DEPTH, layer, 0)
    cache_a = pl.BlockSpec((seqs, n_slots, DA_HEADS * t_ctx, 2 * HEAD_DIM), lambda i: (ctx_idx(i), first, 0, 0))
    cache_b = pl.BlockSpec((seqs, n_slots, WG_KV_HEADS * HEAD_DIM, t_ctx), lambda i: (ctx_idx(i), first, 0, 0))
    in_specs = [
        pl.BlockSpec((tm, D_MODEL), lambda i: (lat_idx(i), 0)),
        pl.BlockSpec((tm, D_MODEL), lambda i: (ctx_idx(i), 0)),
        pl.BlockSpec((None, None, 1, D_MOD), mod_idx),
        pl.BlockSpec(memory_space=pl.ANY),
        rope_spec, rope_spec, rope_spec,
    ]
    n_in = len(in_specs)
    in_specs += [pl.BlockSpec(memory_space=pl.ANY)] * len(caches)
    b_ctx = xc.shape[0] // t_ctx
    cache_shapes = [(b_ctx, DEPTH, DA_HEADS * t_ctx, 2 * HEAD_DIM)] * 2 \
        + [(b_ctx, DEPTH, WG_KV_HEADS * HEAD_DIM, t_ctx)] * 2
    return pl.pallas_call(
        functools.partial(_proj_kernel, n_lat_tiles=n_lat, slot=slot, layer=layer),
        grid=(n_lat + n_ctx,),
        in_specs=in_specs,
        out_specs=[pl.BlockSpec((tm, D_IN), lambda i: (i, 0)), cache_a, cache_a, cache_b, cache_b],
        out_shape=[jax.ShapeDtypeStruct((xl.shape[0] + xc.shape[0], D_IN), F32)]
        + [jax.ShapeDtypeStruct(s, F32) for s in cache_shapes],
        input_output_aliases={n_in + k: 1 + k for k in range(len(caches))},
        scratch_shapes=[
            pltpu.VMEM((D_MODEL, D_IN), BF16),
            pltpu.VMEM((2, D_MODEL, PROJ_COLS), F32),
            pltpu.SemaphoreType.DMA((2,)),
        ],
        compiler_params=_params(),
        name="proj",
    )(xl, xc, mod4, w_in, *rope_tabs, *caches)


def _values_t(v_parts):
    vt = jnp.concatenate([v.T for v in v_parts], axis=1)
    return jnp.concatenate([vt, jnp.ones((ONES_ROWS, vt.shape[1]), F32)], axis=0).astype(BF16)


def _run_chains(chains):
    a, b = {}, {}
    n = len(chains)
    for c in range(n + 2):
        if c < n:
            a[c] = chains[c][0]()
        if 1 <= c <= n:
            b[c - 1] = chains[c - 1][1](a.pop(c - 1))
        if c >= 2:
            chains[c - 2][2](b.pop(c - 2))


def _diff_chains(q_ref, k_ref, v_ref, o_ref, lam_ref, g_ref, cached_kv, *, lam_init, tq):
    t, heads = q_ref.shape[0], q_ref.shape[1] // LANE
    lv = lam_ref[...]
    lam = (jnp.exp(jnp.sum(lv[0:1] * lv[1:2], axis=-1, keepdims=True))
           - jnp.exp(jnp.sum(lv[2:3] * lv[3:4], axis=-1, keepdims=True)) + lam_init)
    lane = lax.broadcasted_iota(jnp.int32, (tq, LANE), 1)
    kv = {}

    def scores(hh, i):
        cols = slice(hh * LANE, (hh + 1) * LANE)
        if hh not in kv:
            k_parts, v_parts = [k_ref[:, cols]], [v_ref[:, cols]]
            extra = cached_kv(hh)
            if extra is not None:
                k_parts.append(extra[0])
                v_parts.append(extra[1])
            kv[hh] = (jnp.concatenate(k_parts, axis=0).astype(BF16), _values_t(v_parts))
        q = q_ref[i * tq:(i + 1) * tq, cols] * (QK_SCALE * LOG2E)
        qs = jnp.concatenate([jnp.where(lane < HEAD_DIM, q, 0.0),
                              jnp.where(lane >= HEAD_DIM, q, 0.0)], axis=0).astype(BF16)
        return lax.dot_general(kv[hh][0], qs, NT_DIMS, preferred_element_type=F32)

    def weights(s):
        return jnp.exp2(s - s.max(axis=0, keepdims=True)).astype(BF16)

    def finish(hh, i, et):
        ot = _dot(kv[hh][1], et)
        o = ot[0:LANE] * (1.0 / ot[LANE:LANE + 1])
        od = o[:, 0:tq] - lam * o[:, tq:2 * tq]
        yt = od * lax.rsqrt(jnp.mean(od * od, axis=0, keepdims=True) + LN_EPS)
        o_ref[i * tq:(i + 1) * tq, hh * LANE:(hh + 1) * LANE] = yt.T * g_ref[...] * (1.0 - lam_init)

    return [(functools.partial(scores, hh, i), weights, functools.partial(finish, hh, i))
            for hh in range(heads) for i in range(t // tq)]


def _win_chains(q_ref, k_ref, v_ref, o_ref, sink_ref, kc_ref=None, vc_ref=None, kv_heads=range(WG_KV_HEADS)):
    latent = kc_ref is not None
    head0 = kv_heads[0] * WG_GROUP
    t = q_ref.shape[0]
    w = WINDOW
    if latent:
        tqb, heads_per_chain = w, WG_GROUP
    else:
        tqb, heads_per_chain = t, 2
    nb = t // tqb
    kb = k_ref[...].astype(BF16)
    vt = v_ref[...].T
    if latent:
        kcb = kc_ref[...].T.astype(BF16)
        vct = vc_ref[...]
        jj = lax.broadcasted_iota(jnp.int32, (w, w), 0)
        ii = lax.broadcasted_iota(jnp.int32, (w, w), 1)
        bias_prev = jnp.concatenate([jnp.where(jj >= ii, 0.0, -1e30)] * heads_per_chain, axis=1)
        bias_next = jnp.concatenate([jnp.where(jj <= ii, 0.0, -1e30)] * heads_per_chain, axis=1)
    chains = [(kv, h0, n)
              for n in range(nb)
              for kv in kv_heads
              for h0 in range(kv * WG_GROUP, (kv + 1) * WG_GROUP, heads_per_chain)]
    sinks = {}

    def sink_row(h0):
        if h0 not in sinks:
            sinks[h0] = jnp.concatenate(
                [jnp.broadcast_to(sink_ref[:, h:h + 1] * LOG2E, (1, tqb))
                 for h in range(h0, h0 + heads_per_chain)], axis=1)
        return sinks[h0]

    def key_blocks(n):
        return (max(n - 1, 0), min(n + 1, nb - 1)) if latent else (0, 0)

    def scores(kv, h0, n):
        lo = kv * HEAD_DIM
        rows = slice(n * tqb, (n + 1) * tqb)
        q_g = (jnp.concatenate([q_ref[rows, (h - head0) * HEAD_DIM:(h - head0 + 1) * HEAD_DIM]
                                for h in range(h0, h0 + heads_per_chain)], axis=0)
               * (QK_SCALE * LOG2E)).astype(BF16)
        b0, b1 = key_blocks(n)
        keys = kb[b0 * tqb:(b1 + 1) * tqb, lo:lo + HEAD_DIM]
        if latent:
            keys = jnp.concatenate([keys, kcb[:, lo:lo + HEAD_DIM]], axis=0)
        st = lax.dot_general(keys, q_g, NT_DIMS, preferred_element_type=F32)
        if latent:
            parts = []
            for blk in range(b0, b1 + 1):
                part = st[(blk - b0) * w:(blk - b0 + 1) * w]
                if blk == n - 1:
                    part = part + bias_prev
                elif blk == n + 1:
                    part = part + bias_next
                parts.append(part)
            parts.append(st[(b1 - b0 + 1) * w:])
            st = jnp.concatenate(parts, axis=0)
        return st

    def weights(h0, st):
        m = jnp.maximum(st.max(axis=0, keepdims=True), sink_row(h0))
        return jnp.exp2(st - m).astype(BF16), m

    def finish(kv, h0, n, et_m):
        et, m = et_m
        lo = kv * HEAD_DIM
        b0, b1 = key_blocks(n)
        vals = [vt[lo:lo + HEAD_DIM, b0 * tqb:(b1 + 1) * tqb]]
        if latent:
            vals.append(vct[lo:lo + HEAD_DIM])
        vals = jnp.concatenate(vals, axis=1)
        vals = jnp.concatenate([vals, jnp.ones((ONES_ROWS, vals.shape[1]), F32)], axis=0).astype(BF16)
        ot = _dot(vals, et)
        d = ot[HEAD_DIM:HEAD_DIM + 1] + jnp.exp2(sink_row(h0) - m)
        on = ot[0:HEAD_DIM] * (1.0 / d)
        for p in range(heads_per_chain // 2):
            pair = jnp.concatenate([on[:, (2 * p) * tqb:(2 * p + 1) * tqb],
                                    on[:, (2 * p + 1) * tqb:(2 * p + 2) * tqb]], axis=0)
            c0 = ((h0 - head0) // 2 + p) * LANE
            o_ref[n * tqb:(n + 1) * tqb, c0:c0 + LANE] = pair.T

    return [(functools.partial(scores, kv, h0, n), functools.partial(weights, h0),
             functools.partial(finish, kv, h0, n)) for kv, h0, n in chains]


def _log_sigmoid(x):
    return jnp.minimum(x, 0.0) - jnp.log1p(jnp.exp(-jnp.abs(x)))


def _ret_fill_decay(d_scr, lg, t, latent):
    bw = min(t, LANE)
    nb = t // bw
    off = (lax.broadcasted_iota(jnp.int32, (bw, bw), 1)
           - lax.broadcasted_iota(jnp.int32, (bw, bw), 0)).astype(F32)
    for hh in range(d_scr.shape[0]):
        lgf, lgb = lg[hh // 2][0:1, hh % 2:hh % 2 + 1], lg[hh // 2][1:2, hh % 2:hh % 2 + 1]
        for o in range(-(nb - 1), nb):
            diff = off + float(o * bw)
            blk = (jnp.where(diff >= 0, jnp.exp(jnp.maximum(diff, 0.0) * lgf), 0.0)
                   + jnp.where(diff <= 0, jnp.exp(jnp.maximum(-diff, 0.0) * lgb), 0.0))
            for bs in range(max(0, -o), min(nb, nb - o)):
                d_scr[hh, bs * bw:(bs + 1) * bw, (bs + o) * bw:(bs + o + 1) * bw] = blk
        if latent:
            tp = lax.broadcasted_iota(jnp.int32, (RT_DK, t), 1).astype(F32)
            d_scr[hh, t:t + RT_DK, :] = jnp.exp((tp + 1.0) * lgf)
            d_scr[hh, t + RT_DK:t + 2 * RT_DK, :] = jnp.exp((float(t) - tp) * lgb)


def _ret_chains(q_ref, k_ref, v_refs, cg_refs, lg, g_ref, o_ref, d_scr, s0_ref=None, sfin_ref=None):
    latent = s0_ref is not None
    seqs, t = q_ref.shape[0], q_ref.shape[1]
    pairs = len(v_refs)
    tq = min(t, 512)
    if latent:
        eye = (lax.broadcasted_iota(jnp.int32, (RT_DK, RT_DK), 0)
               == lax.broadcasted_iota(jnp.int32, (RT_DK, RT_DK), 1)).astype(F32)
    heads = {}

    def head_operands(sq, hh):
        if (sq, hh) not in heads:
            p, j = hh // 2, hh % 2
            kf = k_ref[sq, :, hh * RT_DK:(hh + 1) * RT_DK] * (RT_DK ** -0.5)
            v = v_refs[p][sq, :, j * RT_DV:(j + 1) * RT_DV]
            keys, vals_t = [kf], [v.T]
            if latent:
                keys += [eye, eye]
                vals_t.append(jnp.concatenate([s0_ref[sq, 0, hh], s0_ref[sq, 1, hh]], axis=0).T)
            heads[(sq, hh)] = (jnp.concatenate(keys, axis=0).astype(BF16),
                               jnp.concatenate(vals_t, axis=1).astype(BF16))
            if not latent:
                sp = lax.broadcasted_iota(jnp.int32, (t, 1), 0).astype(F32)
                zf = jnp.exp((float(t) - 1.0 - sp) * lg[p][0:1, j:j + 1])
                zb = jnp.exp(sp * lg[p][1:2, j:j + 1])
                kz = jnp.concatenate([kf * zf, kf * zb], axis=1).astype(BF16)
                s_fb = _dot(heads[(sq, hh)][1], kz).T
                sfin_ref[sq, 0, hh] = s_fb[0:RT_DK]
                sfin_ref[sq, 1, hh] = s_fb[RT_DK:2 * RT_DK]
        return heads[(sq, hh)]

    chains = [(sq, hh, i) for sq in range(seqs) for hh in range(2 * pairs) for i in range(t // tq)]

    def scores(sq, hh, i):
        q = q_ref[sq, i * tq:(i + 1) * tq, hh * RT_DK:(hh + 1) * RT_DK].astype(BF16)
        return lax.dot_general(head_operands(sq, hh)[0], q, NT_DIMS, preferred_element_type=F32)

    def finish(sq, hh, i, at):
        yt = _dot(head_operands(sq, hh)[1], at)
        mu = jnp.mean(yt, axis=0, keepdims=True)
        yc = yt - mu
        var = jnp.mean(yc * yc, axis=0, keepdims=True)
        yn = (yc * lax.rsqrt(var + LN_EPS)).T * g_ref[...]
        cg = cg_refs[hh // 2][sq, i * tq:(i + 1) * tq, (hh % 2) * RT_DV:(hh % 2 + 1) * RT_DV]
        o_ref[sq, i * tq:(i + 1) * tq, hh * RT_DV:(hh + 1) * RT_DV] = yn * (cg * jax.nn.sigmoid(cg))

    def weights(hh, i, st):
        return (st * d_scr[hh, :, i * tq:(i + 1) * tq]).astype(BF16)

    return [(functools.partial(scores, sq, hh, i), functools.partial(weights, hh, i),
             functools.partial(finish, sq, hh, i)) for sq, hh, i in chains]


def _lat_mix_kernel(aq_ref, ak_ref, av_ref, bq_ref, bk_ref, bv_ref, cq_ref, ck_ref, cv_ref, cg_ref,
                    akc_ref, avc_ref, bkc_ref, bvc_ref, s0_ref, lam_ref, dng_ref, sink_ref, dec_ref, rng_ref,
                    oa_ref, ob_ref, oc_ref, d_scr, *, lam_init, half, layer):
    t = aq_ref.shape[1]
    dng_ref, sink_ref, rng_ref = (_layer_row(r, layer) for r in (dng_ref, sink_ref, rng_ref))
    lg = [_log_sigmoid(dec_ref[0])]

    @pl.when(pl.program_id(0) == 0)
    def _():
        _ret_fill_decay(d_scr, lg, t, True)

    def cached_kv(hh):
        mine = pl.ds(2 * half + hh, akc_ref.shape[0] // DA_HEADS, stride=DA_HEADS)
        return akc_ref[mine, :], avc_ref[mine, :]

    chains = _diff_chains(aq_ref.at[0], ak_ref.at[0], av_ref.at[0], oa_ref.at[0], lam_ref, dng_ref,
                          cached_kv, lam_init=lam_init, tq=DIFF_Q_BLOCK)
    chains += _win_chains(bq_ref.at[0], bk_ref.at[0], bv_ref.at[0], ob_ref.at[0], sink_ref,
                          bkc_ref, bvc_ref, kv_heads=(half,))
    chains += _ret_chains(cq_ref, ck_ref, [cv_ref], [cg_ref], lg, rng_ref, oc_ref, d_scr, s0_ref=s0_ref)
    _run_chains(chains)


def _lat_mix_call(z3, half, caches, state6, layer, diff_lam, diff_norm_g, sink, dec4, ret_norm_g, lam_init):
    z3, b0, b = z3
    t = z3.shape[1]
    ck_a, cv_a, ck_b, cv_b = caches
    zcol = lambda blk0, nblk: pl.BlockSpec((1, t, nblk * LANE), lambda bi: (bi + b0, 0, blk0 // nblk))
    layer_spec = lambda *s: pl.BlockSpec((None,) + s, lambda bi: (layer,) + (0,) * len(s))
    cache_spec = lambda a: pl.BlockSpec((None, None) + a.shape[2:], lambda bi: (bi, layer, 0, 0))
    o_spec = pl.BlockSpec((1, t, BRANCH_W // 2), lambda bi: (bi, 0, 0))
    o_shape = jax.ShapeDtypeStruct((b, t, BRANCH_W // 2), F32)
    return pl.pallas_call(
        functools.partial(_lat_mix_kernel, lam_init=lam_init, half=half, layer=layer),
        grid=(b,),
        in_specs=[
            zcol(COL_AQ + 2 * half, 2), zcol(COL_AK + 2 * half, 2), zcol(COL_AV + 2 * half, 2),
            zcol(COL_BQ + 2 * half, 2), zcol(COL_BK, 1), zcol(COL_BV, 1),
            zcol(COL_CQ + half, 1), zcol(COL_CK + half, 1), zcol(COL_CV + 2 * half, 2), zcol(COL_CG + 2 * half, 2),
            cache_spec(ck_a), cache_spec(cv_a), cache_spec(ck_b), cache_spec(cv_b),
            pl.BlockSpec((1, None, 2, 2, RT_DK, RT_DV), lambda bi: (bi, layer, 0, half, 0, 0)),
            layer_spec(4, HEAD_DIM), _all_layers(diff_norm_g), _all_layers(sink),
            pl.BlockSpec((None, 1, 2, 2), lambda bi: (layer, half, 0, 0)), _all_layers(ret_norm_g),
        ],
        out_specs=[o_spec, o_spec, o_spec],
        out_shape=[o_shape, o_shape, o_shape],
        scratch_shapes=[pltpu.VMEM((2, t + 2 * RT_DK, t), F32)],
        compiler_params=_params(),
        name="mix_lat",
    )(*([z3] * 10), ck_a, cv_a, ck_b, cv_b, state6, diff_lam, diff_norm_g, sink, dec4, ret_norm_g)


def _ctx_mix_kernel(z_ref, lam_ref, dng_ref, sink_ref, dec_ref, rng_ref, *refs, lam_init, slot, layer):
    oa_ref, ob_ref, oc_ref, sfin_ref, d_scr = refs[-5:]
    sfin_ref = _zero_other_layers(sfin_ref, slot)
    dng_ref, sink_ref, rng_ref = (_layer_row(r, layer) for r in (dng_ref, sink_ref, rng_ref))
    seqs, t = z_ref.shape[0], z_ref.shape[1]
    cols = lambda c0, c1: slice(c0 * LANE, c1 * LANE)
    lg = [_log_sigmoid(dec_ref[p]) for p in range(RT_HEADS // 2)]

    @pl.when(pl.program_id(0) == 0)
    def _():
        _ret_fill_decay(d_scr, lg, t, False)

    chains = []
    for sq in range(seqs):
        chains += _diff_chains(z_ref.at[sq, :, cols(COL_AQ, COL_AK)], z_ref.at[sq, :, cols(COL_AK, COL_AV)],
                               z_ref.at[sq, :, cols(COL_AV, COL_BQ)], oa_ref.at[sq], lam_ref, dng_ref,
                               lambda _: None, lam_init=lam_init, tq=t)
        chains += _win_chains(z_ref.at[sq, :, cols(COL_BQ, COL_BK)], z_ref.at[sq, :, cols(COL_BK, COL_BV)],
                              z_ref.at[sq, :, cols(COL_BV, COL_CQ)], ob_ref.at[sq], sink_ref)
    chains += _ret_chains(z_ref.at[:, :, cols(COL_CQ, COL_CK)], z_ref.at[:, :, cols(COL_CK, COL_CV)],
                          [z_ref.at[:, :, cols(COL_CV + 2 * p, COL_CV + 2 * p + 2)] for p in range(2)],
                          [z_ref.at[:, :, cols(COL_CG + 2 * p, COL_CG + 2 * p + 2)] for p in range(2)],
                          lg, rng_ref, oc_ref, d_scr, sfin_ref=sfin_ref)
    _run_chains(chains)


def _ctx_mix_call(z3, new_state, layer, diff_lam, diff_norm_g, sink, dec4, ret_norm_g, lam_init, seqs=2):
    z3, b0, b = z3
    t = z3.shape[1]
    layer_spec = lambda *s: pl.BlockSpec((None,) + s, lambda bi: (layer,) + (0,) * len(s))
    in_specs = [
        pl.BlockSpec((seqs, t, D_IN), lambda bi: (bi + b0 // seqs, 0, 0)),
        layer_spec(4, HEAD_DIM), _all_layers(diff_norm_g), _all_layers(sink),
        layer_spec(RT_HEADS // 2, 2, 2), _all_layers(ret_norm_g),
    ]
    args = [z3, diff_lam, diff_norm_g, sink, dec4, ret_norm_g]
    n_slots, slot, first = (DEPTH, layer, 0) if new_state is None else (1, 0, layer)
    aliases = {}
    if new_state is not None:
        aliases = {len(in_specs): 3}
        in_specs.append(pl.BlockSpec(memory_space=pl.ANY))
        args.append(new_state)
    o_spec = pl.BlockSpec((seqs, t, BRANCH_W), lambda bi: (bi, 0, 0))
    o_shape = jax.ShapeDtypeStruct((b, t, BRANCH_W), F32)
    return pl.pallas_call(
        functools.partial(_ctx_mix_kernel, lam_init=lam_init, slot=slot, layer=layer),
        grid=(b // seqs,),
        in_specs=in_specs,
        out_specs=[o_spec, o_spec, o_spec,
                   pl.BlockSpec((seqs, n_slots, 2, RT_HEADS, RT_DK, RT_DV), lambda bi: (bi, first, 0, 0, 0, 0))],
        out_shape=[o_shape, o_shape, o_shape,
                   jax.ShapeDtypeStruct((b, DEPTH, 2, RT_HEADS, RT_DK, RT_DV), F32)],
        input_output_aliases=aliases,
        scratch_shapes=[pltpu.VMEM((RT_HEADS, t, t), F32)],
        compiler_params=_params(),
        name="mix_ctx",
    )(*args)


def _layer_norm(x, g, b):
    mu = jnp.mean(x, axis=-1, keepdims=True)
    xc = x - mu
    var = jnp.mean(xc * xc, axis=-1, keepdims=True)
    return xc * lax.rsqrt(var + LN_EPS) * g + b


def _tail_kernel(xc_ref, xl_ref, mod_ref, oac_ref, obc_ref, occ_ref,
                 oal0_ref, oal1_ref, obl0_ref, obl1_ref, ocl0_ref, ocl1_ref,
                 bg_ref, ln1g_ref, ln1b_ref, ln2g_ref, ln2b_ref,
                 wg_hbm, wpa_hbm, wpb_hbm, wpc_hbm, wo_hbm, w1_hbm, w2_hbm, yc_ref, yl_ref,
                 wg_ref, wpa_ref, wpb_ref, wpc_ref, wo_ref, w1_ref, w2_ref, stage_ref, sem,
                 *, n_ctx_tiles, layer):
    d = D_MODEL
    bg_ref, ln1g_ref, ln1b_ref, ln2g_ref, ln2b_ref = (
        _layer_row(r, layer) for r in (bg_ref, ln1g_ref, ln1b_ref, ln2g_ref, ln2b_ref))
    pieces = [(wg_hbm.at[layer, :, i * d:(i + 1) * d], wg_ref.at[:, i * d:(i + 1) * d]) for i in range(3)]
    pieces += [(w.at[layer], r) for w, r in ((wpa_hbm, wpa_ref), (wpb_hbm, wpb_ref), (wpc_hbm, wpc_ref),
                                             (wo_hbm, wo_ref))]
    pieces += [(w1_hbm.at[layer, :, i * d:(i + 1) * d], w1_ref.at[:, i * d:(i + 1) * d]) for i in range(4)]
    pieces += [(w2_hbm.at[layer, i * d:(i + 1) * d, :], w2_ref.at[i * d:(i + 1) * d, :]) for i in range(4)]
    _stage_weights_bf16(pl.program_id(0) == 0, pieces, stage_ref, sem)

    is_ctx = pl.program_id(0) < n_ctx_tiles
    x = jnp.where(is_ctx, xc_ref[...], xl_ref[...])
    sh1, sc1, g1 = mod_ref[:, 0:d], mod_ref[:, d:2 * d], mod_ref[:, 2 * d:3 * d]
    sh2, sc2, g2 = mod_ref[:, 3 * d:4 * d], mod_ref[:, 4 * d:5 * d], mod_ref[:, 5 * d:6 * d]
    h1 = (x * (1.0 + sc1) + sh1).astype(BF16)
    merged = None
    branches = ((oac_ref, (oal0_ref, oal1_ref), wpa_ref), (obc_ref, (obl0_ref, obl1_ref), wpb_ref),
                (occ_ref, (ocl0_ref, ocl1_ref), wpc_ref))
    for i, (oc_ref_i, ol_refs_i, wp_ref) in enumerate(branches):
        o_lat = jnp.concatenate([r[...] for r in ol_refs_i], axis=1)
        o = jnp.where(is_ctx, oc_ref_i[...], o_lat).astype(BF16)
        gate = jax.nn.sigmoid(_dot(h1, wg_ref[:, i * d:(i + 1) * d]) + bg_ref[:, i * d:(i + 1) * d])
        part = gate * _dot(o, wp_ref[...])
        merged = part if merged is None else merged + part
    y = _dot(merged.astype(BF16), wo_ref[...])
    x1 = _layer_norm(ALPHA * x + g1 * y, ln1g_ref[...], ln1b_ref[...])
    h2 = (x1 * (1.0 + sc2) + sh2).astype(BF16)
    u = jnp.maximum(_dot(h2, w1_ref[...]), 0.0)
    f = _dot((u * u).astype(BF16), w2_ref[...])
    x2 = _layer_norm(ALPHA * x1 + g2 * f, ln2g_ref[...], ln2b_ref[...])

    @pl.when(is_ctx)
    def _():
        yc_ref[...] = x2

    @pl.when(jnp.logical_not(is_ctx))
    def _():
        yl_ref[...] = x2


def _tail_call(xc, xl, mod4, layer, o_ctx, o_lat, wg, bg, wpa, wpb, wpc, wo, ln1, w1, w2, ln2,
               t_lat, tm=TAIL_ROWS):
    n_ctx, n_lat = xc.shape[0] // tm, xl.shape[0] // tm
    lat_tiles_per_seq = t_lat // tm

    def mod_idx(i):
        row = jnp.where(i < n_ctx, 0, 1 + (i - n_ctx) // lat_tiles_per_seq)
        return (layer, row, 0, 0)

    ctx_spec = lambda w: pl.BlockSpec((tm, w), lambda i: (jnp.minimum(i, n_ctx - 1), 0))
    lat_spec = lambda w: pl.BlockSpec((tm, w), lambda i: (jnp.maximum(i - n_ctx, 0), 0))
    in_hbm = pl.BlockSpec(memory_space=pl.ANY)
    return pl.pallas_call(
        functools.partial(_tail_kernel, n_ctx_tiles=n_ctx, layer=layer),
        grid=(n_ctx + n_lat,),
        in_specs=[
            ctx_spec(D_MODEL), lat_spec(D_MODEL),
            pl.BlockSpec((None, None, 1, D_MOD), mod_idx),
            ctx_spec(BRANCH_W), ctx_spec(BRANCH_W), ctx_spec(BRANCH_W),
            *[lat_spec(BRANCH_W // 2)] * 6,
            *[_all_layers(p) for p in (bg, *ln1, *ln2)],
        ] + [in_hbm] * 7,
        out_specs=[ctx_spec(D_MODEL), lat_spec(D_MODEL)],
        out_shape=[jax.ShapeDtypeStruct(xc.shape, F32), jax.ShapeDtypeStruct(xl.shape, F32)],
        scratch_shapes=[
            pltpu.VMEM((D_MODEL, D_GATE), BF16),
            pltpu.VMEM((BRANCH_W, D_MODEL), BF16), pltpu.VMEM((BRANCH_W, D_MODEL), BF16),
            pltpu.VMEM((BRANCH_W, D_MODEL), BF16),
            pltpu.VMEM((D_MODEL, D_MODEL), BF16),
            pltpu.VMEM((D_MODEL, D_FF), BF16), pltpu.VMEM((D_FF, D_MODEL), BF16),
            pltpu.VMEM((2, D_MODEL, D_MODEL), F32),
            pltpu.SemaphoreType.DMA((2,)),
        ],
        compiler_params=_params(),
        name="tail",
    )(xc, xl, mod4, *o_ctx, *o_lat, bg, *ln1, *ln2, wg, wpa, wpb, wpc, wo, w1, w2)


def _rope_tables(n_tokens, identity_rows):
    rows = n_tokens // GRID_W
    r, col = jnp.meshgrid(jnp.arange(rows), jnp.arange(GRID_W), indexing="ij")
    r = r.reshape(-1).astype(F32)
    col = col.reshape(-1).astype(F32)
    nf = HEAD_DIM // 4
    inv = ROPE_BASE ** (-jnp.arange(nf, dtype=F32) / nf)
    ang_r = r[:, None] * inv[None, :]
    ang_c = col[:, None] * inv[None, :]
    zero = jnp.zeros_like(ang_r)
    cos = jnp.concatenate([jnp.cos(ang_r)] * 2 + [jnp.cos(ang_c)] * 2, axis=-1)
    s_next = jnp.concatenate([-jnp.sin(ang_r), zero, -jnp.sin(ang_c), zero], axis=-1)
    s_prev = jnp.concatenate([zero, jnp.sin(ang_r), zero, jnp.sin(ang_c)], axis=-1)
    ident = (jnp.ones, jnp.zeros, jnp.zeros)
    return tuple(jnp.concatenate([fill((identity_rows, LANE), F32), jnp.tile(t, (1, LANE // HEAD_DIM))], axis=0)
                 for fill, t in zip(ident, (cos, s_next, s_prev)))


def kernel(x_prompt, x_sample, c, cache_diff_k, cache_diff_v, cache_win_k, cache_win_v, state_ret,
           c_ctx, w_mod, b_mod, w_in, diff_lam, diff_norm_g, win_sink, ret_decay, ret_norm_g,
           w_pa, w_pb, w_pc, w_gate, b_gate, w_o, ln1_g, ln1_b, w_ff1, w_ff2, ln2_g, ln2_b):
    bp, tp, d = x_prompt.shape
    bs, ts, _ = x_sample.shape
    assert d == D_MODEL and w_in.shape == (DEPTH, D_MODEL, D_IN) and c.shape[0] + 1 <= MOD_ROWS
    past = cache_diff_k.shape[2]

    c_rows = jnp.concatenate(
        [c_ctx[None, :], c, jnp.zeros((MOD_ROWS - 1 - bs, d), F32)], axis=0)
    mod4 = _mod_call(c_rows, w_mod, b_mod).reshape(DEPTH, MOD_ROWS, 1, D_MOD)
    rope_tabs = _rope_tables(ts, PROJ_ROWS)

    ck_a = cache_diff_k.reshape(bs, DEPTH, past * DA_HEADS, 2 * HEAD_DIM)
    cv_a = cache_diff_v.reshape(bs, DEPTH, past * DA_HEADS, 2 * HEAD_DIM)
    ck_b = cache_win_k.transpose(0, 1, 3, 4, 2).reshape(bs, DEPTH, WG_KV_HEADS * HEAD_DIM, past)
    cv_b = cache_win_v.transpose(0, 1, 3, 4, 2).reshape(bs, DEPTH, WG_KV_HEADS * HEAD_DIM, past)

    ln1, ln2 = (ln1_g, ln1_b), (ln2_g, ln2_b)
    dec4 = ret_decay.reshape(DEPTH, 2, RT_HEADS // 2, 2).transpose(0, 2, 1, 3)

    xp = x_prompt.reshape(bp * tp, d)
    xs = x_sample.reshape(bs * ts, d)
    new_caches, new_state = [], None
    for l in range(DEPTH):
        lam_init = 0.8 - 0.6 * math.exp(-0.3 * l)
        z_all, *new_caches = _proj_call(xs, xp, mod4, l, w_in, rope_tabs, new_caches, ts, tp)
        z_ctx = (z_all.reshape(-1, tp, D_IN), bs * ts // tp, bp)
        z_lat = (z_all.reshape(-1, ts, D_IN), 0, bs)
        *o_ctx, new_state = _ctx_mix_call(z_ctx, new_state, l, diff_lam, diff_norm_g, win_sink,
                                          dec4, ret_norm_g, lam_init)
        halves = [_lat_mix_call(z_lat, half, (ck_a, cv_a, ck_b, cv_b), state_ret, l, diff_lam,
                                diff_norm_g, win_sink, dec4, ret_norm_g, lam_init)
                  for half in range(2)]
        o_lat = [o.reshape(bs * ts, BRANCH_W // 2) for pair in zip(*halves) for o in pair]
        xp, xs = _tail_call(xp, xs, mod4, l, [o.reshape(bp * tp, BRANCH_W) for o in o_ctx],
                            o_lat, w_gate, b_gate,
                            w_pa, w_pb, w_pc, w_o, ln1, w_ff1, w_ff2, ln2, ts)
    dk, dv, wk, wv = new_caches
    new_diff = [a.reshape(bp, DEPTH, tp, DA_HEADS, 2 * HEAD_DIM) for a in (dk, dv)]
    new_win = [a.reshape(bp, DEPTH, WG_KV_HEADS, HEAD_DIM, tp).transpose(0, 1, 4, 2, 3) for a in (wk, wv)]
    return (xp.reshape(bp, tp, d), xs.reshape(bs, ts, d), *new_diff, *new_win, new_state)
```

```python
import functools
import math

import jax
import jax.numpy as jnp
from jax import lax
from jax.experimental import pallas as pl
from jax.experimental.pallas import tpu as pltpu

F32 = jnp.float32
BF16 = jnp.bfloat16

D_MODEL = 1024
DEPTH = 2
GRID_W = 64
HEAD_DIM = 64
DA_HEADS = 4
WG_Q_HEADS = 8
WG_KV_HEADS = 2
WG_GROUP = WG_Q_HEADS // WG_KV_HEADS
WINDOW = 128
RT_HEADS = 4
RT_DK = 64
RT_DV = 128
BRANCH_W = 512
D_IN = 3840
D_GATE = 3 * D_MODEL
D_FF = 4 * D_MODEL
D_MOD = 6 * D_MODEL
ROPE_BASE = 10000.0
LN_EPS = 1e-5
ALPHA = (2 * DEPTH) ** 0.25
QK_SCALE = HEAD_DIM ** -0.5
LOG2E = math.log2(math.e)

LANE = 128
MOD_ROWS = 8
ONES_ROWS = 16
PROJ_COLS = 768
PROJ_ROWS = 512
TAIL_ROWS = 256
MOD_COLS = 3072
DIFF_Q_BLOCK = 128

COL_AQ, COL_AK, COL_AV = 0, 4, 8
COL_BQ, COL_BK, COL_BV = 12, 16, 17
COL_CQ, COL_CK, COL_CV, COL_CG = 18, 20, 22, 26
ROPE_BLOCKS = tuple(range(0, 8)) + tuple(range(12, 17))

VMEM_LIMIT = 56 * 1024 * 1024
NT_DIMS = (((1,), (1,)), ((), ()))
TN_DIMS = (((0,), (0,)), ((), ()))


def _params():
    return pltpu.CompilerParams(vmem_limit_bytes=VMEM_LIMIT)


def _resident(shape, index_map):
    return pl.BlockSpec(shape, index_map, pipeline_mode=pl.Buffered(1))


def _dot(a, b):
    return jnp.dot(a, b, preferred_element_type=F32)


def _all_layers(param):
    return _resident(param.shape, lambda *_: (0, 0))


def _layer_row(ref, layer):
    return ref.at[layer:layer + 1, :]


def _zero_other_layers(ref, slot):
    for other in range(ref.shape[1]):
        if other != slot:
            ref[:, other] = jnp.zeros((ref.shape[0],) + ref.shape[2:], F32)
    return ref.at[:, slot]


def _stage_weights_bf16(first_step, pieces, stage_ref, sem):
    def copy(c):
        src = pieces[c][0]
        return pltpu.make_async_copy(src, stage_ref.at[c % 2, 0:src.shape[0], 0:src.shape[1]], sem.at[c % 2])

    @pl.when(first_step)
    def _():
        for c in range(min(2, len(pieces))):
            copy(c).start()
        for c, (src, dst) in enumerate(pieces):
            copy(c).wait()
            dst[...] = stage_ref[c % 2, 0:src.shape[0], 0:src.shape[1]].astype(BF16)
            if c + 2 < len(pieces):
                copy(c + 2).start()


def _mod_kernel(c_ref, w_ref, b_ref, o_ref):
    c = c_ref[...]
    a = (c * jax.nn.sigmoid(c)).astype(BF16)
    o_ref[...] = _dot(a, w_ref[...].astype(BF16)) + b_ref[...]


def _mod_call(c_rows, w_mod, b_mod):
    tn = MOD_COLS
    return pl.pallas_call(
        _mod_kernel,
        grid=(DEPTH, D_MOD // tn),
        in_specs=[
            pl.BlockSpec((MOD_ROWS, D_MODEL), lambda l, n: (0, 0)),
            pl.BlockSpec((None, D_MODEL, tn), lambda l, n: (l, 0, n)),
            pl.BlockSpec((None, 1, tn), lambda l, n: (l, 0, n)),
        ],
        out_specs=pl.BlockSpec((None, MOD_ROWS, tn), lambda l, n: (l, 0, n)),
        out_shape=jax.ShapeDtypeStruct((DEPTH, MOD_ROWS, D_MOD), F32),
        compiler_params=_params(),
        name="mod_vectors",
    )(c_rows, w_mod, b_mod.reshape(DEPTH, 1, D_MOD))


def _proj_kernel(xl_ref, xc_ref, mod_ref, win_hbm, cos_ref, sa_ref, sb_ref, *refs, n_lat_tiles, slot, layer):
    z_ref, dk_ref, dv_ref, wk_ref, wv_ref, win_ref, stage_ref, sem = refs[-8:]
    nc = stage_ref.shape[2]
    _stage_weights_bf16(pl.program_id(0) == 0,
                        [(win_hbm.at[layer, :, c0:c0 + nc], win_ref.at[:, c0:c0 + nc]) for c0 in range(0, D_IN, nc)],
                        stage_ref, sem)
    seqs, t_ctx = dk_ref.shape[0], wk_ref.shape[-1]
    dk_ref, dv_ref, wk_ref, wv_ref = (_zero_other_layers(r, slot) for r in (dk_ref, dv_ref, wk_ref, wv_ref))
    is_lat = pl.program_id(0) < n_lat_tiles
    x = jnp.where(is_lat, xl_ref[...], xc_ref[...])
    sh1 = mod_ref[:, 0:D_MODEL]
    sc1 = mod_ref[:, D_MODEL:2 * D_MODEL]
    h = (x * (1.0 + sc1) + sh1).astype(BF16)
    for c0 in range(0, D_IN, nc):
        z = _dot(h, win_ref[:, c0:c0 + nc])
        for j in range(nc // LANE):
            blk = c0 // LANE + j
            u = z[:, j * LANE:(j + 1) * LANE]
            if blk in ROPE_BLOCKS:
                u = (u * cos_ref[...] + pltpu.roll(u, LANE - 16, 1) * sa_ref[...]
                     + pltpu.roll(u, 16, 1) * sb_ref[...])
            z_ref[:, blk * LANE:(blk + 1) * LANE] = u
            if COL_AK <= blk < COL_BQ:
                ref, head = (dk_ref, blk - COL_AK) if blk < COL_AV else (dv_ref, blk - COL_AV)
                for s in range(seqs):
                    ref[s, pl.ds(head, t_ctx, stride=DA_HEADS), :] = u[s * t_ctx:(s + 1) * t_ctx]
            elif blk in (COL_BK, COL_BV):
                ref = wk_ref if blk == COL_BK else wv_ref
                ut = u.T
                for s in range(seqs):
                    ref[s] = ut[:, s * t_ctx:(s + 1) * t_ctx]


def _proj_call(xl, xc, mod4, layer, w_in, rope_tabs, caches, t_lat, t_ctx, tm=PROJ_ROWS):
    n_lat, n_ctx = xl.shape[0] // tm, xc.shape[0] // tm
    lat_tiles_per_seq = t_lat // tm
    seqs = tm // t_ctx

    def mod_idx(i):
        row = jnp.where(i < n_lat, 1 + i // lat_tiles_per_seq, 0)
        return (layer, row, 0, 0)

    lat_idx = lambda i: jnp.minimum(i, n_lat - 1)
    ctx_idx = lambda i: jnp.maximum(i - n_lat, 0)
    rope_spec = pl.BlockSpec((tm, LANE), lambda i: (jnp.where(i < n_lat, 1 + i % lat_tiles_per_seq, 0), 0))
    n_slots, slot, first = (1, 0, layer) if caches else (DEPTH, layer, 0)
    cache_a = pl.BlockSpec((seqs, n_slots, DA_HEADS * t_ctx, 2 * HEAD_DIM), lambda i: (ctx_idx(i), first, 0, 0))
    cache_b = pl.BlockSpec((seqs, n_slots, WG_KV_HEADS * HEAD_DIM, t_ctx), lambda i: (ctx_idx(i), first, 0, 0))
    in_specs = [
        pl.BlockSpec((tm, D_MODEL), lambda i: (lat_idx(i), 0)),
        pl.BlockSpec((tm, D_MODEL), lambda i: (ctx_idx(i), 0)),
        pl.BlockSpec((None, None, 1, D_MOD), mod_idx),
        pl.BlockSpec(memory_space=pl.ANY),
        rope_spec, rope_spec, rope_spec,
    ]
    n_in = len(in_specs)
    in_specs += [pl.BlockSpec(memory_space=pl.ANY)] * len(caches)
    b_ctx = xc.shape[0] // t_ctx
    cache_shapes = [(b_ctx, DEPTH, DA_HEADS * t_ctx, 2 * HEAD_DIM)] * 2 \
        + [(b_ctx, DEPTH, WG_KV_HEADS * HEAD_DIM, t_ctx)] * 2
    return pl.pallas_call(
        functools.partial(_proj_kernel, n_lat_tiles=n_lat, slot=slot, layer=layer),
        grid=(n_lat + n_ctx,),
        in_specs=in_specs,
        out_specs=[pl.BlockSpec((tm, D_IN), lambda i: (i, 0)), cache_a, cache_a, cache_b, cache_b],
        out_shape=[jax.ShapeDtypeStruct((xl.shape[0] + xc.shape[0], D_IN), F32)]
        + [jax.ShapeDtypeStruct(s, F32) for s in cache_shapes],
        input_output_aliases={n_in + k: 1 + k for k in range(len(caches))},
        scratch_shapes=[
            pltpu.VMEM((D_MODEL, D_IN), BF16),
            pltpu.VMEM((2, D_MODEL, PROJ_COLS), F32),
            pltpu.SemaphoreType.DMA((2,)),
        ],
        compiler_params=_params(),
        name="proj",
    )(xl, xc, mod4, w_in, *rope_tabs, *caches)


def _values_t(v_parts):
    vt = jnp.concatenate([v.T for v in v_parts], axis=1)
    return jnp.concatenate([vt, jnp.ones((ONES_ROWS, vt.shape[1]), F32)], axis=0).astype(BF16)


def _run_chains(chains):
    a, b = {}, {}
    n = len(chains)
    for c in range(n + 2):
        if c < n:
            a[c] = chains[c][0]()
        if 1 <= c <= n:
            b[c - 1] = chains[c - 1][1](a.pop(c - 1))
        if c >= 2:
            chains[c - 2][2](b.pop(c - 2))


def _diff_chains(q_ref, k_ref, v_ref, o_ref, lam_ref, g_ref, cached_kv, *, lam_init, tq):
    t, heads = q_ref.shape[0], q_ref.shape[1] // LANE
    lv = lam_ref[...]
    lam = (jnp.exp(jnp.sum(lv[0:1] * lv[1:2], axis=-1, keepdims=True))
           - jnp.exp(jnp.sum(lv[2:3] * lv[3:4], axis=-1, keepdims=True)) + lam_init)
    lane = lax.broadcasted_iota(jnp.int32, (tq, LANE), 1)
    kv = {}

    def scores(hh, i):
        cols = slice(hh * LANE, (hh + 1) * LANE)
        if hh not in kv:
            k_parts, v_parts = [k_ref[:, cols]], [v_ref[:, cols]]
            extra = cached_kv(hh)
            if extra is not None:
                k_parts.append(extra[0])
                v_parts.append(extra[1])
            kv[hh] = (jnp.concatenate(k_parts, axis=0).astype(BF16), _values_t(v_parts))
        q = q_ref[i * tq:(i + 1) * tq, cols] * (QK_SCALE * LOG2E)
        qs = jnp.concatenate([jnp.where(lane < HEAD_DIM, q, 0.0),
                              jnp.where(lane >= HEAD_DIM, q, 0.0)], axis=0).astype(BF16)
        return lax.dot_general(kv[hh][0], qs, NT_DIMS, preferred_element_type=F32)

    def weights(s):
        return jnp.exp2(s - s.max(axis=0, keepdims=True)).astype(BF16)

    def finish(hh, i, et):
        ot = _dot(kv[hh][1], et)
        o = ot[0:LANE] * (1.0 / ot[LANE:LANE + 1])
        od = o[:, 0:tq] - lam * o[:, tq:2 * tq]
        yt = od * lax.rsqrt(jnp.mean(od * od, axis=0, keepdims=True) + LN_EPS)
        o_ref[i * tq:(i + 1) * tq, hh * LANE:(hh + 1) * LANE] = yt.T * g_ref[...] * (1.0 - lam_init)

    return [(functools.partial(scores, hh, i), weights, functools.partial(finish, hh, i))
            for hh in range(heads) for i in range(t // tq)]


def _win_chains(q_ref, k_ref, v_ref, o_ref, sink_ref, kc_ref=None, vc_ref=None, kv_heads=range(WG_KV_HEADS)):
    latent = kc_ref is not None
    head0 = kv_heads[0] * WG_GROUP
    t = q_ref.shape[0]
    w = WINDOW
    if latent:
        tqb, heads_per_chain = w, WG_GROUP
    else:
        tqb, heads_per_chain = t, 2
    nb = t // tqb
    kb = k_ref[...].astype(BF16)
    vt = v_ref[...].T
    if latent:
        kcb = kc_ref[...].T.astype(BF16)
        vct = vc_ref[...]
        jj = lax.broadcasted_iota(jnp.int32, (w, w), 0)
        ii = lax.broadcasted_iota(jnp.int32, (w, w), 1)
        bias_prev = jnp.concatenate([jnp.where(jj >= ii, 0.0, -1e30)] * heads_per_chain, axis=1)
        bias_next = jnp.concatenate([jnp.where(jj <= ii, 0.0, -1e30)] * heads_per_chain, axis=1)
    chains = [(kv, h0, n)
              for n in range(nb)
              for kv in kv_heads
              for h0 in range(kv * WG_GROUP, (kv + 1) * WG_GROUP, heads_per_chain)]
    sinks = {}

    def sink_row(h0):
        if h0 not in sinks:
            sinks[h0] = jnp.concatenate(
                [jnp.broadcast_to(sink_ref[:, h:h + 1] * LOG2E, (1, tqb))
                 for h in range(h0, h0 + heads_per_chain)], axis=1)
        return sinks[h0]

    def key_blocks(n):
        return (max(n - 1, 0), min(n + 1, nb - 1)) if latent else (0, 0)

    def scores(kv, h0, n):
        lo = kv * HEAD_DIM
        rows = slice(n * tqb, (n + 1) * tqb)
        q_g = (jnp.concatenate([q_ref[rows, (h - head0) * HEAD_DIM:(h - head0 + 1) * HEAD_DIM]
                                for h in range(h0, h0 + heads_per_chain)], axis=0)
               * (QK_SCALE * LOG2E)).astype(BF16)
        b0, b1 = key_blocks(n)
        keys = kb[b0 * tqb:(b1 + 1) * tqb, lo:lo + HEAD_DIM]
        if latent:
            keys = jnp.concatenate([keys, kcb[:, lo:lo + HEAD_DIM]], axis=0)
        st = lax.dot_general(keys, q_g, NT_DIMS, preferred_element_type=F32)
        if latent:
            parts = []
            for blk in range(b0, b1 + 1):
                part = st[(blk - b0) * w:(blk - b0 + 1) * w]
                if blk == n - 1:
                    part = part + bias_prev
                elif blk == n + 1:
                    part = part + bias_next
                parts.append(part)
            parts.append(st[(b1 - b0 + 1) * w:])
            st = jnp.concatenate(parts, axis=0)
        return st

    def weights(h0, st):
        m = jnp.maximum(st.max(axis=0, keepdims=True), sink_row(h0))
        return jnp.exp2(st - m).astype(BF16), m

    def finish(kv, h0, n, et_m):
        et, m = et_m
        lo = kv * HEAD_DIM
        b0, b1 = key_blocks(n)
        vals = [vt[lo:lo + HEAD_DIM, b0 * tqb:(b1 + 1) * tqb]]
        if latent:
            vals.append(vct[lo:lo + HEAD_DIM])
        vals = jnp.concatenate(vals, axis=1)
        vals = jnp.concatenate([vals, jnp.ones((ONES_ROWS, vals.shape[1]), F32)], axis=0).astype(BF16)
        ot = _dot(vals, et)
        d = ot[HEAD_DIM:HEAD_DIM + 1] + jnp.exp2(sink_row(h0) - m)
        on = ot[0:HEAD_DIM] * (1.0 / d)
        for p in range(heads_per_chain // 2):
            pair = jnp.concatenate([on[:, (2 * p) * tqb:(2 * p + 1) * tqb],
                                    on[:, (2 * p + 1) * tqb:(2 * p + 2) * tqb]], axis=0)
            c0 = ((h0 - head0) // 2 + p) * LANE
            o_ref[n * tqb:(n + 1) * tqb, c0:c0 + LANE] = pair.T

    return [(functools.partial(scores, kv, h0, n), functools.partial(weights, h0),
             functools.partial(finish, kv, h0, n)) for kv, h0, n in chains]


def _log_sigmoid(x):
    return jnp.minimum(x, 0.0) - jnp.log1p(jnp.exp(-jnp.abs(x)))


def _ret_fill_decay(d_scr, lg, t, latent):
    bw = min(t, LANE)
    nb = t // bw
    off = (lax.broadcasted_iota(jnp.int32, (bw, bw), 1)
           - lax.broadcasted_iota(jnp.int32, (bw, bw), 0)).astype(F32)
    for hh in range(d_scr.shape[0]):
        lgf, lgb = lg[hh // 2][0:1, hh % 2:hh % 2 + 1], lg[hh // 2][1:2, hh % 2:hh % 2 + 1]
        for o in range(-(nb - 1), nb):
            diff = off + float(o * bw)
            blk = (jnp.where(diff >= 0, jnp.exp(jnp.maximum(diff, 0.0) * lgf), 0.0)
                   + jnp.where(diff <= 0, jnp.exp(jnp.maximum(-diff, 0.0) * lgb), 0.0))
            for bs in range(max(0, -o), min(nb, nb - o)):
                d_scr[hh, bs * bw:(bs + 1) * bw, (bs + o) * bw:(bs + o + 1) * bw] = blk
        if latent:
            tp = lax.broadcasted_iota(jnp.int32, (RT_DK, t), 1).astype(F32)
            d_scr[hh, t:t + RT_DK, :] = jnp.exp((tp + 1.0) * lgf)
            d_scr[hh, t + RT_DK:t + 2 * RT_DK, :] = jnp.exp((float(t) - tp) * lgb)


def _ret_chains(q_ref, k_ref, v_refs, cg_refs, lg, g_ref, o_ref, d_scr, s0_ref=None, sfin_ref=None):
    latent = s0_ref is not None
    seqs, t = q_ref.shape[0], q_ref.shape[1]
    pairs = len(v_refs)
    tq = min(t, 512)
    if latent:
        eye = (lax.broadcasted_iota(jnp.int32, (RT_DK, RT_DK), 0)
               == lax.broadcasted_iota(jnp.int32, (RT_DK, RT_DK), 1)).astype(F32)
    heads = {}

    def head_operands(sq, hh):
        if (sq, hh) not in heads:
            p, j = hh // 2, hh % 2
            kf = k_ref[sq, :, hh * RT_DK:(hh + 1) * RT_DK] * (RT_DK ** -0.5)
            v = v_refs[p][sq, :, j * RT_DV:(j + 1) * RT_DV]
            keys, vals_t = [kf], [v.T]
            if latent:
                keys += [eye, eye]
                vals_t.append(jnp.concatenate([s0_ref[sq, 0, hh], s0_ref[sq, 1, hh]], axis=0).T)
            heads[(sq, hh)] = (jnp.concatenate(keys, axis=0).astype(BF16),
                               jnp.concatenate(vals_t, axis=1).astype(BF16))
            if not latent:
                sp = lax.broadcasted_iota(jnp.int32, (t, 1), 0).astype(F32)
                zf = jnp.exp((float(t) - 1.0 - sp) * lg[p][0:1, j:j + 1])
                zb = jnp.exp(sp * lg[p][1:2, j:j + 1])
                kz = jnp.concatenate([kf * zf, kf * zb], axis=1).astype(BF16)
                s_fb = _dot(heads[(sq, hh)][1], kz).T
                sfin_ref[sq, 0, hh] = s_fb[0:RT_DK]
                sfin_ref[sq, 1, hh] = s_fb[RT_DK:2 * RT_DK]
        return heads[(sq, hh)]

    chains = [(sq, hh, i) for sq in range(seqs) for hh in range(2 * pairs) for i in range(t // tq)]

    def scores(sq, hh, i):
        q = q_ref[sq, i * tq:(i + 1) * tq, hh * RT_DK:(hh + 1) * RT_DK].astype(BF16)
        return lax.dot_general(head_operands(sq, hh)[0], q, NT_DIMS, preferred_element_type=F32)

    def finish(sq, hh, i, at):
        yt = _dot(head_operands(sq, hh)[1], at)
        mu = jnp.mean(yt, axis=0, keepdims=True)
        yc = yt - mu
        var = jnp.mean(yc * yc, axis=0, keepdims=True)
        yn = (yc * lax.rsqrt(var + LN_EPS)).T * g_ref[...]
        cg = cg_refs[hh // 2][sq, i * tq:(i + 1) * tq, (hh % 2) * RT_DV:(hh % 2 + 1) * RT_DV]
        o_ref[sq, i * tq:(i + 1) * tq, hh * RT_DV:(hh + 1) * RT_DV] = yn * (cg * jax.nn.sigmoid(cg))

    def weights(hh, i, st):
        return (st * d_scr[hh, :, i * tq:(i + 1) * tq]).astype(BF16)

    return [(functools.partial(scores, sq, hh, i), functools.partial(weights, hh, i),
             functools.partial(finish, sq, hh, i)) for sq, hh, i in chains]


def _lat_mix_kernel(aq_ref, ak_ref, av_ref, bq_ref, bk_ref, bv_ref, cq_ref, ck_ref, cv_ref, cg_ref,
                    akc_ref, avc_ref, bkc_ref, bvc_ref, s0_ref, lam_ref, dng_ref, sink_ref, dec_ref, rng_ref,
                    oa_ref, ob_ref, oc_ref, d_scr, *, lam_init, half, layer):
    t = aq_ref.shape[1]
    dng_ref, sink_ref, rng_ref = (_layer_row(r, layer) for r in (dng_ref, sink_ref, rng_ref))
    lg = [_log_sigmoid(dec_ref[0])]

    @pl.when(pl.program_id(0) == 0)
    def _():
        _ret_fill_decay(d_scr, lg, t, True)

    def cached_kv(hh):
        mine = pl.ds(2 * half + hh, akc_ref.shape[0] // DA_HEADS, stride=DA_HEADS)
        return akc_ref[mine, :], avc_ref[mine, :]

    chains = _diff_chains(aq_ref.at[0], ak_ref.at[0], av_ref.at[0], oa_ref.at[0], lam_ref, dng_ref,
                          cached_kv, lam_init=lam_init, tq=DIFF_Q_BLOCK)
    chains += _win_chains(bq_ref.at[0], bk_ref.at[0], bv_ref.at[0], ob_ref.at[0], sink_ref,
                          bkc_ref, bvc_ref, kv_heads=(half,))
    chains += _ret_chains(cq_ref, ck_ref, [cv_ref], [cg_ref], lg, rng_ref, oc_ref, d_scr, s0_ref=s0_ref)
    _run_chains(chains)


def _lat_mix_call(z3, half, caches, state6, layer, diff_lam, diff_norm_g, sink, dec4, ret_norm_g, lam_init):
    z3, b0, b = z3
    t = z3.shape[1]
    ck_a, cv_a, ck_b, cv_b = caches
    zcol = lambda blk0, nblk: pl.BlockSpec((1, t, nblk * LANE), lambda bi: (bi + b0, 0, blk0 // nblk))
    layer_spec = lambda *s: pl.BlockSpec((None,) + s, lambda bi: (layer,) + (0,) * len(s))
    cache_spec = lambda a: pl.BlockSpec((None, None) + a.shape[2:], lambda bi: (bi, layer, 0, 0))
    o_spec = pl.BlockSpec((1, t, BRANCH_W // 2), lambda bi: (bi, 0, 0))
    o_shape = jax.ShapeDtypeStruct((b, t, BRANCH_W // 2), F32)
    return pl.pallas_call(
        functools.partial(_lat_mix_kernel, lam_init=lam_init, half=half, layer=layer),
        grid=(b,),
        in_specs=[
            zcol(COL_AQ + 2 * half, 2), zcol(COL_AK + 2 * half, 2), zcol(COL_AV + 2 * half, 2),
            zcol(COL_BQ + 2 * half, 2), zcol(COL_BK, 1), zcol(COL_BV, 1),
            zcol(COL_CQ + half, 1), zcol(COL_CK + half, 1), zcol(COL_CV + 2 * half, 2), zcol(COL_CG + 2 * half, 2),
            cache_spec(ck_a), cache_spec(cv_a), cache_spec(ck_b), cache_spec(cv_b),
            pl.BlockSpec((1, None, 2, 2, RT_DK, RT_DV), lambda bi: (bi, layer, 0, half, 0, 0)),
            layer_spec(4, HEAD_DIM), _all_layers(diff_norm_g), _all_layers(sink),
            pl.BlockSpec((None, 1, 2, 2), lambda bi: (layer, half, 0, 0)), _all_layers(ret_norm_g),
        ],
        out_specs=[o_spec, o_spec, o_spec],
        out_shape=[o_shape, o_shape, o_shape],
        scratch_shapes=[pltpu.VMEM((2, t + 2 * RT_DK, t), F32)],
        compiler_params=_params(),
        name="mix_lat",
    )(*([z3] * 10), ck_a, cv_a, ck_b, cv_b, state6, diff_lam, diff_norm_g, sink, dec4, ret_norm_g)


def _ctx_mix_kernel(z_ref, lam_ref, dng_ref, sink_ref, dec_ref, rng_ref, *refs, lam_init, slot, layer):
    oa_ref, ob_ref, oc_ref, sfin_ref, d_scr = refs[-5:]
    sfin_ref = _zero_other_layers(sfin_ref, slot)
    dng_ref, sink_ref, rng_ref = (_layer_row(r, layer) for r in (dng_ref, sink_ref, rng_ref))
    seqs, t = z_ref.shape[0], z_ref.shape[1]
    cols = lambda c0, c1: slice(c0 * LANE, c1 * LANE)
    lg = [_log_sigmoid(dec_ref[p]) for p in range(RT_HEADS // 2)]

    @pl.when(pl.program_id(0) == 0)
    def _():
        _ret_fill_decay(d_scr, lg, t, False)

    chains = []
    for sq in range(seqs):
        chains += _diff_chains(z_ref.at[sq, :, cols(COL_AQ, COL_AK)], z_ref.at[sq, :, cols(COL_AK, COL_AV)],
                               z_ref.at[sq, :, cols(COL_AV, COL_BQ)], oa_ref.at[sq], lam_ref, dng_ref,
                               lambda _: None, lam_init=lam_init, tq=t)
        chains += _win_chains(z_ref.at[sq, :, cols(COL_BQ, COL_BK)], z_ref.at[sq, :, cols(COL_BK, COL_BV)],
                              z_ref.at[sq, :, cols(COL_BV, COL_CQ)], ob_ref.at[sq], sink_ref)
    chains += _ret_chains(z_ref.at[:, :, cols(COL_CQ, COL_CK)], z_ref.at[:, :, cols(COL_CK, COL_CV)],
                          [z_ref.at[:, :, cols(COL_CV + 2 * p, COL_CV + 2 * p + 2)] for p in range(2)],
                          [z_ref.at[:, :, cols(COL_CG + 2 * p, COL_CG + 2 * p + 2)] for p in range(2)],
                          lg, rng_ref, oc_ref, d_scr, sfin_ref=sfin_ref)
    _run_chains(chains)


def _ctx_mix_call(z3, new_state, layer, diff_lam, diff_norm_g, sink, dec4, ret_norm_g, lam_init, seqs=2):
    z3, b0, b = z3
    t = z3.shape[1]
    layer_spec = lambda *s: pl.BlockSpec((None,) + s, lambda bi: (layer,) + (0,) * len(s))
    in_specs = [
        pl.BlockSpec((seqs, t, D_IN), lambda bi: (bi + b0 // seqs, 0, 0)),
        layer_spec(4, HEAD_DIM), _all_layers(diff_norm_g), _all_layers(sink),
        layer_spec(RT_HEADS // 2, 2, 2), _all_layers(ret_norm_g),
    ]
    args = [z3, diff_lam, diff_norm_g, sink, dec4, ret_norm_g]
    n_slots, slot, first = (DEPTH, layer, 0) if new_state is None else (1, 0, layer)
    aliases = {}
    if new_state is not None:
        aliases = {len(in_specs): 3}
        in_specs.append(pl.BlockSpec(memory_space=pl.ANY))
        args.append(new_state)
    o_spec = pl.BlockSpec((seqs, t, BRANCH_W), lambda bi: (bi, 0, 0))
    o_shape = jax.ShapeDtypeStruct((b, t, BRANCH_W), F32)
    return pl.pallas_call(
        functools.partial(_ctx_mix_kernel, lam_init=lam_init, slot=slot, layer=layer),
        grid=(b // seqs,),
        in_specs=in_specs,
        out_specs=[o_spec, o_spec, o_spec,
                   pl.BlockSpec((seqs, n_slots, 2, RT_HEADS, RT_DK, RT_DV), lambda bi: (bi, first, 0, 0, 0, 0))],
        out_shape=[o_shape, o_shape, o_shape,
                   jax.ShapeDtypeStruct((b, DEPTH, 2, RT_HEADS, RT_DK, RT_DV), F32)],
        input_output_aliases=aliases,
        scratch_shapes=[pltpu.VMEM((RT_HEADS, t, t), F32)],
        compiler_params=_params(),
        name="mix_ctx",
    )(*args)


def _layer_norm(x, g, b):
    mu = jnp.mean(x, axis=-1, keepdims=True)
    xc = x - mu
    var = jnp.mean(xc * xc, axis=-1, keepdims=True)
    return xc * lax.rsqrt(var + LN_EPS) * g + b


def _tail_kernel(xc_ref, xl_ref, mod_ref, oac_ref, obc_ref, occ_ref,
                 oal0_ref, oal1_ref, obl0_ref, obl1_ref, ocl0_ref, ocl1_ref,
                 bg_ref, ln1g_ref, ln1b_ref, ln2g_ref, ln2b_ref,
                 wg_hbm, wpa_hbm, wpb_hbm, wpc_hbm, wo_hbm, w1_hbm, w2_hbm, yc_ref, yl_ref,
                 wg_ref, wpa_ref, wpb_ref, wpc_ref, wo_ref, w1_ref, w2_ref, stage_ref, sem,
                 *, n_ctx_tiles, layer):
    d = D_MODEL
    bg_ref, ln1g_ref, ln1b_ref, ln2g_ref, ln2b_ref = (
        _layer_row(r, layer) for r in (bg_ref, ln1g_ref, ln1b_ref, ln2g_ref, ln2b_ref))
    pieces = [(wg_hbm.at[layer, :, i * d:(i + 1) * d], wg_ref.at[:, i * d:(i + 1) * d]) for i in range(3)]
    pieces += [(w.at[layer], r) for w, r in ((wpa_hbm, wpa_ref), (wpb_hbm, wpb_ref), (wpc_hbm, wpc_ref),
                                             (wo_hbm, wo_ref))]
    pieces += [(w1_hbm.at[layer, :, i * d:(i + 1) * d], w1_ref.at[:, i * d:(i + 1) * d]) for i in range(4)]
    pieces += [(w2_hbm.at[layer, i * d:(i + 1) * d, :], w2_ref.at[i * d:(i + 1) * d, :]) for i in range(4)]
    _stage_weights_bf16(pl.program_id(0) == 0, pieces, stage_ref, sem)

    is_ctx = pl.program_id(0) < n_ctx_tiles
    x = jnp.where(is_ctx, xc_ref[...], xl_ref[...])
    sh1, sc1, g1 = mod_ref[:, 0:d], mod_ref[:, d:2 * d], mod_ref[:, 2 * d:3 * d]
    sh2, sc2, g2 = mod_ref[:, 3 * d:4 * d], mod_ref[:, 4 * d:5 * d], mod_ref[:, 5 * d:6 * d]
    h1 = (x * (1.0 + sc1) + sh1).astype(BF16)
    merged = None
    branches = ((oac_ref, (oal0_ref, oal1_ref), wpa_ref), (obc_ref, (obl0_ref, obl1_ref), wpb_ref),
                (occ_ref, (ocl0_ref, ocl1_ref), wpc_ref))
    for i, (oc_ref_i, ol_refs_i, wp_ref) in enumerate(branches):
        o_lat = jnp.concatenate([r[...] for r in ol_refs_i], axis=1)
        o = jnp.where(is_ctx, oc_ref_i[...], o_lat).astype(BF16)
        gate = jax.nn.sigmoid(_dot(h1, wg_ref[:, i * d:(i + 1) * d]) + bg_ref[:, i * d:(i + 1) * d])
        part = gate * _dot(o, wp_ref[...])
        merged = part if merged is None else merged + part
    y = _dot(merged.astype(BF16), wo_ref[...])
    x1 = _layer_norm(ALPHA * x + g1 * y, ln1g_ref[...], ln1b_ref[...])
    h2 = (x1 * (1.0 + sc2) + sh2).astype(BF16)
    u = jnp.maximum(_dot(h2, w1_ref[...]), 0.0)
    f = _dot((u * u).astype(BF16), w2_ref[...])
    x2 = _layer_norm(ALPHA * x1 + g2 * f, ln2g_ref[...], ln2b_ref[...])

    @pl.when(is_ctx)
    def _():
        yc_ref[...] = x2

    @pl.when(jnp.logical_not(is_ctx))
    def _():
        yl_ref[...] = x2


def _tail_call(xc, xl, mod4, layer, o_ctx, o_lat, wg, bg, wpa, wpb, wpc, wo, ln1, w1, w2, ln2,
               t_lat, tm=TAIL_ROWS):
    n_ctx, n_lat = xc.shape[0] // tm, xl.shape[0] // tm
    lat_tiles_per_seq = t_lat // tm

    def mod_idx(i):
        row = jnp.where(i < n_ctx, 0, 1 + (i - n_ctx) // lat_tiles_per_seq)
        return (layer, row, 0, 0)

    ctx_spec = lambda w: pl.BlockSpec((tm, w), lambda i: (jnp.minimum(i, n_ctx - 1), 0))
    lat_spec = lambda w: pl.BlockSpec((tm, w), lambda i: (jnp.maximum(i - n_ctx, 0), 0))
    in_hbm = pl.BlockSpec(memory_space=pl.ANY)
    return pl.pallas_call(
        functools.partial(_tail_kernel, n_ctx_tiles=n_ctx, layer=layer),
        grid=(n_ctx + n_lat,),
        in_specs=[
            ctx_spec(D_MODEL), lat_spec(D_MODEL),
            pl.BlockSpec((None, None, 1, D_MOD), mod_idx),
            ctx_spec(BRANCH_W), ctx_spec(BRANCH_W), ctx_spec(BRANCH_W),
            *[lat_spec(BRANCH_W // 2)] * 6,
            *[_all_layers(p) for p in (bg, *ln1, *ln2)],
        ] + [in_hbm] * 7,
        out_specs=[ctx_spec(D_MODEL), lat_spec(D_MODEL)],
        out_shape=[jax.ShapeDtypeStruct(xc.shape, F32), jax.ShapeDtypeStruct(xl.shape, F32)],
        scratch_shapes=[
            pltpu.VMEM((D_MODEL, D_GATE), BF16),
            pltpu.VMEM((BRANCH_W, D_MODEL), BF16), pltpu.VMEM((BRANCH_W, D_MODEL), BF16),
            pltpu.VMEM((BRANCH_W, D_MODEL), BF16),
            pltpu.VMEM((D_MODEL, D_MODEL), BF16),
            pltpu.VMEM((D_MODEL, D_FF), BF16), pltpu.VMEM((D_FF, D_MODEL), BF16),
            pltpu.VMEM((2, D_MODEL, D_MODEL), F32),
            pltpu.SemaphoreType.DMA((2,)),
        ],
        compiler_params=_params(),
        name="tail",
    )(xc, xl, mod4, *o_ctx, *o_lat, bg, *ln1, *ln2, wg, wpa, wpb, wpc, wo, w1, w2)


def _rope_tables(n_tokens, identity_rows):
    rows = n_tokens // GRID_W
    r, col = jnp.meshgrid(jnp.arange(rows), jnp.arange(GRID_W), indexing="ij")
    r = r.reshape(-1).astype(F32)
    col = col.reshape(-1).astype(F32)
    nf = HEAD_DIM // 4
    inv = ROPE_BASE ** (-jnp.arange(nf, dtype=F32) / nf)
    ang_r = r[:, None] * inv[None, :]
    ang_c = col[:, None] * inv[None, :]
    zero = jnp.zeros_like(ang_r)
    cos = jnp.concatenate([jnp.cos(ang_r)] * 2 + [jnp.cos(ang_c)] * 2, axis=-1)
    s_next = jnp.concatenate([-jnp.sin(ang_r), zero, -jnp.sin(ang_c), zero], axis=-1)
    s_prev = jnp.concatenate([zero, jnp.sin(ang_r), zero, jnp.sin(ang_c)], axis=-1)
    ident = (jnp.ones, jnp.zeros, jnp.zeros)
    return tuple(jnp.concatenate([fill((identity_rows, LANE), F32), jnp.tile(t, (1, LANE // HEAD_DIM))], axis=0)
                 for fill, t in zip(ident, (cos, s_next, s_prev)))


def kernel(x_prompt, x_sample, c, cache_diff_k, cache_diff_v, cache_win_k, cache_win_v, state_ret,
           c_ctx, w_mod, b_mod, w_in, diff_lam, diff_norm_g, win_sink, ret_decay, ret_norm_g,
           w_pa, w_pb, w_pc, w_gate, b_gate, w_o, ln1_g, ln1_b, w_ff1, w_ff2, ln2_g, ln2_b):
    bp, tp, d = x_prompt.shape
    bs, ts, _ = x_sample.shape
    assert d == D_MODEL and w_in.shape == (DEPTH, D_MODEL, D_IN) and c.shape[0] + 1 <= MOD_ROWS
    past = cache_diff_k.shape[2]

    c_rows = jnp.concatenate(
        [c_ctx[None, :], c, jnp.zeros((MOD_ROWS - 1 - bs, d), F32)], axis=0)
    mod4 = _mod_call(c_rows, w_mod, b_mod).reshape(DEPTH, MOD_ROWS, 1, D_MOD)
    rope_tabs = _rope_tables(ts, PROJ_ROWS)

    ck_a = cache_diff_k.reshape(bs, DEPTH, past * DA_HEADS, 2 * HEAD_DIM)
    cv_a = cache_diff_v.reshape(bs, DEPTH, past * DA_HEADS, 2 * HEAD_DIM)
    ck_b = cache_win_k.transpose(0, 1, 3, 4, 2).reshape(bs, DEPTH, WG_KV_HEADS * HEAD_DIM, past)
    cv_b = cache_win_v.transpose(0, 1, 3, 4, 2).reshape(bs, DEPTH, WG_KV_HEADS * HEAD_DIM, past)

    ln1, ln2 = (ln1_g, ln1_b), (ln2_g, ln2_b)
    dec4 = ret_decay.reshape(DEPTH, 2, RT_HEADS // 2, 2).transpose(0, 2, 1, 3)

    xp = x_prompt.reshape(bp * tp, d)
    xs = x_sample.reshape(bs * ts, d)
    new_caches, new_state = [], None
    for l in range(DEPTH):
        lam_init = 0.8 - 0.6 * math.exp(-0.3 * l)
        z_all, *new_caches = _proj_call(xs, xp, mod4, l, w_in, rope_tabs, new_caches, ts, tp)
        z_ctx = (z_all.reshape(-1, tp, D_IN), bs * ts // tp, bp)
        z_lat = (z_all.reshape(-1, ts, D_IN), 0, bs)
        *o_ctx, new_state = _ctx_mix_call(z_ctx, new_state, l, diff_lam, diff_norm_g, win_sink,
                                          dec4, ret_norm_g, lam_init)
        halves = [_lat_mix_call(z_lat, half, (ck_a, cv_a, ck_b, cv_b), state_ret, l, diff_lam,
                                diff_norm_g, win_sink, dec4, ret_norm_g, lam_init)
                  for half in range(2)]
        o_lat = [o.reshape(bs * ts, BRANCH_W // 2) for pair in zip(*halves) for o in pair]
        xp, xs = _tail_call(xp, xs, mod4, l, [o.reshape(bp * tp, BRANCH_W) for o in o_ctx],
                            o_lat, w_gate, b_gate,
                            w_pa, w_pb, w_pc, w_o, ln1, w_ff1, w_ff2, ln2, ts)
    dk, dv, wk, wv = new_caches
    new_diff = [a.reshape(bp, DEPTH, tp, DA_HEADS, 2 * HEAD_DIM) for a in (dk, dv)]
    new_win = [a.reshape(bp, DEPTH, WG_KV_HEADS, HEAD_DIM, tp).transpose(0, 1, 4, 2, 3) for a in (wk, wv)]
    return (xp.reshape(bp, tp, d), xs.reshape(bs, ts, d), *new_diff, *new_win, new_state)
```

```python
import functools
import math

import jax
import jax.numpy as jnp
from jax import lax
from jax.experimental import pallas as pl
from jax.experimental.pallas import tpu as pltpu

F32 = jnp.float32
BF16 = jnp.bfloat16

D_MODEL = 1024
DEPTH = 2
GRID_W = 64
HEAD_DIM = 64
DA_HEADS = 4
WG_Q_HEADS = 8
WG_KV_HEADS = 2
WG_GROUP = WG_Q_HEADS // WG_KV_HEADS
WINDOW = 128
RT_HEADS = 4
RT_DK = 64
RT_DV = 128
BRANCH_W = 512
D_IN = 3840
D_GATE = 3 * D_MODEL
D_FF = 4 * D_MODEL
D_MOD = 6 * D_MODEL
ROPE_BASE = 10000.0
LN_EPS = 1e-5
ALPHA = (2 * DEPTH) ** 0.25
QK_SCALE = HEAD_DIM ** -0.5
LOG2E = math.log2(math.e)

LANE = 128
MOD_ROWS = 8
ONES_ROWS = 16
PROJ_COLS = 768
PROJ_ROWS = 512
TAIL_ROWS = 256
MOD_COLS = 3072
DIFF_Q_BLOCK = 128
RET_Q_BLOCK_LAT, RET_Q_BLOCK_CTX = 256, 128

COL_AQ, COL_AK, COL_AV = 0, 4, 8
COL_BQ, COL_BK, COL_BV = 12, 16, 17
COL_CQ, COL_CK, COL_CV, COL_CG = 18, 20, 22, 26
ROPE_BLOCKS = tuple(range(0, 8)) + tuple(range(12, 17))

VMEM_LIMIT = 56 * 1024 * 1024
NT_DIMS = (((1,), (1,)), ((), ()))
TN_DIMS = (((0,), (0,)), ((), ()))


def _params():
    return pltpu.CompilerParams(vmem_limit_bytes=VMEM_LIMIT)


def _resident(shape, index_map):
    return pl.BlockSpec(shape, index_map, pipeline_mode=pl.Buffered(1))


def _dot(a, b):
    return jnp.dot(a, b, preferred_element_type=F32)


def _all_layers(param):
    return _resident(param.shape, lambda *_: (0, 0))


def _layer_row(ref, layer):
    return ref.at[layer:layer + 1, :]


def _zero_other_layers(ref, slot):
    for other in range(ref.shape[1]):
        if other != slot:
            ref[:, other] = jnp.zeros((ref.shape[0],) + ref.shape[2:], F32)
    return ref.at[:, slot]


def _stage_weights_bf16(first_step, pieces, stage_ref, sem):
    def copy(c):
        src = pieces[c][0]
        return pltpu.make_async_copy(src, stage_ref.at[c % 2, 0:src.shape[0], 0:src.shape[1]], sem.at[c % 2])

    @pl.when(first_step)
    def _():
        for c in range(min(2, len(pieces))):
            copy(c).start()
        for c, (src, dst) in enumerate(pieces):
            copy(c).wait()
            dst[...] = stage_ref[c % 2, 0:src.shape[0], 0:src.shape[1]].astype(BF16)
            if c + 2 < len(pieces):
                copy(c + 2).start()


def _mod_kernel(c_ref, w_ref, b_ref, o_ref):
    c = c_ref[...]
    a = (c * jax.nn.sigmoid(c)).astype(BF16)
    o_ref[...] = _dot(a, w_ref[...].astype(BF16)) + b_ref[...]


def _mod_call(c_rows, w_mod, b_mod):
    tn = MOD_COLS
    return pl.pallas_call(
        _mod_kernel,
        grid=(DEPTH, D_MOD // tn),
        in_specs=[
            pl.BlockSpec((MOD_ROWS, D_MODEL), lambda l, n: (0, 0)),
            pl.BlockSpec((None, D_MODEL, tn), lambda l, n: (l, 0, n)),
            pl.BlockSpec((None, 1, tn), lambda l, n: (l, 0, n)),
        ],
        out_specs=pl.BlockSpec((None, MOD_ROWS, tn), lambda l, n: (l, 0, n)),
        out_shape=jax.ShapeDtypeStruct((DEPTH, MOD_ROWS, D_MOD), F32),
        compiler_params=_params(),
        name="mod_vectors",
    )(c_rows, w_mod, b_mod.reshape(DEPTH, 1, D_MOD))


def _proj_kernel(xl_ref, xc_ref, mod_ref, win_hbm, cos_ref, sa_ref, sb_ref, *refs, n_lat_tiles, slot, layer):
    z_ref, dk_ref, dv_ref, wk_ref, wv_ref, win_ref, stage_ref, sem = refs[-8:]
    nc = stage_ref.shape[2]
    _stage_weights_bf16(pl.program_id(0) == 0,
                        [(win_hbm.at[layer, :, c0:c0 + nc], win_ref.at[:, c0:c0 + nc]) for c0 in range(0, D_IN, nc)],
                        stage_ref, sem)
    seqs, t_ctx = dk_ref.shape[0], wk_ref.shape[-1]
    dk_ref, dv_ref, wk_ref, wv_ref = (_zero_other_layers(r, slot) for r in (dk_ref, dv_ref, wk_ref, wv_ref))
    is_lat = pl.program_id(0) < n_lat_tiles
    x = jnp.where(is_lat, xl_ref[...], xc_ref[...])
    sh1 = mod_ref[:, 0:D_MODEL]
    sc1 = mod_ref[:, D_MODEL:2 * D_MODEL]
    h = (x * (1.0 + sc1) + sh1).astype(BF16)
    for c0 in range(0, D_IN, nc):
        z = _dot(h, win_ref[:, c0:c0 + nc])
        for j in range(nc // LANE):
            blk = c0 // LANE + j
            u = z[:, j * LANE:(j + 1) * LANE]
            if blk in ROPE_BLOCKS:
                u = (u * cos_ref[...] + pltpu.roll(u, LANE - 16, 1) * sa_ref[...]
                     + pltpu.roll(u, 16, 1) * sb_ref[...])
            z_ref[:, blk * LANE:(blk + 1) * LANE] = u
            if COL_AK <= blk < COL_BQ:
                ref, head = (dk_ref, blk - COL_AK) if blk < COL_AV else (dv_ref, blk - COL_AV)
                for s in range(seqs):
                    ref[s, pl.ds(head, t_ctx, stride=DA_HEADS), :] = u[s * t_ctx:(s + 1) * t_ctx]
            elif blk in (COL_BK, COL_BV):
                ref = wk_ref if blk == COL_BK else wv_ref
                ut = u.T
                for s in range(seqs):
                    ref[s] = ut[:, s * t_ctx:(s + 1) * t_ctx]


def _proj_call(xl, xc, mod4, layer, w_in, rope_tabs, caches, t_lat, t_ctx, tm=PROJ_ROWS):
    n_lat, n_ctx = xl.shape[0] // tm, xc.shape[0] // tm
    lat_tiles_per_seq = t_lat // tm
    seqs = tm // t_ctx

    def mod_idx(i):
        row = jnp.where(i < n_lat, 1 + i // lat_tiles_per_seq, 0)
        return (layer, row, 0, 0)

    lat_idx = lambda i: jnp.minimum(i, n_lat - 1)
    ctx_idx = lambda i: jnp.maximum(i - n_lat, 0)
    rope_spec = pl.BlockSpec((tm, LANE), lambda i: (jnp.where(i < n_lat, 1 + i % lat_tiles_per_seq, 0), 0))
    n_slots, slot, first = (1, 0, layer) if caches else (DEPTH, layer, 0)
    cache_a = pl.BlockSpec((seqs, n_slots, DA_HEADS * t_ctx, 2 * HEAD_DIM), lambda i: (ctx_idx(i), first, 0, 0))
    cache_b = pl.BlockSpec((seqs, n_slots, WG_KV_HEADS * HEAD_DIM, t_ctx), lambda i: (ctx_idx(i), first, 0, 0))
    in_specs = [
        pl.BlockSpec((tm, D_MODEL), lambda i: (lat_idx(i), 0)),
        pl.BlockSpec((tm, D_MODEL), lambda i: (ctx_idx(i), 0)),
        pl.BlockSpec((None, None, 1, D_MOD), mod_idx),
        pl.BlockSpec(memory_space=pl.ANY),
        rope_spec, rope_spec, rope_spec,
    ]
    n_in = len(in_specs)
    in_specs += [pl.BlockSpec(memory_space=pl.ANY)] * len(caches)
    b_ctx = xc.shape[0] // t_ctx
    cache_shapes = [(b_ctx, DEPTH, DA_HEADS * t_ctx, 2 * HEAD_DIM)] * 2 \
        + [(b_ctx, DEPTH, WG_KV_HEADS * HEAD_DIM, t_ctx)] * 2
    return pl.pallas_call(
        functools.partial(_proj_kernel, n_lat_tiles=n_lat, slot=slot, layer=layer),
        grid=(n_lat + n_ctx,),
        in_specs=in_specs,
        out_specs=[pl.BlockSpec((tm, D_IN), lambda i: (i, 0)), cache_a, cache_a, cache_b, cache_b],
        out_shape=[jax.ShapeDtypeStruct((xl.shape[0] + xc.shape[0], D_IN), F32)]
        + [jax.ShapeDtypeStruct(s, F32) for s in cache_shapes],
        input_output_aliases={n_in + k: 1 + k for k in range(len(caches))},
        scratch_shapes=[
            pltpu.VMEM((D_MODEL, D_IN), BF16),
            pltpu.VMEM((2, D_MODEL, PROJ_COLS), F32),
            pltpu.SemaphoreType.DMA((2,)),
        ],
        compiler_params=_params(),
        name="proj",
    )(xl, xc, mod4, w_in, *rope_tabs, *caches)


def _values_t(v_parts):
    vt = jnp.concatenate([v.T for v in v_parts], axis=1)
    return jnp.concatenate([vt, jnp.ones((ONES_ROWS, vt.shape[1]), F32)], axis=0).astype(BF16)


def _run_chains(chains):
    a, b = {}, {}
    n = len(chains)
    for c in range(n + 2):
        if c < n:
            a[c] = chains[c][0]()
        if 1 <= c <= n:
            b[c - 1] = chains[c - 1][1](a.pop(c - 1))
        if c >= 2:
            chains[c - 2][2](b.pop(c - 2))


def _diff_chains(q_ref, k_ref, v_ref, o_ref, lam_ref, g_ref, cached_kv, *, lam_init, tq):
    t, heads = q_ref.shape[0], q_ref.shape[1] // LANE
    lv = lam_ref[...]
    lam = (jnp.exp(jnp.sum(lv[0:1] * lv[1:2], axis=-1, keepdims=True))
           - jnp.exp(jnp.sum(lv[2:3] * lv[3:4], axis=-1, keepdims=True)) + lam_init)
    lane = lax.broadcasted_iota(jnp.int32, (tq, LANE), 1)
    kv = {}

    def scores(hh, i):
        cols = slice(hh * LANE, (hh + 1) * LANE)
        if hh not in kv:
            k_parts, v_parts = [k_ref[:, cols]], [v_ref[:, cols]]
            extra = cached_kv(hh)
            if extra is not None:
                k_parts.append(extra[0])
                v_parts.append(extra[1])
            kv[hh] = (jnp.concatenate(k_parts, axis=0).astype(BF16), _values_t(v_parts))
        q = q_ref[i * tq:(i + 1) * tq, cols] * (QK_SCALE * LOG2E)
        qs = jnp.concatenate([jnp.where(lane < HEAD_DIM, q, 0.0),
                              jnp.where(lane >= HEAD_DIM, q, 0.0)], axis=0).astype(BF16)
        return lax.dot_general(kv[hh][0], qs, NT_DIMS, preferred_element_type=F32)

    def weights(s):
        return jnp.exp2(s - s.max(axis=0, keepdims=True)).astype(BF16)

    def finish(hh, i, et):
        ot = _dot(kv[hh][1], et)
        o = ot[0:LANE] * (1.0 / ot[LANE:LANE + 1])
        od = o[:, 0:tq] - lam * o[:, tq:2 * tq]
        yt = od * lax.rsqrt(jnp.mean(od * od, axis=0, keepdims=True) + LN_EPS)
        o_ref[i * tq:(i + 1) * tq, hh * LANE:(hh + 1) * LANE] = yt.T * g_ref[...] * (1.0 - lam_init)

    return [(functools.partial(scores, hh, i), weights, functools.partial(finish, hh, i))
            for hh in range(heads) for i in range(t // tq)]


def _win_chains(q_ref, k_ref, v_ref, o_ref, sink_ref, kc_ref=None, vc_ref=None, kv_heads=range(WG_KV_HEADS)):
    latent = kc_ref is not None
    head0 = kv_heads[0] * WG_GROUP
    t = q_ref.shape[0]
    w = WINDOW
    if latent:
        tqb, heads_per_chain = w, WG_GROUP
    else:
        tqb, heads_per_chain = t, 2
    nb = t // tqb
    kb = k_ref[...].astype(BF16)
    vt = v_ref[...].T
    if latent:
        kcb = kc_ref[...].T.astype(BF16)
        vct = vc_ref[...]
        jj = lax.broadcasted_iota(jnp.int32, (w, w), 0)
        ii = lax.broadcasted_iota(jnp.int32, (w, w), 1)
        bias_prev = jnp.concatenate([jnp.where(jj >= ii, 0.0, -1e30)] * heads_per_chain, axis=1)
        bias_next = jnp.concatenate([jnp.where(jj <= ii, 0.0, -1e30)] * heads_per_chain, axis=1)
    chains = [(kv, h0, n)
              for n in range(nb)
              for kv in kv_heads
              for h0 in range(kv * WG_GROUP, (kv + 1) * WG_GROUP, heads_per_chain)]
    sinks = {}

    def sink_row(h0):
        if h0 not in sinks:
            sinks[h0] = jnp.concatenate(
                [jnp.broadcast_to(sink_ref[:, h:h + 1] * LOG2E, (1, tqb))
                 for h in range(h0, h0 + heads_per_chain)], axis=1)
        return sinks[h0]

    def key_blocks(n):
        return (max(n - 1, 0), min(n + 1, nb - 1)) if latent else (0, 0)

    def scores(kv, h0, n):
        lo = kv * HEAD_DIM
        rows = slice(n * tqb, (n + 1) * tqb)
        q_g = (jnp.concatenate([q_ref[rows, (h - head0) * HEAD_DIM:(h - head0 + 1) * HEAD_DIM]
                                for h in range(h0, h0 + heads_per_chain)], axis=0)
               * (QK_SCALE * LOG2E)).astype(BF16)
        b0, b1 = key_blocks(n)
        keys = kb[b0 * tqb:(b1 + 1) * tqb, lo:lo + HEAD_DIM]
        if latent:
            keys = jnp.concatenate([keys, kcb[:, lo:lo + HEAD_DIM]], axis=0)
        st = lax.dot_general(keys, q_g, NT_DIMS, preferred_element_type=F32)
        if latent:
            parts = []
            for blk in range(b0, b1 + 1):
                part = st[(blk - b0) * w:(blk - b0 + 1) * w]
                if blk == n - 1:
                    part = part + bias_prev
                elif blk == n + 1:
                    part = part + bias_next
                parts.append(part)
            parts.append(st[(b1 - b0 + 1) * w:])
            st = jnp.concatenate(parts, axis=0)
        return st

    def weights(h0, st):
        m = jnp.maximum(st.max(axis=0, keepdims=True), sink_row(h0))
        return jnp.exp2(st - m).astype(BF16), m

    def finish(kv, h0, n, et_m):
        et, m = et_m
        lo = kv * HEAD_DIM
        b0, b1 = key_blocks(n)
        vals = [vt[lo:lo + HEAD_DIM, b0 * tqb:(b1 + 1) * tqb]]
        if latent:
            vals.append(vct[lo:lo + HEAD_DIM])
        vals = jnp.concatenate(vals, axis=1)
        vals = jnp.concatenate([vals, jnp.ones((ONES_ROWS, vals.shape[1]), F32)], axis=0).astype(BF16)
        ot = _dot(vals, et)
        d = ot[HEAD_DIM:HEAD_DIM + 1] + jnp.exp2(sink_row(h0) - m)
        on = ot[0:HEAD_DIM] * (1.0 / d)
        for p in range(heads_per_chain // 2):
            pair = jnp.concatenate([on[:, (2 * p) * tqb:(2 * p + 1) * tqb],
                                    on[:, (2 * p + 1) * tqb:(2 * p + 2) * tqb]], axis=0)
            c0 = ((h0 - head0) // 2 + p) * LANE
            o_ref[n * tqb:(n + 1) * tqb, c0:c0 + LANE] = pair.T

    return [(functools.partial(scores, kv, h0, n), functools.partial(weights, h0),
             functools.partial(finish, kv, h0, n)) for kv, h0, n in chains]


def _log_sigmoid(x):
    return jnp.minimum(x, 0.0) - jnp.log1p(jnp.exp(-jnp.abs(x)))


def _ret_fill_decay(d_scr, lg, t, latent):
    bw = min(t, LANE)
    nb = t // bw
    off = (lax.broadcasted_iota(jnp.int32, (bw, bw), 1)
           - lax.broadcasted_iota(jnp.int32, (bw, bw), 0)).astype(F32)
    for hh in range(d_scr.shape[0]):
        lgf, lgb = lg[hh // 2][0:1, hh % 2:hh % 2 + 1], lg[hh // 2][1:2, hh % 2:hh % 2 + 1]
        for o in range(-(nb - 1), nb):
            diff = off + float(o * bw)
            blk = (jnp.where(diff >= 0, jnp.exp(jnp.maximum(diff, 0.0) * lgf), 0.0)
                   + jnp.where(diff <= 0, jnp.exp(jnp.maximum(-diff, 0.0) * lgb), 0.0))
            for bs in range(max(0, -o), min(nb, nb - o)):
                d_scr[hh, bs * bw:(bs + 1) * bw, (bs + o) * bw:(bs + o + 1) * bw] = blk
        if latent:
            tp = lax.broadcasted_iota(jnp.int32, (RT_DK, t), 1).astype(F32)
            d_scr[hh, t:t + RT_DK, :] = jnp.exp((tp + 1.0) * lgf)
            d_scr[hh, t + RT_DK:t + 2 * RT_DK, :] = jnp.exp((float(t) - tp) * lgb)


def _ret_chains(q_ref, k_ref, v_refs, cg_refs, lg, g_ref, o_ref, d_scr, s0_ref=None, sfin_ref=None):
    latent = s0_ref is not None
    seqs, t = q_ref.shape[0], q_ref.shape[1]
    pairs = len(v_refs)
    tq = RET_Q_BLOCK_LAT if latent else RET_Q_BLOCK_CTX
    if latent:
        eye = (lax.broadcasted_iota(jnp.int32, (RT_DK, RT_DK), 0)
               == lax.broadcasted_iota(jnp.int32, (RT_DK, RT_DK), 1)).astype(F32)
    heads = {}

    def head_operands(sq, hh):
        if (sq, hh) not in heads:
            p, j = hh // 2, hh % 2
            kf = k_ref[sq, :, hh * RT_DK:(hh + 1) * RT_DK] * (RT_DK ** -0.5)
            v = v_refs[p][sq, :, j * RT_DV:(j + 1) * RT_DV]
            keys, vals_t = [kf], [v.T]
            if latent:
                keys += [eye, eye]
                vals_t.append(jnp.concatenate([s0_ref[sq, 0, hh], s0_ref[sq, 1, hh]], axis=0).T)
            heads[(sq, hh)] = (jnp.concatenate(keys, axis=0).astype(BF16),
                               jnp.concatenate(vals_t, axis=1).astype(BF16))
            if not latent:
                sp = lax.broadcasted_iota(jnp.int32, (t, 1), 0).astype(F32)
                zf = jnp.exp((float(t) - 1.0 - sp) * lg[p][0:1, j:j + 1])
                zb = jnp.exp(sp * lg[p][1:2, j:j + 1])
                kz = jnp.concatenate([kf * zf, kf * zb], axis=1).astype(BF16)
                s_fb = _dot(heads[(sq, hh)][1], kz).T
                sfin_ref[sq, 0, hh] = s_fb[0:RT_DK]
                sfin_ref[sq, 1, hh] = s_fb[RT_DK:2 * RT_DK]
        return heads[(sq, hh)]

    chains = [(sq, hh, i) for sq in range(seqs) for hh in range(2 * pairs) for i in range(t // tq)]

    def scores(sq, hh, i):
        q = q_ref[sq, i * tq:(i + 1) * tq, hh * RT_DK:(hh + 1) * RT_DK].astype(BF16)
        return lax.dot_general(head_operands(sq, hh)[0], q, NT_DIMS, preferred_element_type=F32)

    def finish(sq, hh, i, at):
        yt = _dot(head_operands(sq, hh)[1], at)
        mu = jnp.mean(yt, axis=0, keepdims=True)
        yc = yt - mu
        var = jnp.mean(yc * yc, axis=0, keepdims=True)
        yn = (yc * lax.rsqrt(var + LN_EPS)).T * g_ref[...]
        cg = cg_refs[hh // 2][sq, i * tq:(i + 1) * tq, (hh % 2) * RT_DV:(hh % 2 + 1) * RT_DV]
        o_ref[sq, i * tq:(i + 1) * tq, hh * RT_DV:(hh + 1) * RT_DV] = yn * (cg * jax.nn.sigmoid(cg))

    def weights(hh, i, st):
        return (st * d_scr[hh, :, i * tq:(i + 1) * tq]).astype(BF16)

    return [(functools.partial(scores, sq, hh, i), functools.partial(weights, hh, i),
             functools.partial(finish, sq, hh, i)) for sq, hh, i in chains]


def _lat_mix_kernel(aq_ref, ak_ref, av_ref, bq_ref, bk_ref, bv_ref, cq_ref, ck_ref, cv_ref, cg_ref,
                    akc_ref, avc_ref, bkc_ref, bvc_ref, s0_ref, lam_ref, dng_ref, sink_ref, dec_ref, rng_ref,
                    oa_ref, ob_ref, oc_ref, d_scr, *, lam_init, half, layer):
    t = aq_ref.shape[1]
    dng_ref, sink_ref, rng_ref = (_layer_row(r, layer) for r in (dng_ref, sink_ref, rng_ref))
    lg = [_log_sigmoid(dec_ref[0])]

    @pl.when(pl.program_id(0) == 0)
    def _():
        _ret_fill_decay(d_scr, lg, t, True)

    def cached_kv(hh):
        mine = pl.ds(2 * half + hh, akc_ref.shape[0] // DA_HEADS, stride=DA_HEADS)
        return akc_ref[mine, :], avc_ref[mine, :]

    chains = _diff_chains(aq_ref.at[0], ak_ref.at[0], av_ref.at[0], oa_ref.at[0], lam_ref, dng_ref,
                          cached_kv, lam_init=lam_init, tq=DIFF_Q_BLOCK)
    chains += _win_chains(bq_ref.at[0], bk_ref.at[0], bv_ref.at[0], ob_ref.at[0], sink_ref,
                          bkc_ref, bvc_ref, kv_heads=(half,))
    chains += _ret_chains(cq_ref, ck_ref, [cv_ref], [cg_ref], lg, rng_ref, oc_ref, d_scr, s0_ref=s0_ref)
    _run_chains(chains)


def _lat_mix_call(z3, half, caches, state6, layer, diff_lam, diff_norm_g, sink, dec4, ret_norm_g, lam_init):
    z3, b0, b = z3
    t = z3.shape[1]
    ck_a, cv_a, ck_b, cv_b = caches
    zcol = lambda blk0, nblk: pl.BlockSpec((1, t, nblk * LANE), lambda bi: (bi + b0, 0, blk0 // nblk))
    layer_spec = lambda *s: pl.BlockSpec((None,) + s, lambda bi: (layer,) + (0,) * len(s))
    cache_spec = lambda a: pl.BlockSpec((None, None) + a.shape[2:], lambda bi: (bi, layer, 0, 0))
    o_spec = pl.BlockSpec((1, t, BRANCH_W // 2), lambda bi: (bi, 0, 0))
    o_shape = jax.ShapeDtypeStruct((b, t, BRANCH_W // 2), F32)
    return pl.pallas_call(
        functools.partial(_lat_mix_kernel, lam_init=lam_init, half=half, layer=layer),
        grid=(b,),
        in_specs=[
            zcol(COL_AQ + 2 * half, 2), zcol(COL_AK + 2 * half, 2), zcol(COL_AV + 2 * half, 2),
            zcol(COL_BQ + 2 * half, 2), zcol(COL_BK, 1), zcol(COL_BV, 1),
            zcol(COL_CQ + half, 1), zcol(COL_CK + half, 1), zcol(COL_CV + 2 * half, 2), zcol(COL_CG + 2 * half, 2),
            cache_spec(ck_a), cache_spec(cv_a), cache_spec(ck_b), cache_spec(cv_b),
            pl.BlockSpec((1, None, 2, 2, RT_DK, RT_DV), lambda bi: (bi, layer, 0, half, 0, 0)),
            layer_spec(4, HEAD_DIM), _all_layers(diff_norm_g), _all_layers(sink),
            pl.BlockSpec((None, 1, 2, 2), lambda bi: (layer, half, 0, 0)), _all_layers(ret_norm_g),
        ],
        out_specs=[o_spec, o_spec, o_spec],
        out_shape=[o_shape, o_shape, o_shape],
        scratch_shapes=[pltpu.VMEM((2, t + 2 * RT_DK, t), F32)],
        compiler_params=_params(),
        name="mix_lat",
    )(*([z3] * 10), ck_a, cv_a, ck_b, cv_b, state6, diff_lam, diff_norm_g, sink, dec4, ret_norm_g)


def _ctx_mix_kernel(z_ref, lam_ref, dng_ref, sink_ref, dec_ref, rng_ref, *refs, lam_init, slot, layer):
    oa_ref, ob_ref, oc_ref, sfin_ref, d_scr = refs[-5:]
    sfin_ref = _zero_other_layers(sfin_ref, slot)
    dng_ref, sink_ref, rng_ref = (_layer_row(r, layer) for r in (dng_ref, sink_ref, rng_ref))
    seqs, t = z_ref.shape[0], z_ref.shape[1]
    cols = lambda c0, c1: slice(c0 * LANE, c1 * LANE)
    lg = [_log_sigmoid(dec_ref[p]) for p in range(RT_HEADS // 2)]

    @pl.when(pl.program_id(0) == 0)
    def _():
        _ret_fill_decay(d_scr, lg, t, False)

    chains = []
    for sq in range(seqs):
        chains += _diff_chains(z_ref.at[sq, :, cols(COL_AQ, COL_AK)], z_ref.at[sq, :, cols(COL_AK, COL_AV)],
                               z_ref.at[sq, :, cols(COL_AV, COL_BQ)], oa_ref.at[sq], lam_ref, dng_ref,
                               lambda _: None, lam_init=lam_init, tq=t)
        chains += _win_chains(z_ref.at[sq, :, cols(COL_BQ, COL_BK)], z_ref.at[sq, :, cols(COL_BK, COL_BV)],
                              z_ref.at[sq, :, cols(COL_BV, COL_CQ)], ob_ref.at[sq], sink_ref)
    chains += _ret_chains(z_ref.at[:, :, cols(COL_CQ, COL_CK)], z_ref.at[:, :, cols(COL_CK, COL_CV)],
                          [z_ref.at[:, :, cols(COL_CV + 2 * p, COL_CV + 2 * p + 2)] for p in range(2)],
                          [z_ref.at[:, :, cols(COL_CG + 2 * p, COL_CG + 2 * p + 2)] for p in range(2)],
                          lg, rng_ref, oc_ref, d_scr, sfin_ref=sfin_ref)
    _run_chains(chains)


def _ctx_mix_call(z3, new_state, layer, diff_lam, diff_norm_g, sink, dec4, ret_norm_g, lam_init, seqs=2):
    z3, b0, b = z3
    t = z3.shape[1]
    layer_spec = lambda *s: pl.BlockSpec((None,) + s, lambda bi: (layer,) + (0,) * len(s))
    in_specs = [
        pl.BlockSpec((seqs, t, D_IN), lambda bi: (bi + b0 // seqs, 0, 0)),
        layer_spec(4, HEAD_DIM), _all_layers(diff_norm_g), _all_layers(sink),
        layer_spec(RT_HEADS // 2, 2, 2), _all_layers(ret_norm_g),
    ]
    args = [z3, diff_lam, diff_norm_g, sink, dec4, ret_norm_g]
    n_slots, slot, first = (DEPTH, layer, 0) if new_state is None else (1, 0, layer)
    aliases = {}
    if new_state is not None:
        aliases = {len(in_specs): 3}
        in_specs.append(pl.BlockSpec(memory_space=pl.ANY))
        args.append(new_state)
    o_spec = pl.BlockSpec((seqs, t, BRANCH_W), lambda bi: (bi, 0, 0))
    o_shape = jax.ShapeDtypeStruct((b, t, BRANCH_W), F32)
    return pl.pallas_call(
        functools.partial(_ctx_mix_kernel, lam_init=lam_init, slot=slot, layer=layer),
        grid=(b // seqs,),
        in_specs=in_specs,
        out_specs=[o_spec, o_spec, o_spec,
                   pl.BlockSpec((seqs, n_slots, 2, RT_HEADS, RT_DK, RT_DV), lambda bi: (bi, first, 0, 0, 0, 0))],
        out_shape=[o_shape, o_shape, o_shape,
                   jax.ShapeDtypeStruct((b, DEPTH, 2, RT_HEADS, RT_DK, RT_DV), F32)],
        input_output_aliases=aliases,
        scratch_shapes=[pltpu.VMEM((RT_HEADS, t, t), F32)],
        compiler_params=_params(),
        name="mix_ctx",
    )(*args)


def _layer_norm(x, g, b):
    mu = jnp.mean(x, axis=-1, keepdims=True)
    xc = x - mu
    var = jnp.mean(xc * xc, axis=-1, keepdims=True)
    return xc * lax.rsqrt(var + LN_EPS) * g + b


def _tail_kernel(xc_ref, xl_ref, mod_ref, oac_ref, obc_ref, occ_ref,
                 oal0_ref, oal1_ref, obl0_ref, obl1_ref, ocl0_ref, ocl1_ref,
                 bg_ref, ln1g_ref, ln1b_ref, ln2g_ref, ln2b_ref,
                 wg_hbm, wpa_hbm, wpb_hbm, wpc_hbm, wo_hbm, w1_hbm, w2_hbm, yc_ref, yl_ref,
                 wg_ref, wpa_ref, wpb_ref, wpc_ref, wo_ref, w1_ref, w2_ref, stage_ref, sem,
                 *, n_ctx_tiles, layer):
    d = D_MODEL
    bg_ref, ln1g_ref, ln1b_ref, ln2g_ref, ln2b_ref = (
        _layer_row(r, layer) for r in (bg_ref, ln1g_ref, ln1b_ref, ln2g_ref, ln2b_ref))
    pieces = [(wg_hbm.at[layer, :, i * d:(i + 1) * d], wg_ref.at[:, i * d:(i + 1) * d]) for i in range(3)]
    pieces += [(w.at[layer], r) for w, r in ((wpa_hbm, wpa_ref), (wpb_hbm, wpb_ref), (wpc_hbm, wpc_ref),
                                             (wo_hbm, wo_ref))]
    pieces += [(w1_hbm.at[layer, :, i * d:(i + 1) * d], w1_ref.at[:, i * d:(i + 1) * d]) for i in range(4)]
    pieces += [(w2_hbm.at[layer, i * d:(i + 1) * d, :], w2_ref.at[i * d:(i + 1) * d, :]) for i in range(4)]
    _stage_weights_bf16(pl.program_id(0) == 0, pieces, stage_ref, sem)

    is_ctx = pl.program_id(0) < n_ctx_tiles
    x = jnp.where(is_ctx, xc_ref[...], xl_ref[...])
    sh1, sc1, g1 = mod_ref[:, 0:d], mod_ref[:, d:2 * d], mod_ref[:, 2 * d:3 * d]
    sh2, sc2, g2 = mod_ref[:, 3 * d:4 * d], mod_ref[:, 4 * d:5 * d], mod_ref[:, 5 * d:6 * d]
    h1 = (x * (1.0 + sc1) + sh1).astype(BF16)
    merged = None
    branches = ((oac_ref, (oal0_ref, oal1_ref), wpa_ref), (obc_ref, (obl0_ref, obl1_ref), wpb_ref),
                (occ_ref, (ocl0_ref, ocl1_ref), wpc_ref))
    for i, (oc_ref_i, ol_refs_i, wp_ref) in enumerate(branches):
        o_lat = jnp.concatenate([r[...] for r in ol_refs_i], axis=1)
        o = jnp.where(is_ctx, oc_ref_i[...], o_lat).astype(BF16)
        gate = jax.nn.sigmoid(_dot(h1, wg_ref[:, i * d:(i + 1) * d]) + bg_ref[:, i * d:(i + 1) * d])
        part = gate * _dot(o, wp_ref[...])
        merged = part if merged is None else merged + part
    y = _dot(merged.astype(BF16), wo_ref[...])
    x1 = _layer_norm(ALPHA * x + g1 * y, ln1g_ref[...], ln1b_ref[...])
    h2 = (x1 * (1.0 + sc2) + sh2).astype(BF16)
    u = jnp.maximum(_dot(h2, w1_ref[...]), 0.0)
    f = _dot((u * u).astype(BF16), w2_ref[...])
    x2 = _layer_norm(ALPHA * x1 + g2 * f, ln2g_ref[...], ln2b_ref[...])

    @pl.when(is_ctx)
    def _():
        yc_ref[...] = x2

    @pl.when(jnp.logical_not(is_ctx))
    def _():
        yl_ref[...] = x2


def _tail_call(xc, xl, mod4, layer, o_ctx, o_lat, wg, bg, wpa, wpb, wpc, wo, ln1, w1, w2, ln2,
               t_lat, tm=TAIL_ROWS):
    n_ctx, n_lat = xc.shape[0] // tm, xl.shape[0] // tm
    lat_tiles_per_seq = t_lat // tm

    def mod_idx(i):
        row = jnp.where(i < n_ctx, 0, 1 + (i - n_ctx) // lat_tiles_per_seq)
        return (layer, row, 0, 0)

    ctx_spec = lambda w: pl.BlockSpec((tm, w), lambda i: (jnp.minimum(i, n_ctx - 1), 0))
    lat_spec = lambda w: pl.BlockSpec((tm, w), lambda i: (jnp.maximum(i - n_ctx, 0), 0))
    in_hbm = pl.BlockSpec(memory_space=pl.ANY)
    return pl.pallas_call(
        functools.partial(_tail_kernel, n_ctx_tiles=n_ctx, layer=layer),
        grid=(n_ctx + n_lat,),
        in_specs=[
            ctx_spec(D_MODEL), lat_spec(D_MODEL),
            pl.BlockSpec((None, None, 1, D_MOD), mod_idx),
            ctx_spec(BRANCH_W), ctx_spec(BRANCH_W), ctx_spec(BRANCH_W),
            *[lat_spec(BRANCH_W // 2)] * 6,
            *[_all_layers(p) for p in (bg, *ln1, *ln2)],
        ] + [in_hbm] * 7,
        out_specs=[ctx_spec(D_MODEL), lat_spec(D_MODEL)],
        out_shape=[jax.ShapeDtypeStruct(xc.shape, F32), jax.ShapeDtypeStruct(xl.shape, F32)],
        scratch_shapes=[
            pltpu.VMEM((D_MODEL, D_GATE), BF16),
            pltpu.VMEM((BRANCH_W, D_MODEL), BF16), pltpu.VMEM((BRANCH_W, D_MODEL), BF16),
            pltpu.VMEM((BRANCH_W, D_MODEL), BF16),
            pltpu.VMEM((D_MODEL, D_MODEL), BF16),
            pltpu.VMEM((D_MODEL, D_FF), BF16), pltpu.VMEM((D_FF, D_MODEL), BF16),
            pltpu.VMEM((2, D_MODEL, D_MODEL), F32),
            pltpu.SemaphoreType.DMA((2,)),
        ],
        compiler_params=_params(),
        name="tail",
    )(xc, xl, mod4, *o_ctx, *o_lat, bg, *ln1, *ln2, wg, wpa, wpb, wpc, wo, w1, w2)


def _rope_tables(n_tokens, identity_rows):
    rows = n_tokens // GRID_W
    r, col = jnp.meshgrid(jnp.arange(rows), jnp.arange(GRID_W), indexing="ij")
    r = r.reshape(-1).astype(F32)
    col = col.reshape(-1).astype(F32)
    nf = HEAD_DIM // 4
    inv = ROPE_BASE ** (-jnp.arange(nf, dtype=F32) / nf)
    ang_r = r[:, None] * inv[None, :]
    ang_c = col[:, None] * inv[None, :]
    zero = jnp.zeros_like(ang_r)
    cos = jnp.concatenate([jnp.cos(ang_r)] * 2 + [jnp.cos(ang_c)] * 2, axis=-1)
    s_next = jnp.concatenate([-jnp.sin(ang_r), zero, -jnp.sin(ang_c), zero], axis=-1)
    s_prev = jnp.concatenate([zero, jnp.sin(ang_r), zero, jnp.sin(ang_c)], axis=-1)
    ident = (jnp.ones, jnp.zeros, jnp.zeros)
    return tuple(jnp.concatenate([fill((identity_rows, LANE), F32), jnp.tile(t, (1, LANE // HEAD_DIM))], axis=0)
                 for fill, t in zip(ident, (cos, s_next, s_prev)))


def kernel(x_prompt, x_sample, c, cache_diff_k, cache_diff_v, cache_win_k, cache_win_v, state_ret,
           c_ctx, w_mod, b_mod, w_in, diff_lam, diff_norm_g, win_sink, ret_decay, ret_norm_g,
           w_pa, w_pb, w_pc, w_gate, b_gate, w_o, ln1_g, ln1_b, w_ff1, w_ff2, ln2_g, ln2_b):
    bp, tp, d = x_prompt.shape
    bs, ts, _ = x_sample.shape
    assert d == D_MODEL and w_in.shape == (DEPTH, D_MODEL, D_IN) and c.shape[0] + 1 <= MOD_ROWS
    past = cache_diff_k.shape[2]

    c_rows = jnp.concatenate(
        [c_ctx[None, :], c, jnp.zeros((MOD_ROWS - 1 - bs, d), F32)], axis=0)
    mod4 = _mod_call(c_rows, w_mod, b_mod).reshape(DEPTH, MOD_ROWS, 1, D_MOD)
    rope_tabs = _rope_tables(ts, PROJ_ROWS)

    ck_a = cache_diff_k.reshape(bs, DEPTH, past * DA_HEADS, 2 * HEAD_DIM)
    cv_a = cache_diff_v.reshape(bs, DEPTH, past * DA_HEADS, 2 * HEAD_DIM)
    ck_b = cache_win_k.transpose(0, 1, 3, 4, 2).reshape(bs, DEPTH, WG_KV_HEADS * HEAD_DIM, past)
    cv_b = cache_win_v.transpose(0, 1, 3, 4, 2).reshape(bs, DEPTH, WG_KV_HEADS * HEAD_DIM, past)

    ln1, ln2 = (ln1_g, ln1_b), (ln2_g, ln2_b)
    dec4 = ret_decay.reshape(DEPTH, 2, RT_HEADS // 2, 2).transpose(0, 2, 1, 3)

    xp = x_prompt.reshape(bp * tp, d)
    xs = x_sample.reshape(bs * ts, d)
    new_caches, new_state = [], None
    for l in range(DEPTH):
        lam_init = 0.8 - 0.6 * math.exp(-0.3 * l)
        z_all, *new_caches = _proj_call(xs, xp, mod4, l, w_in, rope_tabs, new_caches, ts, tp)
        z_ctx = (z_all.reshape(-1, tp, D_IN), bs * ts // tp, bp)
        z_lat = (z_all.reshape(-1, ts, D_IN), 0, bs)
        *o_ctx, new_state = _ctx_mix_call(z_ctx, new_state, l, diff_lam, diff_norm_g, win_sink,
                                          dec4, ret_norm_g, lam_init)
        halves = [_lat_mix_call(z_lat, half, (ck_a, cv_a, ck_b, cv_b), state_ret, l, diff_lam,
                                diff_norm_g, win_sink, dec4, ret_norm_g, lam_init)
                  for half in range(2)]
        o_lat = [o.reshape(bs * ts, BRANCH_W // 2) for pair in zip(*halves) for o in pair]
        xp, xs = _tail_call(xp, xs, mod4, l, [o.reshape(bp * tp, BRANCH_W) for o in o_ctx],
                            o_lat, w_gate, b_gate,
                            w_pa, w_pb, w_pc, w_o, ln1, w_ff1, w_ff2, ln2, ts)
    dk, dv, wk, wv = new_caches
    new_diff = [a.reshape(bp, DEPTH, tp, DA_HEADS, 2 * HEAD_DIM) for a in (dk, dv)]
    new_win = [a.reshape(bp, DEPTH, WG_KV_HEADS, HEAD_DIM, tp).transpose(0, 1, 4, 2, 3) for a in (wk, wv)]
    return (xp.reshape(bp, tp, d), xs.reshape(bs, ts, d), *new_diff, *new_win, new_state)
```

```python
import functools
import math

import jax
import jax.numpy as jnp
from jax import lax
from jax.experimental import pallas as pl
from jax.experimental.pallas import tpu as pltpu

F32 = jnp.float32
BF16 = jnp.bfloat16

D_MODEL = 1024
DEPTH = 2
GRID_W = 64
HEAD_DIM = 64
DA_HEADS = 4
WG_Q_HEADS = 8
WG_KV_HEADS = 2
WG_GROUP = WG_Q_HEADS // WG_KV_HEADS
WINDOW = 128
RT_HEADS = 4
RT_DK = 64
RT_DV = 128
BRANCH_W = 512
D_IN = 3840
D_GATE = 3 * D_MODEL
D_FF = 4 * D_MODEL
D_MOD = 6 * D_MODEL
ROPE_BASE = 10000.0
LN_EPS = 1e-5
ALPHA = (2 * DEPTH) ** 0.25
QK_SCALE = HEAD_DIM ** -0.5
LOG2E = math.log2(math.e)

LANE = 128
MOD_ROWS = 8
ONES_ROWS = 16
PROJ_COLS = 768
PROJ_ROWS = 512
TAIL_ROWS = 256
MOD_COLS = 3072
DIFF_Q_BLOCK = 128
RET_Q_BLOCK_LAT, RET_Q_BLOCK_CTX = 512, 128

COL_AQ, COL_AK, COL_AV = 0, 4, 8
COL_BQ, COL_BK, COL_BV = 12, 16, 17
COL_CQ, COL_CK, COL_CV, COL_CG = 18, 20, 22, 26
ROPE_BLOCKS = tuple(range(0, 8)) + tuple(range(12, 17))

VMEM_LIMIT = 56 * 1024 * 1024
NT_DIMS = (((1,), (1,)), ((), ()))
TN_DIMS = (((0,), (0,)), ((), ()))


def _params():
    return pltpu.CompilerParams(vmem_limit_bytes=VMEM_LIMIT)


def _resident(shape, index_map):
    return pl.BlockSpec(shape, index_map, pipeline_mode=pl.Buffered(1))


def _dot(a, b):
    return jnp.dot(a, b, preferred_element_type=F32)


def _all_layers(param):
    return _resident(param.shape, lambda *_: (0, 0))


def _layer_row(ref, layer):
    return ref.at[layer:layer + 1, :]


def _zero_other_layers(ref, slot):
    for other in range(ref.shape[1]):
        if other != slot:
            ref[:, other] = jnp.zeros((ref.shape[0],) + ref.shape[2:], F32)
    return ref.at[:, slot]


def _stage_weights_bf16(first_step, pieces, stage_ref, sem):
    def copy(c):
        src = pieces[c][0]
        return pltpu.make_async_copy(src, stage_ref.at[c % 2, 0:src.shape[0], 0:src.shape[1]], sem.at[c % 2])

    @pl.when(first_step)
    def _():
        for c in range(min(2, len(pieces))):
            copy(c).start()
        for c, (src, dst) in enumerate(pieces):
            copy(c).wait()
            dst[...] = stage_ref[c % 2, 0:src.shape[0], 0:src.shape[1]].astype(BF16)
            if c + 2 < len(pieces):
                copy(c + 2).start()


def _mod_kernel(c_ref, w_ref, b_ref, o_ref):
    c = c_ref[...]
    a = (c * jax.nn.sigmoid(c)).astype(BF16)
    o_ref[...] = _dot(a, w_ref[...].astype(BF16)) + b_ref[...]


def _mod_call(c_rows, w_mod, b_mod):
    tn = MOD_COLS
    return pl.pallas_call(
        _mod_kernel,
        grid=(DEPTH, D_MOD // tn),
        in_specs=[
            pl.BlockSpec((MOD_ROWS, D_MODEL), lambda l, n: (0, 0)),
            pl.BlockSpec((None, D_MODEL, tn), lambda l, n: (l, 0, n)),
            pl.BlockSpec((None, 1, tn), lambda l, n: (l, 0, n)),
        ],
        out_specs=pl.BlockSpec((None, MOD_ROWS, tn), lambda l, n: (l, 0, n)),
        out_shape=jax.ShapeDtypeStruct((DEPTH, MOD_ROWS, D_MOD), F32),
        compiler_params=_params(),
        name="mod_vectors",
    )(c_rows, w_mod, b_mod.reshape(DEPTH, 1, D_MOD))


def _proj_kernel(xl_ref, xc_ref, mod_ref, win_hbm, cos_ref, sa_ref, sb_ref, *refs, n_lat_tiles, slot, layer):
    z_ref, dk_ref, dv_ref, wk_ref, wv_ref, win_ref, stage_ref, sem = refs[-8:]
    nc = stage_ref.shape[2]
    _stage_weights_bf16(pl.program_id(0) == 0,
                        [(win_hbm.at[layer, :, c0:c0 + nc], win_ref.at[:, c0:c0 + nc]) for c0 in range(0, D_IN, nc)],
                        stage_ref, sem)
    seqs, t_ctx = dk_ref.shape[0], wk_ref.shape[-1]
    dk_ref, dv_ref, wk_ref, wv_ref = (_zero_other_layers(r, slot) for r in (dk_ref, dv_ref, wk_ref, wv_ref))
    is_lat = pl.program_id(0) < n_lat_tiles
    x = jnp.where(is_lat, xl_ref[...], xc_ref[...])
    sh1 = mod_ref[:, 0:D_MODEL]
    sc1 = mod_ref[:, D_MODEL:2 * D_MODEL]
    h = (x * (1.0 + sc1) + sh1).astype(BF16)
    for c0 in range(0, D_IN, nc):
        z = _dot(h, win_ref[:, c0:c0 + nc])
        for j in range(nc // LANE):
            blk = c0 // LANE + j
            u = z[:, j * LANE:(j + 1) * LANE]
            if blk in ROPE_BLOCKS:
                u = (u * cos_ref[...] + pltpu.roll(u, LANE - 16, 1) * sa_ref[...]
                     + pltpu.roll(u, 16, 1) * sb_ref[...])
            z_ref[:, blk * LANE:(blk + 1) * LANE] = u
            if COL_AK <= blk < COL_BQ:
                ref, head = (dk_ref, blk - COL_AK) if blk < COL_AV else (dv_ref, blk - COL_AV)
                for s in range(seqs):
                    ref[s, pl.ds(head, t_ctx, stride=DA_HEADS), :] = u[s * t_ctx:(s + 1) * t_ctx]
            elif blk in (COL_BK, COL_BV):
                ref = wk_ref if blk == COL_BK else wv_ref
                ut = u.T
                for s in range(seqs):
                    ref[s] = ut[:, s * t_ctx:(s + 1) * t_ctx]


def _proj_call(xl, xc, mod4, layer, w_in, rope_tabs, caches, t_lat, t_ctx, tm=PROJ_ROWS):
    n_lat, n_ctx = xl.shape[0] // tm, xc.shape[0] // tm
    lat_tiles_per_seq = t_lat // tm
    seqs = tm // t_ctx

    def mod_idx(i):
        row = jnp.where(i < n_lat, 1 + i // lat_tiles_per_seq, 0)
        return (layer, row, 0, 0)

    lat_idx = lambda i: jnp.minimum(i, n_lat - 1)
    ctx_idx = lambda i: jnp.maximum(i - n_lat, 0)
    rope_spec = pl.BlockSpec((tm, LANE), lambda i: (jnp.where(i < n_lat, 1 + i % lat_tiles_per_seq, 0), 0))
    n_slots, slot, first = (1, 0, layer) if caches else (DEPTH, layer, 0)
    cache_a = pl.BlockSpec((seqs, n_slots, DA_HEADS * t_ctx, 2 * HEAD_DIM), lambda i: (ctx_idx(i), first, 0, 0))
    cache_b = pl.BlockSpec((seqs, n_slots, WG_KV_HEADS * HEAD_DIM, t_ctx), lambda i: (ctx_idx(i), first, 0, 0))
    in_specs = [
        pl.BlockSpec((tm, D_MODEL), lambda i: (lat_idx(i), 0)),
        pl.BlockSpec((tm, D_MODEL), lambda i: (ctx_idx(i), 0)),
        pl.BlockSpec((None, None, 1, D_MOD), mod_idx),
        pl.BlockSpec(memory_space=pl.ANY),
        rope_spec, rope_spec, rope_spec,
    ]
    n_in = len(in_specs)
    in_specs += [pl.BlockSpec(memory_space=pl.ANY)] * len(caches)
    b_ctx = xc.shape[0] // t_ctx
    cache_shapes = [(b_ctx, DEPTH, DA_HEADS * t_ctx, 2 * HEAD_DIM)] * 2 \
        + [(b_ctx, DEPTH, WG_KV_HEADS * HEAD_DIM, t_ctx)] * 2
    return pl.pallas_call(
        functools.partial(_proj_kernel, n_lat_tiles=n_lat, slot=slot, layer=layer),
        grid=(n_lat + n_ctx,),
        in_specs=in_specs,
        out_specs=[pl.BlockSpec((tm, D_IN), lambda i: (i, 0)), cache_a, cache_a, cache_b, cache_b],
        out_shape=[jax.ShapeDtypeStruct((xl.shape[0] + xc.shape[0], D_IN), F32)]
        + [jax.ShapeDtypeStruct(s, F32) for s in cache_shapes],
        input_output_aliases={n_in + k: 1 + k for k in range(len(caches))},
        scratch_shapes=[
            pltpu.VMEM((D_MODEL, D_IN), BF16),
            pltpu.VMEM((2, D_MODEL, PROJ_COLS), F32),
            pltpu.SemaphoreType.DMA((2,)),
        ],
        compiler_params=_params(),
        name="proj",
    )(xl, xc, mod4, w_in, *rope_tabs, *caches)


def _values_t(v_parts):
    vt = jnp.concatenate([v.T for v in v_parts], axis=1)
    return jnp.concatenate([vt, jnp.ones((ONES_ROWS, vt.shape[1]), F32)], axis=0).astype(BF16)


def _run_chains(chains):
    a, b = {}, {}
    n = len(chains)
    for c in range(n + 2):
        if c < n:
            a[c] = chains[c][0]()
        if 1 <= c <= n:
            b[c - 1] = chains[c - 1][1](a.pop(c - 1))
        if c >= 2:
            chains[c - 2][2](b.pop(c - 2))


def _diff_chains(q_ref, k_ref, v_ref, o_ref, lam_ref, g_ref, cached_kv, *, lam_init, tq):
    t, heads = q_ref.shape[0], q_ref.shape[1] // LANE
    lv = lam_ref[...]
    lam = (jnp.exp(jnp.sum(lv[0:1] * lv[1:2], axis=-1, keepdims=True))
           - jnp.exp(jnp.sum(lv[2:3] * lv[3:4], axis=-1, keepdims=True)) + lam_init)
    lane = lax.broadcasted_iota(jnp.int32, (tq, LANE), 1)
    kv = {}

    def scores(hh, i):
        cols = slice(hh * LANE, (hh + 1) * LANE)
        if hh not in kv:
            k_parts, v_parts = [k_ref[:, cols]], [v_ref[:, cols]]
            extra = cached_kv(hh)
            if extra is not None:
                k_parts.append(extra[0])
                v_parts.append(extra[1])
            kv[hh] = (jnp.concatenate(k_parts, axis=0).astype(BF16), _values_t(v_parts))
        q = q_ref[i * tq:(i + 1) * tq, cols] * (QK_SCALE * LOG2E)
        qs = jnp.concatenate([jnp.where(lane < HEAD_DIM, q, 0.0),
                              jnp.where(lane >= HEAD_DIM, q, 0.0)], axis=0).astype(BF16)
        return lax.dot_general(kv[hh][0], qs, NT_DIMS, preferred_element_type=F32)

    def weights(s):
        return jnp.exp2(s - s.max(axis=0, keepdims=True)).astype(BF16)

    def finish(hh, i, et):
        ot = _dot(kv[hh][1], et)
        o = ot[0:LANE] * (1.0 / ot[LANE:LANE + 1])
        od = o[:, 0:tq] - lam * o[:, tq:2 * tq]
        yt = od * lax.rsqrt(jnp.mean(od * od, axis=0, keepdims=True) + LN_EPS)
        o_ref[i * tq:(i + 1) * tq, hh * LANE:(hh + 1) * LANE] = yt.T * g_ref[...] * (1.0 - lam_init)

    return [(functools.partial(scores, hh, i), weights, functools.partial(finish, hh, i))
            for hh in range(heads) for i in range(t // tq)]


def _win_chains(q_ref, k_ref, v_ref, o_ref, sink_ref, kc_ref=None, vc_ref=None, kv_heads=range(WG_KV_HEADS)):
    latent = kc_ref is not None
    head0 = kv_heads[0] * WG_GROUP
    t = q_ref.shape[0]
    w = WINDOW
    tqb, heads_per_chain = w, WG_GROUP
    nb = t // tqb
    kb = k_ref[...].astype(BF16)
    vt = v_ref[...].T
    if latent:
        kcb = kc_ref[...].T.astype(BF16)
        vct = vc_ref[...]
        jj = lax.broadcasted_iota(jnp.int32, (w, w), 0)
        ii = lax.broadcasted_iota(jnp.int32, (w, w), 1)
        bias_prev = jnp.concatenate([jnp.where(jj >= ii, 0.0, -1e30)] * heads_per_chain, axis=1)
        bias_next = jnp.concatenate([jnp.where(jj <= ii, 0.0, -1e30)] * heads_per_chain, axis=1)
    chains = [(kv, h0, n)
              for n in range(nb)
              for kv in kv_heads
              for h0 in range(kv * WG_GROUP, (kv + 1) * WG_GROUP, heads_per_chain)]
    sinks = {}

    def sink_row(h0):
        if h0 not in sinks:
            sinks[h0] = jnp.concatenate(
                [jnp.broadcast_to(sink_ref[:, h:h + 1] * LOG2E, (1, tqb))
                 for h in range(h0, h0 + heads_per_chain)], axis=1)
        return sinks[h0]

    def key_blocks(n):
        return (max(n - 1, 0), min(n + 1, nb - 1)) if latent else (0, nb - 1)

    def scores(kv, h0, n):
        lo = kv * HEAD_DIM
        rows = slice(n * tqb, (n + 1) * tqb)
        q_g = (jnp.concatenate([q_ref[rows, (h - head0) * HEAD_DIM:(h - head0 + 1) * HEAD_DIM]
                                for h in range(h0, h0 + heads_per_chain)], axis=0)
               * (QK_SCALE * LOG2E)).astype(BF16)
        b0, b1 = key_blocks(n)
        keys = kb[b0 * tqb:(b1 + 1) * tqb, lo:lo + HEAD_DIM]
        if latent:
            keys = jnp.concatenate([keys, kcb[:, lo:lo + HEAD_DIM]], axis=0)
        st = lax.dot_general(keys, q_g, NT_DIMS, preferred_element_type=F32)
        if latent:
            parts = []
            for blk in range(b0, b1 + 1):
                part = st[(blk - b0) * w:(blk - b0 + 1) * w]
                if blk == n - 1:
                    part = part + bias_prev
                elif blk == n + 1:
                    part = part + bias_next
                parts.append(part)
            parts.append(st[(b1 - b0 + 1) * w:])
            st = jnp.concatenate(parts, axis=0)
        return st

    def weights(h0, st):
        m = jnp.maximum(st.max(axis=0, keepdims=True), sink_row(h0))
        return jnp.exp2(st - m).astype(BF16), m

    def finish(kv, h0, n, et_m):
        et, m = et_m
        lo = kv * HEAD_DIM
        b0, b1 = key_blocks(n)
        vals = [vt[lo:lo + HEAD_DIM, b0 * tqb:(b1 + 1) * tqb]]
        if latent:
            vals.append(vct[lo:lo + HEAD_DIM])
        vals = jnp.concatenate(vals, axis=1)
        vals = jnp.concatenate([vals, jnp.ones((ONES_ROWS, vals.shape[1]), F32)], axis=0).astype(BF16)
        ot = _dot(vals, et)
        d = ot[HEAD_DIM:HEAD_DIM + 1] + jnp.exp2(sink_row(h0) - m)
        on = ot[0:HEAD_DIM] * (1.0 / d)
        for p in range(heads_per_chain // 2):
            pair = jnp.concatenate([on[:, (2 * p) * tqb:(2 * p + 1) * tqb],
                                    on[:, (2 * p + 1) * tqb:(2 * p + 2) * tqb]], axis=0)
            c0 = ((h0 - head0) // 2 + p) * LANE
            o_ref[n * tqb:(n + 1) * tqb, c0:c0 + LANE] = pair.T

    return [(functools.partial(scores, kv, h0, n), functools.partial(weights, h0),
             functools.partial(finish, kv, h0, n)) for kv, h0, n in chains]


def _log_sigmoid(x):
    return jnp.minimum(x, 0.0) - jnp.log1p(jnp.exp(-jnp.abs(x)))


def _ret_fill_decay(d_scr, lg, t, latent):
    bw = min(t, LANE)
    nb = t // bw
    off = (lax.broadcasted_iota(jnp.int32, (bw, bw), 1)
           - lax.broadcasted_iota(jnp.int32, (bw, bw), 0)).astype(F32)
    for hh in range(d_scr.shape[0]):
        lgf, lgb = lg[hh // 2][0:1, hh % 2:hh % 2 + 1], lg[hh // 2][1:2, hh % 2:hh % 2 + 1]
        for o in range(-(nb - 1), nb):
            diff = off + float(o * bw)
            blk = (jnp.where(diff >= 0, jnp.exp(jnp.maximum(diff, 0.0) * lgf), 0.0)
                   + jnp.where(diff <= 0, jnp.exp(jnp.maximum(-diff, 0.0) * lgb), 0.0))
            for bs in range(max(0, -o), min(nb, nb - o)):
                d_scr[hh, bs * bw:(bs + 1) * bw, (bs + o) * bw:(bs + o + 1) * bw] = blk
        if latent:
            tp = lax.broadcasted_iota(jnp.int32, (RT_DK, t), 1).astype(F32)
            d_scr[hh, t:t + RT_DK, :] = jnp.exp((tp + 1.0) * lgf)
            d_scr[hh, t + RT_DK:t + 2 * RT_DK, :] = jnp.exp((float(t) - tp) * lgb)


def _ret_chains(q_ref, k_ref, v_refs, cg_refs, lg, g_ref, o_ref, d_scr, s0_ref=None, sfin_ref=None):
    latent = s0_ref is not None
    seqs, t = q_ref.shape[0], q_ref.shape[1]
    pairs = len(v_refs)
    tq = RET_Q_BLOCK_LAT if latent else RET_Q_BLOCK_CTX
    if latent:
        eye = (lax.broadcasted_iota(jnp.int32, (RT_DK, RT_DK), 0)
               == lax.broadcasted_iota(jnp.int32, (RT_DK, RT_DK), 1)).astype(F32)
    heads = {}

    def head_operands(sq, hh):
        if (sq, hh) not in heads:
            p, j = hh // 2, hh % 2
            kf = k_ref[sq, :, hh * RT_DK:(hh + 1) * RT_DK] * (RT_DK ** -0.5)
            v = v_refs[p][sq, :, j * RT_DV:(j + 1) * RT_DV]
            keys, vals_t = [kf], [v.T]
            if latent:
                keys += [eye, eye]
                vals_t.append(jnp.concatenate([s0_ref[sq, 0, hh], s0_ref[sq, 1, hh]], axis=0).T)
            heads[(sq, hh)] = (jnp.concatenate(keys, axis=0).astype(BF16),
                               jnp.concatenate(vals_t, axis=1).astype(BF16))
            if not latent:
                sp = lax.broadcasted_iota(jnp.int32, (t, 1), 0).astype(F32)
                zf = jnp.exp((float(t) - 1.0 - sp) * lg[p][0:1, j:j + 1])
                zb = jnp.exp(sp * lg[p][1:2, j:j + 1])
                kz = jnp.concatenate([kf * zf, kf * zb], axis=1).astype(BF16)
                s_fb = _dot(heads[(sq, hh)][1], kz).T
                sfin_ref[sq, 0, hh] = s_fb[0:RT_DK]
                sfin_ref[sq, 1, hh] = s_fb[RT_DK:2 * RT_DK]
        return heads[(sq, hh)]

    chains = [(sq, hh, i) for sq in range(seqs) for hh in range(2 * pairs) for i in range(t // tq)]

    def scores(sq, hh, i):
        q = q_ref[sq, i * tq:(i + 1) * tq, hh * RT_DK:(hh + 1) * RT_DK].astype(BF16)
        return lax.dot_general(head_operands(sq, hh)[0], q, NT_DIMS, preferred_element_type=F32)

    def finish(sq, hh, i, at):
        yt = _dot(head_operands(sq, hh)[1], at)
        mu = jnp.mean(yt, axis=0, keepdims=True)
        yc = yt - mu
        var = jnp.mean(yc * yc, axis=0, keepdims=True)
        yn = (yc * lax.rsqrt(var + LN_EPS)).T * g_ref[...]
        cg = cg_refs[hh // 2][sq, i * tq:(i + 1) * tq, (hh % 2) * RT_DV:(hh % 2 + 1) * RT_DV]
        o_ref[sq, i * tq:(i + 1) * tq, hh * RT_DV:(hh + 1) * RT_DV] = yn * (cg * jax.nn.sigmoid(cg))

    def weights(hh, i, st):
        return (st * d_scr[hh, :, i * tq:(i + 1) * tq]).astype(BF16)

    return [(functools.partial(scores, sq, hh, i), functools.partial(weights, hh, i),
             functools.partial(finish, sq, hh, i)) for sq, hh, i in chains]


def _lat_mix_kernel(aq_ref, ak_ref, av_ref, bq_ref, bk_ref, bv_ref, cq_ref, ck_ref, cv_ref, cg_ref,
                    akc_ref, avc_ref, bkc_ref, bvc_ref, s0_ref, lam_ref, dng_ref, sink_ref, dec_ref, rng_ref,
                    oa_ref, ob_ref, oc_ref, d_scr, *, lam_init, half, layer):
    t = aq_ref.shape[1]
    dng_ref, sink_ref, rng_ref = (_layer_row(r, layer) for r in (dng_ref, sink_ref, rng_ref))
    lg = [_log_sigmoid(dec_ref[0])]

    @pl.when(pl.program_id(0) == 0)
    def _():
        _ret_fill_decay(d_scr, lg, t, True)

    def cached_kv(hh):
        mine = pl.ds(2 * half + hh, akc_ref.shape[0] // DA_HEADS, stride=DA_HEADS)
        return akc_ref[mine, :], avc_ref[mine, :]

    chains = _diff_chains(aq_ref.at[0], ak_ref.at[0], av_ref.at[0], oa_ref.at[0], lam_ref, dng_ref,
                          cached_kv, lam_init=lam_init, tq=DIFF_Q_BLOCK)
    chains += _win_chains(bq_ref.at[0], bk_ref.at[0], bv_ref.at[0], ob_ref.at[0], sink_ref,
                          bkc_ref, bvc_ref, kv_heads=(half,))
    chains += _ret_chains(cq_ref, ck_ref, [cv_ref], [cg_ref], lg, rng_ref, oc_ref, d_scr, s0_ref=s0_ref)
    _run_chains(chains)


def _lat_mix_call(z3, half, caches, state6, layer, diff_lam, diff_norm_g, sink, dec4, ret_norm_g, lam_init):
    z3, b0, b = z3
    t = z3.shape[1]
    ck_a, cv_a, ck_b, cv_b = caches
    zcol = lambda blk0, nblk: pl.BlockSpec((1, t, nblk * LANE), lambda bi: (bi + b0, 0, blk0 // nblk))
    layer_spec = lambda *s: pl.BlockSpec((None,) + s, lambda bi: (layer,) + (0,) * len(s))
    cache_spec = lambda a: pl.BlockSpec((None, None) + a.shape[2:], lambda bi: (bi, layer, 0, 0))
    o_spec = pl.BlockSpec((1, t, BRANCH_W // 2), lambda bi: (bi, 0, 0))
    o_shape = jax.ShapeDtypeStruct((b, t, BRANCH_W // 2), F32)
    return pl.pallas_call(
        functools.partial(_lat_mix_kernel, lam_init=lam_init, half=half, layer=layer),
        grid=(b,),
        in_specs=[
            zcol(COL_AQ + 2 * half, 2), zcol(COL_AK + 2 * half, 2), zcol(COL_AV + 2 * half, 2),
            zcol(COL_BQ + 2 * half, 2), zcol(COL_BK, 1), zcol(COL_BV, 1),
            zcol(COL_CQ + half, 1), zcol(COL_CK + half, 1), zcol(COL_CV + 2 * half, 2), zcol(COL_CG + 2 * half, 2),
            cache_spec(ck_a), cache_spec(cv_a), cache_spec(ck_b), cache_spec(cv_b),
            pl.BlockSpec((1, None, 2, 2, RT_DK, RT_DV), lambda bi: (bi, layer, 0, half, 0, 0)),
            layer_spec(4, HEAD_DIM), _all_layers(diff_norm_g), _all_layers(sink),
            pl.BlockSpec((None, 1, 2, 2), lambda bi: (layer, half, 0, 0)), _all_layers(ret_norm_g),
        ],
        out_specs=[o_spec, o_spec, o_spec],
        out_shape=[o_shape, o_shape, o_shape],
        scratch_shapes=[pltpu.VMEM((2, t + 2 * RT_DK, t), F32)],
        compiler_params=_params(),
        name="mix_lat",
    )(*([z3] * 10), ck_a, cv_a, ck_b, cv_b, state6, diff_lam, diff_norm_g, sink, dec4, ret_norm_g)


def _ctx_mix_kernel(z_ref, lam_ref, dng_ref, sink_ref, dec_ref, rng_ref, *refs, lam_init, slot, layer):
    oa_ref, ob_ref, oc_ref, sfin_ref, d_scr = refs[-5:]
    sfin_ref = _zero_other_layers(sfin_ref, slot)
    dng_ref, sink_ref, rng_ref = (_layer_row(r, layer) for r in (dng_ref, sink_ref, rng_ref))
    seqs, t = z_ref.shape[0], z_ref.shape[1]
    cols = lambda c0, c1: slice(c0 * LANE, c1 * LANE)
    lg = [_log_sigmoid(dec_ref[p]) for p in range(RT_HEADS // 2)]

    @pl.when(pl.program_id(0) == 0)
    def _():
        _ret_fill_decay(d_scr, lg, t, False)

    chains = []
    for sq in range(seqs):
        chains += _diff_chains(z_ref.at[sq, :, cols(COL_AQ, COL_AK)], z_ref.at[sq, :, cols(COL_AK, COL_AV)],
                               z_ref.at[sq, :, cols(COL_AV, COL_BQ)], oa_ref.at[sq], lam_ref, dng_ref,
                               lambda _: None, lam_init=lam_init, tq=t)
        chains += _win_chains(z_ref.at[sq, :, cols(COL_BQ, COL_BK)], z_ref.at[sq, :, cols(COL_BK, COL_BV)],
                              z_ref.at[sq, :, cols(COL_BV, COL_CQ)], ob_ref.at[sq], sink_ref)
    chains += _ret_chains(z_ref.at[:, :, cols(COL_CQ, COL_CK)], z_ref.at[:, :, cols(COL_CK, COL_CV)],
                          [z_ref.at[:, :, cols(COL_CV + 2 * p, COL_CV + 2 * p + 2)] for p in range(2)],
                          [z_ref.at[:, :, cols(COL_CG + 2 * p, COL_CG + 2 * p + 2)] for p in range(2)],
                          lg, rng_ref, oc_ref, d_scr, sfin_ref=sfin_ref)
    _run_chains(chains)


def _ctx_mix_call(z3, new_state, layer, diff_lam, diff_norm_g, sink, dec4, ret_norm_g, lam_init, seqs=2):
    z3, b0, b = z3
    t = z3.shape[1]
    layer_spec = lambda *s: pl.BlockSpec((None,) + s, lambda bi: (layer,) + (0,) * len(s))
    in_specs = [
        pl.BlockSpec((seqs, t, D_IN), lambda bi: (bi + b0 // seqs, 0, 0)),
        layer_spec(4, HEAD_DIM), _all_layers(diff_norm_g), _all_layers(sink),
        layer_spec(RT_HEADS // 2, 2, 2), _all_layers(ret_norm_g),
    ]
    args = [z3, diff_lam, diff_norm_g, sink, dec4, ret_norm_g]
    n_slots, slot, first = (DEPTH, layer, 0) if new_state is None else (1, 0, layer)
    aliases = {}
    if new_state is not None:
        aliases = {len(in_specs): 3}
        in_specs.append(pl.BlockSpec(memory_space=pl.ANY))
        args.append(new_state)
    o_spec = pl.BlockSpec((seqs, t, BRANCH_W), lambda bi: (bi, 0, 0))
    o_shape = jax.ShapeDtypeStruct((b, t, BRANCH_W), F32)
    return pl.pallas_call(
        functools.partial(_ctx_mix_kernel, lam_init=lam_init, slot=slot, layer=layer),
        grid=(b // seqs,),
        in_specs=in_specs,
        out_specs=[o_spec, o_spec, o_spec,
                   pl.BlockSpec((seqs, n_slots, 2, RT_HEADS, RT_DK, RT_DV), lambda bi: (bi, first, 0, 0, 0, 0))],
        out_shape=[o_shape, o_shape, o_shape,
                   jax.ShapeDtypeStruct((b, DEPTH, 2, RT_HEADS, RT_DK, RT_DV), F32)],
        input_output_aliases=aliases,
        scratch_shapes=[pltpu.VMEM((RT_HEADS, t, t), F32)],
        compiler_params=_params(),
        name="mix_ctx",
    )(*args)


def _layer_norm(x, g, b):
    mu = jnp.mean(x, axis=-1, keepdims=True)
    xc = x - mu
    var = jnp.mean(xc * xc, axis=-1, keepdims=True)
    return xc * lax.rsqrt(var + LN_EPS) * g + b


def _tail_kernel(xc_ref, xl_ref, mod_ref, oac_ref, obc_ref, occ_ref,
                 oal0_ref, oal1_ref, obl0_ref, obl1_ref, ocl0_ref, ocl1_ref,
                 bg_ref, ln1g_ref, ln1b_ref, ln2g_ref, ln2b_ref,
                 wg_hbm, wpa_hbm, wpb_hbm, wpc_hbm, wo_hbm, w1_hbm, w2_hbm, yc_ref, yl_ref,
                 wg_ref, wpa_ref, wpb_ref, wpc_ref, wo_ref, w1_ref, w2_ref, stage_ref, sem,
                 *, n_ctx_tiles, layer):
    d = D_MODEL
    bg_ref, ln1g_ref, ln1b_ref, ln2g_ref, ln2b_ref = (
        _layer_row(r, layer) for r in (bg_ref, ln1g_ref, ln1b_ref, ln2g_ref, ln2b_ref))
    pieces = [(wg_hbm.at[layer, :, i * d:(i + 1) * d], wg_ref.at[:, i * d:(i + 1) * d]) for i in range(3)]
    pieces += [(w.at[layer], r) for w, r in ((wpa_hbm, wpa_ref), (wpb_hbm, wpb_ref), (wpc_hbm, wpc_ref),
                                             (wo_hbm, wo_ref))]
    pieces += [(w1_hbm.at[layer, :, i * d:(i + 1) * d], w1_ref.at[:, i * d:(i + 1) * d]) for i in range(4)]
    pieces += [(w2_hbm.at[layer, i * d:(i + 1) * d, :], w2_ref.at[i * d:(i + 1) * d, :]) for i in range(4)]
    _stage_weights_bf16(pl.program_id(0) == 0, pieces, stage_ref, sem)

    is_ctx = pl.program_id(0) < n_ctx_tiles
    x = jnp.where(is_ctx, xc_ref[...], xl_ref[...])
    sh1, sc1, g1 = mod_ref[:, 0:d], mod_ref[:, d:2 * d], mod_ref[:, 2 * d:3 * d]
    sh2, sc2, g2 = mod_ref[:, 3 * d:4 * d], mod_ref[:, 4 * d:5 * d], mod_ref[:, 5 * d:6 * d]
    h1 = (x * (1.0 + sc1) + sh1).astype(BF16)
    merged = None
    branches = ((oac_ref, (oal0_ref, oal1_ref), wpa_ref), (obc_ref, (obl0_ref, obl1_ref), wpb_ref),
                (occ_ref, (ocl0_ref, ocl1_ref), wpc_ref))
    for i, (oc_ref_i, ol_refs_i, wp_ref) in enumerate(branches):
        o_lat = jnp.concatenate([r[...] for r in ol_refs_i], axis=1)
        o = jnp.where(is_ctx, oc_ref_i[...], o_lat).astype(BF16)
        gate = jax.nn.sigmoid(_dot(h1, wg_ref[:, i * d:(i + 1) * d]) + bg_ref[:, i * d:(i + 1) * d])
        part = gate * _dot(o, wp_ref[...])
        merged = part if merged is None else merged + part
    y = _dot(merged.astype(BF16), wo_ref[...])
    x1 = _layer_norm(ALPHA * x + g1 * y, ln1g_ref[...], ln1b_ref[...])
    h2 = (x1 * (1.0 + sc2) + sh2).astype(BF16)
    u = jnp.maximum(_dot(h2, w1_ref[...]), 0.0)
    f = _dot((u * u).astype(BF16), w2_ref[...])
    x2 = _layer_norm(ALPHA * x1 + g2 * f, ln2g_ref[...], ln2b_ref[...])

    @pl.when(is_ctx)
    def _():
        yc_ref[...] = x2

    @pl.when(jnp.logical_not(is_ctx))
    def _():
        yl_ref[...] = x2


def _tail_call(xc, xl, mod4, layer, o_ctx, o_lat, wg, bg, wpa, wpb, wpc, wo, ln1, w1, w2, ln2,
               t_lat, tm=TAIL_ROWS):
    n_ctx, n_lat = xc.shape[0] // tm, xl.shape[0] // tm
    lat_tiles_per_seq = t_lat // tm

    def mod_idx(i):
        row = jnp.where(i < n_ctx, 0, 1 + (i - n_ctx) // lat_tiles_per_seq)
        return (layer, row, 0, 0)

    ctx_spec = lambda w: pl.BlockSpec((tm, w), lambda i: (jnp.minimum(i, n_ctx - 1), 0))
    lat_spec = lambda w: pl.BlockSpec((tm, w), lambda i: (jnp.maximum(i - n_ctx, 0), 0))
    in_hbm = pl.BlockSpec(memory_space=pl.ANY)
    return pl.pallas_call(
        functools.partial(_tail_kernel, n_ctx_tiles=n_ctx, layer=layer),
        grid=(n_ctx + n_lat,),
        in_specs=[
            ctx_spec(D_MODEL), lat_spec(D_MODEL),
            pl.BlockSpec((None, None, 1, D_MOD), mod_idx),
            ctx_spec(BRANCH_W), ctx_spec(BRANCH_W), ctx_spec(BRANCH_W),
            *[lat_spec(BRANCH_W // 2)] * 6,
            *[_all_layers(p) for p in (bg, *ln1, *ln2)],
        ] + [in_hbm] * 7,
        out_specs=[ctx_spec(D_MODEL), lat_spec(D_MODEL)],
        out_shape=[jax.ShapeDtypeStruct(xc.shape, F32), jax.ShapeDtypeStruct(xl.shape, F32)],
        scratch_shapes=[
            pltpu.VMEM((D_MODEL, D_GATE), BF16),
            pltpu.VMEM((BRANCH_W, D_MODEL), BF16), pltpu.VMEM((BRANCH_W, D_MODEL), BF16),
            pltpu.VMEM((BRANCH_W, D_MODEL), BF16),
            pltpu.VMEM((D_MODEL, D_MODEL), BF16),
            pltpu.VMEM((D_MODEL, D_FF), BF16), pltpu.VMEM((D_FF, D_MODEL), BF16),
            pltpu.VMEM((2, D_MODEL, D_MODEL), F32),
            pltpu.SemaphoreType.DMA((2,)),
        ],
        compiler_params=_params(),
        name="tail",
    )(xc, xl, mod4, *o_ctx, *o_lat, bg, *ln1, *ln2, wg, wpa, wpb, wpc, wo, w1, w2)


def _rope_tables(n_tokens, identity_rows):
    rows = n_tokens // GRID_W
    r, col = jnp.meshgrid(jnp.arange(rows), jnp.arange(GRID_W), indexing="ij")
    r = r.reshape(-1).astype(F32)
    col = col.reshape(-1).astype(F32)
    nf = HEAD_DIM // 4
    inv = ROPE_BASE ** (-jnp.arange(nf, dtype=F32) / nf)
    ang_r = r[:, None] * inv[None, :]
    ang_c = col[:, None] * inv[None, :]
    zero = jnp.zeros_like(ang_r)
    cos = jnp.concatenate([jnp.cos(ang_r)] * 2 + [jnp.cos(ang_c)] * 2, axis=-1)
    s_next = jnp.concatenate([-jnp.sin(ang_r), zero, -jnp.sin(ang_c), zero], axis=-1)
    s_prev = jnp.concatenate([zero, jnp.sin(ang_r), zero, jnp.sin(ang_c)], axis=-1)
    ident = (jnp.ones, jnp.zeros, jnp.zeros)
    return tuple(jnp.concatenate([fill((identity_rows, LANE), F32), jnp.tile(t, (1, LANE // HEAD_DIM))], axis=0)
                 for fill, t in zip(ident, (cos, s_next, s_prev)))


def kernel(x_prompt, x_sample, c, cache_diff_k, cache_diff_v, cache_win_k, cache_win_v, state_ret,
           c_ctx, w_mod, b_mod, w_in, diff_lam, diff_norm_g, win_sink, ret_decay, ret_norm_g,
           w_pa, w_pb, w_pc, w_gate, b_gate, w_o, ln1_g, ln1_b, w_ff1, w_ff2, ln2_g, ln2_b):
    bp, tp, d = x_prompt.shape
    bs, ts, _ = x_sample.shape
    assert d == D_MODEL and w_in.shape == (DEPTH, D_MODEL, D_IN) and c.shape[0] + 1 <= MOD_ROWS
    past = cache_diff_k.shape[2]

    c_rows = jnp.concatenate(
        [c_ctx[None, :], c, jnp.zeros((MOD_ROWS - 1 - bs, d), F32)], axis=0)
    mod4 = _mod_call(c_rows, w_mod, b_mod).reshape(DEPTH, MOD_ROWS, 1, D_MOD)
    rope_tabs = _rope_tables(ts, PROJ_ROWS)

    ck_a = cache_diff_k.reshape(bs, DEPTH, past * DA_HEADS, 2 * HEAD_DIM)
    cv_a = cache_diff_v.reshape(bs, DEPTH, past * DA_HEADS, 2 * HEAD_DIM)
    ck_b = cache_win_k.transpose(0, 1, 3, 4, 2).reshape(bs, DEPTH, WG_KV_HEADS * HEAD_DIM, past)
    cv_b = cache_win_v.transpose(0, 1, 3, 4, 2).reshape(bs, DEPTH, WG_KV_HEADS * HEAD_DIM, past)

    ln1, ln2 = (ln1_g, ln1_b), (ln2_g, ln2_b)
    dec4 = ret_decay.reshape(DEPTH, 2, RT_HEADS // 2, 2).transpose(0, 2, 1, 3)

    xp = x_prompt.reshape(bp * tp, d)
    xs = x_sample.reshape(bs * ts, d)
    new_caches, new_state = [], None
    for l in range(DEPTH):
        lam_init = 0.8 - 0.6 * math.exp(-0.3 * l)
        z_all, *new_caches = _proj_call(xs, xp, mod4, l, w_in, rope_tabs, new_caches, ts, tp)
        z_ctx = (z_all.reshape(-1, tp, D_IN), bs * ts // tp, bp)
        z_lat = (z_all.reshape(-1, ts, D_IN), 0, bs)
        *o_ctx, new_state = _ctx_mix_call(z_ctx, new_state, l, diff_lam, diff_norm_g, win_sink,
                                          dec4, ret_norm_g, lam_init)
        halves = [_lat_mix_call(z_lat, half, (ck_a, cv_a, ck_b, cv_b), state_ret, l, diff_lam,
                                diff_norm_g, win_sink, dec4, ret_norm_g, lam_init)
                  for half in range(2)]
        o_lat = [o.reshape(bs * ts, BRANCH_W // 2) for pair in zip(*halves) for o in pair]
        xp, xs = _tail_call(xp, xs, mod4, l, [o.reshape(bp * tp, BRANCH_W) for o in o_ctx],
                            o_lat, w_gate, b_gate,
                            w_pa, w_pb, w_pc, w_o, ln1, w_ff1, w_ff2, ln2, ts)
    dk, dv, wk, wv = new_caches
    new_diff = [a.reshape(bp, DEPTH, tp, DA_HEADS, 2 * HEAD_DIM) for a in (dk, dv)]
    new_win = [a.reshape(bp, DEPTH, WG_KV_HEADS, HEAD_DIM, tp).transpose(0, 1, 4, 2, 3) for a in (wk, wv)]
    return (xp.reshape(bp, tp, d), xs.reshape(bs, ts, d), *new_diff, *new_win, new_state)
```

```python
import functools
import math

import jax
import jax.numpy as jnp
from jax import lax
from jax.experimental import pallas as pl
from jax.experimental.pallas import tpu as pltpu

F32 = jnp.float32
BF16 = jnp.bfloat16

D_MODEL = 1024
DEPTH = 2
GRID_W = 64
HEAD_DIM = 64
DA_HEADS = 4
WG_Q_HEADS = 8
WG_KV_HEADS = 2
WG_GROUP = WG_Q_HEADS // WG_KV_HEADS
WINDOW = 128
RT_HEADS = 4
RT_DK = 64
RT_DV = 128
BRANCH_W = 512
D_IN = 3840
D_GATE = 3 * D_MODEL
D_FF = 4 * D_MODEL
D_MOD = 6 * D_MODEL
ROPE_BASE = 10000.0
LN_EPS = 1e-5
ALPHA = (2 * DEPTH) ** 0.25
QK_SCALE = HEAD_DIM ** -0.5
LOG2E = math.log2(math.e)

LANE = 128
MOD_ROWS = 8
ONES_ROWS = 16
PROJ_COLS = 768
PROJ_ROWS = 512
TAIL_ROWS = 256
MOD_K_ROWS = 256
DIFF_Q_BLOCK = 128
RET_Q_BLOCK_LAT, RET_Q_BLOCK_CTX = 512, 128

COL_AQ, COL_AK, COL_AV = 0, 4, 8
COL_BQ, COL_BK, COL_BV = 12, 16, 17
COL_CQ, COL_CK, COL_CV, COL_CG = 18, 20, 22, 26
ROPE_BLOCKS = tuple(range(0, 8)) + tuple(range(12, 17))

VMEM_LIMIT = 56 * 1024 * 1024
NT_DIMS = (((1,), (1,)), ((), ()))
TN_DIMS = (((0,), (0,)), ((), ()))


def _params():
    return pltpu.CompilerParams(vmem_limit_bytes=VMEM_LIMIT)


def _resident(shape, index_map):
    return pl.BlockSpec(shape, index_map, pipeline_mode=pl.Buffered(1))


def _dot(a, b):
    return jnp.dot(a, b, preferred_element_type=F32)


def _all_layers(param):
    return _resident(param.shape, lambda *_: (0, 0))


def _layer_row(ref, layer):
    return ref.at[layer:layer + 1, :]


def _zero_other_layers(ref, slot):
    for other in range(ref.shape[1]):
        if other != slot:
            ref[:, other] = jnp.zeros((ref.shape[0],) + ref.shape[2:], F32)
    return ref.at[:, slot]


def _stage_weights_bf16(first_step, pieces, stage_ref, sem):
    def copy(c):
        src = pieces[c][0]
        return pltpu.make_async_copy(src, stage_ref.at[c % 2, 0:src.shape[0], 0:src.shape[1]], sem.at[c % 2])

    @pl.when(first_step)
    def _():
        for c in range(min(2, len(pieces))):
            copy(c).start()
        for c, (src, dst) in enumerate(pieces):
            copy(c).wait()
            dst[...] = stage_ref[c % 2, 0:src.shape[0], 0:src.shape[1]].astype(BF16)
            if c + 2 < len(pieces):
                copy(c + 2).start()


def _mod_kernel(c_ref, w_ref, b_ref, o_ref):
    @pl.when(pl.program_id(1) == 0)
    def _():
        o_ref[...] = jnp.broadcast_to(b_ref[...], o_ref.shape)

    c = c_ref[...]
    a = (c * jax.nn.sigmoid(c)).astype(BF16)
    o_ref[...] += _dot(a, w_ref[...].astype(BF16))


def _mod_call(c_rows, w_mod, b_mod):
    tk = MOD_K_ROWS
    return pl.pallas_call(
        _mod_kernel,
        grid=(DEPTH, D_MODEL // tk),
        in_specs=[
            pl.BlockSpec((MOD_ROWS, tk), lambda l, k: (0, k)),
            pl.BlockSpec((None, tk, D_MOD), lambda l, k: (l, k, 0)),
            pl.BlockSpec((None, 1, D_MOD), lambda l, k: (l, 0, 0)),
        ],
        out_specs=pl.BlockSpec((None, MOD_ROWS, D_MOD), lambda l, k: (l, 0, 0)),
        out_shape=jax.ShapeDtypeStruct((DEPTH, MOD_ROWS, D_MOD), F32),
        compiler_params=_params(),
        name="mod_vectors",
    )(c_rows, w_mod, b_mod.reshape(DEPTH, 1, D_MOD))


def _proj_kernel(xl_ref, xc_ref, mod_ref, win_hbm, cos_ref, sa_ref, sb_ref, *refs, n_lat_tiles, slot, layer):
    z_ref, dk_ref, dv_ref, wk_ref, wv_ref, win_ref, stage_ref, sem = refs[-8:]
    nc = stage_ref.shape[2]
    _stage_weights_bf16(pl.program_id(0) == 0,
                        [(win_hbm.at[layer, :, c0:c0 + nc], win_ref.at[:, c0:c0 + nc]) for c0 in range(0, D_IN, nc)],
                        stage_ref, sem)
    seqs, t_ctx = dk_ref.shape[0], wk_ref.shape[-1]
    dk_ref, dv_ref, wk_ref, wv_ref = (_zero_other_layers(r, slot) for r in (dk_ref, dv_ref, wk_ref, wv_ref))
    is_lat = pl.program_id(0) < n_lat_tiles
    x = jnp.where(is_lat, xl_ref[...], xc_ref[...])
    sh1 = mod_ref[:, 0:D_MODEL]
    sc1 = mod_ref[:, D_MODEL:2 * D_MODEL]
    h = (x * (1.0 + sc1) + sh1).astype(BF16)
    for c0 in range(0, D_IN, nc):
        z = _dot(h, win_ref[:, c0:c0 + nc])
        for j in range(nc // LANE):
            blk = c0 // LANE + j
            u = z[:, j * LANE:(j + 1) * LANE]
            if blk in ROPE_BLOCKS:
                u = (u * cos_ref[...] + pltpu.roll(u, LANE - 16, 1) * sa_ref[...]
                     + pltpu.roll(u, 16, 1) * sb_ref[...])
            z_ref[:, blk * LANE:(blk + 1) * LANE] = u
            if COL_AK <= blk < COL_BQ:
                ref, head = (dk_ref, blk - COL_AK) if blk < COL_AV else (dv_ref, blk - COL_AV)
                for s in range(seqs):
                    ref[s, pl.ds(head, t_ctx, stride=DA_HEADS), :] = u[s * t_ctx:(s + 1) * t_ctx]
            elif blk in (COL_BK, COL_BV):
                ref = wk_ref if blk == COL_BK else wv_ref
                ut = u.T
                for s in range(seqs):
                    ref[s] = ut[:, s * t_ctx:(s + 1) * t_ctx]


def _proj_call(xl, xc, mod4, layer, w_in, rope_tabs, caches, t_lat, t_ctx, tm=PROJ_ROWS):
    n_lat, n_ctx = xl.shape[0] // tm, xc.shape[0] // tm
    lat_tiles_per_seq = t_lat // tm
    seqs = tm // t_ctx

    def mod_idx(i):
        row = jnp.where(i < n_lat, 1 + i // lat_tiles_per_seq, 0)
        return (layer, row, 0, 0)

    lat_idx = lambda i: jnp.minimum(i, n_lat - 1)
    ctx_idx = lambda i: jnp.maximum(i - n_lat, 0)
    rope_spec = pl.BlockSpec((tm, LANE), lambda i: (jnp.where(i < n_lat, 1 + i % lat_tiles_per_seq, 0), 0))
    n_slots, slot, first = (1, 0, layer) if caches else (DEPTH, layer, 0)
    cache_a = pl.BlockSpec((seqs, n_slots, DA_HEADS * t_ctx, 2 * HEAD_DIM), lambda i: (ctx_idx(i), first, 0, 0))
    cache_b = pl.BlockSpec((seqs, n_slots, WG_KV_HEADS * HEAD_DIM, t_ctx), lambda i: (ctx_idx(i), first, 0, 0))
    in_specs = [
        pl.BlockSpec((tm, D_MODEL), lambda i: (lat_idx(i), 0)),
        pl.BlockSpec((tm, D_MODEL), lambda i: (ctx_idx(i), 0)),
        pl.BlockSpec((None, None, 1, D_MOD), mod_idx),
        pl.BlockSpec(memory_space=pl.ANY),
        rope_spec, rope_spec, rope_spec,
    ]
    n_in = len(in_specs)
    in_specs += [pl.BlockSpec(memory_space=pl.ANY)] * len(caches)
    b_ctx = xc.shape[0] // t_ctx
    cache_shapes = [(b_ctx, DEPTH, DA_HEADS * t_ctx, 2 * HEAD_DIM)] * 2 \
        + [(b_ctx, DEPTH, WG_KV_HEADS * HEAD_DIM, t_ctx)] * 2
    return pl.pallas_call(
        functools.partial(_proj_kernel, n_lat_tiles=n_lat, slot=slot, layer=layer),
        grid=(n_lat + n_ctx,),
        in_specs=in_specs,
        out_specs=[pl.BlockSpec((tm, D_IN), lambda i: (i, 0)), cache_a, cache_a, cache_b, cache_b],
        out_shape=[jax.ShapeDtypeStruct((xl.shape[0] + xc.shape[0], D_IN), F32)]
        + [jax.ShapeDtypeStruct(s, F32) for s in cache_shapes],
        input_output_aliases={n_in + k: 1 + k for k in range(len(caches))},
        scratch_shapes=[
            pltpu.VMEM((D_MODEL, D_IN), BF16),
            pltpu.VMEM((2, D_MODEL, PROJ_COLS), F32),
            pltpu.SemaphoreType.DMA((2,)),
        ],
        compiler_params=_params(),
        name="proj",
    )(xl, xc, mod4, w_in, *rope_tabs, *caches)


def _values_t(v_parts):
    vt = jnp.concatenate([v.T for v in v_parts], axis=1)
    return jnp.concatenate([vt, jnp.ones((ONES_ROWS, vt.shape[1]), F32)], axis=0).astype(BF16)


def _run_chains(chains):
    a, b = {}, {}
    n = len(chains)
    for c in range(n + 2):
        if c < n:
            a[c] = chains[c][0]()
        if 1 <= c <= n:
            b[c - 1] = chains[c - 1][1](a.pop(c - 1))
        if c >= 2:
            chains[c - 2][2](b.pop(c - 2))


def _diff_chains(q_ref, k_ref, v_ref, o_ref, lam_ref, g_ref, cached_kv, *, lam_init, tq):
    t, heads = q_ref.shape[0], q_ref.shape[1] // LANE
    lv = lam_ref[...]
    lam = (jnp.exp(jnp.sum(lv[0:1] * lv[1:2], axis=-1, keepdims=True))
           - jnp.exp(jnp.sum(lv[2:3] * lv[3:4], axis=-1, keepdims=True)) + lam_init)
    lane = lax.broadcasted_iota(jnp.int32, (tq, LANE), 1)
    kv = {}

    def scores(hh, i):
        cols = slice(hh * LANE, (hh + 1) * LANE)
        if hh not in kv:
            k_parts, v_parts = [k_ref[:, cols]], [v_ref[:, cols]]
            extra = cached_kv(hh)
            if extra is not None:
                k_parts.append(extra[0])
                v_parts.append(extra[1])
            kv[hh] = (jnp.concatenate(k_parts, axis=0).astype(BF16), _values_t(v_parts))
        q = q_ref[i * tq:(i + 1) * tq, cols] * (QK_SCALE * LOG2E)
        qs = jnp.concatenate([jnp.where(lane < HEAD_DIM, q, 0.0),
                              jnp.where(lane >= HEAD_DIM, q, 0.0)], axis=0).astype(BF16)
        return lax.dot_general(kv[hh][0], qs, NT_DIMS, preferred_element_type=F32)

    def weights(s):
        return jnp.exp2(s - s.max(axis=0, keepdims=True)).astype(BF16)

    def finish(hh, i, et):
        ot = _dot(kv[hh][1], et)
        o = ot[0:LANE] * (1.0 / ot[LANE:LANE + 1])
        od = o[:, 0:tq] - lam * o[:, tq:2 * tq]
        yt = od * lax.rsqrt(jnp.mean(od * od, axis=0, keepdims=True) + LN_EPS)
        o_ref[i * tq:(i + 1) * tq, hh * LANE:(hh + 1) * LANE] = yt.T * g_ref[...] * (1.0 - lam_init)

    return [(functools.partial(scores, hh, i), weights, functools.partial(finish, hh, i))
            for hh in range(heads) for i in range(t // tq)]


def _win_chains(q_ref, k_ref, v_ref, o_ref, sink_ref, kc_ref=None, vc_ref=None, kv_heads=range(WG_KV_HEADS)):
    latent = kc_ref is not None
    head0 = kv_heads[0] * WG_GROUP
    t = q_ref.shape[0]
    w = WINDOW
    tqb, heads_per_chain = w, WG_GROUP
    nb = t // tqb
    kb = k_ref[...].astype(BF16)
    vt = v_ref[...].T
    if latent:
        kcb = kc_ref[...].T.astype(BF16)
        vct = vc_ref[...]
        jj = lax.broadcasted_iota(jnp.int32, (w, w), 0)
        ii = lax.broadcasted_iota(jnp.int32, (w, w), 1)
        bias_prev = jnp.concatenate([jnp.where(jj >= ii, 0.0, -1e30)] * heads_per_chain, axis=1)
        bias_next = jnp.concatenate([jnp.where(jj <= ii, 0.0, -1e30)] * heads_per_chain, axis=1)
    chains = [(kv, h0, n)
              for n in range(nb)
              for kv in kv_heads
              for h0 in range(kv * WG_GROUP, (kv + 1) * WG_GROUP, heads_per_chain)]
    sinks = {}

    def sink_row(h0):
        if h0 not in sinks:
            sinks[h0] = jnp.concatenate(
                [jnp.broadcast_to(sink_ref[:, h:h + 1] * LOG2E, (1, tqb))
                 for h in range(h0, h0 + heads_per_chain)], axis=1)
        return sinks[h0]

    def key_blocks(n):
        return (max(n - 1, 0), min(n + 1, nb - 1)) if latent else (0, nb - 1)

    def scores(kv, h0, n):
        lo = kv * HEAD_DIM
        rows = slice(n * tqb, (n + 1) * tqb)
        q_g = (jnp.concatenate([q_ref[rows, (h - head0) * HEAD_DIM:(h - head0 + 1) * HEAD_DIM]
                                for h in range(h0, h0 + heads_per_chain)], axis=0)
               * (QK_SCALE * LOG2E)).astype(BF16)
        b0, b1 = key_blocks(n)
        keys = kb[b0 * tqb:(b1 + 1) * tqb, lo:lo + HEAD_DIM]
        if latent:
            keys = jnp.concatenate([keys, kcb[:, lo:lo + HEAD_DIM]], axis=0)
        st = lax.dot_general(keys, q_g, NT_DIMS, preferred_element_type=F32)
        if latent:
            parts = []
            for blk in range(b0, b1 + 1):
                part = st[(blk - b0) * w:(blk - b0 + 1) * w]
                if blk == n - 1:
                    part = part + bias_prev
                elif blk == n + 1:
                    part = part + bias_next
                parts.append(part)
            parts.append(st[(b1 - b0 + 1) * w:])
            st = jnp.concatenate(parts, axis=0)
        return st

    def weights(h0, st):
        m = jnp.maximum(st.max(axis=0, keepdims=True), sink_row(h0))
        return jnp.exp2(st - m).astype(BF16), m

    def finish(kv, h0, n, et_m):
        et, m = et_m
        lo = kv * HEAD_DIM
        b0, b1 = key_blocks(n)
        vals = [vt[lo:lo + HEAD_DIM, b0 * tqb:(b1 + 1) * tqb]]
        if latent:
            vals.append(vct[lo:lo + HEAD_DIM])
        vals = jnp.concatenate(vals, axis=1)
        vals = jnp.concatenate([vals, jnp.ones((ONES_ROWS, vals.shape[1]), F32)], axis=0).astype(BF16)
        ot = _dot(vals, et)
        d = ot[HEAD_DIM:HEAD_DIM + 1] + jnp.exp2(sink_row(h0) - m)
        on = ot[0:HEAD_DIM] * (1.0 / d)
        for p in range(heads_per_chain // 2):
            pair = jnp.concatenate([on[:, (2 * p) * tqb:(2 * p + 1) * tqb],
                                    on[:, (2 * p + 1) * tqb:(2 * p + 2) * tqb]], axis=0)
            c0 = ((h0 - head0) // 2 + p) * LANE
            o_ref[n * tqb:(n + 1) * tqb, c0:c0 + LANE] = pair.T

    return [(functools.partial(scores, kv, h0, n), functools.partial(weights, h0),
             functools.partial(finish, kv, h0, n)) for kv, h0, n in chains]


def _log_sigmoid(x):
    return jnp.minimum(x, 0.0) - jnp.log1p(jnp.exp(-jnp.abs(x)))


def _ret_fill_decay(d_scr, lg, t, latent):
    bw = min(t, LANE)
    nb = t // bw
    off = (lax.broadcasted_iota(jnp.int32, (bw, bw), 1)
           - lax.broadcasted_iota(jnp.int32, (bw, bw), 0)).astype(F32)
    for hh in range(d_scr.shape[0]):
        lgf, lgb = lg[hh // 2][0:1, hh % 2:hh % 2 + 1], lg[hh // 2][1:2, hh % 2:hh % 2 + 1]
        for o in range(-(nb - 1), nb):
            diff = off + float(o * bw)
            blk = (jnp.where(diff >= 0, jnp.exp(jnp.maximum(diff, 0.0) * lgf), 0.0)
                   + jnp.where(diff <= 0, jnp.exp(jnp.maximum(-diff, 0.0) * lgb), 0.0))
            for bs in range(max(0, -o), min(nb, nb - o)):
                d_scr[hh, bs * bw:(bs + 1) * bw, (bs + o) * bw:(bs + o + 1) * bw] = blk
        if latent:
            tp = lax.broadcasted_iota(jnp.int32, (RT_DK, t), 1).astype(F32)
            d_scr[hh, t:t + RT_DK, :] = jnp.exp((tp + 1.0) * lgf)
            d_scr[hh, t + RT_DK:t + 2 * RT_DK, :] = jnp.exp((float(t) - tp) * lgb)


def _ret_chains(q_ref, k_ref, v_refs, cg_refs, lg, g_ref, o_ref, d_scr, s0_ref=None, sfin_ref=None):
    latent = s0_ref is not None
    seqs, t = q_ref.shape[0], q_ref.shape[1]
    pairs = len(v_refs)
    tq = RET_Q_BLOCK_LAT if latent else RET_Q_BLOCK_CTX
    if latent:
        eye = (lax.broadcasted_iota(jnp.int32, (RT_DK, RT_DK), 0)
               == lax.broadcasted_iota(jnp.int32, (RT_DK, RT_DK), 1)).astype(F32)
    heads = {}

    def head_operands(sq, hh):
        if (sq, hh) not in heads:
            p, j = hh // 2, hh % 2
            kf = k_ref[sq, :, hh * RT_DK:(hh + 1) * RT_DK] * (RT_DK ** -0.5)
            v = v_refs[p][sq, :, j * RT_DV:(j + 1) * RT_DV]
            keys, vals_t = [kf], [v.T]
            if latent:
                keys += [eye, eye]
                vals_t.append(jnp.concatenate([s0_ref[sq, 0, hh], s0_ref[sq, 1, hh]], axis=0).T)
            heads[(sq, hh)] = (jnp.concatenate(keys, axis=0).astype(BF16),
                               jnp.concatenate(vals_t, axis=1).astype(BF16))
            if not latent:
                sp = lax.broadcasted_iota(jnp.int32, (t, 1), 0).astype(F32)
                zf = jnp.exp((float(t) - 1.0 - sp) * lg[p][0:1, j:j + 1])
                zb = jnp.exp(sp * lg[p][1:2, j:j + 1])
                kz = jnp.concatenate([kf * zf, kf * zb], axis=1).astype(BF16)
                s_fb = _dot(heads[(sq, hh)][1], kz).T
                sfin_ref[sq, 0, hh] = s_fb[0:RT_DK]
                sfin_ref[sq, 1, hh] = s_fb[RT_DK:2 * RT_DK]
        return heads[(sq, hh)]

    chains = [(sq, hh, i) for sq in range(seqs) for hh in range(2 * pairs) for i in range(t // tq)]

    def scores(sq, hh, i):
        q = q_ref[sq, i * tq:(i + 1) * tq, hh * RT_DK:(hh + 1) * RT_DK].astype(BF16)
        return lax.dot_general(head_operands(sq, hh)[0], q, NT_DIMS, preferred_element_type=F32)

    def finish(sq, hh, i, at):
        yt = _dot(head_operands(sq, hh)[1], at)
        mu = jnp.mean(yt, axis=0, keepdims=True)
        yc = yt - mu
        var = jnp.mean(yc * yc, axis=0, keepdims=True)
        yn = (yc * lax.rsqrt(var + LN_EPS)).T * g_ref[...]
        cg = cg_refs[hh // 2][sq, i * tq:(i + 1) * tq, (hh % 2) * RT_DV:(hh % 2 + 1) * RT_DV]
        o_ref[sq, i * tq:(i + 1) * tq, hh * RT_DV:(hh + 1) * RT_DV] = yn * (cg * jax.nn.sigmoid(cg))

    def weights(hh, i, st):
        return (st * d_scr[hh, :, i * tq:(i + 1) * tq]).astype(BF16)

    return [(functools.partial(scores, sq, hh, i), functools.partial(weights, hh, i),
             functools.partial(finish, sq, hh, i)) for sq, hh, i in chains]


def _lat_mix_kernel(aq_ref, ak_ref, av_ref, bq_ref, bk_ref, bv_ref, cq_ref, ck_ref, cv_ref, cg_ref,
                    akc_ref, avc_ref, bkc_ref, bvc_ref, s0_ref, lam_ref, dng_ref, sink_ref, dec_ref, rng_ref,
                    oa_ref, ob_ref, oc_ref, d_scr, *, lam_init, half, layer):
    t = aq_ref.shape[1]
    dng_ref, sink_ref, rng_ref = (_layer_row(r, layer) for r in (dng_ref, sink_ref, rng_ref))
    lg = [_log_sigmoid(dec_ref[0])]

    @pl.when(pl.program_id(0) == 0)
    def _():
        _ret_fill_decay(d_scr, lg, t, True)

    def cached_kv(hh):
        mine = pl.ds(2 * half + hh, akc_ref.shape[0] // DA_HEADS, stride=DA_HEADS)
        return akc_ref[mine, :], avc_ref[mine, :]

    chains = _diff_chains(aq_ref.at[0], ak_ref.at[0], av_ref.at[0], oa_ref.at[0], lam_ref, dng_ref,
                          cached_kv, lam_init=lam_init, tq=DIFF_Q_BLOCK)
    chains += _win_chains(bq_ref.at[0], bk_ref.at[0], bv_ref.at[0], ob_ref.at[0], sink_ref,
                          bkc_ref, bvc_ref, kv_heads=(half,))
    chains += _ret_chains(cq_ref, ck_ref, [cv_ref], [cg_ref], lg, rng_ref, oc_ref, d_scr, s0_ref=s0_ref)
    _run_chains(chains)


def _lat_mix_call(z3, half, caches, state6, layer, diff_lam, diff_norm_g, sink, dec4, ret_norm_g, lam_init):
    z3, b0, b = z3
    t = z3.shape[1]
    ck_a, cv_a, ck_b, cv_b = caches
    zcol = lambda blk0, nblk: pl.BlockSpec((1, t, nblk * LANE), lambda bi: (bi + b0, 0, blk0 // nblk))
    layer_spec = lambda *s: pl.BlockSpec((None,) + s, lambda bi: (layer,) + (0,) * len(s))
    cache_spec = lambda a: pl.BlockSpec((None, None) + a.shape[2:], lambda bi: (bi, layer, 0, 0))
    o_spec = pl.BlockSpec((1, t, BRANCH_W // 2), lambda bi: (bi, 0, 0))
    o_shape = jax.ShapeDtypeStruct((b, t, BRANCH_W // 2), F32)
    return pl.pallas_call(
        functools.partial(_lat_mix_kernel, lam_init=lam_init, half=half, layer=layer),
        grid=(b,),
        in_specs=[
            zcol(COL_AQ + 2 * half, 2), zcol(COL_AK + 2 * half, 2), zcol(COL_AV + 2 * half, 2),
            zcol(COL_BQ + 2 * half, 2), zcol(COL_BK, 1), zcol(COL_BV, 1),
            zcol(COL_CQ + half, 1), zcol(COL_CK + half, 1), zcol(COL_CV + 2 * half, 2), zcol(COL_CG + 2 * half, 2),
            cache_spec(ck_a), cache_spec(cv_a), cache_spec(ck_b), cache_spec(cv_b),
            pl.BlockSpec((1, None, 2, 2, RT_DK, RT_DV), lambda bi: (bi, layer, 0, half, 0, 0)),
            layer_spec(4, HEAD_DIM), _all_layers(diff_norm_g), _all_layers(sink),
            pl.BlockSpec((None, 1, 2, 2), lambda bi: (layer, half, 0, 0)), _all_layers(ret_norm_g),
        ],
        out_specs=[o_spec, o_spec, o_spec],
        out_shape=[o_shape, o_shape, o_shape],
        scratch_shapes=[pltpu.VMEM((2, t + 2 * RT_DK, t), F32)],
        compiler_params=_params(),
        name="mix_lat",
    )(*([z3] * 10), ck_a, cv_a, ck_b, cv_b, state6, diff_lam, diff_norm_g, sink, dec4, ret_norm_g)


def _ctx_mix_kernel(z_ref, lam_ref, dng_ref, sink_ref, dec_ref, rng_ref, *refs, lam_init, slot, layer):
    oa_ref, ob_ref, oc_ref, sfin_ref, d_scr = refs[-5:]
    sfin_ref = _zero_other_layers(sfin_ref, slot)
    dng_ref, sink_ref, rng_ref = (_layer_row(r, layer) for r in (dng_ref, sink_ref, rng_ref))
    seqs, t = z_ref.shape[0], z_ref.shape[1]
    cols = lambda c0, c1: slice(c0 * LANE, c1 * LANE)
    lg = [_log_sigmoid(dec_ref[p]) for p in range(RT_HEADS // 2)]

    @pl.when(pl.program_id(0) == 0)
    def _():
        _ret_fill_decay(d_scr, lg, t, False)

    chains = []
    for sq in range(seqs):
        chains += _diff_chains(z_ref.at[sq, :, cols(COL_AQ, COL_AK)], z_ref.at[sq, :, cols(COL_AK, COL_AV)],
                               z_ref.at[sq, :, cols(COL_AV, COL_BQ)], oa_ref.at[sq], lam_ref, dng_ref,
                               lambda _: None, lam_init=lam_init, tq=t)
        chains += _win_chains(z_ref.at[sq, :, cols(COL_BQ, COL_BK)], z_ref.at[sq, :, cols(COL_BK, COL_BV)],
                              z_ref.at[sq, :, cols(COL_BV, COL_CQ)], ob_ref.at[sq], sink_ref)
    chains += _ret_chains(z_ref.at[:, :, cols(COL_CQ, COL_CK)], z_ref.at[:, :, cols(COL_CK, COL_CV)],
                          [z_ref.at[:, :, cols(COL_CV + 2 * p, COL_CV + 2 * p + 2)] for p in range(2)],
                          [z_ref.at[:, :, cols(COL_CG + 2 * p, COL_CG + 2 * p + 2)] for p in range(2)],
                          lg, rng_ref, oc_ref, d_scr, sfin_ref=sfin_ref)
    _run_chains(chains)


def _ctx_mix_call(z3, new_state, layer, diff_lam, diff_norm_g, sink, dec4, ret_norm_g, lam_init, seqs=2):
    z3, b0, b = z3
    t = z3.shape[1]
    layer_spec = lambda *s: pl.BlockSpec((None,) + s, lambda bi: (layer,) + (0,) * len(s))
    in_specs = [
        pl.BlockSpec((seqs, t, D_IN), lambda bi: (bi + b0 // seqs, 0, 0)),
        layer_spec(4, HEAD_DIM), _all_layers(diff_norm_g), _all_layers(sink),
        layer_spec(RT_HEADS // 2, 2, 2), _all_layers(ret_norm_g),
    ]
    args = [z3, diff_lam, diff_norm_g, sink, dec4, ret_norm_g]
    n_slots, slot, first = (DEPTH, layer, 0) if new_state is None else (1, 0, layer)
    aliases = {}
    if new_state is not None:
        aliases = {len(in_specs): 3}
        in_specs.append(pl.BlockSpec(memory_space=pl.ANY))
        args.append(new_state)
    o_spec = pl.BlockSpec((seqs, t, BRANCH_W), lambda bi: (bi, 0, 0))
    o_shape = jax.ShapeDtypeStruct((b, t, BRANCH_W), F32)
    return pl.pallas_call(
        functools.partial(_ctx_mix_kernel, lam_init=lam_init, slot=slot, layer=layer),
        grid=(b // seqs,),
        in_specs=in_specs,
        out_specs=[o_spec, o_spec, o_spec,
                   pl.BlockSpec((seqs, n_slots, 2, RT_HEADS, RT_DK, RT_DV), lambda bi: (bi, first, 0, 0, 0, 0))],
        out_shape=[o_shape, o_shape, o_shape,
                   jax.ShapeDtypeStruct((b, DEPTH, 2, RT_HEADS, RT_DK, RT_DV), F32)],
        input_output_aliases=aliases,
        scratch_shapes=[pltpu.VMEM((RT_HEADS, t, t), F32)],
        compiler_params=_params(),
        name="mix_ctx",
    )(*args)


def _layer_norm(x, g, b):
    mu = jnp.mean(x, axis=-1, keepdims=True)
    xc = x - mu
    var = jnp.mean(xc * xc, axis=-1, keepdims=True)
    return xc * lax.rsqrt(var + LN_EPS) * g + b


def _tail_kernel(xc_ref, xl_ref, mod_ref, oac_ref, obc_ref, occ_ref,
                 oal0_ref, oal1_ref, obl0_ref, obl1_ref, ocl0_ref, ocl1_ref,
                 bg_ref, ln1g_ref, ln1b_ref, ln2g_ref, ln2b_ref,
                 wg_hbm, wpa_hbm, wpb_hbm, wpc_hbm, wo_hbm, w1_hbm, w2_hbm, yc_ref, yl_ref,
                 wg_ref, wpa_ref, wpb_ref, wpc_ref, wo_ref, w1_ref, w2_ref, stage_ref, sem,
                 *, n_ctx_tiles, layer):
    d = D_MODEL
    bg_ref, ln1g_ref, ln1b_ref, ln2g_ref, ln2b_ref = (
        _layer_row(r, layer) for r in (bg_ref, ln1g_ref, ln1b_ref, ln2g_ref, ln2b_ref))
    pieces = [(wg_hbm.at[layer, :, i * d:(i + 1) * d], wg_ref.at[:, i * d:(i + 1) * d]) for i in range(3)]
    pieces += [(w.at[layer], r) for w, r in ((wpa_hbm, wpa_ref), (wpb_hbm, wpb_ref), (wpc_hbm, wpc_ref),
                                             (wo_hbm, wo_ref))]
    pieces += [(w1_hbm.at[layer, :, i * d:(i + 1) * d], w1_ref.at[:, i * d:(i + 1) * d]) for i in range(4)]
    pieces += [(w2_hbm.at[layer, i * d:(i + 1) * d, :], w2_ref.at[i * d:(i + 1) * d, :]) for i in range(4)]
    _stage_weights_bf16(pl.program_id(0) == 0, pieces, stage_ref, sem)

    is_ctx = pl.program_id(0) < n_ctx_tiles
    x = jnp.where(is_ctx, xc_ref[...], xl_ref[...])
    sh1, sc1, g1 = mod_ref[:, 0:d], mod_ref[:, d:2 * d], mod_ref[:, 2 * d:3 * d]
    sh2, sc2, g2 = mod_ref[:, 3 * d:4 * d], mod_ref[:, 4 * d:5 * d], mod_ref[:, 5 * d:6 * d]
    h1 = (x * (1.0 + sc1) + sh1).astype(BF16)
    merged = None
    branches = ((oac_ref, (oal0_ref, oal1_ref), wpa_ref), (obc_ref, (obl0_ref, obl1_ref), wpb_ref),
                (occ_ref, (ocl0_ref, ocl1_ref), wpc_ref))
    for i, (oc_ref_i, ol_refs_i, wp_ref) in enumerate(branches):
        o_lat = jnp.concatenate([r[...] for r in ol_refs_i], axis=1)
        o = jnp.where(is_ctx, oc_ref_i[...], o_lat).astype(BF16)
        gate = jax.nn.sigmoid(_dot(h1, wg_ref[:, i * d:(i + 1) * d]) + bg_ref[:, i * d:(i + 1) * d])
        part = gate * _dot(o, wp_ref[...])
        merged = part if merged is None else merged + part
    y = _dot(merged.astype(BF16), wo_ref[...])
    x1 = _layer_norm(ALPHA * x + g1 * y, ln1g_ref[...], ln1b_ref[...])
    h2 = (x1 * (1.0 + sc2) + sh2).astype(BF16)
    u = jnp.maximum(_dot(h2, w1_ref[...]), 0.0)
    f = _dot((u * u).astype(BF16), w2_ref[...])
    x2 = _layer_norm(ALPHA * x1 + g2 * f, ln2g_ref[...], ln2b_ref[...])

    @pl.when(is_ctx)
    def _():
        yc_ref[...] = x2

    @pl.when(jnp.logical_not(is_ctx))
    def _():
        yl_ref[...] = x2


def _tail_call(xc, xl, mod4, layer, o_ctx, o_lat, wg, bg, wpa, wpb, wpc, wo, ln1, w1, w2, ln2,
               t_lat, tm=TAIL_ROWS):
    n_ctx, n_lat = xc.shape[0] // tm, xl.shape[0] // tm
    lat_tiles_per_seq = t_lat // tm

    def mod_idx(i):
        row = jnp.where(i < n_ctx, 0, 1 + (i - n_ctx) // lat_tiles_per_seq)
        return (layer, row, 0, 0)

    ctx_spec = lambda w: pl.BlockSpec((tm, w), lambda i: (jnp.minimum(i, n_ctx - 1), 0))
    lat_spec = lambda w: pl.BlockSpec((tm, w), lambda i: (jnp.maximum(i - n_ctx, 0), 0))
    in_hbm = pl.BlockSpec(memory_space=pl.ANY)
    return pl.pallas_call(
        functools.partial(_tail_kernel, n_ctx_tiles=n_ctx, layer=layer),
        grid=(n_ctx + n_lat,),
        in_specs=[
            ctx_spec(D_MODEL), lat_spec(D_MODEL),
            pl.BlockSpec((None, None, 1, D_MOD), mod_idx),
            ctx_spec(BRANCH_W), ctx_spec(BRANCH_W), ctx_spec(BRANCH_W),
            *[lat_spec(BRANCH_W // 2)] * 6,
            *[_all_layers(p) for p in (bg, *ln1, *ln2)],
        ] + [in_hbm] * 7,
        out_specs=[ctx_spec(D_MODEL), lat_spec(D_MODEL)],
        out_shape=[jax.ShapeDtypeStruct(xc.shape, F32), jax.ShapeDtypeStruct(xl.shape, F32)],
        scratch_shapes=[
            pltpu.VMEM((D_MODEL, D_GATE), BF16),
            pltpu.VMEM((BRANCH_W, D_MODEL), BF16), pltpu.VMEM((BRANCH_W, D_MODEL), BF16),
            pltpu.VMEM((BRANCH_W, D_MODEL), BF16),
            pltpu.VMEM((D_MODEL, D_MODEL), BF16),
            pltpu.VMEM((D_MODEL, D_FF), BF16), pltpu.VMEM((D_FF, D_MODEL), BF16),
            pltpu.VMEM((2, D_MODEL, D_MODEL), F32),
            pltpu.SemaphoreType.DMA((2,)),
        ],
        compiler_params=_params(),
        name="tail",
    )(xc, xl, mod4, *o_ctx, *o_lat, bg, *ln1, *ln2, wg, wpa, wpb, wpc, wo, w1, w2)


def _rope_tables(n_tokens, identity_rows):
    rows = n_tokens // GRID_W
    r, col = jnp.meshgrid(jnp.arange(rows), jnp.arange(GRID_W), indexing="ij")
    r = r.reshape(-1).astype(F32)
    col = col.reshape(-1).astype(F32)
    nf = HEAD_DIM // 4
    inv = ROPE_BASE ** (-jnp.arange(nf, dtype=F32) / nf)
    ang_r = r[:, None] * inv[None, :]
    ang_c = col[:, None] * inv[None, :]
    zero = jnp.zeros_like(ang_r)
    cos = jnp.concatenate([jnp.cos(ang_r)] * 2 + [jnp.cos(ang_c)] * 2, axis=-1)
    s_next = jnp.concatenate([-jnp.sin(ang_r), zero, -jnp.sin(ang_c), zero], axis=-1)
    s_prev = jnp.concatenate([zero, jnp.sin(ang_r), zero, jnp.sin(ang_c)], axis=-1)
    ident = (jnp.ones, jnp.zeros, jnp.zeros)
    return tuple(jnp.concatenate([fill((identity_rows, LANE), F32), jnp.tile(t, (1, LANE // HEAD_DIM))], axis=0)
                 for fill, t in zip(ident, (cos, s_next, s_prev)))


def kernel(x_prompt, x_sample, c, cache_diff_k, cache_diff_v, cache_win_k, cache_win_v, state_ret,
           c_ctx, w_mod, b_mod, w_in, diff_lam, diff_norm_g, win_sink, ret_decay, ret_norm_g,
           w_pa, w_pb, w_pc, w_gate, b_gate, w_o, ln1_g, ln1_b, w_ff1, w_ff2, ln2_g, ln2_b):
    bp, tp, d = x_prompt.shape
    bs, ts, _ = x_sample.shape
    assert d == D_MODEL and w_in.shape == (DEPTH, D_MODEL, D_IN) and c.shape[0] + 1 <= MOD_ROWS
    past = cache_diff_k.shape[2]

    c_rows = jnp.concatenate(
        [c_ctx[None, :], c, jnp.zeros((MOD_ROWS - 1 - bs, d), F32)], axis=0)
    mod4 = _mod_call(c_rows, w_mod, b_mod).reshape(DEPTH, MOD_ROWS, 1, D_MOD)
    rope_tabs = _rope_tables(ts, PROJ_ROWS)

    ck_a = cache_diff_k.reshape(bs, DEPTH, past * DA_HEADS, 2 * HEAD_DIM)
    cv_a = cache_diff_v.reshape(bs, DEPTH, past * DA_HEADS, 2 * HEAD_DIM)
    ck_b = cache_win_k.transpose(0, 1, 3, 4, 2).reshape(bs, DEPTH, WG_KV_HEADS * HEAD_DIM, past)
    cv_b = cache_win_v.transpose(0, 1, 3, 4, 2).reshape(bs, DEPTH, WG_KV_HEADS * HEAD_DIM, past)

    ln1, ln2 = (ln1_g, ln1_b), (ln2_g, ln2_b)
    dec4 = ret_decay.reshape(DEPTH, 2, RT_HEADS // 2, 2).transpose(0, 2, 1, 3)

    xp = x_prompt.reshape(bp * tp, d)
    xs = x_sample.reshape(bs * ts, d)
    new_caches, new_state = [], None
    for l in range(DEPTH):
        lam_init = 0.8 - 0.6 * math.exp(-0.3 * l)
        z_all, *new_caches = _proj_call(xs, xp, mod4, l, w_in, rope_tabs, new_caches, ts, tp)
        z_ctx = (z_all.reshape(-1, tp, D_IN), bs * ts // tp, bp)
        z_lat = (z_all.reshape(-1, ts, D_IN), 0, bs)
        *o_ctx, new_state = _ctx_mix_call(z_ctx, new_state, l, diff_lam, diff_norm_g, win_sink,
                                          dec4, ret_norm_g, lam_init)
        halves = [_lat_mix_call(z_lat, half, (ck_a, cv_a, ck_b, cv_b), state_ret, l, diff_lam,
                                diff_norm_g, win_sink, dec4, ret_norm_g, lam_init)
                  for half in range(2)]
        o_lat = [o.reshape(bs * ts, BRANCH_W // 2) for pair in zip(*halves) for o in pair]
        xp, xs = _tail_call(xp, xs, mod4, l, [o.reshape(bp * tp, BRANCH_W) for o in o_ctx],
                            o_lat, w_gate, b_gate,
                            w_pa, w_pb, w_pc, w_o, ln1, w_ff1, w_ff2, ln2, ts)
    dk, dv, wk, wv = new_caches
    new_diff = [a.reshape(bp, DEPTH, tp, DA_HEADS, 2 * HEAD_DIM) for a in (dk, dv)]
    new_win = [a.reshape(bp, DEPTH, WG_KV_HEADS, HEAD_DIM, tp).transpose(0, 1, 4, 2, 3) for a in (wk, wv)]
    return (xp.reshape(bp, tp, d), xs.reshape(bs, ts, d), *new_diff, *new_win, new_state)
```

```python
import functools
import math

import jax
import jax.numpy as jnp
import numpy as np
from jax import lax
from jax.experimental import pallas as pl
from jax.experimental.pallas import tpu as pltpu

F32 = jnp.float32
BF16 = jnp.bfloat16

D_MODEL = 1024
DEPTH = 2
GRID_W = 64
HEAD_DIM = 64
DA_HEADS = 4
WG_Q_HEADS = 8
WG_KV_HEADS = 2
WG_GROUP = WG_Q_HEADS // WG_KV_HEADS
WINDOW = 128
RT_HEADS = 4
RT_DK = 64
RT_DV = 128
BRANCH_W = 512
D_IN = 3840
D_GATE = 3 * D_MODEL
D_FF = 4 * D_MODEL
D_MOD = 6 * D_MODEL
ROPE_BASE = 10000.0
LN_EPS = 1e-5
ALPHA = (2 * DEPTH) ** 0.25
QK_SCALE = HEAD_DIM ** -0.5
LOG2E = math.log2(math.e)

LANE = 128
MOD_ROWS = 8
ONES_ROWS = 16
PROJ_COLS = 768
PROJ_ROWS = 512
TAIL_ROWS = 256
MOD_K_ROWS = 256
DIFF_Q_BLOCK = 128
RET_Q_BLOCK_LAT, RET_Q_BLOCK_CTX = 512, 128

COL_AQ, COL_AK, COL_AV = 0, 4, 8
COL_BQ, COL_BK, COL_BV = 12, 16, 17
COL_CQ, COL_CK, COL_CV, COL_CG = 18, 20, 22, 26
ROPE_BLOCKS = tuple(range(0, 8)) + tuple(range(12, 17))

VMEM_LIMIT = 56 * 1024 * 1024
NT_DIMS = (((1,), (1,)), ((), ()))
TN_DIMS = (((0,), (0,)), ((), ()))


def _params():
    return pltpu.CompilerParams(vmem_limit_bytes=VMEM_LIMIT)


def _resident(shape, index_map):
    return pl.BlockSpec(shape, index_map, pipeline_mode=pl.Buffered(1))


def _dot(a, b):
    return jnp.dot(a, b, preferred_element_type=F32)


def _all_layers(param):
    return _resident(param.shape, lambda *_: (0, 0))


def _layer_row(ref, layer):
    return ref.at[layer:layer + 1, :]


def _zero_other_layers(ref, slot):
    for other in range(ref.shape[1]):
        if other != slot:
            ref[:, other] = jnp.zeros((ref.shape[0],) + ref.shape[2:], F32)
    return ref.at[:, slot]


def _stage_weights_bf16(first_step, pieces, stage_ref, sem):
    def copy(c):
        src = pieces[c][0]
        return pltpu.make_async_copy(src, stage_ref.at[c % 2, 0:src.shape[0], 0:src.shape[1]], sem.at[c % 2])

    @pl.when(first_step)
    def _():
        for c in range(min(2, len(pieces))):
            copy(c).start()
        for c, (src, dst) in enumerate(pieces):
            copy(c).wait()
            dst[...] = stage_ref[c % 2, 0:src.shape[0], 0:src.shape[1]].astype(BF16)
            if c + 2 < len(pieces):
                copy(c + 2).start()


def _mod_kernel(c_ref, w_ref, b_ref, o_ref):
    @pl.when(pl.program_id(1) == 0)
    def _():
        o_ref[...] = jnp.broadcast_to(b_ref[...], o_ref.shape)

    c = c_ref[...]
    a = (c * jax.nn.sigmoid(c)).astype(BF16)
    o_ref[...] += _dot(a, w_ref[...].astype(BF16))


def _mod_call(c_rows, w_mod, b_mod):
    tk = MOD_K_ROWS
    return pl.pallas_call(
        _mod_kernel,
        grid=(DEPTH, D_MODEL // tk),
        in_specs=[
            pl.BlockSpec((MOD_ROWS, tk), lambda l, k: (0, k)),
            pl.BlockSpec((None, tk, D_MOD), lambda l, k: (l, k, 0)),
            pl.BlockSpec((None, 1, D_MOD), lambda l, k: (l, 0, 0)),
        ],
        out_specs=pl.BlockSpec((None, MOD_ROWS, D_MOD), lambda l, k: (l, 0, 0)),
        out_shape=jax.ShapeDtypeStruct((DEPTH, MOD_ROWS, D_MOD), F32),
        compiler_params=_params(),
        name="mod_vectors",
    )(c_rows, w_mod, b_mod.reshape(DEPTH, 1, D_MOD))


def _proj_kernel(xl_ref, xc_ref, mod_ref, win_hbm, cos_ref, sa_ref, sb_ref, *refs, n_lat_tiles, slot, layer):
    z_ref, dk_ref, dv_ref, wk_ref, wv_ref, win_ref, stage_ref, sem = refs[-8:]
    nc = stage_ref.shape[2]
    _stage_weights_bf16(pl.program_id(0) == 0,
                        [(win_hbm.at[layer, :, c0:c0 + nc], win_ref.at[:, c0:c0 + nc]) for c0 in range(0, D_IN, nc)],
                        stage_ref, sem)
    seqs, t_ctx = dk_ref.shape[0], wk_ref.shape[-1]
    dk_ref, dv_ref, wk_ref, wv_ref = (_zero_other_layers(r, slot) for r in (dk_ref, dv_ref, wk_ref, wv_ref))
    is_lat = pl.program_id(0) < n_lat_tiles
    x = jnp.where(is_lat, xl_ref[...], xc_ref[...])
    sh1 = mod_ref[:, 0:D_MODEL]
    sc1 = mod_ref[:, D_MODEL:2 * D_MODEL]
    h = (x * (1.0 + sc1) + sh1).astype(BF16)
    for c0 in range(0, D_IN, nc):
        z = _dot(h, win_ref[:, c0:c0 + nc])
        for j in range(nc // LANE):
            blk = c0 // LANE + j
            u = z[:, j * LANE:(j + 1) * LANE]
            if blk in ROPE_BLOCKS:
                u = (u * cos_ref[...] + pltpu.roll(u, LANE - 16, 1) * sa_ref[...]
                     + pltpu.roll(u, 16, 1) * sb_ref[...])
            z_ref[:, blk * LANE:(blk + 1) * LANE] = u
            if COL_AK <= blk < COL_BQ:
                ref, head = (dk_ref, blk - COL_AK) if blk < COL_AV else (dv_ref, blk - COL_AV)
                for s in range(seqs):
                    ref[s, pl.ds(head, t_ctx, stride=DA_HEADS), :] = u[s * t_ctx:(s + 1) * t_ctx]
            elif blk in (COL_BK, COL_BV):
                ref = wk_ref if blk == COL_BK else wv_ref
                ut = u.T
                for s in range(seqs):
                    ref[s] = ut[:, s * t_ctx:(s + 1) * t_ctx]


def _proj_call(xl, xc, mod4, layer, w_in, rope_tabs, caches, t_lat, t_ctx, tm=PROJ_ROWS):
    n_lat, n_ctx = xl.shape[0] // tm, xc.shape[0] // tm
    lat_tiles_per_seq = t_lat // tm
    seqs = tm // t_ctx

    def mod_idx(i):
        row = jnp.where(i < n_lat, 1 + i // lat_tiles_per_seq, 0)
        return (layer, row, 0, 0)

    lat_idx = lambda i: jnp.minimum(i, n_lat - 1)
    ctx_idx = lambda i: jnp.maximum(i - n_lat, 0)
    rope_spec = pl.BlockSpec((tm, LANE), lambda i: (jnp.where(i < n_lat, 1 + i % lat_tiles_per_seq, 0), 0))
    n_slots, slot, first = (1, 0, layer) if caches else (DEPTH, layer, 0)
    cache_a = pl.BlockSpec((seqs, n_slots, DA_HEADS * t_ctx, 2 * HEAD_DIM), lambda i: (ctx_idx(i), first, 0, 0))
    cache_b = pl.BlockSpec((seqs, n_slots, WG_KV_HEADS * HEAD_DIM, t_ctx), lambda i: (ctx_idx(i), first, 0, 0))
    in_specs = [
        pl.BlockSpec((tm, D_MODEL), lambda i: (lat_idx(i), 0)),
        pl.BlockSpec((tm, D_MODEL), lambda i: (ctx_idx(i), 0)),
        pl.BlockSpec((None, None, 1, D_MOD), mod_idx),
        pl.BlockSpec(memory_space=pl.ANY),
        rope_spec, rope_spec, rope_spec,
    ]
    n_in = len(in_specs)
    in_specs += [pl.BlockSpec(memory_space=pl.ANY)] * len(caches)
    b_ctx = xc.shape[0] // t_ctx
    cache_shapes = [(b_ctx, DEPTH, DA_HEADS * t_ctx, 2 * HEAD_DIM)] * 2 \
        + [(b_ctx, DEPTH, WG_KV_HEADS * HEAD_DIM, t_ctx)] * 2
    return pl.pallas_call(
        functools.partial(_proj_kernel, n_lat_tiles=n_lat, slot=slot, layer=layer),
        grid=(n_lat + n_ctx,),
        in_specs=in_specs,
        out_specs=[pl.BlockSpec((tm, D_IN), lambda i: (i, 0)), cache_a, cache_a, cache_b, cache_b],
        out_shape=[jax.ShapeDtypeStruct((xl.shape[0] + xc.shape[0], D_IN), F32)]
        + [jax.ShapeDtypeStruct(s, F32) for s in cache_shapes],
        input_output_aliases={n_in + k: 1 + k for k in range(len(caches))},
        scratch_shapes=[
            pltpu.VMEM((D_MODEL, D_IN), BF16),
            pltpu.VMEM((2, D_MODEL, PROJ_COLS), F32),
            pltpu.SemaphoreType.DMA((2,)),
        ],
        compiler_params=_params(),
        name="proj",
    )(xl, xc, mod4, w_in, *rope_tabs, *caches)


def _values_t(v_parts):
    vt = jnp.concatenate([v.T for v in v_parts], axis=1)
    return jnp.concatenate([vt, jnp.ones((ONES_ROWS, vt.shape[1]), F32)], axis=0).astype(BF16)


def _run_chains(chains):
    a, b = {}, {}
    n = len(chains)
    for c in range(n + 2):
        if c < n:
            a[c] = chains[c][0]()
        if 1 <= c <= n:
            b[c - 1] = chains[c - 1][1](a.pop(c - 1))
        if c >= 2:
            chains[c - 2][2](b.pop(c - 2))


def _diff_chains(q_ref, k_ref, v_ref, o_ref, lam_ref, g_ref, cached_kv, *, lam_init, tq):
    t, heads = q_ref.shape[0], q_ref.shape[1] // LANE
    lv = lam_ref[...]
    lam = (jnp.exp(jnp.sum(lv[0:1] * lv[1:2], axis=-1, keepdims=True))
           - jnp.exp(jnp.sum(lv[2:3] * lv[3:4], axis=-1, keepdims=True)) + lam_init)
    lane = lax.broadcasted_iota(jnp.int32, (tq, LANE), 1)
    kv = {}

    def scores(hh, i):
        cols = slice(hh * LANE, (hh + 1) * LANE)
        if hh not in kv:
            k_parts, v_parts = [k_ref[:, cols]], [v_ref[:, cols]]
            extra = cached_kv(hh)
            if extra is not None:
                k_parts.append(extra[0])
                v_parts.append(extra[1])
            kv[hh] = (jnp.concatenate(k_parts, axis=0).astype(BF16), _values_t(v_parts))
        q = q_ref[i * tq:(i + 1) * tq, cols] * (QK_SCALE * LOG2E)
        qs = jnp.concatenate([jnp.where(lane < HEAD_DIM, q, 0.0),
                              jnp.where(lane >= HEAD_DIM, q, 0.0)], axis=0).astype(BF16)
        return lax.dot_general(kv[hh][0], qs, NT_DIMS, preferred_element_type=F32)

    def weights(s):
        return jnp.exp2(s - s.max(axis=0, keepdims=True)).astype(BF16)

    def finish(hh, i, et):
        ot = _dot(kv[hh][1], et)
        o = ot[0:LANE] * (1.0 / ot[LANE:LANE + 1])
        od = o[:, 0:tq] - lam * o[:, tq:2 * tq]
        yt = od * lax.rsqrt(jnp.mean(od * od, axis=0, keepdims=True) + LN_EPS)
        o_ref[i * tq:(i + 1) * tq, hh * LANE:(hh + 1) * LANE] = yt.T * g_ref[...] * (1.0 - lam_init)

    return [(functools.partial(scores, hh, i), weights, functools.partial(finish, hh, i))
            for hh in range(heads) for i in range(t // tq)]


def _win_chains(q_ref, k_ref, v_ref, o_ref, sink_ref, kc_ref=None, vc_ref=None, kv_heads=range(WG_KV_HEADS)):
    latent = kc_ref is not None
    head0 = kv_heads[0] * WG_GROUP
    t = q_ref.shape[0]
    w = WINDOW
    tqb, heads_per_chain = w, WG_GROUP
    nb = t // tqb
    kb = k_ref[...].astype(BF16)
    vt = v_ref[...].T
    if latent:
        kcb = kc_ref[...].T.astype(BF16)
        vct = vc_ref[...]
        jj = lax.broadcasted_iota(jnp.int32, (w, w), 0)
        ii = lax.broadcasted_iota(jnp.int32, (w, w), 1)
        bias_prev = jnp.concatenate([jnp.where(jj >= ii, 0.0, -1e30)] * heads_per_chain, axis=1)
        bias_next = jnp.concatenate([jnp.where(jj <= ii, 0.0, -1e30)] * heads_per_chain, axis=1)
    chains = [(kv, h0, n)
              for n in range(nb)
              for kv in kv_heads
              for h0 in range(kv * WG_GROUP, (kv + 1) * WG_GROUP, heads_per_chain)]
    sinks = {}

    def sink_row(h0):
        if h0 not in sinks:
            sinks[h0] = jnp.concatenate(
                [jnp.broadcast_to(sink_ref[:, h:h + 1] * LOG2E, (1, tqb))
                 for h in range(h0, h0 + heads_per_chain)], axis=1)
        return sinks[h0]

    def key_blocks(n):
        return (max(n - 1, 0), min(n + 1, nb - 1)) if latent else (0, nb - 1)

    def scores(kv, h0, n):
        lo = kv * HEAD_DIM
        rows = slice(n * tqb, (n + 1) * tqb)
        q_g = (jnp.concatenate([q_ref[rows, (h - head0) * HEAD_DIM:(h - head0 + 1) * HEAD_DIM]
                                for h in range(h0, h0 + heads_per_chain)], axis=0)
               * (QK_SCALE * LOG2E)).astype(BF16)
        b0, b1 = key_blocks(n)
        keys = kb[b0 * tqb:(b1 + 1) * tqb, lo:lo + HEAD_DIM]
        if latent:
            keys = jnp.concatenate([keys, kcb[:, lo:lo + HEAD_DIM]], axis=0)
        st = lax.dot_general(keys, q_g, NT_DIMS, preferred_element_type=F32)
        if latent:
            parts = []
            for blk in range(b0, b1 + 1):
                part = st[(blk - b0) * w:(blk - b0 + 1) * w]
                if blk == n - 1:
                    part = part + bias_prev
                elif blk == n + 1:
                    part = part + bias_next
                parts.append(part)
            parts.append(st[(b1 - b0 + 1) * w:])
            st = jnp.concatenate(parts, axis=0)
        return st

    def weights(h0, st):
        m = jnp.maximum(st.max(axis=0, keepdims=True), sink_row(h0))
        return jnp.exp2(st - m).astype(BF16), m

    def finish(kv, h0, n, et_m):
        et, m = et_m
        lo = kv * HEAD_DIM
        b0, b1 = key_blocks(n)
        vals = [vt[lo:lo + HEAD_DIM, b0 * tqb:(b1 + 1) * tqb]]
        if latent:
            vals.append(vct[lo:lo + HEAD_DIM])
        vals = jnp.concatenate(vals, axis=1)
        vals = jnp.concatenate([vals, jnp.ones((ONES_ROWS, vals.shape[1]), F32)], axis=0).astype(BF16)
        ot = _dot(vals, et)
        d = ot[HEAD_DIM:HEAD_DIM + 1] + jnp.exp2(sink_row(h0) - m)
        on = ot[0:HEAD_DIM] * (1.0 / d)
        for p in range(heads_per_chain // 2):
            pair = jnp.concatenate([on[:, (2 * p) * tqb:(2 * p + 1) * tqb],
                                    on[:, (2 * p + 1) * tqb:(2 * p + 2) * tqb]], axis=0)
            c0 = ((h0 - head0) // 2 + p) * LANE
            o_ref[n * tqb:(n + 1) * tqb, c0:c0 + LANE] = pair.T

    return [(functools.partial(scores, kv, h0, n), functools.partial(weights, h0),
             functools.partial(finish, kv, h0, n)) for kv, h0, n in chains]


def _log_sigmoid(x):
    return jnp.minimum(x, 0.0) - jnp.log1p(jnp.exp(-jnp.abs(x)))


def _ret_fill_decay(d_scr, lg, t, latent):
    bw = min(t, LANE)
    nb = t // bw
    off = (lax.broadcasted_iota(jnp.int32, (bw, bw), 1)
           - lax.broadcasted_iota(jnp.int32, (bw, bw), 0)).astype(F32)
    for hh in range(d_scr.shape[0]):
        lgf, lgb = lg[hh // 2][0:1, hh % 2:hh % 2 + 1], lg[hh // 2][1:2, hh % 2:hh % 2 + 1]
        for o in range(-(nb - 1), nb):
            diff = off + float(o * bw)
            blk = (jnp.where(diff >= 0, jnp.exp(jnp.maximum(diff, 0.0) * lgf), 0.0)
                   + jnp.where(diff <= 0, jnp.exp(jnp.maximum(-diff, 0.0) * lgb), 0.0))
            for bs in range(max(0, -o), min(nb, nb - o)):
                d_scr[hh, bs * bw:(bs + 1) * bw, (bs + o) * bw:(bs + o + 1) * bw] = blk
        if latent:
            tp = lax.broadcasted_iota(jnp.int32, (RT_DK, t), 1).astype(F32)
            d_scr[hh, t:t + RT_DK, :] = jnp.exp((tp + 1.0) * lgf)
            d_scr[hh, t + RT_DK:t + 2 * RT_DK, :] = jnp.exp((float(t) - tp) * lgb)


def _ret_chains(q_ref, k_ref, v_refs, cg_refs, lg, g_ref, o_ref, d_scr, s0_ref=None, sfin_ref=None):
    latent = s0_ref is not None
    seqs, t = q_ref.shape[0], q_ref.shape[1]
    pairs = len(v_refs)
    tq = RET_Q_BLOCK_LAT if latent else RET_Q_BLOCK_CTX
    if latent:
        eye = (lax.broadcasted_iota(jnp.int32, (RT_DK, RT_DK), 0)
               == lax.broadcasted_iota(jnp.int32, (RT_DK, RT_DK), 1)).astype(F32)
    heads = {}

    def head_operands(sq, hh):
        if (sq, hh) not in heads:
            p, j = hh // 2, hh % 2
            kf = k_ref[sq, :, hh * RT_DK:(hh + 1) * RT_DK] * (RT_DK ** -0.5)
            v = v_refs[p][sq, :, j * RT_DV:(j + 1) * RT_DV]
            keys, vals_t = [kf], [v.T]
            if latent:
                keys += [eye, eye]
                vals_t.append(jnp.concatenate([s0_ref[sq, 0, hh], s0_ref[sq, 1, hh]], axis=0).T)
            heads[(sq, hh)] = (jnp.concatenate(keys, axis=0).astype(BF16),
                               jnp.concatenate(vals_t, axis=1).astype(BF16))
            if not latent:
                sp = lax.broadcasted_iota(jnp.int32, (t, 1), 0).astype(F32)
                zf = jnp.exp((float(t) - 1.0 - sp) * lg[p][0:1, j:j + 1])
                zb = jnp.exp(sp * lg[p][1:2, j:j + 1])
                kz = jnp.concatenate([kf * zf, kf * zb], axis=1).astype(BF16)
                s_fb = _dot(heads[(sq, hh)][1], kz).T
                sfin_ref[sq, 0, hh] = s_fb[0:RT_DK]
                sfin_ref[sq, 1, hh] = s_fb[RT_DK:2 * RT_DK]
        return heads[(sq, hh)]

    chains = [(sq, hh, i) for sq in range(seqs) for hh in range(2 * pairs) for i in range(t // tq)]

    def scores(sq, hh, i):
        q = q_ref[sq, i * tq:(i + 1) * tq, hh * RT_DK:(hh + 1) * RT_DK].astype(BF16)
        return lax.dot_general(head_operands(sq, hh)[0], q, NT_DIMS, preferred_element_type=F32)

    def finish(sq, hh, i, at):
        yt = _dot(head_operands(sq, hh)[1], at)
        mu = jnp.mean(yt, axis=0, keepdims=True)
        yc = yt - mu
        var = jnp.mean(yc * yc, axis=0, keepdims=True)
        yn = (yc * lax.rsqrt(var + LN_EPS)).T * g_ref[...]
        cg = cg_refs[hh // 2][sq, i * tq:(i + 1) * tq, (hh % 2) * RT_DV:(hh % 2 + 1) * RT_DV]
        o_ref[sq, i * tq:(i + 1) * tq, hh * RT_DV:(hh + 1) * RT_DV] = yn * (cg * jax.nn.sigmoid(cg))

    def weights(hh, i, st):
        return (st * d_scr[hh, :, i * tq:(i + 1) * tq]).astype(BF16)

    return [(functools.partial(scores, sq, hh, i), functools.partial(weights, hh, i),
             functools.partial(finish, sq, hh, i)) for sq, hh, i in chains]


def _lat_mix_kernel(aq_ref, ak_ref, av_ref, bq_ref, bk_ref, bv_ref, cq_ref, ck_ref, cv_ref, cg_ref,
                    akc_ref, avc_ref, bkc_ref, bvc_ref, s0_ref, lam_ref, dng_ref, sink_ref, dec_ref, rng_ref,
                    oa_ref, ob_ref, oc_ref, d_scr, *, lam_init, half, layer):
    t = aq_ref.shape[1]
    dng_ref, sink_ref, rng_ref = (_layer_row(r, layer) for r in (dng_ref, sink_ref, rng_ref))
    lg = [_log_sigmoid(dec_ref[0])]

    @pl.when(pl.program_id(0) == 0)
    def _():
        _ret_fill_decay(d_scr, lg, t, True)

    def cached_kv(hh):
        mine = pl.ds(2 * half + hh, akc_ref.shape[0] // DA_HEADS, stride=DA_HEADS)
        return akc_ref[mine, :], avc_ref[mine, :]

    chains = _diff_chains(aq_ref.at[0], ak_ref.at[0], av_ref.at[0], oa_ref.at[0], lam_ref, dng_ref,
                          cached_kv, lam_init=lam_init, tq=DIFF_Q_BLOCK)
    chains += _win_chains(bq_ref.at[0], bk_ref.at[0], bv_ref.at[0], ob_ref.at[0], sink_ref,
                          bkc_ref, bvc_ref, kv_heads=(half,))
    chains += _ret_chains(cq_ref, ck_ref, [cv_ref], [cg_ref], lg, rng_ref, oc_ref, d_scr, s0_ref=s0_ref)
    _run_chains(chains)


def _lat_mix_call(z3, half, caches, state6, layer, diff_lam, diff_norm_g, sink, dec4, ret_norm_g, lam_init):
    z3, b0, b = z3
    t = z3.shape[1]
    ck_a, cv_a, ck_b, cv_b = caches
    zcol = lambda blk0, nblk: pl.BlockSpec((1, t, nblk * LANE), lambda bi: (bi + b0, 0, blk0 // nblk))
    layer_spec = lambda *s: pl.BlockSpec((None,) + s, lambda bi: (layer,) + (0,) * len(s))
    cache_spec = lambda a: pl.BlockSpec((None, None) + a.shape[2:], lambda bi: (bi, layer, 0, 0))
    o_spec = pl.BlockSpec((1, t, BRANCH_W // 2), lambda bi: (bi, 0, 0))
    o_shape = jax.ShapeDtypeStruct((b, t, BRANCH_W // 2), F32)
    return pl.pallas_call(
        functools.partial(_lat_mix_kernel, lam_init=lam_init, half=half, layer=layer),
        grid=(b,),
        in_specs=[
            zcol(COL_AQ + 2 * half, 2), zcol(COL_AK + 2 * half, 2), zcol(COL_AV + 2 * half, 2),
            zcol(COL_BQ + 2 * half, 2), zcol(COL_BK, 1), zcol(COL_BV, 1),
            zcol(COL_CQ + half, 1), zcol(COL_CK + half, 1), zcol(COL_CV + 2 * half, 2), zcol(COL_CG + 2 * half, 2),
            cache_spec(ck_a), cache_spec(cv_a), cache_spec(ck_b), cache_spec(cv_b),
            pl.BlockSpec((1, None, 2, 2, RT_DK, RT_DV), lambda bi: (bi, layer, 0, half, 0, 0)),
            layer_spec(4, HEAD_DIM), _all_layers(diff_norm_g), _all_layers(sink),
            pl.BlockSpec((None, 1, 2, 2), lambda bi: (layer, half, 0, 0)), _all_layers(ret_norm_g),
        ],
        out_specs=[o_spec, o_spec, o_spec],
        out_shape=[o_shape, o_shape, o_shape],
        scratch_shapes=[pltpu.VMEM((2, t + 2 * RT_DK, t), F32)],
        compiler_params=_params(),
        name="mix_lat",
    )(*([z3] * 10), ck_a, cv_a, ck_b, cv_b, state6, diff_lam, diff_norm_g, sink, dec4, ret_norm_g)


def _ctx_mix_kernel(z_ref, lam_ref, dng_ref, sink_ref, dec_ref, rng_ref, *refs, lam_init, slot, layer):
    oa_ref, ob_ref, oc_ref, sfin_ref, d_scr = refs[-5:]
    sfin_ref = _zero_other_layers(sfin_ref, slot)
    dng_ref, sink_ref, rng_ref = (_layer_row(r, layer) for r in (dng_ref, sink_ref, rng_ref))
    seqs, t = z_ref.shape[0], z_ref.shape[1]
    cols = lambda c0, c1: slice(c0 * LANE, c1 * LANE)
    lg = [_log_sigmoid(dec_ref[p]) for p in range(RT_HEADS // 2)]

    @pl.when(pl.program_id(0) == 0)
    def _():
        _ret_fill_decay(d_scr, lg, t, False)

    chains = []
    for sq in range(seqs):
        chains += _diff_chains(z_ref.at[sq, :, cols(COL_AQ, COL_AK)], z_ref.at[sq, :, cols(COL_AK, COL_AV)],
                               z_ref.at[sq, :, cols(COL_AV, COL_BQ)], oa_ref.at[sq], lam_ref, dng_ref,
                               lambda _: None, lam_init=lam_init, tq=t)
        chains += _win_chains(z_ref.at[sq, :, cols(COL_BQ, COL_BK)], z_ref.at[sq, :, cols(COL_BK, COL_BV)],
                              z_ref.at[sq, :, cols(COL_BV, COL_CQ)], ob_ref.at[sq], sink_ref)
    chains += _ret_chains(z_ref.at[:, :, cols(COL_CQ, COL_CK)], z_ref.at[:, :, cols(COL_CK, COL_CV)],
                          [z_ref.at[:, :, cols(COL_CV + 2 * p, COL_CV + 2 * p + 2)] for p in range(2)],
                          [z_ref.at[:, :, cols(COL_CG + 2 * p, COL_CG + 2 * p + 2)] for p in range(2)],
                          lg, rng_ref, oc_ref, d_scr, sfin_ref=sfin_ref)
    _run_chains(chains)


def _ctx_mix_call(z3, new_state, layer, diff_lam, diff_norm_g, sink, dec4, ret_norm_g, lam_init, seqs=2):
    z3, b0, b = z3
    t = z3.shape[1]
    layer_spec = lambda *s: pl.BlockSpec((None,) + s, lambda bi: (layer,) + (0,) * len(s))
    in_specs = [
        pl.BlockSpec((seqs, t, D_IN), lambda bi: (bi + b0 // seqs, 0, 0)),
        layer_spec(4, HEAD_DIM), _all_layers(diff_norm_g), _all_layers(sink),
        layer_spec(RT_HEADS // 2, 2, 2), _all_layers(ret_norm_g),
    ]
    args = [z3, diff_lam, diff_norm_g, sink, dec4, ret_norm_g]
    n_slots, slot, first = (DEPTH, layer, 0) if new_state is None else (1, 0, layer)
    aliases = {}
    if new_state is not None:
        aliases = {len(in_specs): 3}
        in_specs.append(pl.BlockSpec(memory_space=pl.ANY))
        args.append(new_state)
    o_spec = pl.BlockSpec((seqs, t, BRANCH_W), lambda bi: (bi, 0, 0))
    o_shape = jax.ShapeDtypeStruct((b, t, BRANCH_W), F32)
    return pl.pallas_call(
        functools.partial(_ctx_mix_kernel, lam_init=lam_init, slot=slot, layer=layer),
        grid=(b // seqs,),
        in_specs=in_specs,
        out_specs=[o_spec, o_spec, o_spec,
                   pl.BlockSpec((seqs, n_slots, 2, RT_HEADS, RT_DK, RT_DV), lambda bi: (bi, first, 0, 0, 0, 0))],
        out_shape=[o_shape, o_shape, o_shape,
                   jax.ShapeDtypeStruct((b, DEPTH, 2, RT_HEADS, RT_DK, RT_DV), F32)],
        input_output_aliases=aliases,
        scratch_shapes=[pltpu.VMEM((RT_HEADS, t, t), F32)],
        compiler_params=_params(),
        name="mix_ctx",
    )(*args)


def _layer_norm(x, g, b):
    mu = jnp.mean(x, axis=-1, keepdims=True)
    xc = x - mu
    var = jnp.mean(xc * xc, axis=-1, keepdims=True)
    return xc * lax.rsqrt(var + LN_EPS) * g + b


def _tail_kernel(xc_ref, xl_ref, mod_ref, oac_ref, obc_ref, occ_ref,
                 oal0_ref, oal1_ref, obl0_ref, obl1_ref, ocl0_ref, ocl1_ref,
                 bg_ref, ln1g_ref, ln1b_ref, ln2g_ref, ln2b_ref,
                 wg_hbm, wpa_hbm, wpb_hbm, wpc_hbm, wo_hbm, w1_hbm, w2_hbm, yc_ref, yl_ref,
                 wg_ref, wpa_ref, wpb_ref, wpc_ref, wo_ref, w1_ref, w2_ref, stage_ref, sem,
                 *, n_ctx_tiles, layer):
    d = D_MODEL
    bg_ref, ln1g_ref, ln1b_ref, ln2g_ref, ln2b_ref = (
        _layer_row(r, layer) for r in (bg_ref, ln1g_ref, ln1b_ref, ln2g_ref, ln2b_ref))
    pieces = [(wg_hbm.at[layer, :, i * d:(i + 1) * d], wg_ref.at[:, i * d:(i + 1) * d]) for i in range(3)]
    pieces += [(w.at[layer], r) for w, r in ((wpa_hbm, wpa_ref), (wpb_hbm, wpb_ref), (wpc_hbm, wpc_ref),
                                             (wo_hbm, wo_ref))]
    pieces += [(w1_hbm.at[layer, :, i * d:(i + 1) * d], w1_ref.at[:, i * d:(i + 1) * d]) for i in range(4)]
    pieces += [(w2_hbm.at[layer, i * d:(i + 1) * d, :], w2_ref.at[i * d:(i + 1) * d, :]) for i in range(4)]
    _stage_weights_bf16(pl.program_id(0) == 0, pieces, stage_ref, sem)

    is_ctx = pl.program_id(0) < n_ctx_tiles
    x = jnp.where(is_ctx, xc_ref[...], xl_ref[...])
    sh1, sc1, g1 = mod_ref[:, 0:d], mod_ref[:, d:2 * d], mod_ref[:, 2 * d:3 * d]
    sh2, sc2, g2 = mod_ref[:, 3 * d:4 * d], mod_ref[:, 4 * d:5 * d], mod_ref[:, 5 * d:6 * d]
    h1 = (x * (1.0 + sc1) + sh1).astype(BF16)
    merged = None
    branches = ((oac_ref, (oal0_ref, oal1_ref), wpa_ref), (obc_ref, (obl0_ref, obl1_ref), wpb_ref),
                (occ_ref, (ocl0_ref, ocl1_ref), wpc_ref))
    for i, (oc_ref_i, ol_refs_i, wp_ref) in enumerate(branches):
        o_lat = jnp.concatenate([r[...] for r in ol_refs_i], axis=1)
        o = jnp.where(is_ctx, oc_ref_i[...], o_lat).astype(BF16)
        gate = jax.nn.sigmoid(_dot(h1, wg_ref[:, i * d:(i + 1) * d]) + bg_ref[:, i * d:(i + 1) * d])
        part = gate * _dot(o, wp_ref[...])
        merged = part if merged is None else merged + part
    y = _dot(merged.astype(BF16), wo_ref[...])
    x1 = _layer_norm(ALPHA * x + g1 * y, ln1g_ref[...], ln1b_ref[...])
    h2 = (x1 * (1.0 + sc2) + sh2).astype(BF16)
    u = jnp.maximum(_dot(h2, w1_ref[...]), 0.0)
    f = _dot((u * u).astype(BF16), w2_ref[...])
    x2 = _layer_norm(ALPHA * x1 + g2 * f, ln2g_ref[...], ln2b_ref[...])

    @pl.when(is_ctx)
    def _():
        yc_ref[...] = x2

    @pl.when(jnp.logical_not(is_ctx))
    def _():
        yl_ref[...] = x2


def _tail_call(xc, xl, mod4, layer, o_ctx, o_lat, wg, bg, wpa, wpb, wpc, wo, ln1, w1, w2, ln2,
               t_lat, tm=TAIL_ROWS):
    n_ctx, n_lat = xc.shape[0] // tm, xl.shape[0] // tm
    lat_tiles_per_seq = t_lat // tm

    def mod_idx(i):
        row = jnp.where(i < n_ctx, 0, 1 + (i - n_ctx) // lat_tiles_per_seq)
        return (layer, row, 0, 0)

    ctx_spec = lambda w: pl.BlockSpec((tm, w), lambda i: (jnp.minimum(i, n_ctx - 1), 0))
    lat_spec = lambda w: pl.BlockSpec((tm, w), lambda i: (jnp.maximum(i - n_ctx, 0), 0))
    in_hbm = pl.BlockSpec(memory_space=pl.ANY)
    return pl.pallas_call(
        functools.partial(_tail_kernel, n_ctx_tiles=n_ctx, layer=layer),
        grid=(n_ctx + n_lat,),
        in_specs=[
            ctx_spec(D_MODEL), lat_spec(D_MODEL),
            pl.BlockSpec((None, None, 1, D_MOD), mod_idx),
            ctx_spec(BRANCH_W), ctx_spec(BRANCH_W), ctx_spec(BRANCH_W),
            *[lat_spec(BRANCH_W // 2)] * 6,
            *[_all_layers(p) for p in (bg, *ln1, *ln2)],
        ] + [in_hbm] * 7,
        out_specs=[ctx_spec(D_MODEL), lat_spec(D_MODEL)],
        out_shape=[jax.ShapeDtypeStruct(xc.shape, F32), jax.ShapeDtypeStruct(xl.shape, F32)],
        scratch_shapes=[
            pltpu.VMEM((D_MODEL, D_GATE), BF16),
            pltpu.VMEM((BRANCH_W, D_MODEL), BF16), pltpu.VMEM((BRANCH_W, D_MODEL), BF16),
            pltpu.VMEM((BRANCH_W, D_MODEL), BF16),
            pltpu.VMEM((D_MODEL, D_MODEL), BF16),
            pltpu.VMEM((D_MODEL, D_FF), BF16), pltpu.VMEM((D_FF, D_MODEL), BF16),
            pltpu.VMEM((2, D_MODEL, D_MODEL), F32),
            pltpu.SemaphoreType.DMA((2,)),
        ],
        compiler_params=_params(),
        name="tail",
    )(xc, xl, mod4, *o_ctx, *o_lat, bg, *ln1, *ln2, wg, wpa, wpb, wpc, wo, w1, w2)


def _rope_tables(n_tokens, identity_rows):
    f32 = np.float32
    rows = n_tokens // GRID_W
    r, col = np.meshgrid(np.arange(rows), np.arange(GRID_W), indexing="ij")
    r = r.reshape(-1).astype(f32)
    col = col.reshape(-1).astype(f32)
    nf = HEAD_DIM // 4
    inv = (f32(ROPE_BASE) ** (-np.arange(nf, dtype=f32) / f32(nf))).astype(f32)
    ang_r = r[:, None] * inv[None, :]
    ang_c = col[:, None] * inv[None, :]
    zero = np.zeros_like(ang_r)
    cos = np.concatenate([np.cos(ang_r)] * 2 + [np.cos(ang_c)] * 2, axis=-1)
    s_next = np.concatenate([-np.sin(ang_r), zero, -np.sin(ang_c), zero], axis=-1)
    s_prev = np.concatenate([zero, np.sin(ang_r), zero, np.sin(ang_c)], axis=-1)
    ident = (np.ones, np.zeros, np.zeros)
    return tuple(jnp.asarray(np.concatenate([fill((identity_rows, LANE), f32),
                                             np.tile(t.astype(f32), (1, LANE // HEAD_DIM))], axis=0))
                 for fill, t in zip(ident, (cos, s_next, s_prev)))


def kernel(x_prompt, x_sample, c, cache_diff_k, cache_diff_v, cache_win_k, cache_win_v, state_ret,
           c_ctx, w_mod, b_mod, w_in, diff_lam, diff_norm_g, win_sink, ret_decay, ret_norm_g,
           w_pa, w_pb, w_pc, w_gate, b_gate, w_o, ln1_g, ln1_b, w_ff1, w_ff2, ln2_g, ln2_b):
    bp, tp, d = x_prompt.shape
    bs, ts, _ = x_sample.shape
    assert d == D_MODEL and w_in.shape == (DEPTH, D_MODEL, D_IN) and c.shape[0] + 1 <= MOD_ROWS
    past = cache_diff_k.shape[2]

    c_rows = jnp.concatenate(
        [c_ctx[None, :], c, jnp.zeros((MOD_ROWS - 1 - bs, d), F32)], axis=0)
    mod4 = _mod_call(c_rows, w_mod, b_mod).reshape(DEPTH, MOD_ROWS, 1, D_MOD)
    rope_tabs = _rope_tables(ts, PROJ_ROWS)

    ck_a = cache_diff_k.reshape(bs, DEPTH, past * DA_HEADS, 2 * HEAD_DIM)
    cv_a = cache_diff_v.reshape(bs, DEPTH, past * DA_HEADS, 2 * HEAD_DIM)
    ck_b = cache_win_k.transpose(0, 1, 3, 4, 2).reshape(bs, DEPTH, WG_KV_HEADS * HEAD_DIM, past)
    cv_b = cache_win_v.transpose(0, 1, 3, 4, 2).reshape(bs, DEPTH, WG_KV_HEADS * HEAD_DIM, past)

    ln1, ln2 = (ln1_g, ln1_b), (ln2_g, ln2_b)
    dec4 = ret_decay.reshape(DEPTH, 2, RT_HEADS // 2, 2).transpose(0, 2, 1, 3)

    xp = x_prompt.reshape(bp * tp, d)
    xs = x_sample.reshape(bs * ts, d)
    new_caches, new_state = [], None
    for l in range(DEPTH):
        lam_init = 0.8 - 0.6 * math.exp(-0.3 * l)
        z_all, *new_caches = _proj_call(xs, xp, mod4, l, w_in, rope_tabs, new_caches, ts, tp)
        z_ctx = (z_all.reshape(-1, tp, D_IN), bs * ts // tp, bp)
        z_lat = (z_all.reshape(-1, ts, D_IN), 0, bs)
        *o_ctx, new_state = _ctx_mix_call(z_ctx, new_state, l, diff_lam, diff_norm_g, win_sink,
                                          dec4, ret_norm_g, lam_init)
        halves = [_lat_mix_call(z_lat, half, (ck_a, cv_a, ck_b, cv_b), state_ret, l, diff_lam,
                                diff_norm_g, win_sink, dec4, ret_norm_g, lam_init)
                  for half in range(2)]
        o_lat = [o.reshape(bs * ts, BRANCH_W // 2) for pair in zip(*halves) for o in pair]
        xp, xs = _tail_call(xp, xs, mod4, l, [o.reshape(bp * tp, BRANCH_W) for o in o_ctx],
                            o_lat, w_gate, b_gate,
                            w_pa, w_pb, w_pc, w_o, ln1, w_ff1, w_ff2, ln2, ts)
    dk, dv, wk, wv = new_caches
    new_diff = [a.reshape(bp, DEPTH, tp, DA_HEADS, 2 * HEAD_DIM) for a in (dk, dv)]
    new_win = [a.reshape(bp, DEPTH, WG_KV_HEADS, HEAD_DIM, tp).transpose(0, 1, 4, 2, 3) for a in (wk, wv)]
    return (xp.reshape(bp, tp, d), xs.reshape(bs, ts, d), *new_diff, *new_win, new_state)
```
